```python
import math
import jax, jax.numpy as jnp
from jax import lax
import numpy as np

D_MODEL = 2048
BATCH = 4
SEQ = 4096
DEPTH = 2

N_EVEN = (DEPTH + 1) // 2
N_ODD = DEPTH // 2
EPS = 1e-6
NEG_INF = -1e30

ATTN_HEADS = D_MODEL // 256
ATTN_KV_HEADS = ATTN_HEADS // 4
HEAD_DIM = 128
ATTN_WIDTH = ATTN_HEADS * HEAD_DIM
KV_WIDTH = ATTN_KV_HEADS * HEAD_DIM
WINDOW = 128
BLOCK = 128
REL_BUCKETS = 32
REL_MAX_DIST = 128

SSM_WIDTH = D_MODEL - ATTN_WIDTH
SSM_GROUP = 16
SSM_GROUPS = SSM_WIDTH // SSM_GROUP
SSM_STATE = 64

EVEN_IN = ATTN_WIDTH + 2 * KV_WIDTH + SSM_WIDTH

GMLP_WIDTH = D_MODEL
GMLP_CHUNK = 128
GMLP_HEADS = 16
GMLP_HEAD_DIM = GMLP_WIDTH // GMLP_HEADS

N_EXPERTS = 16
EXPERT_FF = D_MODEL
CAPACITY_FACTOR = 2

kernel_name = 'hybrid_swa_s5_gmlp_ecmoe_encoder'


def rms_norm(x, g):
    xf = x.astype(jnp.float32)
    y = xf * lax.rsqrt(jnp.mean(xf * xf, axis=-1, keepdims=True) + EPS)
    return (y * g.astype(jnp.float32)).astype(x.dtype)


def t5_bucket(rel):
    nb = REL_BUCKETS // 2
    max_exact = nb // 2
    base = jnp.where(rel > 0, nb, 0)
    n = jnp.abs(rel)
    nf = jnp.maximum(n, 1).astype(jnp.float32)
    large = max_exact + (jnp.log(nf / max_exact) / math.log(REL_MAX_DIST / max_exact)
                         * (nb - max_exact)).astype(jnp.int32)
    large = jnp.minimum(large, nb - 1)
    return base + jnp.where(n < max_exact, n, large)


def windowed_attention(q, k, v, sink, rel_bias):
    b, s = q.shape[0], q.shape[1]
    nblk = s // BLOCK
    grp = ATTN_HEADS // ATTN_KV_HEADS
    qb = q.reshape(b, nblk, BLOCK, ATTN_KV_HEADS, grp, HEAD_DIM)

    def band(t):
        tp = jnp.pad(t, ((0, 0), (BLOCK, BLOCK), (0, 0), (0, 0)))
        tp = tp.reshape(b, nblk + 2, BLOCK, ATTN_KV_HEADS, HEAD_DIM)
        return jnp.concatenate([tp[:, :-2], tp[:, 1:-1], tp[:, 2:]], axis=2)

    kb, vb = band(k), band(v)
    scores = jnp.einsum('bnqkgd,bnckd->bnkgqc', qb, kb).astype(jnp.float32) * (HEAD_DIM ** -0.5)

    q_off = jnp.arange(BLOCK, dtype=jnp.int32)
    c_off = jnp.arange(3 * BLOCK, dtype=jnp.int32)
    rel = c_off[None, :] - BLOCK - q_off[:, None]
    bias = rel_bias.astype(jnp.float32)[t5_bucket(rel)]
    bias = jnp.transpose(bias, (2, 0, 1)).reshape(ATTN_KV_HEADS, grp, BLOCK, 3 * BLOCK)
    key_pos = jnp.arange(nblk, dtype=jnp.int32)[:, None] * BLOCK - BLOCK + c_off[None, :]
    in_range = (key_pos >= 0) & (key_pos < s)
    mask = in_range[:, None, :] & (jnp.abs(rel) <= WINDOW)[None]

    scores = jnp.where(mask[None, :, None, None], scores + bias[None, None], NEG_INF)
    sink_b = sink.astype(jnp.float32).reshape(ATTN_KV_HEADS, grp)[None, None, :, :, None, None]
    m = jnp.maximum(jnp.max(scores, axis=-1, keepdims=True), sink_b)
    p = jnp.exp(scores - m)
    denom = jnp.sum(p, axis=-1, keepdims=True) + jnp.exp(sink_b - m)
    probs = (p / denom).astype(vb.dtype)
    out = jnp.einsum('bnkgqc,bnckd->bnqkgd', probs, vb)
    return out.reshape(b, s, ATTN_WIDTH)


def s5_scan(u, a_re, a_im, log_dt, b_re, b_im, c_re, c_im):
    dt = jnp.exp(log_dt)[:, None]
    mag = jnp.exp(a_re * dt)
    lb_re = mag * jnp.cos(a_im * dt)
    lb_im = mag * jnp.sin(a_im * dt)
    den = a_re * a_re + a_im * a_im
    nr = lb_re - 1.0
    coef_re = (nr * a_re + lb_im * a_im) / den
    coef_im = (lb_im * a_re - nr * a_im) / den
    bb_re = coef_re[..., None] * b_re - coef_im[..., None] * b_im
    bb_im = coef_re[..., None] * b_im + coef_im[..., None] * b_re
    x_re = jnp.einsum('bsgh,gph->bsgp', u, bb_re)
    x_im = jnp.einsum('bsgh,gph->bsgp', u, bb_im)
    s = u.shape[1]
    la_re = jnp.broadcast_to(lb_re[None, None], (1, s) + lb_re.shape)
    la_im = jnp.broadcast_to(lb_im[None, None], (1, s) + lb_im.shape)

    def combine(e1, e2):
        a1r, a1i, b1r, b1i = e1
        a2r, a2i, b2r, b2i = e2
        return (a2r * a1r - a2i * a1i,
                a2r * a1i + a2i * a1r,
                a2r * b1r - a2i * b1i + b2r,
                a2r * b1i + a2i * b1r + b2i)

    _, _, h_re, h_im = lax.associative_scan(combine, (la_re, la_im, x_re, x_im), axis=1)
    return jnp.einsum('bsgp,ghp->bsgh', h_re, c_re) - jnp.einsum('bsgp,ghp->bsgh', h_im, c_im)


def s5_mixer(u, a_re, a_im, log_dt, b_re, b_im, c_re, c_im, d_skip, glu_w, glu_b):
    b, s, _ = u.shape
    f = lambda t: t.astype(jnp.float32)
    uf = f(u).reshape(b, s, SSM_GROUPS, SSM_GROUP)
    y_fwd = s5_scan(uf, f(a_re[0]), f(a_im[0]), f(log_dt[0]), f(b_re[0]), f(b_im[0]), f(c_re[0]), f(c_im[0]))
    y_bwd = s5_scan(uf[:, ::-1], f(a_re[1]), f(a_im[1]), f(log_dt[1]), f(b_re[1]), f(b_im[1]),
                    f(c_re[1]), f(c_im[1]))[:, ::-1]
    y = (y_fwd + y_bwd).reshape(b, s, SSM_WIDTH) + f(d_skip) * f(u)
    g = jax.nn.gelu(y)
    out = g * jax.nn.sigmoid(g @ f(glu_w) + f(glu_b))
    return out.astype(u.dtype)


def gmlp_mixer(h, w_in, ln_g, ln_b, w_s, b_s):
    b, s, _ = h.shape
    z = jax.nn.gelu(h @ w_in)
    u, v = jnp.split(z, 2, axis=-1)
    vf = v.astype(jnp.float32)
    mu = jnp.mean(vf, axis=-1, keepdims=True)
    var = jnp.mean(jnp.square(vf - mu), axis=-1, keepdims=True)
    vn = ((vf - mu) * lax.rsqrt(var + EPS) * ln_g.astype(jnp.float32) + ln_b.astype(jnp.float32)).astype(v.dtype)
    vc = vn.reshape(b, s // GMLP_CHUNK, GMLP_CHUNK, GMLP_HEADS, GMLP_HEAD_DIM)
    mixed = jnp.einsum('hts,bnshd->bnthd', w_s, vc) + b_s.T[None, None, :, :, None]
    return u * mixed.reshape(b, s, GMLP_WIDTH)


def expert_choice_ffn(h, router, w1, w3, w2):
    b, s, _ = h.shape
    cap = CAPACITY_FACTOR * s // N_EXPERTS
    aff = jax.nn.softmax((h @ router).astype(jnp.float32), axis=-1)
    gate, idx = lax.top_k(jnp.swapaxes(aff, 1, 2), cap)
    bidx = jnp.arange(b)[:, None, None]
    xs = h[bidx, idx]
    hid = jax.nn.silu(jnp.einsum('becd,edf->becf', xs, w1)) * jnp.einsum('becd,edf->becf', xs, w3)
    ys = jnp.einsum('becf,efd->becd', hid, w2) * gate[..., None].astype(h.dtype)
    return jnp.zeros_like(h).at[bidx, idx].add(ys)


def setup_inputs(seed: int = 0) -> dict:
    key = jax.random.key(seed)
    ks = jax.random.split(key, 32)
    f32 = jnp.float32

    def nrm(k, shape, scale):
        return jax.random.normal(k, shape, f32) * scale

    dir_shape = (N_EVEN, 2, SSM_GROUPS)
    n_idx = jnp.arange(SSM_STATE, dtype=f32)
    return {
        'x': nrm(ks[0], (BATCH, SEQ, D_MODEL), 1.0),
        'rel_bias': nrm(ks[1], (REL_BUCKETS, ATTN_HEADS), 0.5),
        'mix_norm': 1.0 + nrm(ks[2], (DEPTH, D_MODEL), 0.02),
        'ffn_norm': 1.0 + nrm(ks[3], (DEPTH, D_MODEL), 0.02),
        'final_norm': 1.0 + nrm(ks[4], (D_MODEL,), 0.02),
        'even_w_in': nrm(ks[5], (N_EVEN, D_MODEL, EVEN_IN), D_MODEL ** -0.5),
        'attn_sink': nrm(ks[6], (N_EVEN, ATTN_HEADS), 1.0),
        'ssm_a_re': -0.5 + nrm(ks[7], dir_shape + (SSM_STATE,), 0.01),
        'ssm_a_im': math.pi * n_idx + nrm(ks[8], dir_shape + (SSM_STATE,), 0.01),
        'ssm_log_dt': jax.random.uniform(ks[9], dir_shape, f32, math.log(1e-3), math.log(1e-1)),
        'ssm_b_re': nrm(ks[10], dir_shape + (SSM_STATE, SSM_GROUP), (2 * SSM_GROUP) ** -0.5),
        'ssm_b_im': nrm(ks[11], dir_shape + (SSM_STATE, SSM_GROUP), (2 * SSM_GROUP) ** -0.5),
        'ssm_c_re': nrm(ks[12], dir_shape + (SSM_GROUP, SSM_STATE), (2 * SSM_STATE) ** -0.5),
        'ssm_c_im': nrm(ks[13], dir_shape + (SSM_GROUP, SSM_STATE), (2 * SSM_STATE) ** -0.5),
        'ssm_d': nrm(ks[14], (N_EVEN, SSM_WIDTH), 1.0),
        'glu_w': nrm(ks[15], (N_EVEN, SSM_WIDTH, SSM_WIDTH), SSM_WIDTH ** -0.5),
        'glu_b': nrm(ks[16], (N_EVEN, SSM_WIDTH), 0.02),
        'even_w_out': nrm(ks[17], (N_EVEN, ATTN_WIDTH + SSM_WIDTH, D_MODEL), (ATTN_WIDTH + SSM_WIDTH) ** -0.5),
        'odd_w_in': nrm(ks[18], (N_ODD, D_MODEL, 2 * GMLP_WIDTH), D_MODEL ** -0.5),
        'sgu_ln_g': 1.0 + nrm(ks[19], (N_ODD, GMLP_WIDTH), 0.02),
        'sgu_ln_b': nrm(ks[20], (N_ODD, GMLP_WIDTH), 0.02),
        'sgu_w': nrm(ks[21], (N_ODD, GMLP_HEADS, GMLP_CHUNK, GMLP_CHUNK), GMLP_CHUNK ** -0.5),
        'sgu_b': 1.0 + nrm(ks[22], (N_ODD, GMLP_HEADS, GMLP_CHUNK), 0.02),
        'odd_w_out': nrm(ks[23], (N_ODD, GMLP_WIDTH, D_MODEL), GMLP_WIDTH ** -0.5),
        'router': nrm(ks[24], (DEPTH, D_MODEL, N_EXPERTS), D_MODEL ** -0.5),
        'moe_w1': nrm(ks[25], (DEPTH, N_EXPERTS, D_MODEL, EXPERT_FF), D_MODEL ** -0.5),
        'moe_w3': nrm(ks[26], (DEPTH, N_EXPERTS, D_MODEL, EXPERT_FF), D_MODEL ** -0.5),
        'moe_w2': nrm(ks[27], (DEPTH, N_EXPERTS, EXPERT_FF, D_MODEL), EXPERT_FF ** -0.5),
    }


def reference(x, rel_bias, mix_norm, ffn_norm, final_norm, even_w_in, attn_sink,
              ssm_a_re, ssm_a_im, ssm_log_dt, ssm_b_re, ssm_b_im, ssm_c_re, ssm_c_im,
              ssm_d, glu_w, glu_b, even_w_out, odd_w_in, sgu_ln_g, sgu_ln_b, sgu_w, sgu_b,
              odd_w_out, router, moe_w1, moe_w3, moe_w2):
    b, s, _ = x.shape
    for layer in range(DEPTH):
        h = rms_norm(x, mix_norm[layer])
        i = layer // 2
        if layer % 2 == 0:
            proj = h @ even_w_in[i]
            q, k, v, u = jnp.split(proj, [ATTN_WIDTH, ATTN_WIDTH + KV_WIDTH, ATTN_WIDTH + 2 * KV_WIDTH], axis=-1)
            attn = windowed_attention(q.reshape(b, s, ATTN_HEADS, HEAD_DIM),
                                      k.reshape(b, s, ATTN_KV_HEADS, HEAD_DIM),
                                      v.reshape(b, s, ATTN_KV_HEADS, HEAD_DIM),
                                      attn_sink[i], rel_bias)
            ssm = s5_mixer(u, ssm_a_re[i], ssm_a_im[i], ssm_log_dt[i], ssm_b_re[i], ssm_b_im[i],
                           ssm_c_re[i], ssm_c_im[i], ssm_d[i], glu_w[i], glu_b[i])
            x = x + jnp.concatenate([attn, ssm], axis=-1) @ even_w_out[i]
        else:
            g = gmlp_mixer(h, odd_w_in[i], sgu_ln_g[i], sgu_ln_b[i], sgu_w[i], sgu_b[i])
            x = x + g @ odd_w_out[i]
        h = rms_norm(x, ffn_norm[layer])
        x = x + expert_choice_ffn(h, router[layer], moe_w1[layer], moe_w3[layer], moe_w2[layer])
    return rms_norm(x, final_norm)
```

```python
import functools
import math

import jax
import jax.numpy as jnp
from jax import lax
from jax.experimental import pallas as pl
from jax.experimental.pallas import tpu as pltpu

F32 = jnp.float32
BF16 = jnp.bfloat16
I32 = jnp.int32
U32 = jnp.uint32

EPS = 1e-6
NEG_INF = -1e30

LANES = 128
SUBLANES = 8
VMEM_LIMIT = 56 << 20

ATTN_BLOCK = 128
HEAD_DIM = 128
KV_GROUP = 4
REL_BUCKETS = 32
REL_MAX_DIST = 128
SSM_GROUP = 16
SSM_STATE = 64
S5_CHUNK = 32
GMLP_CHUNK = 128
N_EXPERTS = 16
CAPACITY_FACTOR = 2


def _params(sem, vmem=VMEM_LIMIT):
    return pltpu.CompilerParams(dimension_semantics=sem, vmem_limit_bytes=vmem)


def _tile(n, want):
    t = min(n, want)
    while n % t:
        t //= 2
    return t


def _rms(x, g):
    ms = jnp.mean(x * x, axis=-1, keepdims=True)
    return x * lax.rsqrt(ms + EPS) * g


def _gelu(x):
    c = math.sqrt(2.0 / math.pi)
    return x * (0.5 * (1.0 + jnp.tanh(c * (x + 0.044715 * (x * x * x)))))


def _norm_mm_kernel(x_ref, g_ref, w_ref, o_ref, hn_ref, *, act):
    @pl.when(pl.program_id(1) == 0)
    def _():
        hn_ref[...] = _rms(x_ref[...], g_ref[...]).astype(BF16)

    y = jnp.dot(hn_ref[...], w_ref[...], preferred_element_type=F32)
    if act == "gelu":
        y = _gelu(y)
    o_ref[...] = y.astype(o_ref.dtype)


def norm_matmul(x, g, w_bf16, *, act=None, out_dtype=F32, tm=1024, tn=512):
    m, k = x.shape
    n = w_bf16.shape[1]
    tm, tn = _tile(m, tm), _tile(n, tn)
    return pl.pallas_call(
        functools.partial(_norm_mm_kernel, act=act),
        grid=(m // tm, n // tn),
        in_specs=[
            pl.BlockSpec((tm, k), lambda i, j: (i, 0)),
            pl.BlockSpec((1, k), lambda i, j: (0, 0)),
            pl.BlockSpec((k, tn), lambda i, j: (0, j)),
        ],
        out_specs=pl.BlockSpec((tm, tn), lambda i, j: (i, j)),
        out_shape=jax.ShapeDtypeStruct((m, n), out_dtype),
        scratch_shapes=[pltpu.VMEM((tm, k), BF16)],
        compiler_params=_params(("parallel", "arbitrary")),
        name="norm_matmul",
    )(x, g.reshape(1, k), w_bf16)


def _mm_res_kernel(*refs, n_lhs):
    lhs = refs[:n_lhs]
    ws = refs[n_lhs:2 * n_lhs]
    res_ref, o_ref = refs[2 * n_lhs], refs[2 * n_lhs + 1]
    acc = res_ref[...]
    for l_ref, w_ref in zip(lhs, ws):
        acc = acc + jnp.dot(l_ref[...], w_ref[...], preferred_element_type=F32)
    o_ref[...] = acc


def matmul_residual(lhs_list, w_bf16, res, *, tm=1024, tn=512):
    m, n = res.shape
    kk = lhs_list[0].shape[1]
    n_lhs = len(lhs_list)
    assert all(l.shape == (m, kk) for l in lhs_list) and w_bf16.shape == (n_lhs * kk, n)
    tm, tn = _tile(m, tm), _tile(n, tn)
    in_specs = [pl.BlockSpec((tm, kk), lambda i, j: (i, 0)) for _ in lhs_list]
    in_specs += [pl.BlockSpec((kk, tn), functools.partial(lambda i, j, r: (r, j), r=r))
                 for r in range(n_lhs)]
    in_specs += [pl.BlockSpec((tm, tn), lambda i, j: (i, j))]
    return pl.pallas_call(
        functools.partial(_mm_res_kernel, n_lhs=n_lhs),
        grid=(m // tm, n // tn),
        in_specs=in_specs,
        out_specs=pl.BlockSpec((tm, tn), lambda i, j: (i, j)),
        out_shape=jax.ShapeDtypeStruct((m, n), F32),
        compiler_params=_params(("parallel", "parallel")),
        name="matmul_residual",
    )(*lhs_list, *([w_bf16] * n_lhs), res)


def _t5_bucket(rel):
    nb = REL_BUCKETS // 2
    max_exact = nb // 2
    base = jnp.where(rel > 0, nb, 0)
    n = jnp.abs(rel)
    nf = jnp.maximum(n, 1).astype(F32)
    large = max_exact + (jnp.log(nf / max_exact) / math.log(REL_MAX_DIST / max_exact)
                         * (nb - max_exact)).astype(I32)
    large = jnp.minimum(large, nb - 1)
    return base + jnp.where(n < max_exact, n, large)


def _bias_kernel(rb_ref, bucket_ref, o_ref, *, n_heads):
    bkt = bucket_ref[...]
    for h in range(n_heads):
        acc = jnp.zeros(bkt.shape, F32)
        for k in range(REL_BUCKETS):
            acc = jnp.where(bkt == k, rb_ref[k, h], acc)
        o_ref[h * ATTN_BLOCK:(h + 1) * ATTN_BLOCK, :] = acc


def attention_bias(rel_bias):
    n_heads = rel_bias.shape[1]
    q_off = jnp.arange(ATTN_BLOCK, dtype=I32)
    c_off = jnp.arange(3 * ATTN_BLOCK, dtype=I32)
    bucket = _t5_bucket(c_off[None, :] - ATTN_BLOCK - q_off[:, None])
    return pl.pallas_call(
        functools.partial(_bias_kernel, n_heads=n_heads),
        in_specs=[pl.BlockSpec(memory_space=pltpu.SMEM),
                  pl.BlockSpec(memory_space=pltpu.VMEM)],
        out_specs=pl.BlockSpec(memory_space=pltpu.VMEM),
        out_shape=jax.ShapeDtypeStruct((n_heads * ATTN_BLOCK, 3 * ATTN_BLOCK), F32),
        name="attention_bias",
    )(rel_bias.astype(F32), bucket)


def _attn_kernel(sink_ref, q_ref, kp_ref, kc_ref, kn_ref, vp_ref, vc_ref, vn_ref, bias_ref,
                 o_ref, *, seq, n_kv):
    blk = ATTN_BLOCK
    n = pl.program_id(1)
    row = lax.broadcasted_iota(I32, (blk, 3 * blk), 0)
    col = lax.broadcasted_iota(I32, (blk, 3 * blk), 1)
    rel = col - blk - row
    kpos = n * blk - blk + col
    valid1 = (jnp.abs(rel) <= blk) & (kpos >= 0) & (kpos < seq)
    valid = jnp.concatenate([valid1] * KV_GROUP, axis=0)
    scale = HEAD_DIM ** -0.5
    q = q_ref[0]
    for kh in range(n_kv):
        ksl = slice(kh * HEAD_DIM, (kh + 1) * HEAD_DIM)
        kband = jnp.concatenate([kp_ref[0, :, ksl], kc_ref[0, :, ksl], kn_ref[0, :, ksl]],
                                axis=0).astype(BF16)
        vband = jnp.concatenate([vp_ref[0, :, ksl], vc_ref[0, :, ksl], vn_ref[0, :, ksl]],
                                axis=0).astype(BF16)
        heads = [kh * KV_GROUP + g for g in range(KV_GROUP)]
        qs = jnp.concatenate([q[:, h * HEAD_DIM:(h + 1) * HEAD_DIM] for h in heads],
                             axis=0).astype(BF16)
        s = lax.dot_general(qs, kband, (((1,), (1,)), ((), ())),
                            preferred_element_type=F32) * scale
        bias = bias_ref[kh * KV_GROUP * blk:(kh + 1) * KV_GROUP * blk, :]
        s = jnp.where(valid, s + bias, NEG_INF)
        sink = jnp.concatenate([jnp.full((blk, 1), sink_ref[h], F32) for h in heads], axis=0)
        m = jnp.maximum(jnp.max(s, axis=-1, keepdims=True), sink)
        p = jnp.exp(s - m)
        denom = jnp.sum(p, axis=-1, keepdims=True) + jnp.exp(sink - m)
        probs = (p / denom).astype(BF16)
        o = jnp.dot(probs, vband, preferred_element_type=F32)
        for g, h in enumerate(heads):
            o_ref[0, :, h * HEAD_DIM:(h + 1) * HEAD_DIM] = o[g * blk:(g + 1) * blk].astype(o_ref.dtype)


def windowed_attention(proj, sink, bias, *, attn_w, kv_w):
    b, s, _ = proj.shape
    blk = ATTN_BLOCK
    nblk = s // blk
    n_kv = kv_w // HEAD_DIM
    kcol = attn_w // kv_w
    assert attn_w % kv_w == 0 and n_kv * KV_GROUP * HEAD_DIM == attn_w

    def kv_spec(col, d):
        return pl.BlockSpec(
            (1, blk, kv_w),
            lambda bi, n: (bi, jnp.clip(n + d, 0, nblk - 1), col))

    return pl.pallas_call(
        functools.partial(_attn_kernel, seq=s, n_kv=n_kv),
        grid=(b, nblk),
        in_specs=[
            pl.BlockSpec(memory_space=pltpu.SMEM),
            pl.BlockSpec((1, blk, attn_w), lambda bi, n: (bi, n, 0)),
            kv_spec(kcol, -1), kv_spec(kcol, 0), kv_spec(kcol, 1),
            kv_spec(kcol + 1, -1), kv_spec(kcol + 1, 0), kv_spec(kcol + 1, 1),
            pl.BlockSpec(bias.shape, lambda bi, n: (0, 0)),
        ],
        out_specs=pl.BlockSpec((1, blk, attn_w), lambda bi, n: (bi, n, 0)),
        out_shape=jax.ShapeDtypeStruct((b, s, attn_w), BF16),
        compiler_params=_params(("parallel", "parallel")),
        name="windowed_attention",
    )(sink.astype(F32), proj, proj, proj, proj, proj, proj, proj, bias)


def _s5_direction_tables(a_re, a_im, log_dt, b_re, b_im, c_re, c_im, chunk, n_levels):
    hi = lax.Precision.HIGHEST
    g, p = a_re.shape
    h = b_re.shape[-1]
    ell = chunk
    dt = jnp.exp(log_dt)[:, None]
    mag = jnp.exp(a_re * dt)
    lb_re = mag * jnp.cos(a_im * dt)
    lb_im = mag * jnp.sin(a_im * dt)
    den = a_re * a_re + a_im * a_im
    nr = lb_re - 1.0
    coef_re = (nr * a_re + lb_im * a_im) / den
    coef_im = (lb_im * a_re - nr * a_im) / den
    bb_re = coef_re[..., None] * b_re - coef_im[..., None] * b_im
    bb_im = coef_re[..., None] * b_im + coef_im[..., None] * b_re

    def powers(tau):
        t = tau.astype(F32)[:, None, None]
        pm = jnp.exp(a_re * dt * t)
        return pm * jnp.cos(a_im * dt * t), pm * jnp.sin(a_im * dt * t)

    pw_re, pw_im = powers(jnp.arange(ell + 1))
    m_re = pw_re[..., None] * bb_re - pw_im[..., None] * bb_im
    m_im = pw_re[..., None] * bb_im + pw_im[..., None] * bb_re
    kern = (jnp.einsum("ghp,tgpk->tghk", c_re, m_re, precision=hi)
            - jnp.einsum("ghp,tgpk->tghk", c_im, m_im, precision=hi))
    ii = jnp.arange(ell)
    lag = ii[:, None] - ii[None, :]
    t_full = jnp.where((lag >= 0)[:, :, None, None, None], kern[jnp.clip(lag, 0, ell)], 0.0)
    t_tab = t_full.transpose(2, 1, 4, 0, 3).reshape(g, ell * h, ell * h)
    f_re = m_re[ell - 1 - ii].transpose(1, 0, 3, 2).reshape(g, ell * h, p)
    f_im = m_im[ell - 1 - ii].transpose(1, 0, 3, 2).reshape(g, ell * h, p)
    ar = c_re[None] * pw_re[1:, :, None, :] - c_im[None] * pw_im[1:, :, None, :]
    ai = c_re[None] * pw_im[1:, :, None, :] + c_im[None] * pw_re[1:, :, None, :]
    e_re = ar.transpose(1, 3, 0, 2).reshape(g, p, ell * h)
    e_im = (-ai).transpose(1, 3, 0, 2).reshape(g, p, ell * h)
    lv_re, lv_im = powers(ell * (2 ** jnp.arange(n_levels)))
    return t_tab, f_re, f_im, e_re, e_im, lv_re, lv_im


def _s5_tables(a_re, a_im, log_dt, b_re, b_im, c_re, c_im, chunk, n_levels):
    g, p = a_re.shape[1:]
    h = b_re.shape[-1]
    lh = chunk * h
    fw = _s5_direction_tables(a_re[0], a_im[0], log_dt[0], b_re[0], b_im[0], c_re[0], c_im[0],
                              chunk, n_levels)
    bw = _s5_direction_tables(a_re[1], a_im[1], log_dt[1], b_re[1], b_im[1], c_re[1], c_im[1],
                              chunk, n_levels)

    def flip_rows(t):
        return t.reshape(g, chunk, h, -1)[:, ::-1].reshape(g, lh, -1)

    def flip_cols(t):
        return t.reshape(g, -1, chunk, h)[:, :, ::-1].reshape(g, -1, lh)

    t_tab = fw[0] + flip_cols(flip_rows(bw[0]))
    pad_p = lambda t: jnp.pad(t, ((0, 0), (0, 0), (0, LANES - p)))
    f_tab = jnp.concatenate([pad_p(fw[1]), pad_p(fw[2]),
                             pad_p(flip_rows(bw[1])), pad_p(flip_rows(bw[2]))], axis=-1)
    pad_r = lambda t: jnp.pad(t, ((0, 0), (0, LANES - p), (0, 0)))
    e_tab = jnp.concatenate([pad_r(fw[3]), pad_r(fw[4]),
                             pad_r(flip_cols(bw[3])), pad_r(flip_cols(bw[4]))], axis=1)
    te_tab = jnp.concatenate([t_tab, e_tab], axis=1)
    pad_l = lambda t: jnp.pad(t, ((0, 0), (0, 0), (0, LANES - p)))
    pa = jnp.concatenate([pad_l(fw[5]), pad_l(fw[5]), pad_l(bw[5]), pad_l(bw[5])], axis=-1)
    pb = jnp.concatenate([-pad_l(fw[6]), pad_l(fw[6]), -pad_l(bw[6]), pad_l(bw[6])], axis=-1)
    lev = SUBLANES * pl.cdiv(n_levels, SUBLANES)
    pad_lv = lambda t: jnp.pad(t.transpose(1, 0, 2), ((0, 0), (0, lev - n_levels), (0, 0)))
    return f_tab.astype(BF16), te_tab.astype(BF16), pad_lv(pa), pad_lv(pb)


def _shift_rows(z, r):
    rows = z.shape[0]
    rolled = pltpu.roll(z, r % rows, 0)
    ridx = lax.broadcasted_iota(I32, z.shape, 0)
    keep = (ridx >= r) if r > 0 else (ridx < rows + r)
    return jnp.where(keep, rolled, 0.0)


def _s5_kernel(u_ref, f_ref, te_ref, pa_ref, pb_ref, o_ref, *, rows_per_chunk, n_levels, lh):
    u = u_ref[0]
    z = jnp.dot(u, f_ref[0], preferred_element_type=F32)
    half = 2 * LANES

    def swap(x):
        return jnp.concatenate([x[:, LANES:], x[:, :LANES]], axis=1)

    zf, zb = z[:, :half], z[:, half:]
    for k in range(n_levels):
        r = rows_per_chunk * (2 ** k)
        pa = pa_ref[0, k:k + 1, :]
        pb = pb_ref[0, k:k + 1, :]
        sf = _shift_rows(zf, r)
        sb = _shift_rows(zb, -r)
        zf = zf + pa[:, :half] * sf + pb[:, :half] * swap(sf)
        zb = zb + pa[:, half:] * sb + pb[:, half:] * swap(sb)
    hin = jnp.concatenate([_shift_rows(zf, rows_per_chunk), _shift_rows(zb, -rows_per_chunk)],
                          axis=1).astype(BF16)
    y = jnp.dot(u, te_ref[0, :lh, :], preferred_element_type=F32)
    y = y + jnp.dot(hin, te_ref[0, lh:, :], preferred_element_type=F32)
    o_ref[0] = y


def s5_scan(u, tables, *, chunk):
    f_tab, te_tab, pa, pb = tables
    b, s, w = u.shape
    g = f_tab.shape[0]
    h = w // g
    n = s // chunk
    lh = chunk * h
    n_levels = max(1, (n - 1).bit_length())
    rows = n * b
    ut = u.astype(BF16).reshape(b, n, chunk, g, h).transpose(3, 1, 0, 2, 4).reshape(g, rows, lh)
    y = pl.pallas_call(
        functools.partial(_s5_kernel, rows_per_chunk=b, n_levels=n_levels, lh=lh),
        grid=(g,),
        in_specs=[
            pl.BlockSpec((1, rows, lh), lambda i: (i, 0, 0)),
            pl.BlockSpec((1, lh, 4 * LANES), lambda i: (i, 0, 0)),
            pl.BlockSpec((1, lh + 4 * LANES, lh), lambda i: (i, 0, 0)),
            pl.BlockSpec((1,) + pa.shape[1:], lambda i: (i, 0, 0)),
            pl.BlockSpec((1,) + pb.shape[1:], lambda i: (i, 0, 0)),
        ],
        out_specs=pl.BlockSpec((1, rows, lh), lambda i: (i, 0, 0)),
        out_shape=jax.ShapeDtypeStruct((g, rows, lh), F32),
        compiler_params=_params(("parallel",)),
        name="s5_scan",
    )(ut, f_tab, te_tab, pa, pb)
    return y.reshape(g, n, b, chunk, h).transpose(2, 1, 3, 0, 4).reshape(b, s, w)


def _glu_kernel(y_ref, u_ref, d_ref, w_ref, b_ref, o_ref):
    y = y_ref[...] + d_ref[...] * u_ref[...]
    g = _gelu(y)
    z = jnp.dot(g.astype(BF16), w_ref[...], preferred_element_type=F32) + b_ref[...]
    o_ref[...] = (g * jax.nn.sigmoid(z)).astype(o_ref.dtype)


def s5_glu(y, u, d_skip, glu_w_bf16, glu_b, *, tm=512):
    m, w = y.shape
    tm = _tile(m, tm)
    return pl.pallas_call(
        _glu_kernel,
        grid=(m // tm,),
        in_specs=[
            pl.BlockSpec((tm, w), lambda i: (i, 0)),
            pl.BlockSpec((tm, w), lambda i: (i, 0)),
            pl.BlockSpec((1, w), lambda i: (0, 0)),
            pl.BlockSpec((w, w), lambda i: (0, 0)),
            pl.BlockSpec((1, w), lambda i: (0, 0)),
        ],
        out_specs=pl.BlockSpec((tm, w), lambda i: (i, 0)),
        out_shape=jax.ShapeDtypeStruct((m, w), BF16),
        compiler_params=_params(("parallel",)),
        name="s5_glu",
    )(y, u, d_skip.reshape(1, w), glu_w_bf16, glu_b.reshape(1, w))


def _sgu_kernel(z_ref, g_ref, b_ref, ws_ref, bs_ref, o_ref, *, width, n_heads):
    hd = width // n_heads
    z = z_ref[...]
    u, v = z[:, :width], z[:, width:]
    mu = jnp.mean(v, axis=-1, keepdims=True)
    var = jnp.mean(jnp.square(v - mu), axis=-1, keepdims=True)
    vn = ((v - mu) * lax.rsqrt(var + EPS) * g_ref[...] + b_ref[...]).astype(BF16)
    for h in range(n_heads):
        sl = slice(h * hd, (h + 1) * hd)
        mixed = jnp.dot(ws_ref[h], vn[:, sl], preferred_element_type=F32) + bs_ref[:, h:h + 1]
        o_ref[:, sl] = (u[:, sl] * mixed).astype(o_ref.dtype)


def gmlp_sgu(z, ln_g, ln_b, w_s_bf16, b_s):
    m, w2 = z.shape
    width = w2 // 2
    n_heads = w_s_bf16.shape[0]
    ck = GMLP_CHUNK
    return pl.pallas_call(
        functools.partial(_sgu_kernel, width=width, n_heads=n_heads),
        grid=(m // ck,),
        in_specs=[
            pl.BlockSpec((ck, w2), lambda i: (i, 0)),
            pl.BlockSpec((1, width), lambda i: (0, 0)),
            pl.BlockSpec((1, width), lambda i: (0, 0)),
            pl.BlockSpec((n_heads, ck, ck), lambda i: (0, 0, 0)),
            pl.BlockSpec((ck, n_heads), lambda i: (0, 0)),
        ],
        out_specs=pl.BlockSpec((ck, width), lambda i: (i, 0)),
        out_shape=jax.ShapeDtypeStruct((m, width), BF16),
        compiler_params=_params(("parallel",)),
        name="gmlp_sgu",
    )(z, ln_g.reshape(1, width), ln_b.reshape(1, width), w_s_bf16, b_s.T.astype(F32))


def _slab_pitch(half):
    return SUBLANES * pl.cdiv(half // LANES, SUBLANES)


def _router_kernel(x_ref, g_ref, rt_ref, aff_ref, slab_ref, *, tm, half, pitch):
    h = _rms(x_ref[0], g_ref[...])
    logits = lax.dot_general(rt_ref[...], h, (((1,), (1,)), ((), ())),
                             precision=lax.Precision.HIGHEST,
                             preferred_element_type=F32)
    mx = jnp.max(logits, axis=0, keepdims=True)
    ex = jnp.exp(logits - mx)
    aff_ref[0] = ex / jnp.sum(ex, axis=0, keepdims=True)
    lo = lax.bitcast_convert_type(h[:, :half].astype(BF16).astype(F32), U32) >> 16
    hi = lax.bitcast_convert_type(h[:, half:].astype(BF16).astype(F32), U32) & jnp.uint32(0xFFFF0000)
    word = lo | hi
    for s in range(half // LANES):
        slab_ref[pl.ds(s, tm, stride=pitch), :] = word[:, s * LANES:(s + 1) * LANES]
    if pitch > half // LANES:
        for s in range(half // LANES, pitch):
            slab_ref[pl.ds(s, tm, stride=pitch), :] = jnp.zeros((tm, LANES), U32)


def moe_router(x, g, router, *, tm=512):
    b, s, d = x.shape
    e = router.shape[1]
    half = d // 2
    pitch = _slab_pitch(half)
    tm = _tile(s, tm)
    nt = s // tm
    return pl.pallas_call(
        functools.partial(_router_kernel, tm=tm, half=half, pitch=pitch),
        grid=(b, nt),
        in_specs=[
            pl.BlockSpec((1, tm, d), lambda bi, j: (bi, j, 0)),
            pl.BlockSpec((1, d), lambda bi, j: (0, 0)),
            pl.BlockSpec((e, d), lambda bi, j: (0, 0)),
        ],
        out_specs=[
            pl.BlockSpec((1, e, tm), lambda bi, j: (bi, 0, j)),
            pl.BlockSpec((tm * pitch, LANES), lambda bi, j: (bi * nt + j, 0)),
        ],
        out_shape=[jax.ShapeDtypeStruct((b, e, s), F32),
                   jax.ShapeDtypeStruct((b * s * pitch, LANES), U32)],
        compiler_params=_params(("parallel", "parallel")),
        name="moe_router",
    )(x, g.reshape(1, d), router.T.astype(F32))


def _lane_cumsum(m_bf16, tri):
    e, s = m_bf16.shape
    off = jnp.zeros((e, 1), F32)
    out = []
    for j in range(s // LANES):
        c = jnp.dot(m_bf16[:, j * LANES:(j + 1) * LANES], tri, preferred_element_type=F32) + off
        out.append(c)
        off = c[:, LANES - 1:LANES]
    return jnp.concatenate(out, axis=1)


def _topk_kernel(aff_ref, idx_ref, gate_ref, *, cap, kchunk):
    aff = aff_ref[0]
    e, s = aff.shape
    bits = lax.bitcast_convert_type(aff, I32)

    def count_ge(t):
        return jnp.sum((bits >= t).astype(F32), axis=1, keepdims=True)

    def body(_, carry):
        lo, hi = carry
        mid = lo + ((hi - lo) >> 1)
        ok = count_ge(mid) >= cap
        return jnp.where(ok, mid, lo), jnp.where(ok, hi, mid)

    lo0 = jnp.zeros((e, 1), I32)
    hi0 = jnp.full((e, 1), 0x7F800000, I32)
    thr, _ = lax.fori_loop(0, 32, body, (lo0, hi0))
    gt = bits > thr
    eq = bits == thr
    need = cap - jnp.sum(gt.astype(F32), axis=1, keepdims=True)
    tri = (lax.broadcasted_iota(I32, (LANES, LANES), 0)
           <= lax.broadcasted_iota(I32, (LANES, LANES), 1)).astype(BF16)
    eq_rank = _lane_cumsum(jnp.where(eq, 1.0, 0.0).astype(BF16), tri)
    sel = gt | (eq & (eq_rank <= need))
    pos = _lane_cumsum(jnp.where(sel, 1.0, 0.0).astype(BF16), tri)
    pos = jnp.where(sel, pos, 0.0)

    tok = lax.broadcasted_iota(I32, (1, s), 1)
    t_hi = (tok >> 6).astype(F32)
    t_lo = (tok & 63).astype(F32)
    slot = lax.broadcasted_iota(I32, (cap, kchunk), 0).astype(F32) + 1.0
    lrow = lax.broadcasted_iota(I32, (SUBLANES, s), 0)
    for ei in range(e):
        a = aff[ei:ei + 1, :]
        g1 = a.astype(BF16).astype(F32)
        g2 = (a - g1).astype(BF16).astype(F32)
        g3 = (a - g1) - g2
        lhs = jnp.where(lrow == 0, t_hi, jnp.where(lrow == 1, t_lo, jnp.where(
            lrow == 2, g1, jnp.where(lrow == 3, g2, jnp.where(lrow == 4, g3, 0.0)))))
        lhs = lhs.astype(BF16)
        acc = jnp.zeros((SUBLANES, cap), F32)
        for c0 in range(0, s, kchunk):
            onehot = jnp.where(pos[ei:ei + 1, c0:c0 + kchunk] == slot, 1.0, 0.0).astype(BF16)
            acc = acc + lax.dot_general(lhs[:, c0:c0 + kchunk], onehot, (((1,), (1,)), ((), ())),
                                        preferred_element_type=F32)
        idx_ref[0, ei:ei + 1, :] = (acc[0:1] * 64.0 + acc[1:2]).astype(I32)
        gate_ref[0, ei:ei + 1, :] = acc[2:3] + acc[3:4] + acc[4:5]


def moe_topk(aff, cap):
    b, e, s = aff.shape
    return pl.pallas_call(
        functools.partial(_topk_kernel, cap=cap, kchunk=_tile(s, 1024)),
        grid=(b,),
        in_specs=[pl.BlockSpec((1, e, s), lambda i: (i, 0, 0))],
        out_specs=[pl.BlockSpec((1, e, cap), lambda i: (i, 0, 0)),
                   pl.BlockSpec((1, e, cap), lambda i: (i, 0, 0))],
        out_shape=[jax.ShapeDtypeStruct((b, e, cap), I32),
                   jax.ShapeDtypeStruct((b, e, cap), F32)],
        compiler_params=_params(("parallel",)),
        name="moe_topk",
    )(aff)


def _gather_kernel(idx_ref, slab_ref, o_ref, rows_ref, *, cap, half, pitch, unroll):
    def body(j0, c):
        for u in range(unroll):
            j = j0 * unroll + u
            src = pl.multiple_of(idx_ref[0, 0, j] * pitch, pitch)
            rows_ref[pl.ds(pl.multiple_of(j * pitch, pitch), pitch), :] = slab_ref[pl.ds(src, pitch), :]
        return c

    lax.fori_loop(0, cap // unroll, body, 0)
    for s in range(half // LANES):
        word = rows_ref[pl.ds(s, cap, stride=pitch), :]
        lo = lax.bitcast_convert_type(word << 16, F32)
        hi = lax.bitcast_convert_type(word & jnp.uint32(0xFFFF0000), F32)
        o_ref[0, :, s * LANES:(s + 1) * LANES] = lo.astype(BF16)
        o_ref[0, :, half + s * LANES:half + (s + 1) * LANES] = hi.astype(BF16)


def moe_gather(idx, slabs, *, seq, d):
    b, e, cap = idx.shape
    half = d // 2
    pitch = _slab_pitch(half)
    return pl.pallas_call(
        functools.partial(_gather_kernel, cap=cap, half=half, pitch=pitch, unroll=8),
        grid=(b, e),
        in_specs=[
            pl.BlockSpec((1, 1, cap), lambda bi, ei: (bi * e + ei, 0, 0), memory_space=pltpu.SMEM),
            pl.BlockSpec((seq * pitch, LANES), lambda bi, ei: (bi, 0)),
        ],
        out_specs=pl.BlockSpec((1, cap, d), lambda bi, ei: (ei, bi, 0)),
        out_shape=jax.ShapeDtypeStruct((e, b * cap, d), BF16),
        scratch_shapes=[pltpu.VMEM((cap * pitch, LANES), U32)],
        compiler_params=_params(("parallel", "arbitrary")),
        name="moe_gather",
    )(idx.reshape(b * e, 1, cap), slabs)


def _ffn_kernel(xs_ref, w1_ref, w3_ref, w2_ref, o_ref, hid_ref, *, nt, tf):
    step = pl.program_id(1)

    @pl.when(step < nt)
    def _():
        xs = xs_ref[0]
        a = jnp.dot(xs, w1_ref[0].astype(BF16), preferred_element_type=F32)
        g = jnp.dot(xs, w3_ref[0].astype(BF16), preferred_element_type=F32)
        hid_ref[step] = (a * jax.nn.sigmoid(a) * g).astype(BF16)

    @pl.when(step >= nt)
    def _():
        acc = jnp.zeros(o_ref.shape[1:], F32)
        for k in range(nt):
            acc = acc + jnp.dot(hid_ref[k], w2_ref[0, k * tf:(k + 1) * tf, :].astype(BF16),
                                preferred_element_type=F32)
        o_ref[0] = acc


def moe_ffn(xs, w1, w3, w2, *, tf=256):
    e, m, d = xs.shape
    f = w1.shape[2]
    tf = _tile(f, tf)
    tn = _tile(d, tf)
    nt = f // tf
    nn = d // tn
    return pl.pallas_call(
        functools.partial(_ffn_kernel, nt=nt, tf=tf),
        grid=(e, nt + nn),
        in_specs=[
            pl.BlockSpec((1, m, d), lambda ei, t: (ei, 0, 0)),
            pl.BlockSpec((1, d, tf), lambda ei, t: (ei, 0, jnp.minimum(t, nt - 1))),
            pl.BlockSpec((1, d, tf), lambda ei, t: (ei, 0, jnp.minimum(t, nt - 1))),
            pl.BlockSpec((1, f, tn), lambda ei, t: (ei, 0, jnp.maximum(t - nt, 0))),
        ],
        out_specs=pl.BlockSpec((1, m, tn), lambda ei, t: (ei, 0, jnp.maximum(t - nt, 0))),
        out_shape=jax.ShapeDtypeStruct((e, m, d), F32),
        scratch_shapes=[pltpu.VMEM((nt, m, tf), BF16)],
        compiler_params=_params(("parallel", "arbitrary")),
        name="moe_ffn",
    )(xs, w1, w3, w2)


def _combine_kernel(idx_ref, gate_ref, ys_ref, o_ref, rows_ref, *, cap, half, pitch, unroll):
    ei = pl.program_id(2)

    @pl.when(ei == 0)
    def _():
        o_ref[...] = jnp.zeros(o_ref.shape, F32)

    for s in range(half // LANES):
        rows_ref[pl.ds(s, cap, stride=pitch), :] = ys_ref[0, :, s * LANES:(s + 1) * LANES]

    def body(j0, c):
        dsts, vals = [], []
        for u in range(unroll):
            j = j0 * unroll + u
            dst = pl.multiple_of(idx_ref[0, 0, j] * pitch, pitch)
            src = pl.multiple_of(j * pitch, pitch)
            vals.append(o_ref[0, 0, pl.ds(dst, pitch), :]
                        + gate_ref[0, 0, j] * rows_ref[pl.ds(src, pitch), :])
            dsts.append(dst)
        for dst, val in zip(dsts, vals):
            o_ref[0, 0, pl.ds(dst, pitch), :] = val
        return c

    lax.fori_loop(0, cap // unroll, body, 0)


def moe_combine(idx, gate, ys, *, seq, d):
    b, e, cap = idx.shape
    half = d // 2
    pitch = _slab_pitch(half)
    return pl.pallas_call(
        functools.partial(_combine_kernel, cap=cap, half=half, pitch=pitch, unroll=8),
        grid=(b, 2, e),
        in_specs=[
            pl.BlockSpec((1, 1, cap), lambda bi, dh, ei: (bi * e + ei, 0, 0), memory_space=pltpu.SMEM),
            pl.BlockSpec((1, 1, cap), lambda bi, dh, ei: (bi * e + ei, 0, 0), memory_space=pltpu.SMEM),
            pl.BlockSpec((1, cap, half), lambda bi, dh, ei: (ei, bi, dh)),
        ],
        out_specs=pl.BlockSpec((1, 1, seq * pitch, LANES), lambda bi, dh, ei: (bi, dh, 0, 0)),
        out_shape=jax.ShapeDtypeStruct((b, 2, seq * pitch, LANES), F32),
        scratch_shapes=[pltpu.VMEM((cap * pitch, LANES), F32)],
        compiler_params=_params(("parallel", "parallel", "arbitrary")),
        name="moe_combine",
    )(idx.reshape(b * e, 1, cap), gate.reshape(b * e, 1, cap), ys)


def _unslab_kernel(*refs, tm, half, pitch, final_norm):
    x_ref, s0_ref, s1_ref = refs[:3]
    o_ref = refs[-1]
    parts = []
    for s_ref in (s0_ref, s1_ref):
        for s in range(half // LANES):
            parts.append(s_ref[0, 0, pl.ds(s, tm, stride=pitch), :])
    y = x_ref[0] + jnp.concatenate(parts, axis=1)
    if final_norm:
        y = _rms(y, refs[3][...])
    o_ref[0] = y


def moe_residual(x, moe_slabs, final_g=None, *, tm=256):
    b, s, d = x.shape
    half = d // 2
    pitch = _slab_pitch(half)
    tm = _tile(s, tm)
    final_norm = final_g is not None
    in_specs = [
        pl.BlockSpec((1, tm, d), lambda bi, j: (bi, j, 0)),
        pl.BlockSpec((1, 1, tm * pitch, LANES), lambda bi, j: (bi, 0, j, 0)),
        pl.BlockSpec((1, 1, tm * pitch, LANES), lambda bi, j: (bi, 1, j, 0)),
    ]
    args = [x, moe_slabs, moe_slabs]
    if final_norm:
        in_specs.append(pl.BlockSpec((1, d), lambda bi, j: (0, 0)))
        args.append(final_g.reshape(1, d))
    return pl.pallas_call(
        functools.partial(_unslab_kernel, tm=tm, half=half, pitch=pitch, final_norm=final_norm),
        grid=(b, s // tm),
        in_specs=in_specs,
        out_specs=pl.BlockSpec((1, tm, d), lambda bi, j: (bi, j, 0)),
        out_shape=jax.ShapeDtypeStruct((b, s, d), F32),
        compiler_params=_params(("parallel", "parallel")),
        name="moe_residual",
    )(*args)


def expert_choice_layer(x, norm_g, router, w1, w3, w2, final_g=None):
    b, s, d = x.shape
    cap = CAPACITY_FACTOR * s // N_EXPERTS
    aff, slabs = moe_router(x, norm_g, router)
    idx, gate = moe_topk(aff, cap)
    xs = moe_gather(idx, slabs, seq=s, d=d)
    ys = moe_ffn(xs, w1, w3, w2)
    moe = moe_combine(idx, gate, ys, seq=s, d=d)
    return moe_residual(x, moe, final_g)


def kernel(x, rel_bias, mix_norm, ffn_norm, final_norm, even_w_in, attn_sink, ssm_a_re, ssm_a_im, ssm_log_dt, ssm_b_re, ssm_b_im, ssm_c_re, ssm_c_im, ssm_d, glu_w, glu_b, even_w_out, odd_w_in, sgu_ln_g, sgu_ln_b, sgu_w, sgu_b, odd_w_out, router, moe_w1, moe_w3, moe_w2):
    b, s, d = x.shape
    t = b * s
    depth = mix_norm.shape[0]
    ssm_w = ssm_d.shape[-1]
    attn_w = even_w_out.shape[1] - ssm_w
    kv_w = (even_w_in.shape[-1] - attn_w - ssm_w) // 2
    bias = attention_bias(rel_bias)

    for layer in range(depth):
        i = layer // 2
        x2 = x.reshape(t, d)
        if layer % 2 == 0:
            proj = norm_matmul(x2, mix_norm[layer], even_w_in[i].astype(BF16))
            attn = windowed_attention(proj.reshape(b, s, -1), attn_sink[i], bias,
                                      attn_w=attn_w, kv_w=kv_w)
            u = proj[:, attn_w + 2 * kv_w:]
            n_chunks = s // S5_CHUNK
            tables = _s5_tables(ssm_a_re[i], ssm_a_im[i], ssm_log_dt[i], ssm_b_re[i], ssm_b_im[i],
                                ssm_c_re[i], ssm_c_im[i], S5_CHUNK,
                                max(1, (n_chunks - 1).bit_length()))
            y = s5_scan(u.reshape(b, s, ssm_w), tables, chunk=S5_CHUNK)
            ssm = s5_glu(y.reshape(t, ssm_w), u, ssm_d[i], glu_w[i].astype(BF16), glu_b[i])
            x2 = matmul_residual([attn.reshape(t, attn_w), ssm], even_w_out[i].astype(BF16), x2)
        else:
            z = norm_matmul(x2, mix_norm[layer], odd_w_in[i].astype(BF16), act="gelu")
            gated = gmlp_sgu(z, sgu_ln_g[i], sgu_ln_b[i], sgu_w[i].astype(BF16), sgu_b[i])
            x2 = matmul_residual([gated], odd_w_out[i].astype(BF16), x2)
        x = expert_choice_layer(x2.reshape(b, s, d), ffn_norm[layer], router[layer], moe_w1[layer],
                                moe_w3[layer], moe_w2[layer],
                                final_norm if layer == depth - 1 else None)
    return x
```

```python
import functools
import math

import jax
import jax.numpy as jnp
from jax import lax
from jax.experimental import pallas as pl
from jax.experimental.pallas import tpu as pltpu

F32 = jnp.float32
BF16 = jnp.bfloat16
I32 = jnp.int32
U32 = jnp.uint32

EPS = 1e-6
NEG_INF = -1e30

LANES = 128
SUBLANES = 8
VMEM_LIMIT = 56 << 20

ATTN_BLOCK = 128
HEAD_DIM = 128
KV_GROUP = 4
REL_BUCKETS = 32
REL_MAX_DIST = 128
SSM_GROUP = 16
SSM_STATE = 64
S5_CHUNK = 32
GMLP_CHUNK = 128
N_EXPERTS = 16
CAPACITY_FACTOR = 2


def _params(sem, vmem=VMEM_LIMIT):
    return pltpu.CompilerParams(dimension_semantics=sem, vmem_limit_bytes=vmem)


def _tile(n, want):
    t = min(n, want)
    while n % t:
        t //= 2
    return t


def _rms(x, g):
    ms = jnp.mean(x * x, axis=-1, keepdims=True)
    return x * lax.rsqrt(ms + EPS) * g


def _gelu(x):
    c = math.sqrt(2.0 / math.pi)
    return x * (0.5 * (1.0 + jnp.tanh(c * (x + 0.044715 * (x * x * x)))))


def _norm_mm_kernel(x_ref, g_ref, w_ref, o_ref, hn_ref, *, act):
    @pl.when(pl.program_id(1) == 0)
    def _():
        hn_ref[...] = _rms(x_ref[...], g_ref[...]).astype(BF16)

    y = jnp.dot(hn_ref[...], w_ref[...], preferred_element_type=F32)
    if act == "gelu":
        y = _gelu(y)
    o_ref[...] = y.astype(o_ref.dtype)


def norm_matmul(x, g, w_bf16, *, act=None, out_dtype=F32, tm=1024, tn=512):
    m, k = x.shape
    n = w_bf16.shape[1]
    tm, tn = _tile(m, tm), _tile(n, tn)
    return pl.pallas_call(
        functools.partial(_norm_mm_kernel, act=act),
        grid=(m // tm, n // tn),
        in_specs=[
            pl.BlockSpec((tm, k), lambda i, j: (i, 0)),
            pl.BlockSpec((1, k), lambda i, j: (0, 0)),
            pl.BlockSpec((k, tn), lambda i, j: (0, j)),
        ],
        out_specs=pl.BlockSpec((tm, tn), lambda i, j: (i, j)),
        out_shape=jax.ShapeDtypeStruct((m, n), out_dtype),
        scratch_shapes=[pltpu.VMEM((tm, k), BF16)],
        compiler_params=_params(("parallel", "arbitrary")),
        name="norm_matmul",
    )(x, g.reshape(1, k), w_bf16)


def _mm_res_kernel(*refs, n_lhs):
    lhs = refs[:n_lhs]
    ws = refs[n_lhs:2 * n_lhs]
    res_ref, o_ref = refs[2 * n_lhs], refs[2 * n_lhs + 1]
    acc = res_ref[...]
    for l_ref, w_ref in zip(lhs, ws):
        acc = acc + jnp.dot(l_ref[...], w_ref[...], preferred_element_type=F32)
    o_ref[...] = acc


def matmul_residual(lhs_list, w_bf16, res, *, tm=1024, tn=512):
    m, n = res.shape
    kk = lhs_list[0].shape[1]
    n_lhs = len(lhs_list)
    assert all(l.shape == (m, kk) for l in lhs_list) and w_bf16.shape == (n_lhs * kk, n)
    tm, tn = _tile(m, tm), _tile(n, tn)
    in_specs = [pl.BlockSpec((tm, kk), lambda i, j: (i, 0)) for _ in lhs_list]
    in_specs += [pl.BlockSpec((kk, tn), functools.partial(lambda i, j, r: (r, j), r=r))
                 for r in range(n_lhs)]
    in_specs += [pl.BlockSpec((tm, tn), lambda i, j: (i, j))]
    return pl.pallas_call(
        functools.partial(_mm_res_kernel, n_lhs=n_lhs),
        grid=(m // tm, n // tn),
        in_specs=in_specs,
        out_specs=pl.BlockSpec((tm, tn), lambda i, j: (i, j)),
        out_shape=jax.ShapeDtypeStruct((m, n), F32),
        compiler_params=_params(("parallel", "parallel")),
        name="matmul_residual",
    )(*lhs_list, *([w_bf16] * n_lhs), res)


def _t5_bucket(rel):
    nb = REL_BUCKETS // 2
    max_exact = nb // 2
    base = jnp.where(rel > 0, nb, 0)
    n = jnp.abs(rel)
    nf = jnp.maximum(n, 1).astype(F32)
    large = max_exact + (jnp.log(nf / max_exact) / math.log(REL_MAX_DIST / max_exact)
                         * (nb - max_exact)).astype(I32)
    large = jnp.minimum(large, nb - 1)
    return base + jnp.where(n < max_exact, n, large)


def _bias_kernel(rb_ref, bucket_ref, o_ref, *, n_heads):
    bkt = bucket_ref[...]
    for h in range(n_heads):
        acc = jnp.zeros(bkt.shape, F32)
        for k in range(REL_BUCKETS):
            acc = jnp.where(bkt == k, rb_ref[k, h], acc)
        o_ref[h * ATTN_BLOCK:(h + 1) * ATTN_BLOCK, :] = acc


def attention_bias(rel_bias):
    n_heads = rel_bias.shape[1]
    q_off = jnp.arange(ATTN_BLOCK, dtype=I32)
    c_off = jnp.arange(3 * ATTN_BLOCK, dtype=I32)
    bucket = _t5_bucket(c_off[None, :] - ATTN_BLOCK - q_off[:, None])
    return pl.pallas_call(
        functools.partial(_bias_kernel, n_heads=n_heads),
        in_specs=[pl.BlockSpec(memory_space=pltpu.SMEM),
                  pl.BlockSpec(memory_space=pltpu.VMEM)],
        out_specs=pl.BlockSpec(memory_space=pltpu.VMEM),
        out_shape=jax.ShapeDtypeStruct((n_heads * ATTN_BLOCK, 3 * ATTN_BLOCK), F32),
        name="attention_bias",
    )(rel_bias.astype(F32), bucket)


def _attn_kernel(sink_ref, q_ref, kp_ref, kc_ref, kn_ref, vp_ref, vc_ref, vn_ref, bias_ref,
                 o_ref, *, seq, n_kv):
    blk = ATTN_BLOCK
    n = pl.program_id(1)
    row = lax.broadcasted_iota(I32, (blk, 3 * blk), 0)
    col = lax.broadcasted_iota(I32, (blk, 3 * blk), 1)
    rel = col - blk - row
    kpos = n * blk - blk + col
    valid1 = (jnp.abs(rel) <= blk) & (kpos >= 0) & (kpos < seq)
    valid = jnp.concatenate([valid1] * KV_GROUP, axis=0)
    scale = HEAD_DIM ** -0.5
    q = q_ref[0]
    for kh in range(n_kv):
        ksl = slice(kh * HEAD_DIM, (kh + 1) * HEAD_DIM)
        kband = jnp.concatenate([kp_ref[0, :, ksl], kc_ref[0, :, ksl], kn_ref[0, :, ksl]],
                                axis=0).astype(BF16)
        vband = jnp.concatenate([vp_ref[0, :, ksl], vc_ref[0, :, ksl], vn_ref[0, :, ksl]],
                                axis=0).astype(BF16)
        heads = [kh * KV_GROUP + g for g in range(KV_GROUP)]
        qs = jnp.concatenate([q[:, h * HEAD_DIM:(h + 1) * HEAD_DIM] for h in heads],
                             axis=0).astype(BF16)
        s = lax.dot_general(qs, kband, (((1,), (1,)), ((), ())),
                            preferred_element_type=F32) * scale
        bias = bias_ref[kh * KV_GROUP * blk:(kh + 1) * KV_GROUP * blk, :]
        s = jnp.where(valid, s + bias, NEG_INF)
        sink = jnp.concatenate([jnp.full((blk, 1), sink_ref[h], F32) for h in heads], axis=0)
        m = jnp.maximum(jnp.max(s, axis=-1, keepdims=True), sink)
        p = jnp.exp(s - m)
        denom = jnp.sum(p, axis=-1, keepdims=True) + jnp.exp(sink - m)
        probs = (p / denom).astype(BF16)
        o = jnp.dot(probs, vband, preferred_element_type=F32)
        for g, h in enumerate(heads):
            o_ref[0, :, h * HEAD_DIM:(h + 1) * HEAD_DIM] = o[g * blk:(g + 1) * blk].astype(o_ref.dtype)


def windowed_attention(proj, sink, bias, *, attn_w, kv_w):
    b, s, _ = proj.shape
    blk = ATTN_BLOCK
    nblk = s // blk
    n_kv = kv_w // HEAD_DIM
    kcol = attn_w // kv_w
    assert attn_w % kv_w == 0 and n_kv * KV_GROUP * HEAD_DIM == attn_w

    def kv_spec(col, d):
        return pl.BlockSpec(
            (1, blk, kv_w),
            lambda bi, n: (bi, jnp.clip(n + d, 0, nblk - 1), col))

    return pl.pallas_call(
        functools.partial(_attn_kernel, seq=s, n_kv=n_kv),
        grid=(b, nblk),
        in_specs=[
            pl.BlockSpec(memory_space=pltpu.SMEM),
            pl.BlockSpec((1, blk, attn_w), lambda bi, n: (bi, n, 0)),
            kv_spec(kcol, -1), kv_spec(kcol, 0), kv_spec(kcol, 1),
            kv_spec(kcol + 1, -1), kv_spec(kcol + 1, 0), kv_spec(kcol + 1, 1),
            pl.BlockSpec(bias.shape, lambda bi, n: (0, 0)),
        ],
        out_specs=pl.BlockSpec((1, blk, attn_w), lambda bi, n: (bi, n, 0)),
        out_shape=jax.ShapeDtypeStruct((b, s, attn_w), BF16),
        compiler_params=_params(("parallel", "parallel")),
        name="windowed_attention",
    )(sink.astype(F32), proj, proj, proj, proj, proj, proj, proj, bias)


def _s5_direction_tables(a_re, a_im, log_dt, b_re, b_im, c_re, c_im, chunk, n_levels):
    hi = lax.Precision.HIGHEST
    g, p = a_re.shape
    h = b_re.shape[-1]
    ell = chunk
    dt = jnp.exp(log_dt)[:, None]
    mag = jnp.exp(a_re * dt)
    lb_re = mag * jnp.cos(a_im * dt)
    lb_im = mag * jnp.sin(a_im * dt)
    den = a_re * a_re + a_im * a_im
    nr = lb_re - 1.0
    coef_re = (nr * a_re + lb_im * a_im) / den
    coef_im = (lb_im * a_re - nr * a_im) / den
    bb_re = coef_re[..., None] * b_re - coef_im[..., None] * b_im
    bb_im = coef_re[..., None] * b_im + coef_im[..., None] * b_re

    def powers(tau):
        t = tau.astype(F32)[:, None, None]
        pm = jnp.exp(a_re * dt * t)
        return pm * jnp.cos(a_im * dt * t), pm * jnp.sin(a_im * dt * t)

    pw_re, pw_im = powers(jnp.arange(ell + 1))
    m_re = pw_re[..., None] * bb_re - pw_im[..., None] * bb_im
    m_im = pw_re[..., None] * bb_im + pw_im[..., None] * bb_re
    kern = (jnp.einsum("ghp,tgpk->tghk", c_re, m_re, precision=hi)
            - jnp.einsum("ghp,tgpk->tghk", c_im, m_im, precision=hi))
    ii = jnp.arange(ell)
    lag = ii[:, None] - ii[None, :]
    t_full = jnp.where((lag >= 0)[:, :, None, None, None], kern[jnp.clip(lag, 0, ell)], 0.0)
    t_tab = t_full.transpose(2, 1, 4, 0, 3).reshape(g, ell * h, ell * h)
    f_re = m_re[ell - 1 - ii].transpose(1, 0, 3, 2).reshape(g, ell * h, p)
    f_im = m_im[ell - 1 - ii].transpose(1, 0, 3, 2).reshape(g, ell * h, p)
    ar = c_re[None] * pw_re[1:, :, None, :] - c_im[None] * pw_im[1:, :, None, :]
    ai = c_re[None] * pw_im[1:, :, None, :] + c_im[None] * pw_re[1:, :, None, :]
    e_re = ar.transpose(1, 3, 0, 2).reshape(g, p, ell * h)
    e_im = (-ai).transpose(1, 3, 0, 2).reshape(g, p, ell * h)
    lv_re, lv_im = powers(ell * (2 ** jnp.arange(n_levels)))
    return t_tab, f_re, f_im, e_re, e_im, lv_re, lv_im


def _s5_tables(a_re, a_im, log_dt, b_re, b_im, c_re, c_im, chunk, n_levels):
    g, p = a_re.shape[1:]
    h = b_re.shape[-1]
    lh = chunk * h
    fw = _s5_direction_tables(a_re[0], a_im[0], log_dt[0], b_re[0], b_im[0], c_re[0], c_im[0],
                              chunk, n_levels)
    bw = _s5_direction_tables(a_re[1], a_im[1], log_dt[1], b_re[1], b_im[1], c_re[1], c_im[1],
                              chunk, n_levels)

    def flip_rows(t):
        return t.reshape(g, chunk, h, -1)[:, ::-1].reshape(g, lh, -1)

    def flip_cols(t):
        return t.reshape(g, -1, chunk, h)[:, :, ::-1].reshape(g, -1, lh)

    t_tab = fw[0] + flip_cols(flip_rows(bw[0]))
    pad_p = lambda t: jnp.pad(t, ((0, 0), (0, 0), (0, LANES - p)))
    f_tab = jnp.concatenate([pad_p(fw[1]), pad_p(fw[2]),
                             pad_p(flip_rows(bw[1])), pad_p(flip_rows(bw[2]))], axis=-1)
    pad_r = lambda t: jnp.pad(t, ((0, 0), (0, LANES - p), (0, 0)))
    e_tab = jnp.concatenate([pad_r(fw[3]), pad_r(fw[4]),
                             pad_r(flip_cols(bw[3])), pad_r(flip_cols(bw[4]))], axis=1)
    te_tab = jnp.concatenate([t_tab, e_tab], axis=1)
    pad_l = lambda t: jnp.pad(t, ((0, 0), (0, 0), (0, LANES - p)))
    pa = jnp.concatenate([pad_l(fw[5]), pad_l(fw[5]), pad_l(bw[5]), pad_l(bw[5])], axis=-1)
    pb = jnp.concatenate([-pad_l(fw[6]), pad_l(fw[6]), -pad_l(bw[6]), pad_l(bw[6])], axis=-1)
    lev = SUBLANES * pl.cdiv(n_levels, SUBLANES)
    pad_lv = lambda t: jnp.pad(t.transpose(1, 0, 2), ((0, 0), (0, lev - n_levels), (0, 0)))
    return f_tab.astype(BF16), te_tab.astype(BF16), pad_lv(pa), pad_lv(pb)


def _shift_rows(z, r):
    rows = z.shape[0]
    rolled = pltpu.roll(z, r % rows, 0)
    ridx = lax.broadcasted_iota(I32, z.shape, 0)
    keep = (ridx >= r) if r > 0 else (ridx < rows + r)
    return jnp.where(keep, rolled, 0.0)


def _s5_kernel(u_ref, f_ref, te_ref, pa_ref, pb_ref, o_ref, *, rows_per_chunk, n_levels, lh):
    u = u_ref[0]
    z = jnp.dot(u, f_ref[0], preferred_element_type=F32)
    half = 2 * LANES

    def swap(x):
        return jnp.concatenate([x[:, LANES:], x[:, :LANES]], axis=1)

    zf, zb = z[:, :half], z[:, half:]
    for k in range(n_levels):
        r = rows_per_chunk * (2 ** k)
        pa = pa_ref[0, k:k + 1, :]
        pb = pb_ref[0, k:k + 1, :]
        sf = _shift_rows(zf, r)
        sb = _shift_rows(zb, -r)
        zf = zf + pa[:, :half] * sf + pb[:, :half] * swap(sf)
        zb = zb + pa[:, half:] * sb + pb[:, half:] * swap(sb)
    hin = jnp.concatenate([_shift_rows(zf, rows_per_chunk), _shift_rows(zb, -rows_per_chunk)],
                          axis=1).astype(BF16)
    y = jnp.dot(u, te_ref[0, :lh, :], preferred_element_type=F32)
    y = y + jnp.dot(hin, te_ref[0, lh:, :], preferred_element_type=F32)
    o_ref[0] = y


def s5_scan(u, tables, *, chunk):
    f_tab, te_tab, pa, pb = tables
    b, s, w = u.shape
    g = f_tab.shape[0]
    h = w // g
    n = s // chunk
    lh = chunk * h
    n_levels = max(1, (n - 1).bit_length())
    rows = n * b
    ut = u.astype(BF16).reshape(b, n, chunk, g, h).transpose(3, 1, 0, 2, 4).reshape(g, rows, lh)
    y = pl.pallas_call(
        functools.partial(_s5_kernel, rows_per_chunk=b, n_levels=n_levels, lh=lh),
        grid=(g,),
        in_specs=[
            pl.BlockSpec((1, rows, lh), lambda i: (i, 0, 0)),
            pl.BlockSpec((1, lh, 4 * LANES), lambda i: (i, 0, 0)),
            pl.BlockSpec((1, lh + 4 * LANES, lh), lambda i: (i, 0, 0)),
            pl.BlockSpec((1,) + pa.shape[1:], lambda i: (i, 0, 0)),
            pl.BlockSpec((1,) + pb.shape[1:], lambda i: (i, 0, 0)),
        ],
        out_specs=pl.BlockSpec((1, rows, lh), lambda i: (i, 0, 0)),
        out_shape=jax.ShapeDtypeStruct((g, rows, lh), F32),
        compiler_params=_params(("parallel",)),
        name="s5_scan",
    )(ut, f_tab, te_tab, pa, pb)
    return y.reshape(g, n, b, chunk, h).transpose(2, 1, 3, 0, 4).reshape(b, s, w)


def _glu_kernel(y_ref, u_ref, d_ref, w_ref, b_ref, o_ref):
    y = y_ref[...] + d_ref[...] * u_ref[...]
    g = _gelu(y)
    z = jnp.dot(g.astype(BF16), w_ref[...], preferred_element_type=F32) + b_ref[...]
    o_ref[...] = (g * jax.nn.sigmoid(z)).astype(o_ref.dtype)


def s5_glu(y, u, d_skip, glu_w_bf16, glu_b, *, tm=512):
    m, w = y.shape
    tm = _tile(m, tm)
    return pl.pallas_call(
        _glu_kernel,
        grid=(m // tm,),
        in_specs=[
            pl.BlockSpec((tm, w), lambda i: (i, 0)),
            pl.BlockSpec((tm, w), lambda i: (i, 0)),
            pl.BlockSpec((1, w), lambda i: (0, 0)),
            pl.BlockSpec((w, w), lambda i: (0, 0)),
            pl.BlockSpec((1, w), lambda i: (0, 0)),
        ],
        out_specs=pl.BlockSpec((tm, w), lambda i: (i, 0)),
        out_shape=jax.ShapeDtypeStruct((m, w), BF16),
        compiler_params=_params(("parallel",)),
        name="s5_glu",
    )(y, u, d_skip.reshape(1, w), glu_w_bf16, glu_b.reshape(1, w))


def _sgu_kernel(z_ref, g_ref, b_ref, ws_ref, bs_ref, o_ref, *, width, n_heads):
    hd = width // n_heads
    z = z_ref[...]
    u, v = z[:, :width], z[:, width:]
    mu = jnp.mean(v, axis=-1, keepdims=True)
    var = jnp.mean(jnp.square(v - mu), axis=-1, keepdims=True)
    vn = ((v - mu) * lax.rsqrt(var + EPS) * g_ref[...] + b_ref[...]).astype(BF16)
    for h in range(n_heads):
        sl = slice(h * hd, (h + 1) * hd)
        mixed = jnp.dot(ws_ref[h], vn[:, sl], preferred_element_type=F32) + bs_ref[:, h:h + 1]
        o_ref[:, sl] = (u[:, sl] * mixed).astype(o_ref.dtype)


def gmlp_sgu(z, ln_g, ln_b, w_s_bf16, b_s):
    m, w2 = z.shape
    width = w2 // 2
    n_heads = w_s_bf16.shape[0]
    ck = GMLP_CHUNK
    return pl.pallas_call(
        functools.partial(_sgu_kernel, width=width, n_heads=n_heads),
        grid=(m // ck,),
        in_specs=[
            pl.BlockSpec((ck, w2), lambda i: (i, 0)),
            pl.BlockSpec((1, width), lambda i: (0, 0)),
            pl.BlockSpec((1, width), lambda i: (0, 0)),
            pl.BlockSpec((n_heads, ck, ck), lambda i: (0, 0, 0)),
            pl.BlockSpec((ck, n_heads), lambda i: (0, 0)),
        ],
        out_specs=pl.BlockSpec((ck, width), lambda i: (i, 0)),
        out_shape=jax.ShapeDtypeStruct((m, width), BF16),
        compiler_params=_params(("parallel",)),
        name="gmlp_sgu",
    )(z, ln_g.reshape(1, width), ln_b.reshape(1, width), w_s_bf16, b_s.T.astype(F32))


def _slab_pitch(half):
    return SUBLANES * pl.cdiv(half // LANES, SUBLANES)


def _router_kernel(x_ref, g_ref, rt_ref, aff_ref, slab_ref, *, tm, half, pitch):
    h = _rms(x_ref[0], g_ref[...])
    logits = lax.dot_general(rt_ref[...], h, (((1,), (1,)), ((), ())),
                             precision=lax.Precision.HIGHEST,
                             preferred_element_type=F32)
    mx = jnp.max(logits, axis=0, keepdims=True)
    ex = jnp.exp(logits - mx)
    aff_ref[0] = ex / jnp.sum(ex, axis=0, keepdims=True)
    lo = lax.bitcast_convert_type(h[:, :half].astype(BF16).astype(F32), U32) >> 16
    hi = lax.bitcast_convert_type(h[:, half:].astype(BF16).astype(F32), U32) & jnp.uint32(0xFFFF0000)
    word = lo | hi
    for s in range(half // LANES):
        slab_ref[pl.ds(s, tm, stride=pitch), :] = word[:, s * LANES:(s + 1) * LANES]
    if pitch > half // LANES:
        for s in range(half // LANES, pitch):
            slab_ref[pl.ds(s, tm, stride=pitch), :] = jnp.zeros((tm, LANES), U32)


def moe_router(x, g, router, *, tm=512):
    b, s, d = x.shape
    e = router.shape[1]
    half = d // 2
    pitch = _slab_pitch(half)
    tm = _tile(s, tm)
    nt = s // tm
    return pl.pallas_call(
        functools.partial(_router_kernel, tm=tm, half=half, pitch=pitch),
        grid=(b, nt),
        in_specs=[
            pl.BlockSpec((1, tm, d), lambda bi, j: (bi, j, 0)),
            pl.BlockSpec((1, d), lambda bi, j: (0, 0)),
            pl.BlockSpec((e, d), lambda bi, j: (0, 0)),
        ],
        out_specs=[
            pl.BlockSpec((1, e, tm), lambda bi, j: (bi, 0, j)),
            pl.BlockSpec((tm * pitch, LANES), lambda bi, j: (bi * nt + j, 0)),
        ],
        out_shape=[jax.ShapeDtypeStruct((b, e, s), F32),
                   jax.ShapeDtypeStruct((b * s * pitch, LANES), U32)],
        compiler_params=_params(("parallel", "parallel")),
        name="moe_router",
    )(x, g.reshape(1, d), router.T.astype(F32))


def _lane_cumsum(m_bf16, tri):
    e, s = m_bf16.shape
    off = jnp.zeros((e, 1), F32)
    out = []
    for j in range(s // LANES):
        c = jnp.dot(m_bf16[:, j * LANES:(j + 1) * LANES], tri, preferred_element_type=F32) + off
        out.append(c)
        off = c[:, LANES - 1:LANES]
    return jnp.concatenate(out, axis=1)


def _topk_kernel(aff_ref, idx_ref, gate_ref, *, cap, kchunk):
    aff = aff_ref[0]
    e, s = aff.shape
    bits = lax.bitcast_convert_type(aff, I32)

    def count_ge(t):
        return jnp.sum((bits >= t).astype(F32), axis=1, keepdims=True)

    def body(_, carry):
        lo, hi = carry
        mid = lo + ((hi - lo) >> 1)
        ok = count_ge(mid) >= cap
        return jnp.where(ok, mid, lo), jnp.where(ok, hi, mid)

    lo0 = jnp.zeros((e, 1), I32)
    hi0 = jnp.full((e, 1), 0x7F800000, I32)
    thr, _ = lax.fori_loop(0, 32, body, (lo0, hi0))
    gt = bits > thr
    eq = bits == thr
    need = cap - jnp.sum(gt.astype(F32), axis=1, keepdims=True)
    tri = (lax.broadcasted_iota(I32, (LANES, LANES), 0)
           <= lax.broadcasted_iota(I32, (LANES, LANES), 1)).astype(BF16)
    eq_rank = _lane_cumsum(jnp.where(eq, 1.0, 0.0).astype(BF16), tri)
    sel = gt | (eq & (eq_rank <= need))
    pos = _lane_cumsum(jnp.where(sel, 1.0, 0.0).astype(BF16), tri)
    pos = jnp.where(sel, pos, 0.0)

    tok = lax.broadcasted_iota(I32, (1, s), 1)
    t_hi = (tok >> 6).astype(F32)
    t_lo = (tok & 63).astype(F32)
    slot = lax.broadcasted_iota(I32, (cap, kchunk), 0).astype(F32) + 1.0
    lrow = lax.broadcasted_iota(I32, (SUBLANES, s), 0)
    for ei in range(e):
        a = aff[ei:ei + 1, :]
        g1 = a.astype(BF16).astype(F32)
        g2 = (a - g1).astype(BF16).astype(F32)
        g3 = (a - g1) - g2
        lhs = jnp.where(lrow == 0, t_hi, jnp.where(lrow == 1, t_lo, jnp.where(
            lrow == 2, g1, jnp.where(lrow == 3, g2, jnp.where(lrow == 4, g3, 0.0)))))
        lhs = lhs.astype(BF16)
        acc = jnp.zeros((SUBLANES, cap), F32)
        for c0 in range(0, s, kchunk):
            onehot = jnp.where(pos[ei:ei + 1, c0:c0 + kchunk] == slot, 1.0, 0.0).astype(BF16)
            acc = acc + lax.dot_general(lhs[:, c0:c0 + kchunk], onehot, (((1,), (1,)), ((), ())),
                                        preferred_element_type=F32)
        idx_ref[0, ei:ei + 1, :] = (acc[0:1] * 64.0 + acc[1:2]).astype(I32)
        gate_ref[0, ei:ei + 1, :] = acc[2:3] + acc[3:4] + acc[4:5]


def moe_topk(aff, cap):
    b, e, s = aff.shape
    return pl.pallas_call(
        functools.partial(_topk_kernel, cap=cap, kchunk=_tile(s, 1024)),
        grid=(b,),
        in_specs=[pl.BlockSpec((1, e, s), lambda i: (i, 0, 0))],
        out_specs=[pl.BlockSpec((1, e, cap), lambda i: (i, 0, 0)),
                   pl.BlockSpec((1, e, cap), lambda i: (i, 0, 0))],
        out_shape=[jax.ShapeDtypeStruct((b, e, cap), I32),
                   jax.ShapeDtypeStruct((b, e, cap), F32)],
        compiler_params=_params(("parallel",)),
        name="moe_topk",
    )(aff)


def _gather_kernel(idx_ref, slab_ref, o_ref, rows_ref, *, cap, half, pitch, unroll):
    def body(j0, c):
        for u in range(unroll):
            j = j0 * unroll + u
            src = pl.multiple_of(idx_ref[0, 0, j] * pitch, pitch)
            rows_ref[pl.ds(pl.multiple_of(j * pitch, pitch), pitch), :] = slab_ref[pl.ds(src, pitch), :]
        return c

    lax.fori_loop(0, cap // unroll, body, 0)
    for s in range(half // LANES):
        word = rows_ref[pl.ds(s, cap, stride=pitch), :]
        lo = lax.bitcast_convert_type(word << 16, F32)
        hi = lax.bitcast_convert_type(word & jnp.uint32(0xFFFF0000), F32)
        o_ref[0, :, s * LANES:(s + 1) * LANES] = lo.astype(BF16)
        o_ref[0, :, half + s * LANES:half + (s + 1) * LANES] = hi.astype(BF16)


def moe_gather(idx, slabs, *, seq, d):
    b, e, cap = idx.shape
    half = d // 2
    pitch = _slab_pitch(half)
    return pl.pallas_call(
        functools.partial(_gather_kernel, cap=cap, half=half, pitch=pitch, unroll=8),
        grid=(b, e),
        in_specs=[
            pl.BlockSpec((1, 1, cap), lambda bi, ei: (bi * e + ei, 0, 0), memory_space=pltpu.SMEM),
            pl.BlockSpec((seq * pitch, LANES), lambda bi, ei: (bi, 0)),
        ],
        out_specs=pl.BlockSpec((1, cap, d), lambda bi, ei: (ei, bi, 0)),
        out_shape=jax.ShapeDtypeStruct((e, b * cap, d), BF16),
        scratch_shapes=[pltpu.VMEM((cap * pitch, LANES), U32)],
        compiler_params=_params(("parallel", "arbitrary")),
        name="moe_gather",
    )(idx.reshape(b * e, 1, cap), slabs)


def _ffn_kernel(xs_ref, w1_ref, w3_ref, w2_ref, o_ref, hid_ref, *, nt, tf):
    step = pl.program_id(1)

    @pl.when(step < nt)
    def _():
        xs = xs_ref[0]
        a = jnp.dot(xs, w1_ref[0, 0].astype(BF16), preferred_element_type=F32)
        g = jnp.dot(xs, w3_ref[0, 0].astype(BF16), preferred_element_type=F32)
        hid_ref[step] = (a * jax.nn.sigmoid(a) * g).astype(BF16)

    @pl.when(step >= nt)
    def _():
        acc = jnp.zeros(o_ref.shape[1:], F32)
        for k in range(nt):
            acc = acc + jnp.dot(hid_ref[k], w2_ref[0, 0, k * tf:(k + 1) * tf, :].astype(BF16),
                                preferred_element_type=F32)
        o_ref[0] = acc


def moe_ffn(xs, w1, w3, w2, layer, *, tf=256):
    e, m, d = xs.shape
    f = w1.shape[3]
    tf = _tile(f, tf)
    tn = _tile(d, tf)
    nt = f // tf
    nn = d // tn
    return pl.pallas_call(
        functools.partial(_ffn_kernel, nt=nt, tf=tf),
        grid=(e, nt + nn),
        in_specs=[
            pl.BlockSpec((1, m, d), lambda ei, t: (ei, 0, 0)),
            pl.BlockSpec((1, 1, d, tf), lambda ei, t: (layer, ei, 0, jnp.minimum(t, nt - 1))),
            pl.BlockSpec((1, 1, d, tf), lambda ei, t: (layer, ei, 0, jnp.minimum(t, nt - 1))),
            pl.BlockSpec((1, 1, f, tn), lambda ei, t: (layer, ei, 0, jnp.maximum(t - nt, 0))),
        ],
        out_specs=pl.BlockSpec((1, m, tn), lambda ei, t: (ei, 0, jnp.maximum(t - nt, 0))),
        out_shape=jax.ShapeDtypeStruct((e, m, d), F32),
        scratch_shapes=[pltpu.VMEM((nt, m, tf), BF16)],
        compiler_params=_params(("parallel", "arbitrary")),
        name="moe_ffn",
    )(xs, w1, w3, w2)


def _combine_kernel(idx_ref, gate_ref, ys_ref, o_ref, rows_ref, *, cap, half, pitch, unroll):
    ei = pl.program_id(2)

    @pl.when(ei == 0)
    def _():
        o_ref[...] = jnp.zeros(o_ref.shape, F32)

    for s in range(half // LANES):
        rows_ref[pl.ds(s, cap, stride=pitch), :] = ys_ref[0, :, s * LANES:(s + 1) * LANES]

    def body(j0, c):
        dsts, vals = [], []
        for u in range(unroll):
            j = j0 * unroll + u
            dst = pl.multiple_of(idx_ref[0, 0, j] * pitch, pitch)
            src = pl.multiple_of(j * pitch, pitch)
            vals.append(o_ref[0, 0, pl.ds(dst, pitch), :]
                        + gate_ref[0, 0, j] * rows_ref[pl.ds(src, pitch), :])
            dsts.append(dst)
        for dst, val in zip(dsts, vals):
            o_ref[0, 0, pl.ds(dst, pitch), :] = val
        return c

    lax.fori_loop(0, cap // unroll, body, 0)


def moe_combine(idx, gate, ys, *, seq, d):
    b, e, cap = idx.shape
    half = d // 2
    pitch = _slab_pitch(half)
    return pl.pallas_call(
        functools.partial(_combine_kernel, cap=cap, half=half, pitch=pitch, unroll=8),
        grid=(b, 2, e),
        in_specs=[
            pl.BlockSpec((1, 1, cap), lambda bi, dh, ei: (bi * e + ei, 0, 0), memory_space=pltpu.SMEM),
            pl.BlockSpec((1, 1, cap), lambda bi, dh, ei: (bi * e + ei, 0, 0), memory_space=pltpu.SMEM),
            pl.BlockSpec((1, cap, half), lambda bi, dh, ei: (ei, bi, dh)),
        ],
        out_specs=pl.BlockSpec((1, 1, seq * pitch, LANES), lambda bi, dh, ei: (bi, dh, 0, 0)),
        out_shape=jax.ShapeDtypeStruct((b, 2, seq * pitch, LANES), F32),
        scratch_shapes=[pltpu.VMEM((cap * pitch, LANES), F32)],
        compiler_params=_params(("parallel", "parallel", "arbitrary")),
        name="moe_combine",
    )(idx.reshape(b * e, 1, cap), gate.reshape(b * e, 1, cap), ys)


def _unslab_kernel(*refs, tm, half, pitch, final_norm):
    x_ref, s0_ref, s1_ref = refs[:3]
    o_ref = refs[-1]
    parts = []
    for s_ref in (s0_ref, s1_ref):
        for s in range(half // LANES):
            parts.append(s_ref[0, 0, pl.ds(s, tm, stride=pitch), :])
    y = x_ref[0] + jnp.concatenate(parts, axis=1)
    if final_norm:
        y = _rms(y, refs[3][...])
    o_ref[0] = y


def moe_residual(x, moe_slabs, final_g=None, *, tm=256):
    b, s, d = x.shape
    half = d // 2
    pitch = _slab_pitch(half)
    tm = _tile(s, tm)
    final_norm = final_g is not None
    in_specs = [
        pl.BlockSpec((1, tm, d), lambda bi, j: (bi, j, 0)),
        pl.BlockSpec((1, 1, tm * pitch, LANES), lambda bi, j: (bi, 0, j, 0)),
        pl.BlockSpec((1, 1, tm * pitch, LANES), lambda bi, j: (bi, 1, j, 0)),
    ]
    args = [x, moe_slabs, moe_slabs]
    if final_norm:
        in_specs.append(pl.BlockSpec((1, d), lambda bi, j: (0, 0)))
        args.append(final_g.reshape(1, d))
    return pl.pallas_call(
        functools.partial(_unslab_kernel, tm=tm, half=half, pitch=pitch, final_norm=final_norm),
        grid=(b, s // tm),
        in_specs=in_specs,
        out_specs=pl.BlockSpec((1, tm, d), lambda bi, j: (bi, j, 0)),
        out_shape=jax.ShapeDtypeStruct((b, s, d), F32),
        compiler_params=_params(("parallel", "parallel")),
        name="moe_residual",
    )(*args)


def expert_choice_layer(x, norm_g, router, w1, w3, w2, layer, final_g=None):
    b, s, d = x.shape
    cap = CAPACITY_FACTOR * s // N_EXPERTS
    aff, slabs = moe_router(x, norm_g, router)
    idx, gate = moe_topk(aff, cap)
    xs = moe_gather(idx, slabs, seq=s, d=d)
    ys = moe_ffn(xs, w1, w3, w2, layer)
    moe = moe_combine(idx, gate, ys, seq=s, d=d)
    return moe_residual(x, moe, final_g)


def kernel(x, rel_bias, mix_norm, ffn_norm, final_norm, even_w_in, attn_sink, ssm_a_re, ssm_a_im, ssm_log_dt, ssm_b_re, ssm_b_im, ssm_c_re, ssm_c_im, ssm_d, glu_w, glu_b, even_w_out, odd_w_in, sgu_ln_g, sgu_ln_b, sgu_w, sgu_b, odd_w_out, router, moe_w1, moe_w3, moe_w2):
    b, s, d = x.shape
    t = b * s
    depth = mix_norm.shape[0]
    ssm_w = ssm_d.shape[-1]
    attn_w = even_w_out.shape[1] - ssm_w
    kv_w = (even_w_in.shape[-1] - attn_w - ssm_w) // 2
    bias = attention_bias(rel_bias)

    for layer in range(depth):
        i = layer // 2
        x2 = x.reshape(t, d)
        if layer % 2 == 0:
            proj = norm_matmul(x2, mix_norm[layer], even_w_in[i].astype(BF16))
            attn = windowed_attention(proj.reshape(b, s, -1), attn_sink[i], bias,
                                      attn_w=attn_w, kv_w=kv_w)
            u = proj[:, attn_w + 2 * kv_w:]
            n_chunks = s // S5_CHUNK
            tables = _s5_tables(ssm_a_re[i], ssm_a_im[i], ssm_log_dt[i], ssm_b_re[i], ssm_b_im[i],
                                ssm_c_re[i], ssm_c_im[i], S5_CHUNK,
                                max(1, (n_chunks - 1).bit_length()))
            y = s5_scan(u.reshape(b, s, ssm_w), tables, chunk=S5_CHUNK)
            ssm = s5_glu(y.reshape(t, ssm_w), u, ssm_d[i], glu_w[i].astype(BF16), glu_b[i])
            x2 = matmul_residual([attn.reshape(t, attn_w), ssm], even_w_out[i].astype(BF16), x2)
        else:
            z = norm_matmul(x2, mix_norm[layer], odd_w_in[i].astype(BF16), act="gelu")
            gated = gmlp_sgu(z, sgu_ln_g[i], sgu_ln_b[i], sgu_w[i].astype(BF16), sgu_b[i])
            x2 = matmul_residual([gated], odd_w_out[i].astype(BF16), x2)
        x = expert_choice_layer(x2.reshape(b, s, d), ffn_norm[layer], router[layer], moe_w1, moe_w3,
                                moe_w2, layer, final_norm if layer == depth - 1 else None)
    return x
```

```python
import functools
import math

import jax
import jax.numpy as jnp
from jax import lax
from jax.experimental import pallas as pl
from jax.experimental.pallas import tpu as pltpu

F32 = jnp.float32
BF16 = jnp.bfloat16
I32 = jnp.int32
U32 = jnp.uint32

EPS = 1e-6
NEG_INF = -1e30

LANES = 128
SUBLANES = 8
VMEM_LIMIT = 56 << 20

ATTN_BLOCK = 128
HEAD_DIM = 128
KV_GROUP = 4
REL_BUCKETS = 32
REL_MAX_DIST = 128
SSM_GROUP = 16
SSM_STATE = 64
S5_CHUNK = 32
GMLP_CHUNK = 128
N_EXPERTS = 16
CAPACITY_FACTOR = 2


def _params(sem, vmem=VMEM_LIMIT):
    return pltpu.CompilerParams(dimension_semantics=sem, vmem_limit_bytes=vmem)


def _tile(n, want):
    t = min(n, want)
    while n % t:
        t //= 2
    return t


def _rms(x, g):
    ms = jnp.mean(x * x, axis=-1, keepdims=True)
    return x * lax.rsqrt(ms + EPS) * g


def _gelu(x):
    c = math.sqrt(2.0 / math.pi)
    return x * (0.5 * (1.0 + jnp.tanh(c * (x + 0.044715 * (x * x * x)))))


def _norm_mm_kernel(x_ref, g_ref, w_ref, o_ref, hn_ref, *, act):
    @pl.when(pl.program_id(1) == 0)
    def _():
        hn_ref[...] = _rms(x_ref[...], g_ref[...]).astype(BF16)

    y = jnp.dot(hn_ref[...], w_ref[...], preferred_element_type=F32)
    if act == "gelu":
        y = _gelu(y)
    o_ref[...] = y.astype(o_ref.dtype)


def norm_matmul(x, g, w_bf16, *, act=None, out_dtype=F32, tm=1024, tn=512):
    m, k = x.shape
    n = w_bf16.shape[1]
    tm, tn = _tile(m, tm), _tile(n, tn)
    return pl.pallas_call(
        functools.partial(_norm_mm_kernel, act=act),
        grid=(m // tm, n // tn),
        in_specs=[
            pl.BlockSpec((tm, k), lambda i, j: (i, 0)),
            pl.BlockSpec((1, k), lambda i, j: (0, 0)),
            pl.BlockSpec((k, tn), lambda i, j: (0, j)),
        ],
        out_specs=pl.BlockSpec((tm, tn), lambda i, j: (i, j)),
        out_shape=jax.ShapeDtypeStruct((m, n), out_dtype),
        scratch_shapes=[pltpu.VMEM((tm, k), BF16)],
        compiler_params=_params(("parallel", "arbitrary")),
        name="norm_matmul",
    )(x, g.reshape(1, k), w_bf16)


def _mm_res_kernel(*refs, n_lhs):
    lhs = refs[:n_lhs]
    ws = refs[n_lhs:2 * n_lhs]
    res_ref, o_ref = refs[2 * n_lhs], refs[2 * n_lhs + 1]
    acc = res_ref[...]
    for l_ref, w_ref in zip(lhs, ws):
        acc = acc + jnp.dot(l_ref[...], w_ref[...], preferred_element_type=F32)
    o_ref[...] = acc


def matmul_residual(lhs_list, w_bf16, res, *, tm=1024, tn=512):
    m, n = res.shape
    kk = lhs_list[0].shape[1]
    n_lhs = len(lhs_list)
    assert all(l.shape == (m, kk) for l in lhs_list) and w_bf16.shape == (n_lhs * kk, n)
    tm, tn = _tile(m, tm), _tile(n, tn)
    in_specs = [pl.BlockSpec((tm, kk), lambda i, j: (i, 0)) for _ in lhs_list]
    in_specs += [pl.BlockSpec((kk, tn), functools.partial(lambda i, j, r: (r, j), r=r))
                 for r in range(n_lhs)]
    in_specs += [pl.BlockSpec((tm, tn), lambda i, j: (i, j))]
    return pl.pallas_call(
        functools.partial(_mm_res_kernel, n_lhs=n_lhs),
        grid=(m // tm, n // tn),
        in_specs=in_specs,
        out_specs=pl.BlockSpec((tm, tn), lambda i, j: (i, j)),
        out_shape=jax.ShapeDtypeStruct((m, n), F32),
        compiler_params=_params(("parallel", "parallel")),
        name="matmul_residual",
    )(*lhs_list, *([w_bf16] * n_lhs), res)


def _t5_bucket(rel):
    nb = REL_BUCKETS // 2
    max_exact = nb // 2
    base = jnp.where(rel > 0, nb, 0)
    n = jnp.abs(rel)
    nf = jnp.maximum(n, 1).astype(F32)
    large = max_exact + (jnp.log(nf / max_exact) / math.log(REL_MAX_DIST / max_exact)
                         * (nb - max_exact)).astype(I32)
    large = jnp.minimum(large, nb - 1)
    return base + jnp.where(n < max_exact, n, large)


def _bias_kernel(rb_ref, bucket_ref, o_ref, *, n_heads):
    bkt = bucket_ref[...]
    for h in range(n_heads):
        acc = jnp.zeros(bkt.shape, F32)
        for k in range(REL_BUCKETS):
            acc = jnp.where(bkt == k, rb_ref[k, h], acc)
        o_ref[h * ATTN_BLOCK:(h + 1) * ATTN_BLOCK, :] = acc


def attention_bias(rel_bias):
    n_heads = rel_bias.shape[1]
    q_off = jnp.arange(ATTN_BLOCK, dtype=I32)
    c_off = jnp.arange(3 * ATTN_BLOCK, dtype=I32)
    bucket = _t5_bucket(c_off[None, :] - ATTN_BLOCK - q_off[:, None])
    return pl.pallas_call(
        functools.partial(_bias_kernel, n_heads=n_heads),
        in_specs=[pl.BlockSpec(memory_space=pltpu.SMEM),
                  pl.BlockSpec(memory_space=pltpu.VMEM)],
        out_specs=pl.BlockSpec(memory_space=pltpu.VMEM),
        out_shape=jax.ShapeDtypeStruct((n_heads * ATTN_BLOCK, 3 * ATTN_BLOCK), F32),
        name="attention_bias",
    )(rel_bias.astype(F32), bucket)


def _attn_kernel(sink_ref, q_ref, kp_ref, kc_ref, kn_ref, vp_ref, vc_ref, vn_ref, bias_ref,
                 o_ref, *, seq, n_kv):
    blk = ATTN_BLOCK
    n = pl.program_id(1)
    row = lax.broadcasted_iota(I32, (blk, 3 * blk), 0)
    col = lax.broadcasted_iota(I32, (blk, 3 * blk), 1)
    rel = col - blk - row
    kpos = n * blk - blk + col
    valid1 = (jnp.abs(rel) <= blk) & (kpos >= 0) & (kpos < seq)
    valid = jnp.concatenate([valid1] * KV_GROUP, axis=0)
    scale = HEAD_DIM ** -0.5
    q = q_ref[0]
    for kh in range(n_kv):
        ksl = slice(kh * HEAD_DIM, (kh + 1) * HEAD_DIM)
        kband = jnp.concatenate([kp_ref[0, :, ksl], kc_ref[0, :, ksl], kn_ref[0, :, ksl]],
                                axis=0).astype(BF16)
        vband = jnp.concatenate([vp_ref[0, :, ksl], vc_ref[0, :, ksl], vn_ref[0, :, ksl]],
                                axis=0).astype(BF16)
        heads = [kh * KV_GROUP + g for g in range(KV_GROUP)]
        qs = jnp.concatenate([q[:, h * HEAD_DIM:(h + 1) * HEAD_DIM] for h in heads],
                             axis=0).astype(BF16)
        s = lax.dot_general(qs, kband, (((1,), (1,)), ((), ())),
                            preferred_element_type=F32) * scale
        bias = bias_ref[kh * KV_GROUP * blk:(kh + 1) * KV_GROUP * blk, :]
        s = jnp.where(valid, s + bias, NEG_INF)
        sink = jnp.concatenate([jnp.full((blk, 1), sink_ref[h], F32) for h in heads], axis=0)
        m = jnp.maximum(jnp.max(s, axis=-1, keepdims=True), sink)
        p = jnp.exp(s - m)
        denom = jnp.sum(p, axis=-1, keepdims=True) + jnp.exp(sink - m)
        probs = (p / denom).astype(BF16)
        o = jnp.dot(probs, vband, preferred_element_type=F32)
        for g, h in enumerate(heads):
            o_ref[0, :, h * HEAD_DIM:(h + 1) * HEAD_DIM] = o[g * blk:(g + 1) * blk].astype(o_ref.dtype)


def windowed_attention(proj, sink, bias, *, attn_w, kv_w):
    b, s, _ = proj.shape
    blk = ATTN_BLOCK
    nblk = s // blk
    n_kv = kv_w // HEAD_DIM
    kcol = attn_w // kv_w
    assert attn_w % kv_w == 0 and n_kv * KV_GROUP * HEAD_DIM == attn_w

    def kv_spec(col, d):
        return pl.BlockSpec(
            (1, blk, kv_w),
            lambda bi, n: (bi, jnp.clip(n + d, 0, nblk - 1), col))

    return pl.pallas_call(
        functools.partial(_attn_kernel, seq=s, n_kv=n_kv),
        grid=(b, nblk),
        in_specs=[
            pl.BlockSpec(memory_space=pltpu.SMEM),
            pl.BlockSpec((1, blk, attn_w), lambda bi, n: (bi, n, 0)),
            kv_spec(kcol, -1), kv_spec(kcol, 0), kv_spec(kcol, 1),
            kv_spec(kcol + 1, -1), kv_spec(kcol + 1, 0), kv_spec(kcol + 1, 1),
            pl.BlockSpec(bias.shape, lambda bi, n: (0, 0)),
        ],
        out_specs=pl.BlockSpec((1, blk, attn_w), lambda bi, n: (bi, n, 0)),
        out_shape=jax.ShapeDtypeStruct((b, s, attn_w), BF16),
        compiler_params=_params(("parallel", "parallel")),
        name="windowed_attention",
    )(sink.astype(F32), proj, proj, proj, proj, proj, proj, proj, bias)


def _s5_operands(a_re, a_im, log_dt, b_re, b_im, c_re, c_im, chunk, n_levels):
    ell = chunk
    p = a_re.shape[-1]
    dt = jnp.exp(log_dt)[..., None]
    mag = jnp.exp(a_re * dt)
    lb_re = mag * jnp.cos(a_im * dt)
    lb_im = mag * jnp.sin(a_im * dt)
    den = a_re * a_re + a_im * a_im
    nr = lb_re - 1.0
    coef_re = (nr * a_re + lb_im * a_im) / den
    coef_im = (lb_im * a_re - nr * a_im) / den
    bb_re = coef_re[..., None] * b_re - coef_im[..., None] * b_im
    bb_im = coef_re[..., None] * b_im + coef_im[..., None] * b_re
    bt_re, bt_im = bb_re.transpose(0, 1, 3, 2), bb_im.transpose(0, 1, 3, 2)

    def powers(tau):
        t = tau.astype(F32)[None, None, :, None]
        pm = jnp.exp((a_re * dt)[:, :, None, :] * t)
        ang = (a_im * dt)[:, :, None, :] * t
        return pm * jnp.cos(ang), pm * jnp.sin(ang)

    lp = lambda t: jnp.pad(t, [(0, 0)] * (t.ndim - 1) + [(0, LANES - p)])
    cat = lambda *ts: jnp.concatenate(ts, axis=-1)
    ii = jnp.arange(ell)
    dn_re, dn_im = powers(ell - 1 - ii)
    up_re, up_im = powers(ii)
    e1_re, e1_im = powers(ii + 1)
    e2_re, e2_im = powers(ell - ii)
    lv_re, lv_im = powers(ell * (2 ** jnp.arange(n_levels)))
    lev = SUBLANES * pl.cdiv(n_levels, SUBLANES)
    lvp = lambda t: jnp.pad(lp(t), ((0, 0), (0, lev - n_levels), (0, 0)))
    wide = jnp.concatenate([
        cat(lp(dn_re[0]), lp(dn_re[0]), lp(up_re[1]), lp(up_re[1])),
        cat(lp(dn_im[0]), lp(dn_im[0]), lp(up_im[1]), lp(up_im[1])),
        cat(lp(e1_re[0]), lp(e1_re[0]), lp(e2_re[1]), lp(e2_re[1])),
        cat(lp(e1_im[0]), lp(e1_im[0]), lp(e2_im[1]), lp(e2_im[1])),
        cat(lp(bt_re[0]), lp(bt_im[0]), lp(bt_re[1]), lp(bt_im[1])),
        cat(-lp(bt_im[0]), lp(bt_re[0]), -lp(bt_im[1]), lp(bt_re[1])),
        cat(lp(c_re[0]), -lp(c_im[0]), lp(c_re[1]), -lp(c_im[1])),
        cat(-lp(c_im[0]), -lp(c_re[0]), -lp(c_im[1]), -lp(c_re[1])),
        cat(lvp(lv_re[0]), lvp(lv_re[0]), lvp(lv_re[1]), lvp(lv_re[1])),
        cat(-lvp(lv_im[0]), lvp(lv_im[0]), -lvp(lv_im[1]), lvp(lv_im[1])),
    ], axis=1)

    def lag_rows(fwd, bwd):
        f = jnp.pad(lp(fwd), ((0, 0), (ell - 1, 1), (0, 0)))
        b = jnp.pad(lp(bwd), ((0, 0), (0, ell), (0, 0)))
        return cat(f, b)

    narrow = jnp.concatenate([
        lag_rows(up_re[0], dn_re[1]), lag_rows(up_im[0], dn_im[1]),
        cat(lp(c_re[0]), lp(c_re[1])), cat(lp(c_im[0]), lp(c_im[1])),
        cat(lp(bt_re[0]), lp(bt_re[1])), cat(lp(bt_im[0]), lp(bt_im[1])),
    ], axis=1)
    return wide, narrow


def _shift_rows(z, r):
    rows = z.shape[0]
    rolled = pltpu.roll(z, r % rows, 0)
    ridx = lax.broadcasted_iota(I32, z.shape, 0)
    keep = (ridx >= r) if r > 0 else (ridx < rows + r)
    return jnp.where(keep, rolled, 0.0)


def _rep_rows(x, reps):
    return jnp.concatenate([jnp.broadcast_to(x[j:j + 1, :], (reps, x.shape[1]))
                            for j in range(x.shape[0])], axis=0)


def _s5_kernel(u_ref, wide_ref, narrow_ref, o_ref, *, rows_per_chunk, n_levels, chunk, h):
    ell, lh = chunk, chunk * h
    lev = SUBLANES * pl.cdiv(n_levels, SUBLANES)
    nt = (((1,), (1,)), ((), ()))
    w = wide_ref[0]
    nr = narrow_ref[0]
    pwf_re, pwf_im, pwe_re, pwe_im = (w[k * ell:(k + 1) * ell] for k in range(4))
    o = 4 * ell
    bb1, bb2, cc1, cc2 = (w[o + k * h:o + (k + 1) * h] for k in range(4))
    o += 4 * h
    pa_all, pb_all = w[o:o + lev], w[o + lev:o + 2 * lev]
    tile = lambda t, n: jnp.concatenate([t] * n, axis=0)
    f_tab = (_rep_rows(pwf_re, h) * tile(bb1, ell) + _rep_rows(pwf_im, h) * tile(bb2, ell)).astype(BF16)
    et_tab = (_rep_rows(pwe_re, h) * tile(cc1, ell) + _rep_rows(pwe_im, h) * tile(cc2, ell)).astype(BF16)
    pwm_re, pwm_im = nr[:2 * ell], nr[2 * ell:4 * ell]
    o = 4 * ell
    cr, ci, br, bi = (nr[o + k * h:o + (k + 1) * h] for k in range(4))
    pr, pi = _rep_rows(pwm_re, h), _rep_rows(pwm_im, h)
    crt, cit = tile(cr, 2 * ell), tile(ci, 2 * ell)
    hi = lax.Precision.HIGHEST
    strip = (lax.dot_general(br, pr * crt - pi * cit, nt, precision=hi, preferred_element_type=F32)
             - lax.dot_general(bi, pr * cit + pi * crt, nt, precision=hi, preferred_element_type=F32))
    t_tab = jnp.concatenate([strip[:, (ell - 1 - j) * h:(ell - 1 - j) * h + lh] for j in range(ell)],
                            axis=0).astype(BF16)

    u = u_ref[0]
    z = jnp.dot(u, f_tab, preferred_element_type=F32)
    half = 2 * LANES

    def swap(x):
        return jnp.concatenate([x[:, LANES:], x[:, :LANES]], axis=1)

    zf, zb = z[:, :half], z[:, half:]
    for k in range(n_levels):
        r = rows_per_chunk * (2 ** k)
        pa = pa_all[k:k + 1, :]
        pb = pb_all[k:k + 1, :]
        sf = _shift_rows(zf, r)
        sb = _shift_rows(zb, -r)
        zf = zf + pa[:, :half] * sf + pb[:, :half] * swap(sf)
        zb = zb + pa[:, half:] * sb + pb[:, half:] * swap(sb)
    hin = jnp.concatenate([_shift_rows(zf, rows_per_chunk), _shift_rows(zb, -rows_per_chunk)],
                          axis=1).astype(BF16)
    y = jnp.dot(u, t_tab, preferred_element_type=F32)
    y = y + lax.dot_general(hin, et_tab, nt, preferred_element_type=F32)
    o_ref[0] = y


def s5_scan(u, operands, *, chunk, n_levels):
    wide, narrow = operands
    b, s, w = u.shape
    g = wide.shape[0]
    h = w // g
    n = s // chunk
    lh = chunk * h
    rows = n * b
    ut = u.astype(BF16).reshape(b, n, chunk, g, h).transpose(3, 1, 0, 2, 4).reshape(g, rows, lh)
    y = pl.pallas_call(
        functools.partial(_s5_kernel, rows_per_chunk=b, n_levels=n_levels, chunk=chunk, h=h),
        grid=(g,),
        in_specs=[
            pl.BlockSpec((1, rows, lh), lambda i: (i, 0, 0)),
            pl.BlockSpec((1,) + wide.shape[1:], lambda i: (i, 0, 0)),
            pl.BlockSpec((1,) + narrow.shape[1:], lambda i: (i, 0, 0)),
        ],
        out_specs=pl.BlockSpec((1, rows, lh), lambda i: (i, 0, 0)),
        out_shape=jax.ShapeDtypeStruct((g, rows, lh), F32),
        compiler_params=_params(("parallel",)),
        name="s5_scan",
    )(ut, wide, narrow)
    return y.reshape(g, n, b, chunk, h).transpose(2, 1, 3, 0, 4).reshape(b, s, w)


def _glu_kernel(y_ref, u_ref, d_ref, w_ref, b_ref, o_ref):
    y = y_ref[...] + d_ref[...] * u_ref[...]
    g = _gelu(y)
    z = jnp.dot(g.astype(BF16), w_ref[...], preferred_element_type=F32) + b_ref[...]
    o_ref[...] = (g * jax.nn.sigmoid(z)).astype(o_ref.dtype)


def s5_glu(y, u, d_skip, glu_w_bf16, glu_b, *, tm=512):
    m, w = y.shape
    tm = _tile(m, tm)
    return pl.pallas_call(
        _glu_kernel,
        grid=(m // tm,),
        in_specs=[
            pl.BlockSpec((tm, w), lambda i: (i, 0)),
            pl.BlockSpec((tm, w), lambda i: (i, 0)),
            pl.BlockSpec((1, w), lambda i: (0, 0)),
            pl.BlockSpec((w, w), lambda i: (0, 0)),
            pl.BlockSpec((1, w), lambda i: (0, 0)),
        ],
        out_specs=pl.BlockSpec((tm, w), lambda i: (i, 0)),
        out_shape=jax.ShapeDtypeStruct((m, w), BF16),
        compiler_params=_params(("parallel",)),
        name="s5_glu",
    )(y, u, d_skip.reshape(1, w), glu_w_bf16, glu_b.reshape(1, w))


def _sgu_kernel(z_ref, g_ref, b_ref, ws_ref, bs_ref, o_ref, *, width, n_heads):
    hd = width // n_heads
    z = z_ref[...]
    u, v = z[:, :width], z[:, width:]
    mu = jnp.mean(v, axis=-1, keepdims=True)
    var = jnp.mean(jnp.square(v - mu), axis=-1, keepdims=True)
    vn = ((v - mu) * lax.rsqrt(var + EPS) * g_ref[...] + b_ref[...]).astype(BF16)
    for h in range(n_heads):
        sl = slice(h * hd, (h + 1) * hd)
        mixed = jnp.dot(ws_ref[h], vn[:, sl], preferred_element_type=F32) + bs_ref[:, h:h + 1]
        o_ref[:, sl] = (u[:, sl] * mixed).astype(o_ref.dtype)


def gmlp_sgu(z, ln_g, ln_b, w_s_bf16, b_s):
    m, w2 = z.shape
    width = w2 // 2
    n_heads = w_s_bf16.shape[0]
    ck = GMLP_CHUNK
    return pl.pallas_call(
        functools.partial(_sgu_kernel, width=width, n_heads=n_heads),
        grid=(m // ck,),
        in_specs=[
            pl.BlockSpec((ck, w2), lambda i: (i, 0)),
            pl.BlockSpec((1, width), lambda i: (0, 0)),
            pl.BlockSpec((1, width), lambda i: (0, 0)),
            pl.BlockSpec((n_heads, ck, ck), lambda i: (0, 0, 0)),
            pl.BlockSpec((ck, n_heads), lambda i: (0, 0)),
        ],
        out_specs=pl.BlockSpec((ck, width), lambda i: (i, 0)),
        out_shape=jax.ShapeDtypeStruct((m, width), BF16),
        compiler_params=_params(("parallel",)),
        name="gmlp_sgu",
    )(z, ln_g.reshape(1, width), ln_b.reshape(1, width), w_s_bf16, b_s.T.astype(F32))


def _slab_pitch(half):
    return SUBLANES * pl.cdiv(half // LANES, SUBLANES)


def _router_kernel(x_ref, g_ref, rt_ref, aff_ref, slab_ref, *, tm, half, pitch):
    h = _rms(x_ref[0], g_ref[...])
    logits = lax.dot_general(rt_ref[...], h, (((1,), (1,)), ((), ())),
                             precision=lax.Precision.HIGHEST,
                             preferred_element_type=F32)
    mx = jnp.max(logits, axis=0, keepdims=True)
    ex = jnp.exp(logits - mx)
    aff_ref[0] = ex / jnp.sum(ex, axis=0, keepdims=True)
    lo = lax.bitcast_convert_type(h[:, :half].astype(BF16).astype(F32), U32) >> 16
    hi = lax.bitcast_convert_type(h[:, half:].astype(BF16).astype(F32), U32) & jnp.uint32(0xFFFF0000)
    word = lo | hi
    for s in range(half // LANES):
        slab_ref[pl.ds(s, tm, stride=pitch), :] = word[:, s * LANES:(s + 1) * LANES]
    if pitch > half // LANES:
        for s in range(half // LANES, pitch):
            slab_ref[pl.ds(s, tm, stride=pitch), :] = jnp.zeros((tm, LANES), U32)


def moe_router(x, g, router, *, tm=512):
    b, s, d = x.shape
    e = router.shape[1]
    half = d // 2
    pitch = _slab_pitch(half)
    tm = _tile(s, tm)
    nt = s // tm
    return pl.pallas_call(
        functools.partial(_router_kernel, tm=tm, half=half, pitch=pitch),
        grid=(b, nt),
        in_specs=[
            pl.BlockSpec((1, tm, d), lambda bi, j: (bi, j, 0)),
            pl.BlockSpec((1, d), lambda bi, j: (0, 0)),
            pl.BlockSpec((e, d), lambda bi, j: (0, 0)),
        ],
        out_specs=[
            pl.BlockSpec((1, e, tm), lambda bi, j: (bi, 0, j)),
            pl.BlockSpec((tm * pitch, LANES), lambda bi, j: (bi * nt + j, 0)),
        ],
        out_shape=[jax.ShapeDtypeStruct((b, e, s), F32),
                   jax.ShapeDtypeStruct((b * s * pitch, LANES), U32)],
        compiler_params=_params(("parallel", "parallel")),
        name="moe_router",
    )(x, g.reshape(1, d), router.T.astype(F32))


def _lane_cumsum(m_bf16, tri):
    e, s = m_bf16.shape
    off = jnp.zeros((e, 1), F32)
    out = []
    for j in range(s // LANES):
        c = jnp.dot(m_bf16[:, j * LANES:(j + 1) * LANES], tri, preferred_element_type=F32) + off
        out.append(c)
        off = c[:, LANES - 1:LANES]
    return jnp.concatenate(out, axis=1)


def _topk_kernel(aff_ref, idx_ref, gate_ref, *, cap, kchunk):
    aff = aff_ref[0]
    e, s = aff.shape
    bits = lax.bitcast_convert_type(aff, I32)

    def count_ge(t):
        return jnp.sum((bits >= t).astype(F32), axis=1, keepdims=True)

    def body(_, carry):
        lo, hi = carry
        mid = lo + ((hi - lo) >> 1)
        ok = count_ge(mid) >= cap
        return jnp.where(ok, mid, lo), jnp.where(ok, hi, mid)

    lo0 = jnp.zeros((e, 1), I32)
    hi0 = jnp.full((e, 1), 0x7F800000, I32)
    thr, _ = lax.fori_loop(0, 32, body, (lo0, hi0))
    gt = bits > thr
    eq = bits == thr
    need = cap - jnp.sum(gt.astype(F32), axis=1, keepdims=True)
    tri = (lax.broadcasted_iota(I32, (LANES, LANES), 0)
           <= lax.broadcasted_iota(I32, (LANES, LANES), 1)).astype(BF16)
    eq_rank = _lane_cumsum(jnp.where(eq, 1.0, 0.0).astype(BF16), tri)
    sel = gt | (eq & (eq_rank <= need))
    pos = _lane_cumsum(jnp.where(sel, 1.0, 0.0).astype(BF16), tri)
    pos = jnp.where(sel, pos, 0.0)

    tok = lax.broadcasted_iota(I32, (1, s), 1)
    t_hi = (tok >> 6).astype(F32)
    t_lo = (tok & 63).astype(F32)
    slot = lax.broadcasted_iota(I32, (cap, kchunk), 0).astype(F32) + 1.0
    lrow = lax.broadcasted_iota(I32, (SUBLANES, s), 0)
    for ei in range(e):
        a = aff[ei:ei + 1, :]
        g1 = a.astype(BF16).astype(F32)
        g2 = (a - g1).astype(BF16).astype(F32)
        g3 = (a - g1) - g2
        lhs = jnp.where(lrow == 0, t_hi, jnp.where(lrow == 1, t_lo, jnp.where(
            lrow == 2, g1, jnp.where(lrow == 3, g2, jnp.where(lrow == 4, g3, 0.0)))))
        lhs = lhs.astype(BF16)
        acc = jnp.zeros((SUBLANES, cap), F32)
        for c0 in range(0, s, kchunk):
            onehot = jnp.where(pos[ei:ei + 1, c0:c0 + kchunk] == slot, 1.0, 0.0).astype(BF16)
            acc = acc + lax.dot_general(lhs[:, c0:c0 + kchunk], onehot, (((1,), (1,)), ((), ())),
                                        preferred_element_type=F32)
        idx_ref[0, ei:ei + 1, :] = (acc[0:1] * 64.0 + acc[1:2]).astype(I32)
        gate_ref[0, ei:ei + 1, :] = acc[2:3] + acc[3:4] + acc[4:5]


def moe_topk(aff, cap):
    b, e, s = aff.shape
    return pl.pallas_call(
        functools.partial(_topk_kernel, cap=cap, kchunk=_tile(s, 1024)),
        grid=(b,),
        in_specs=[pl.BlockSpec((1, e, s), lambda i: (i, 0, 0))],
        out_specs=[pl.BlockSpec((1, e, cap), lambda i: (i, 0, 0)),
                   pl.BlockSpec((1, e, cap), lambda i: (i, 0, 0))],
        out_shape=[jax.ShapeDtypeStruct((b, e, cap), I32),
                   jax.ShapeDtypeStruct((b, e, cap), F32)],
        compiler_params=_params(("parallel",)),
        name="moe_topk",
    )(aff)


def _gather_kernel(idx_ref, slab_ref, o_ref, rows_ref, *, cap, half, pitch, unroll):
    def body(j0, c):
        for u in range(unroll):
            j = j0 * unroll + u
            src = pl.multiple_of(idx_ref[0, 0, j] * pitch, pitch)
            rows_ref[pl.ds(pl.multiple_of(j * pitch, pitch), pitch), :] = slab_ref[pl.ds(src, pitch), :]
        return c

    lax.fori_loop(0, cap // unroll, body, 0)
    for s in range(half // LANES):
        word = rows_ref[pl.ds(s, cap, stride=pitch), :]
        lo = lax.bitcast_convert_type(word << 16, F32)
        hi = lax.bitcast_convert_type(word & jnp.uint32(0xFFFF0000), F32)
        o_ref[0, :, s * LANES:(s + 1) * LANES] = lo.astype(BF16)
        o_ref[0, :, half + s * LANES:half + (s + 1) * LANES] = hi.astype(BF16)


def moe_gather(idx, slabs, *, seq, d):
    b, e, cap = idx.shape
    half = d // 2
    pitch = _slab_pitch(half)
    return pl.pallas_call(
        functools.partial(_gather_kernel, cap=cap, half=half, pitch=pitch, unroll=8),
        grid=(b, e),
        in_specs=[
            pl.BlockSpec((1, 1, cap), lambda bi, ei: (bi * e + ei, 0, 0), memory_space=pltpu.SMEM),
            pl.BlockSpec((seq * pitch, LANES), lambda bi, ei: (bi, 0)),
        ],
        out_specs=pl.BlockSpec((1, cap, d), lambda bi, ei: (ei, bi, 0)),
        out_shape=jax.ShapeDtypeStruct((e, b * cap, d), BF16),
        scratch_shapes=[pltpu.VMEM((cap * pitch, LANES), U32)],
        compiler_params=_params(("parallel", "arbitrary")),
        name="moe_gather",
    )(idx.reshape(b * e, 1, cap), slabs)


def _ffn_kernel(xs_ref, w1_ref, w3_ref, w2_ref, o_ref, hid_ref, *, nt, tf):
    step = pl.program_id(1)

    @pl.when(step < nt)
    def _():
        xs = xs_ref[0]
        a = jnp.dot(xs, w1_ref[0, 0].astype(BF16), preferred_element_type=F32)
        g = jnp.dot(xs, w3_ref[0, 0].astype(BF16), preferred_element_type=F32)
        hid_ref[step] = (a * jax.nn.sigmoid(a) * g).astype(BF16)

    @pl.when(step >= nt)
    def _():
        acc = jnp.zeros(o_ref.shape[1:], F32)
        for k in range(nt):
            acc = acc + jnp.dot(hid_ref[k], w2_ref[0, 0, k * tf:(k + 1) * tf, :].astype(BF16),
                                preferred_element_type=F32)
        o_ref[0] = acc


def moe_ffn(xs, w1, w3, w2, layer, *, tf=256):
    e, m, d = xs.shape
    f = w1.shape[3]
    tf = _tile(f, tf)
    tn = _tile(d, tf)
    nt = f // tf
    nn = d // tn
    return pl.pallas_call(
        functools.partial(_ffn_kernel, nt=nt, tf=tf),
        grid=(e, nt + nn),
        in_specs=[
            pl.BlockSpec((1, m, d), lambda ei, t: (ei, 0, 0)),
            pl.BlockSpec((1, 1, d, tf), lambda ei, t: (layer, ei, 0, jnp.minimum(t, nt - 1))),
            pl.BlockSpec((1, 1, d, tf), lambda ei, t: (layer, ei, 0, jnp.minimum(t, nt - 1))),
            pl.BlockSpec((1, 1, f, tn), lambda ei, t: (layer, ei, 0, jnp.maximum(t - nt, 0))),
        ],
        out_specs=pl.BlockSpec((1, m, tn), lambda ei, t: (ei, 0, jnp.maximum(t - nt, 0))),
        out_shape=jax.ShapeDtypeStruct((e, m, d), F32),
        scratch_shapes=[pltpu.VMEM((nt, m, tf), BF16)],
        compiler_params=_params(("parallel", "arbitrary")),
        name="moe_ffn",
    )(xs, w1, w3, w2)


def _combine_kernel(idx_ref, gate_ref, ys_ref, o_ref, rows_ref, *, cap, half, pitch, unroll):
    ei = pl.program_id(2)

    @pl.when(ei == 0)
    def _():
        o_ref[...] = jnp.zeros(o_ref.shape, F32)

    for s in range(half // LANES):
        rows_ref[pl.ds(s, cap, stride=pitch), :] = ys_ref[0, :, s * LANES:(s + 1) * LANES]

    def body(j0, c):
        dsts, vals = [], []
        for u in range(unroll):
            j = j0 * unroll + u
            dst = pl.multiple_of(idx_ref[0, 0, j] * pitch, pitch)
            src = pl.multiple_of(j * pitch, pitch)
            vals.append(o_ref[0, 0, pl.ds(dst, pitch), :]
                        + gate_ref[0, 0, j] * rows_ref[pl.ds(src, pitch), :])
            dsts.append(dst)
        for dst, val in zip(dsts, vals):
            o_ref[0, 0, pl.ds(dst, pitch), :] = val
        return c

    lax.fori_loop(0, cap // unroll, body, 0)


def moe_combine(idx, gate, ys, *, seq, d):
    b, e, cap = idx.shape
    half = d // 2
    pitch = _slab_pitch(half)
    return pl.pallas_call(
        functools.partial(_combine_kernel, cap=cap, half=half, pitch=pitch, unroll=8),
        grid=(b, 2, e),
        in_specs=[
            pl.BlockSpec((1, 1, cap), lambda bi, dh, ei: (bi * e + ei, 0, 0), memory_space=pltpu.SMEM),
            pl.BlockSpec((1, 1, cap), lambda bi, dh, ei: (bi * e + ei, 0, 0), memory_space=pltpu.SMEM),
            pl.BlockSpec((1, cap, half), lambda bi, dh, ei: (ei, bi, dh)),
        ],
        out_specs=pl.BlockSpec((1, 1, seq * pitch, LANES), lambda bi, dh, ei: (bi, dh, 0, 0)),
        out_shape=jax.ShapeDtypeStruct((b, 2, seq * pitch, LANES), F32),
        scratch_shapes=[pltpu.VMEM((cap * pitch, LANES), F32)],
        compiler_params=_params(("parallel", "parallel", "arbitrary")),
        name="moe_combine",
    )(idx.reshape(b * e, 1, cap), gate.reshape(b * e, 1, cap), ys)


def _unslab_kernel(*refs, tm, half, pitch, final_norm):
    x_ref, s0_ref, s1_ref = refs[:3]
    o_ref = refs[-1]
    parts = []
    for s_ref in (s0_ref, s1_ref):
        for s in range(half // LANES):
            parts.append(s_ref[0, 0, pl.ds(s, tm, stride=pitch), :])
    y = x_ref[0] + jnp.concatenate(parts, axis=1)
    if final_norm:
        y = _rms(y, refs[3][...])
    o_ref[0] = y


def moe_residual(x, moe_slabs, final_g=None, *, tm=256):
    b, s, d = x.shape
    half = d // 2
    pitch = _slab_pitch(half)
    tm = _tile(s, tm)
    final_norm = final_g is not None
    in_specs = [
        pl.BlockSpec((1, tm, d), lambda bi, j: (bi, j, 0)),
        pl.BlockSpec((1, 1, tm * pitch, LANES), lambda bi, j: (bi, 0, j, 0)),
        pl.BlockSpec((1, 1, tm * pitch, LANES), lambda bi, j: (bi, 1, j, 0)),
    ]
    args = [x, moe_slabs, moe_slabs]
    if final_norm:
        in_specs.append(pl.BlockSpec((1, d), lambda bi, j: (0, 0)))
        args.append(final_g.reshape(1, d))
    return pl.pallas_call(
        functools.partial(_unslab_kernel, tm=tm, half=half, pitch=pitch, final_norm=final_norm),
        grid=(b, s // tm),
        in_specs=in_specs,
        out_specs=pl.BlockSpec((1, tm, d), lambda bi, j: (bi, j, 0)),
        out_shape=jax.ShapeDtypeStruct((b, s, d), F32),
        compiler_params=_params(("parallel", "parallel")),
        name="moe_residual",
    )(*args)


def expert_choice_layer(x, norm_g, router, w1, w3, w2, layer, final_g=None):
    b, s, d = x.shape
    cap = CAPACITY_FACTOR * s // N_EXPERTS
    aff, slabs = moe_router(x, norm_g, router)
    idx, gate = moe_topk(aff, cap)
    xs = moe_gather(idx, slabs, seq=s, d=d)
    ys = moe_ffn(xs, w1, w3, w2, layer)
    moe = moe_combine(idx, gate, ys, seq=s, d=d)
    return moe_residual(x, moe, final_g)


def kernel(x, rel_bias, mix_norm, ffn_norm, final_norm, even_w_in, attn_sink, ssm_a_re, ssm_a_im, ssm_log_dt, ssm_b_re, ssm_b_im, ssm_c_re, ssm_c_im, ssm_d, glu_w, glu_b, even_w_out, odd_w_in, sgu_ln_g, sgu_ln_b, sgu_w, sgu_b, odd_w_out, router, moe_w1, moe_w3, moe_w2):
    b, s, d = x.shape
    t = b * s
    depth = mix_norm.shape[0]
    ssm_w = ssm_d.shape[-1]
    attn_w = even_w_out.shape[1] - ssm_w
    kv_w = (even_w_in.shape[-1] - attn_w - ssm_w) // 2
    bias = attention_bias(rel_bias)

    for layer in range(depth):
        i = layer // 2
        x2 = x.reshape(t, d)
        if layer % 2 == 0:
            proj = norm_matmul(x2, mix_norm[layer], even_w_in[i].astype(BF16))
            attn = windowed_attention(proj.reshape(b, s, -1), attn_sink[i], bias,
                                      attn_w=attn_w, kv_w=kv_w)
            u = proj[:, attn_w + 2 * kv_w:]
            n_levels = max(1, (s // S5_CHUNK - 1).bit_length())
            operands = _s5_operands(ssm_a_re[i], ssm_a_im[i], ssm_log_dt[i], ssm_b_re[i], ssm_b_im[i],
                                    ssm_c_re[i], ssm_c_im[i], S5_CHUNK, n_levels)
            y = s5_scan(u.reshape(b, s, ssm_w), operands, chunk=S5_CHUNK, n_levels=n_levels)
            ssm = s5_glu(y.reshape(t, ssm_w), u, ssm_d[i], glu_w[i].astype(BF16), glu_b[i])
            x2 = matmul_residual([attn.reshape(t, attn_w), ssm], even_w_out[i].astype(BF16), x2)
        else:
            z = norm_matmul(x2, mix_norm[layer], odd_w_in[i].astype(BF16), act="gelu")
            gated = gmlp_sgu(z, sgu_ln_g[i], sgu_ln_b[i], sgu_w[i].astype(BF16), sgu_b[i])
            x2 = matmul_residual([gated], odd_w_out[i].astype(BF16), x2)
        x = expert_choice_layer(x2.reshape(b, s, d), ffn_norm[layer], router[layer], moe_w1, moe_w3,
                                moe_w2, layer, final_norm if layer == depth - 1 else None)
    return x
```

```python
import functools
import math

import jax
import jax.numpy as jnp
from jax import lax
from jax.experimental import pallas as pl
from jax.experimental.pallas import tpu as pltpu

F32 = jnp.float32
BF16 = jnp.bfloat16
I32 = jnp.int32
U32 = jnp.uint32

EPS = 1e-6
NEG_INF = -1e30

LANES = 128
SUBLANES = 8
VMEM_LIMIT = 56 << 20

ATTN_BLOCK = 128
HEAD_DIM = 128
KV_GROUP = 4
REL_BUCKETS = 32
REL_MAX_DIST = 128
SSM_GROUP = 16
SSM_STATE = 64
S5_CHUNK = 32
GMLP_CHUNK = 128
N_EXPERTS = 16
CAPACITY_FACTOR = 2


def _params(sem, vmem=VMEM_LIMIT):
    return pltpu.CompilerParams(dimension_semantics=sem, vmem_limit_bytes=vmem)


def _tile(n, want):
    t = min(n, want)
    while n % t:
        t //= 2
    return t


def _rms(x, g):
    ms = jnp.mean(x * x, axis=-1, keepdims=True)
    return x * lax.rsqrt(ms + EPS) * g


def _gelu(x):
    c = math.sqrt(2.0 / math.pi)
    return x * (0.5 * (1.0 + jnp.tanh(c * (x + 0.044715 * (x * x * x)))))


def _norm_mm_kernel(x_ref, g_ref, w_ref, o_ref, hn_ref, *, act):
    @pl.when(pl.program_id(1) == 0)
    def _():
        hn_ref[...] = _rms(x_ref[...], g_ref[...]).astype(BF16)

    y = jnp.dot(hn_ref[...], w_ref[...], preferred_element_type=F32)
    if act == "gelu":
        y = _gelu(y)
    o_ref[...] = y.astype(o_ref.dtype)


def norm_matmul(x, g, w_bf16, *, act=None, out_dtype=F32, tm=512, tn=2560):
    m, k = x.shape
    n = w_bf16.shape[1]
    tm, tn = _tile(m, tm), _tile(n, tn)
    return pl.pallas_call(
        functools.partial(_norm_mm_kernel, act=act),
        grid=(m // tm, n // tn),
        in_specs=[
            pl.BlockSpec((tm, k), lambda i, j: (i, 0)),
            pl.BlockSpec((1, k), lambda i, j: (0, 0)),
            pl.BlockSpec((k, tn), lambda i, j: (0, j)),
        ],
        out_specs=pl.BlockSpec((tm, tn), lambda i, j: (i, j)),
        out_shape=jax.ShapeDtypeStruct((m, n), out_dtype),
        scratch_shapes=[pltpu.VMEM((tm, k), BF16)],
        compiler_params=_params(("parallel", "arbitrary")),
        name="norm_matmul",
    )(x, g.reshape(1, k), w_bf16)


def _mm_res_kernel(*refs, n_lhs):
    lhs = refs[:n_lhs]
    ws = refs[n_lhs:2 * n_lhs]
    res_ref, o_ref = refs[2 * n_lhs], refs[2 * n_lhs + 1]
    acc = res_ref[...]
    for l_ref, w_ref in zip(lhs, ws):
        acc = acc + jnp.dot(l_ref[...], w_ref[...], preferred_element_type=F32)
    o_ref[...] = acc


def matmul_residual(lhs_list, w_bf16, res, *, tm=512, tn=2048):
    m, n = res.shape
    kk = lhs_list[0].shape[1]
    n_lhs = len(lhs_list)
    assert all(l.shape == (m, kk) for l in lhs_list) and w_bf16.shape == (n_lhs * kk, n)
    tm, tn = _tile(m, tm), _tile(n, tn)
    in_specs = [pl.BlockSpec((tm, kk), lambda i, j: (i, 0)) for _ in lhs_list]
    in_specs += [pl.BlockSpec((kk, tn), functools.partial(lambda i, j, r: (r, j), r=r))
                 for r in range(n_lhs)]
    in_specs += [pl.BlockSpec((tm, tn), lambda i, j: (i, j))]
    return pl.pallas_call(
        functools.partial(_mm_res_kernel, n_lhs=n_lhs),
        grid=(m // tm, n // tn),
        in_specs=in_specs,
        out_specs=pl.BlockSpec((tm, tn), lambda i, j: (i, j)),
        out_shape=jax.ShapeDtypeStruct((m, n), F32),
        compiler_params=_params(("parallel", "parallel")),
        name="matmul_residual",
    )(*lhs_list, *([w_bf16] * n_lhs), res)


def _t5_bucket(rel):
    nb = REL_BUCKETS // 2
    max_exact = nb // 2
    base = jnp.where(rel > 0, nb, 0)
    n = jnp.abs(rel)
    nf = jnp.maximum(n, 1).astype(F32)
    large = max_exact + (jnp.log(nf / max_exact) / math.log(REL_MAX_DIST / max_exact)
                         * (nb - max_exact)).astype(I32)
    large = jnp.minimum(large, nb - 1)
    return base + jnp.where(n < max_exact, n, large)


def _bias_kernel(rb_ref, bucket_ref, o_ref, *, n_heads):
    bkt = bucket_ref[...]
    for h in range(n_heads):
        acc = jnp.zeros(bkt.shape, F32)
        for k in range(REL_BUCKETS):
            acc = jnp.where(bkt == k, rb_ref[k, h], acc)
        o_ref[h * ATTN_BLOCK:(h + 1) * ATTN_BLOCK, :] = acc


def attention_bias(rel_bias):
    n_heads = rel_bias.shape[1]
    q_off = jnp.arange(ATTN_BLOCK, dtype=I32)
    c_off = jnp.arange(3 * ATTN_BLOCK, dtype=I32)
    bucket = _t5_bucket(c_off[None, :] - ATTN_BLOCK - q_off[:, None])
    return pl.pallas_call(
        functools.partial(_bias_kernel, n_heads=n_heads),
        in_specs=[pl.BlockSpec(memory_space=pltpu.SMEM),
                  pl.BlockSpec(memory_space=pltpu.VMEM)],
        out_specs=pl.BlockSpec(memory_space=pltpu.VMEM),
        out_shape=jax.ShapeDtypeStruct((n_heads * ATTN_BLOCK, 3 * ATTN_BLOCK), F32),
        name="attention_bias",
    )(rel_bias.astype(F32), bucket)


def _attn_kernel(sink_ref, q_ref, kp_ref, kc_ref, kn_ref, vp_ref, vc_ref, vn_ref, bias_ref,
                 o_ref, *, seq, n_kv):
    blk = ATTN_BLOCK
    n = pl.program_id(1)
    row = lax.broadcasted_iota(I32, (blk, 3 * blk), 0)
    col = lax.broadcasted_iota(I32, (blk, 3 * blk), 1)
    rel = col - blk - row
    kpos = n * blk - blk + col
    valid1 = (jnp.abs(rel) <= blk) & (kpos >= 0) & (kpos < seq)
    valid = jnp.concatenate([valid1] * KV_GROUP, axis=0)
    scale = HEAD_DIM ** -0.5
    q = q_ref[0]
    for kh in range(n_kv):
        ksl = slice(kh * HEAD_DIM, (kh + 1) * HEAD_DIM)
        kband = jnp.concatenate([kp_ref[0, :, ksl], kc_ref[0, :, ksl], kn_ref[0, :, ksl]],
                                axis=0).astype(BF16)
        vband = jnp.concatenate([vp_ref[0, :, ksl], vc_ref[0, :, ksl], vn_ref[0, :, ksl]],
                                axis=0).astype(BF16)
        heads = [kh * KV_GROUP + g for g in range(KV_GROUP)]
        qs = jnp.concatenate([q[:, h * HEAD_DIM:(h + 1) * HEAD_DIM] for h in heads],
                             axis=0).astype(BF16)
        s = lax.dot_general(qs, kband, (((1,), (1,)), ((), ())),
                            preferred_element_type=F32) * scale
        bias = bias_ref[kh * KV_GROUP * blk:(kh + 1) * KV_GROUP * blk, :]
        s = jnp.where(valid, s + bias, NEG_INF)
        sink = jnp.concatenate([jnp.full((blk, 1), sink_ref[h], F32) for h in heads], axis=0)
        m = jnp.maximum(jnp.max(s, axis=-1, keepdims=True), sink)
        p = jnp.exp(s - m)
        denom = jnp.sum(p, axis=-1, keepdims=True) + jnp.exp(sink - m)
        probs = (p / denom).astype(BF16)
        o = jnp.dot(probs, vband, preferred_element_type=F32)
        for g, h in enumerate(heads):
            o_ref[0, :, h * HEAD_DIM:(h + 1) * HEAD_DIM] = o[g * blk:(g + 1) * blk].astype(o_ref.dtype)


def windowed_attention(proj, sink, bias, *, attn_w, kv_w):
    b, s, _ = proj.shape
    blk = ATTN_BLOCK
    nblk = s // blk
    n_kv = kv_w // HEAD_DIM
    kcol = attn_w // kv_w
    assert attn_w % kv_w == 0 and n_kv * KV_GROUP * HEAD_DIM == attn_w

    def kv_spec(col, d):
        return pl.BlockSpec(
            (1, blk, kv_w),
            lambda bi, n: (bi, jnp.clip(n + d, 0, nblk - 1), col))

    return pl.pallas_call(
        functools.partial(_attn_kernel, seq=s, n_kv=n_kv),
        grid=(b, nblk),
        in_specs=[
            pl.BlockSpec(memory_space=pltpu.SMEM),
            pl.BlockSpec((1, blk, attn_w), lambda bi, n: (bi, n, 0)),
            kv_spec(kcol, -1), kv_spec(kcol, 0), kv_spec(kcol, 1),
            kv_spec(kcol + 1, -1), kv_spec(kcol + 1, 0), kv_spec(kcol + 1, 1),
            pl.BlockSpec(bias.shape, lambda bi, n: (0, 0)),
        ],
        out_specs=pl.BlockSpec((1, blk, attn_w), lambda bi, n: (bi, n, 0)),
        out_shape=jax.ShapeDtypeStruct((b, s, attn_w), BF16),
        compiler_params=_params(("parallel", "parallel")),
        name="windowed_attention",
    )(sink.astype(F32), proj, proj, proj, proj, proj, proj, proj, bias)


def _s5_operands(a_re, a_im, log_dt, b_re, b_im, c_re, c_im, chunk, n_levels):
    ell = chunk
    p = a_re.shape[-1]
    dt = jnp.exp(log_dt)[..., None]
    mag = jnp.exp(a_re * dt)
    lb_re = mag * jnp.cos(a_im * dt)
    lb_im = mag * jnp.sin(a_im * dt)
    den = a_re * a_re + a_im * a_im
    nr = lb_re - 1.0
    coef_re = (nr * a_re + lb_im * a_im) / den
    coef_im = (lb_im * a_re - nr * a_im) / den
    bb_re = coef_re[..., None] * b_re - coef_im[..., None] * b_im
    bb_im = coef_re[..., None] * b_im + coef_im[..., None] * b_re
    bt_re, bt_im = bb_re.transpose(0, 1, 3, 2), bb_im.transpose(0, 1, 3, 2)

    def powers(tau):
        t = tau.astype(F32)[None, None, :, None]
        pm = jnp.exp((a_re * dt)[:, :, None, :] * t)
        ang = (a_im * dt)[:, :, None, :] * t
        return pm * jnp.cos(ang), pm * jnp.sin(ang)

    lp = lambda t: jnp.pad(t, [(0, 0)] * (t.ndim - 1) + [(0, LANES - p)])
    cat = lambda *ts: jnp.concatenate(ts, axis=-1)
    ii = jnp.arange(ell)
    dn_re, dn_im = powers(ell - 1 - ii)
    up_re, up_im = powers(ii)
    e1_re, e1_im = powers(ii + 1)
    e2_re, e2_im = powers(ell - ii)
    lv_re, lv_im = powers(ell * (2 ** jnp.arange(n_levels)))
    lev = SUBLANES * pl.cdiv(n_levels, SUBLANES)
    lvp = lambda t: jnp.pad(lp(t), ((0, 0), (0, lev - n_levels), (0, 0)))
    wide = jnp.concatenate([
        cat(lp(dn_re[0]), lp(dn_re[0]), lp(up_re[1]), lp(up_re[1])),
        cat(lp(dn_im[0]), lp(dn_im[0]), lp(up_im[1]), lp(up_im[1])),
        cat(lp(e1_re[0]), lp(e1_re[0]), lp(e2_re[1]), lp(e2_re[1])),
        cat(lp(e1_im[0]), lp(e1_im[0]), lp(e2_im[1]), lp(e2_im[1])),
        cat(lp(bt_re[0]), lp(bt_im[0]), lp(bt_re[1]), lp(bt_im[1])),
        cat(-lp(bt_im[0]), lp(bt_re[0]), -lp(bt_im[1]), lp(bt_re[1])),
        cat(lp(c_re[0]), -lp(c_im[0]), lp(c_re[1]), -lp(c_im[1])),
        cat(-lp(c_im[0]), -lp(c_re[0]), -lp(c_im[1]), -lp(c_re[1])),
        cat(lvp(lv_re[0]), lvp(lv_re[0]), lvp(lv_re[1]), lvp(lv_re[1])),
        cat(-lvp(lv_im[0]), lvp(lv_im[0]), -lvp(lv_im[1]), lvp(lv_im[1])),
    ], axis=1)

    def lag_rows(fwd, bwd):
        f = jnp.pad(lp(fwd), ((0, 0), (ell - 1, 1), (0, 0)))
        b = jnp.pad(lp(bwd), ((0, 0), (0, ell), (0, 0)))
        return cat(f, b)

    narrow = jnp.concatenate([
        lag_rows(up_re[0], dn_re[1]), lag_rows(up_im[0], dn_im[1]),
        cat(lp(c_re[0]), lp(c_re[1])), cat(lp(c_im[0]), lp(c_im[1])),
        cat(lp(bt_re[0]), lp(bt_re[1])), cat(lp(bt_im[0]), lp(bt_im[1])),
    ], axis=1)
    return wide, narrow


def _shift_rows(z, r):
    rows = z.shape[0]
    rolled = pltpu.roll(z, r % rows, 0)
    ridx = lax.broadcasted_iota(I32, z.shape, 0)
    keep = (ridx >= r) if r > 0 else (ridx < rows + r)
    return jnp.where(keep, rolled, 0.0)


def _rep_rows(x, reps):
    return jnp.concatenate([jnp.broadcast_to(x[j:j + 1, :], (reps, x.shape[1]))
                            for j in range(x.shape[0])], axis=0)


def _s5_kernel(u_ref, wide_ref, narrow_ref, o_ref, *, rows_per_chunk, n_levels, chunk, h):
    ell, lh = chunk, chunk * h
    lev = SUBLANES * pl.cdiv(n_levels, SUBLANES)
    nt = (((1,), (1,)), ((), ()))
    w = wide_ref[0]
    nr = narrow_ref[0]
    pwf_re, pwf_im, pwe_re, pwe_im = (w[k * ell:(k + 1) * ell] for k in range(4))
    o = 4 * ell
    bb1, bb2, cc1, cc2 = (w[o + k * h:o + (k + 1) * h] for k in range(4))
    o += 4 * h
    pa_all, pb_all = w[o:o + lev], w[o + lev:o + 2 * lev]
    tile = lambda t, n: jnp.concatenate([t] * n, axis=0)
    f_tab = (_rep_rows(pwf_re, h) * tile(bb1, ell) + _rep_rows(pwf_im, h) * tile(bb2, ell)).astype(BF16)
    et_tab = (_rep_rows(pwe_re, h) * tile(cc1, ell) + _rep_rows(pwe_im, h) * tile(cc2, ell)).astype(BF16)
    pwm_re, pwm_im = nr[:2 * ell], nr[2 * ell:4 * ell]
    o = 4 * ell
    cr, ci, br, bi = (nr[o + k * h:o + (k + 1) * h] for k in range(4))
    pr, pi = _rep_rows(pwm_re, h), _rep_rows(pwm_im, h)
    crt, cit = tile(cr, 2 * ell), tile(ci, 2 * ell)
    hi = lax.Precision.HIGHEST
    strip = (lax.dot_general(br, pr * crt - pi * cit, nt, precision=hi, preferred_element_type=F32)
             - lax.dot_general(bi, pr * cit + pi * crt, nt, precision=hi, preferred_element_type=F32))
    t_tab = jnp.concatenate([strip[:, (ell - 1 - j) * h:(ell - 1 - j) * h + lh] for j in range(ell)],
                            axis=0).astype(BF16)

    u = u_ref[0]
    z = jnp.dot(u, f_tab, preferred_element_type=F32)
    half = 2 * LANES

    def swap(x):
        return jnp.concatenate([x[:, LANES:], x[:, :LANES]], axis=1)

    zf, zb = z[:, :half], z[:, half:]
    for k in range(n_levels):
        r = rows_per_chunk * (2 ** k)
        pa = pa_all[k:k + 1, :]
        pb = pb_all[k:k + 1, :]
        sf = _shift_rows(zf, r)
        sb = _shift_rows(zb, -r)
        zf = zf + pa[:, :half] * sf + pb[:, :half] * swap(sf)
        zb = zb + pa[:, half:] * sb + pb[:, half:] * swap(sb)
    hin = jnp.concatenate([_shift_rows(zf, rows_per_chunk), _shift_rows(zb, -rows_per_chunk)],
                          axis=1).astype(BF16)
    y = jnp.dot(u, t_tab, preferred_element_type=F32)
    y = y + lax.dot_general(hin, et_tab, nt, preferred_element_type=F32)
    o_ref[0] = y


def s5_scan(u, operands, *, chunk, n_levels):
    wide, narrow = operands
    b, s, w = u.shape
    g = wide.shape[0]
    h = w // g
    n = s // chunk
    lh = chunk * h
    rows = n * b
    ut = u.astype(BF16).reshape(b, n, chunk, g, h).transpose(3, 1, 0, 2, 4).reshape(g, rows, lh)
    y = pl.pallas_call(
        functools.partial(_s5_kernel, rows_per_chunk=b, n_levels=n_levels, chunk=chunk, h=h),
        grid=(g,),
        in_specs=[
            pl.BlockSpec((1, rows, lh), lambda i: (i, 0, 0)),
            pl.BlockSpec((1,) + wide.shape[1:], lambda i: (i, 0, 0)),
            pl.BlockSpec((1,) + narrow.shape[1:], lambda i: (i, 0, 0)),
        ],
        out_specs=pl.BlockSpec((1, rows, lh), lambda i: (i, 0, 0)),
        out_shape=jax.ShapeDtypeStruct((g, rows, lh), F32),
        compiler_params=_params(("parallel",)),
        name="s5_scan",
    )(ut, wide, narrow)
    return y.reshape(g, n, b, chunk, h).transpose(2, 1, 3, 0, 4).reshape(b, s, w)


def _glu_kernel(y_ref, u_ref, d_ref, w_ref, b_ref, o_ref):
    y = y_ref[...] + d_ref[...] * u_ref[...]
    g = _gelu(y)
    z = jnp.dot(g.astype(BF16), w_ref[...], preferred_element_type=F32) + b_ref[...]
    o_ref[...] = (g * jax.nn.sigmoid(z)).astype(o_ref.dtype)


def s5_glu(y, u, d_skip, glu_w_bf16, glu_b, *, tm=512):
    m, w = y.shape
    tm = _tile(m, tm)
    return pl.pallas_call(
        _glu_kernel,
        grid=(m // tm,),
        in_specs=[
            pl.BlockSpec((tm, w), lambda i: (i, 0)),
            pl.BlockSpec((tm, w), lambda i: (i, 0)),
            pl.BlockSpec((1, w), lambda i: (0, 0)),
            pl.BlockSpec((w, w), lambda i: (0, 0)),
            pl.BlockSpec((1, w), lambda i: (0, 0)),
        ],
        out_specs=pl.BlockSpec((tm, w), lambda i: (i, 0)),
        out_shape=jax.ShapeDtypeStruct((m, w), BF16),
        compiler_params=_params(("parallel",)),
        name="s5_glu",
    )(y, u, d_skip.reshape(1, w), glu_w_bf16, glu_b.reshape(1, w))


def _sgu_kernel(z_ref, g_ref, b_ref, ws_ref, bs_ref, o_ref, *, width, n_heads):
    hd = width // n_heads
    z = z_ref[...]
    u, v = z[:, :width], z[:, width:]
    mu = jnp.mean(v, axis=-1, keepdims=True)
    var = jnp.mean(jnp.square(v - mu), axis=-1, keepdims=True)
    vn = ((v - mu) * lax.rsqrt(var + EPS) * g_ref[...] + b_ref[...]).astype(BF16)
    for h in range(n_heads):
        sl = slice(h * hd, (h + 1) * hd)
        mixed = jnp.dot(ws_ref[h], vn[:, sl], preferred_element_type=F32) + bs_ref[:, h:h + 1]
        o_ref[:, sl] = (u[:, sl] * mixed).astype(o_ref.dtype)


def gmlp_sgu(z, ln_g, ln_b, w_s_bf16, b_s):
    m, w2 = z.shape
    width = w2 // 2
    n_heads = w_s_bf16.shape[0]
    ck = GMLP_CHUNK
    return pl.pallas_call(
        functools.partial(_sgu_kernel, width=width, n_heads=n_heads),
        grid=(m // ck,),
        in_specs=[
            pl.BlockSpec((ck, w2), lambda i: (i, 0)),
            pl.BlockSpec((1, width), lambda i: (0, 0)),
            pl.BlockSpec((1, width), lambda i: (0, 0)),
            pl.BlockSpec((n_heads, ck, ck), lambda i: (0, 0, 0)),
            pl.BlockSpec((ck, n_heads), lambda i: (0, 0)),
        ],
        out_specs=pl.BlockSpec((ck, width), lambda i: (i, 0)),
        out_shape=jax.ShapeDtypeStruct((m, width), BF16),
        compiler_params=_params(("parallel",)),
        name="gmlp_sgu",
    )(z, ln_g.reshape(1, width), ln_b.reshape(1, width), w_s_bf16, b_s.T.astype(F32))


def _slab_pitch(half):
    return SUBLANES * pl.cdiv(half // LANES, SUBLANES)


def _router_kernel(x_ref, g_ref, rt_ref, aff_ref, slab_ref, *, tm, half, pitch):
    h = _rms(x_ref[0], g_ref[...])
    logits = lax.dot_general(rt_ref[...], h, (((1,), (1,)), ((), ())),
                             precision=lax.Precision.HIGHEST,
                             preferred_element_type=F32)
    mx = jnp.max(logits, axis=0, keepdims=True)
    ex = jnp.exp(logits - mx)
    aff_ref[0] = ex / jnp.sum(ex, axis=0, keepdims=True)
    lo = lax.bitcast_convert_type(h[:, :half].astype(BF16).astype(F32), U32) >> 16
    hi = lax.bitcast_convert_type(h[:, half:].astype(BF16).astype(F32), U32) & jnp.uint32(0xFFFF0000)
    word = lo | hi
    for s in range(half // LANES):
        slab_ref[pl.ds(s, tm, stride=pitch), :] = word[:, s * LANES:(s + 1) * LANES]
    if pitch > half // LANES:
        for s in range(half // LANES, pitch):
            slab_ref[pl.ds(s, tm, stride=pitch), :] = jnp.zeros((tm, LANES), U32)


def moe_router(x, g, router, *, tm=512):
    b, s, d = x.shape
    e = router.shape[1]
    half = d // 2
    pitch = _slab_pitch(half)
    tm = _tile(s, tm)
    nt = s // tm
    return pl.pallas_call(
        functools.partial(_router_kernel, tm=tm, half=half, pitch=pitch),
        grid=(b, nt),
        in_specs=[
            pl.BlockSpec((1, tm, d), lambda bi, j: (bi, j, 0)),
            pl.BlockSpec((1, d), lambda bi, j: (0, 0)),
            pl.BlockSpec((e, d), lambda bi, j: (0, 0)),
        ],
        out_specs=[
            pl.BlockSpec((1, e, tm), lambda bi, j: (bi, 0, j)),
            pl.BlockSpec((tm * pitch, LANES), lambda bi, j: (bi * nt + j, 0)),
        ],
        out_shape=[jax.ShapeDtypeStruct((b, e, s), F32),
                   jax.ShapeDtypeStruct((b * s * pitch, LANES), U32)],
        compiler_params=_params(("parallel", "parallel")),
        name="moe_router",
    )(x, g.reshape(1, d), router.T.astype(F32))


def _lane_cumsum(m_bf16, tri):
    e, s = m_bf16.shape
    off = jnp.zeros((e, 1), F32)
    out = []
    for j in range(s // LANES):
        c = jnp.dot(m_bf16[:, j * LANES:(j + 1) * LANES], tri, preferred_element_type=F32) + off
        out.append(c)
        off = c[:, LANES - 1:LANES]
    return jnp.concatenate(out, axis=1)


def _topk_kernel(aff_ref, idx_ref, gate_ref, *, cap, kchunk):
    aff = aff_ref[0]
    e, s = aff.shape
    bits = lax.bitcast_convert_type(aff, I32)

    def count_ge(t):
        return jnp.sum((bits >= t).astype(F32), axis=1, keepdims=True)

    def body(_, carry):
        lo, hi = carry
        mid = lo + ((hi - lo) >> 1)
        ok = count_ge(mid) >= cap
        return jnp.where(ok, mid, lo), jnp.where(ok, hi, mid)

    lo0 = jnp.zeros((e, 1), I32)
    hi0 = jnp.full((e, 1), 0x7F800000, I32)
    thr, _ = lax.fori_loop(0, 32, body, (lo0, hi0))
    gt = bits > thr
    eq = bits == thr
    need = cap - jnp.sum(gt.astype(F32), axis=1, keepdims=True)
    tri = (lax.broadcasted_iota(I32, (LANES, LANES), 0)
           <= lax.broadcasted_iota(I32, (LANES, LANES), 1)).astype(BF16)
    eq_rank = _lane_cumsum(jnp.where(eq, 1.0, 0.0).astype(BF16), tri)
    sel = gt | (eq & (eq_rank <= need))
    pos = _lane_cumsum(jnp.where(sel, 1.0, 0.0).astype(BF16), tri)
    pos = jnp.where(sel, pos, 0.0)

    tok = lax.broadcasted_iota(I32, (1, s), 1)
    t_hi = (tok >> 6).astype(F32)
    t_lo = (tok & 63).astype(F32)
    slot = lax.broadcasted_iota(I32, (cap, kchunk), 0).astype(F32) + 1.0
    lrow = lax.broadcasted_iota(I32, (SUBLANES, s), 0)
    for ei in range(e):
        a = aff[ei:ei + 1, :]
        g1 = a.astype(BF16).astype(F32)
        g2 = (a - g1).astype(BF16).astype(F32)
        g3 = (a - g1) - g2
        lhs = jnp.where(lrow == 0, t_hi, jnp.where(lrow == 1, t_lo, jnp.where(
            lrow == 2, g1, jnp.where(lrow == 3, g2, jnp.where(lrow == 4, g3, 0.0)))))
        lhs = lhs.astype(BF16)
        acc = jnp.zeros((SUBLANES, cap), F32)
        for c0 in range(0, s, kchunk):
            onehot = jnp.where(pos[ei:ei + 1, c0:c0 + kchunk] == slot, 1.0, 0.0).astype(BF16)
            acc = acc + lax.dot_general(lhs[:, c0:c0 + kchunk], onehot, (((1,), (1,)), ((), ())),
                                        preferred_element_type=F32)
        idx_ref[0, ei:ei + 1, :] = (acc[0:1] * 64.0 + acc[1:2]).astype(I32)
        gate_ref[0, ei:ei + 1, :] = acc[2:3] + acc[3:4] + acc[4:5]


def moe_topk(aff, cap):
    b, e, s = aff.shape
    return pl.pallas_call(
        functools.partial(_topk_kernel, cap=cap, kchunk=_tile(s, 1024)),
        grid=(b,),
        in_specs=[pl.BlockSpec((1, e, s), lambda i: (i, 0, 0))],
        out_specs=[pl.BlockSpec((1, e, cap), lambda i: (i, 0, 0)),
                   pl.BlockSpec((1, e, cap), lambda i: (i, 0, 0))],
        out_shape=[jax.ShapeDtypeStruct((b, e, cap), I32),
                   jax.ShapeDtypeStruct((b, e, cap), F32)],
        compiler_params=_params(("parallel",)),
        name="moe_topk",
    )(aff)


def _gather_kernel(idx_ref, slab_ref, o_ref, rows_ref, *, cap, half, pitch, unroll):
    def body(j0, c):
        for u in range(unroll):
            j = j0 * unroll + u
            src = pl.multiple_of(idx_ref[0, 0, j] * pitch, pitch)
            rows_ref[pl.ds(pl.multiple_of(j * pitch, pitch), pitch), :] = slab_ref[pl.ds(src, pitch), :]
        return c

    lax.fori_loop(0, cap // unroll, body, 0)
    for s in range(half // LANES):
        word = rows_ref[pl.ds(s, cap, stride=pitch), :]
        lo = lax.bitcast_convert_type(word << 16, F32)
        hi = lax.bitcast_convert_type(word & jnp.uint32(0xFFFF0000), F32)
        o_ref[0, :, s * LANES:(s + 1) * LANES] = lo.astype(BF16)
        o_ref[0, :, half + s * LANES:half + (s + 1) * LANES] = hi.astype(BF16)


def moe_gather(idx, slabs, *, seq, d):
    b, e, cap = idx.shape
    half = d // 2
    pitch = _slab_pitch(half)
    return pl.pallas_call(
        functools.partial(_gather_kernel, cap=cap, half=half, pitch=pitch, unroll=8),
        grid=(b, e),
        in_specs=[
            pl.BlockSpec((1, 1, cap), lambda bi, ei: (bi * e + ei, 0, 0), memory_space=pltpu.SMEM),
            pl.BlockSpec((seq * pitch, LANES), lambda bi, ei: (bi, 0)),
        ],
        out_specs=pl.BlockSpec((1, cap, d), lambda bi, ei: (ei, bi, 0)),
        out_shape=jax.ShapeDtypeStruct((e, b * cap, d), BF16),
        scratch_shapes=[pltpu.VMEM((cap * pitch, LANES), U32)],
        compiler_params=_params(("parallel", "arbitrary")),
        name="moe_gather",
    )(idx.reshape(b * e, 1, cap), slabs)


def _ffn_kernel(xs_ref, w1_ref, w3_ref, w2_ref, o_ref, hid_ref, *, nt, tf):
    step = pl.program_id(1)

    @pl.when(step < nt)
    def _():
        xs = xs_ref[0]
        a = jnp.dot(xs, w1_ref[0, 0].astype(BF16), preferred_element_type=F32)
        g = jnp.dot(xs, w3_ref[0, 0].astype(BF16), preferred_element_type=F32)
        hid_ref[step] = (a * jax.nn.sigmoid(a) * g).astype(BF16)

    @pl.when(step >= nt)
    def _():
        acc = jnp.zeros(o_ref.shape[1:], F32)
        for k in range(nt):
            acc = acc + jnp.dot(hid_ref[k], w2_ref[0, 0, k * tf:(k + 1) * tf, :].astype(BF16),
                                preferred_element_type=F32)
        o_ref[0] = acc


def moe_ffn(xs, w1, w3, w2, layer, *, tf=256):
    e, m, d = xs.shape
    f = w1.shape[3]
    tf = _tile(f, tf)
    tn = _tile(d, tf)
    nt = f // tf
    nn = d // tn
    return pl.pallas_call(
        functools.partial(_ffn_kernel, nt=nt, tf=tf),
        grid=(e, nt + nn),
        in_specs=[
            pl.BlockSpec((1, m, d), lambda ei, t: (ei, 0, 0)),
            pl.BlockSpec((1, 1, d, tf), lambda ei, t: (layer, ei, 0, jnp.minimum(t, nt - 1))),
            pl.BlockSpec((1, 1, d, tf), lambda ei, t: (layer, ei, 0, jnp.minimum(t, nt - 1))),
            pl.BlockSpec((1, 1, f, tn), lambda ei, t: (layer, ei, 0, jnp.maximum(t - nt, 0))),
        ],
        out_specs=pl.BlockSpec((1, m, tn), lambda ei, t: (ei, 0, jnp.maximum(t - nt, 0))),
        out_shape=jax.ShapeDtypeStruct((e, m, d), F32),
        scratch_shapes=[pltpu.VMEM((nt, m, tf), BF16)],
        compiler_params=_params(("parallel", "arbitrary")),
        name="moe_ffn",
    )(xs, w1, w3, w2)


def _combine_kernel(idx_ref, gate_ref, ys_ref, o_ref, rows_ref, *, cap, half, pitch, unroll):
    ei = pl.program_id(2)

    @pl.when(ei == 0)
    def _():
        o_ref[...] = jnp.zeros(o_ref.shape, F32)

    for s in range(half // LANES):
        rows_ref[pl.ds(s, cap, stride=pitch), :] = ys_ref[0, :, s * LANES:(s + 1) * LANES]

    def body(j0, c):
        dsts, vals = [], []
        for u in range(unroll):
            j = j0 * unroll + u
            dst = pl.multiple_of(idx_ref[0, 0, j] * pitch, pitch)
            src = pl.multiple_of(j * pitch, pitch)
            vals.append(o_ref[0, 0, pl.ds(dst, pitch), :]
                        + gate_ref[0, 0, j] * rows_ref[pl.ds(src, pitch), :])
            dsts.append(dst)
        for dst, val in zip(dsts, vals):
            o_ref[0, 0, pl.ds(dst, pitch), :] = val
        return c

    lax.fori_loop(0, cap // unroll, body, 0)


def moe_combine(idx, gate, ys, *, seq, d):
    b, e, cap = idx.shape
    half = d // 2
    pitch = _slab_pitch(half)
    return pl.pallas_call(
        functools.partial(_combine_kernel, cap=cap, half=half, pitch=pitch, unroll=8),
        grid=(b, 2, e),
        in_specs=[
            pl.BlockSpec((1, 1, cap), lambda bi, dh, ei: (bi * e + ei, 0, 0), memory_space=pltpu.SMEM),
            pl.BlockSpec((1, 1, cap), lambda bi, dh, ei: (bi * e + ei, 0, 0), memory_space=pltpu.SMEM),
            pl.BlockSpec((1, cap, half), lambda bi, dh, ei: (ei, bi, dh)),
        ],
        out_specs=pl.BlockSpec((1, 1, seq * pitch, LANES), lambda bi, dh, ei: (bi, dh, 0, 0)),
        out_shape=jax.ShapeDtypeStruct((b, 2, seq * pitch, LANES), F32),
        scratch_shapes=[pltpu.VMEM((cap * pitch, LANES), F32)],
        compiler_params=_params(("parallel", "parallel", "arbitrary")),
        name="moe_combine",
    )(idx.reshape(b * e, 1, cap), gate.reshape(b * e, 1, cap), ys)


def _unslab_kernel(*refs, tm, half, pitch, final_norm):
    x_ref, s0_ref, s1_ref = refs[:3]
    o_ref = refs[-1]
    parts = []
    for s_ref in (s0_ref, s1_ref):
        for s in range(half // LANES):
            parts.append(s_ref[0, 0, pl.ds(s, tm, stride=pitch), :])
    y = x_ref[0] + jnp.concatenate(parts, axis=1)
    if final_norm:
        y = _rms(y, refs[3][...])
    o_ref[0] = y


def moe_residual(x, moe_slabs, final_g=None, *, tm=256):
    b, s, d = x.shape
    half = d // 2
    pitch = _slab_pitch(half)
    tm = _tile(s, tm)
    final_norm = final_g is not None
    in_specs = [
        pl.BlockSpec((1, tm, d), lambda bi, j: (bi, j, 0)),
        pl.BlockSpec((1, 1, tm * pitch, LANES), lambda bi, j: (bi, 0, j, 0)),
        pl.BlockSpec((1, 1, tm * pitch, LANES), lambda bi, j: (bi, 1, j, 0)),
    ]
    args = [x, moe_slabs, moe_slabs]
    if final_norm:
        in_specs.append(pl.BlockSpec((1, d), lambda bi, j: (0, 0)))
        args.append(final_g.reshape(1, d))
    return pl.pallas_call(
        functools.partial(_unslab_kernel, tm=tm, half=half, pitch=pitch, final_norm=final_norm),
        grid=(b, s // tm),
        in_specs=in_specs,
        out_specs=pl.BlockSpec((1, tm, d), lambda bi, j: (bi, j, 0)),
        out_shape=jax.ShapeDtypeStruct((b, s, d), F32),
        compiler_params=_params(("parallel", "parallel")),
        name="moe_residual",
    )(*args)


def expert_choice_layer(x, norm_g, router, w1, w3, w2, layer, final_g=None):
    b, s, d = x.shape
    cap = CAPACITY_FACTOR * s // N_EXPERTS
    aff, slabs = moe_router(x, norm_g, router)
    idx, gate = moe_topk(aff, cap)
    xs = moe_gather(idx, slabs, seq=s, d=d)
    ys = moe_ffn(xs, w1, w3, w2, layer)
    moe = moe_combine(idx, gate, ys, seq=s, d=d)
    return moe_residual(x, moe, final_g)


def kernel(x, rel_bias, mix_norm, ffn_norm, final_norm, even_w_in, attn_sink, ssm_a_re, ssm_a_im, ssm_log_dt, ssm_b_re, ssm_b_im, ssm_c_re, ssm_c_im, ssm_d, glu_w, glu_b, even_w_out, odd_w_in, sgu_ln_g, sgu_ln_b, sgu_w, sgu_b, odd_w_out, router, moe_w1, moe_w3, moe_w2):
    b, s, d = x.shape
    t = b * s
    depth = mix_norm.shape[0]
    ssm_w = ssm_d.shape[-1]
    attn_w = even_w_out.shape[1] - ssm_w
    kv_w = (even_w_in.shape[-1] - attn_w - ssm_w) // 2
    bias = attention_bias(rel_bias)

    for layer in range(depth):
        i = layer // 2
        x2 = x.reshape(t, d)
        if layer % 2 == 0:
            proj = norm_matmul(x2, mix_norm[layer], even_w_in[i].astype(BF16))
            attn = windowed_attention(proj.reshape(b, s, -1), attn_sink[i], bias,
                                      attn_w=attn_w, kv_w=kv_w)
            u = proj[:, attn_w + 2 * kv_w:]
            n_levels = max(1, (s // S5_CHUNK - 1).bit_length())
            operands = _s5_operands(ssm_a_re[i], ssm_a_im[i], ssm_log_dt[i], ssm_b_re[i], ssm_b_im[i],
                                    ssm_c_re[i], ssm_c_im[i], S5_CHUNK, n_levels)
            y = s5_scan(u.reshape(b, s, ssm_w), operands, chunk=S5_CHUNK, n_levels=n_levels)
            ssm = s5_glu(y.reshape(t, ssm_w), u, ssm_d[i], glu_w[i].astype(BF16), glu_b[i])
            x2 = matmul_residual([attn.reshape(t, attn_w), ssm], even_w_out[i].astype(BF16), x2)
        else:
            z = norm_matmul(x2, mix_norm[layer], odd_w_in[i].astype(BF16), act="gelu", tn=2048)
            gated = gmlp_sgu(z, sgu_ln_g[i], sgu_ln_b[i], sgu_w[i].astype(BF16), sgu_b[i])
            x2 = matmul_residual([gated], odd_w_out[i].astype(BF16), x2)
        x = expert_choice_layer(x2.reshape(b, s, d), ffn_norm[layer], router[layer], moe_w1, moe_w3,
                                moe_w2, layer, final_norm if layer == depth - 1 else None)
    return x
```

```python
import functools
import math

import jax
import jax.numpy as jnp
from jax import lax
from jax.experimental import pallas as pl
from jax.experimental.pallas import tpu as pltpu

F32 = jnp.float32
BF16 = jnp.bfloat16
I32 = jnp.int32
U32 = jnp.uint32

EPS = 1e-6
NEG_INF = -1e30

LANES = 128
SUBLANES = 8
VMEM_LIMIT = 56 << 20

ATTN_BLOCK = 128
ATTN_ROW_CHUNK = 32
HEAD_DIM = 128
KV_GROUP = 4
REL_BUCKETS = 32
REL_MAX_DIST = 128
SSM_GROUP = 16
SSM_STATE = 64
S5_CHUNK = 32
GMLP_CHUNK = 128
N_EXPERTS = 16
CAPACITY_FACTOR = 2


def _params(sem, vmem=VMEM_LIMIT):
    return pltpu.CompilerParams(dimension_semantics=sem, vmem_limit_bytes=vmem)


def _tile(n, want):
    t = min(n, want)
    while n % t:
        t //= 2
    return t


def _rms(x, g):
    ms = jnp.mean(x * x, axis=-1, keepdims=True)
    return x * lax.rsqrt(ms + EPS) * g


def _gelu(x):
    c = math.sqrt(2.0 / math.pi)
    return x * (0.5 * (1.0 + jnp.tanh(c * (x + 0.044715 * (x * x * x)))))


def _in_proj_kernel(x_ref, g_ref, w_ref, qkv_ref, u_ref):
    hn = _rms(x_ref[...], g_ref[...]).astype(BF16)
    y = jnp.dot(hn, w_ref[...], preferred_element_type=F32)
    split = qkv_ref.shape[1]
    qkv_ref[...] = y[:, :split]
    u_ref[...] = y[:, split:]


def in_projection(x, g, w_bf16, split, *, tm=512):
    m, k = x.shape
    n = w_bf16.shape[1]
    tm = _tile(m, tm)
    return pl.pallas_call(
        _in_proj_kernel,
        grid=(m // tm,),
        in_specs=[
            pl.BlockSpec((tm, k), lambda i: (i, 0)),
            pl.BlockSpec((1, k), lambda i: (0, 0)),
            pl.BlockSpec((k, n), lambda i: (0, 0)),
        ],
        out_specs=[pl.BlockSpec((tm, split), lambda i: (i, 0)),
                   pl.BlockSpec((tm, n - split), lambda i: (i, 0))],
        out_shape=[jax.ShapeDtypeStruct((m, split), F32),
                   jax.ShapeDtypeStruct((m, n - split), F32)],
        compiler_params=_params(("parallel",)),
        name="in_projection",
    )(x, g.reshape(1, k), w_bf16)


def _mm_res_kernel(*refs, n_lhs):
    lhs = refs[:n_lhs]
    ws = refs[n_lhs:2 * n_lhs]
    res_ref, o_ref = refs[2 * n_lhs], refs[2 * n_lhs + 1]
    acc = res_ref[...]
    for l_ref, w_ref in zip(lhs, ws):
        acc = acc + jnp.dot(l_ref[...], w_ref[...], preferred_element_type=F32)
    o_ref[...] = acc


def matmul_residual(lhs_list, w_bf16, res, *, tm=512, tn=2048):
    m, n = res.shape
    kk = lhs_list[0].shape[1]
    n_lhs = len(lhs_list)
    assert all(l.shape == (m, kk) for l in lhs_list) and w_bf16.shape == (n_lhs * kk, n)
    tm, tn = _tile(m, tm), _tile(n, tn)
    in_specs = [pl.BlockSpec((tm, kk), lambda i, j: (i, 0)) for _ in lhs_list]
    in_specs += [pl.BlockSpec((kk, tn), functools.partial(lambda i, j, r: (r, j), r=r))
                 for r in range(n_lhs)]
    in_specs += [pl.BlockSpec((tm, tn), lambda i, j: (i, j))]
    return pl.pallas_call(
        functools.partial(_mm_res_kernel, n_lhs=n_lhs),
        grid=(m // tm, n // tn),
        in_specs=in_specs,
        out_specs=pl.BlockSpec((tm, tn), lambda i, j: (i, j)),
        out_shape=jax.ShapeDtypeStruct((m, n), F32),
        compiler_params=_params(("parallel", "parallel")),
        name="matmul_residual",
    )(*lhs_list, *([w_bf16] * n_lhs), res)


def _t5_bucket(rel):
    nb = REL_BUCKETS // 2
    max_exact = nb // 2
    base = jnp.where(rel > 0, nb, 0)
    n = jnp.abs(rel)
    nf = jnp.maximum(n, 1).astype(F32)
    large = max_exact + (jnp.log(nf / max_exact) / math.log(REL_MAX_DIST / max_exact)
                         * (nb - max_exact)).astype(I32)
    large = jnp.minimum(large, nb - 1)
    return base + jnp.where(n < max_exact, n, large)


def _bias_kernel(rb_ref, bucket_ref, o_ref, *, n_heads):
    bkt = bucket_ref[...]
    row = lax.broadcasted_iota(I32, bkt.shape, 0)
    col = lax.broadcasted_iota(I32, bkt.shape, 1)
    in_window = jnp.abs(col - ATTN_BLOCK - row) <= ATTN_BLOCK
    for h in range(n_heads):
        acc = jnp.zeros(bkt.shape, F32)
        for k in range(REL_BUCKETS):
            acc = jnp.where(bkt == k, rb_ref[k, h], acc)
        o_ref[h * ATTN_BLOCK:(h + 1) * ATTN_BLOCK, :] = jnp.where(in_window, acc, NEG_INF)


def attention_bias(rel_bias):
    n_heads = rel_bias.shape[1]
    q_off = jnp.arange(ATTN_BLOCK, dtype=I32)
    c_off = jnp.arange(3 * ATTN_BLOCK, dtype=I32)
    bucket = _t5_bucket(c_off[None, :] - ATTN_BLOCK - q_off[:, None])
    return pl.pallas_call(
        functools.partial(_bias_kernel, n_heads=n_heads),
        in_specs=[pl.BlockSpec(memory_space=pltpu.SMEM),
                  pl.BlockSpec(memory_space=pltpu.VMEM)],
        out_specs=pl.BlockSpec(memory_space=pltpu.VMEM),
        out_shape=jax.ShapeDtypeStruct((n_heads * ATTN_BLOCK, 3 * ATTN_BLOCK), F32),
        name="attention_bias",
    )(rel_bias.astype(F32), bucket)


def _attn_kernel(sink_ref, q_ref, kp_ref, kc_ref, kn_ref, vp_ref, vc_ref, vn_ref, bias_ref,
                 o_ref, s_ref, p_ref, *, seq, n_kv, qb):
    blk = ATTN_BLOCK
    col = lax.broadcasted_iota(I32, (1, 3 * blk), 1)
    scale = HEAD_DIM ** -0.5
    rc = ATTN_ROW_CHUNK

    def band(p_ref, c_ref, n_ref, ksl, j):
        parts = []
        for i in (j - 1, j, j + 1):
            if i < 0:
                parts.append(p_ref[0, :, ksl])
            elif i >= qb:
                parts.append(n_ref[0, :, ksl])
            else:
                parts.append(c_ref[0, i * blk:(i + 1) * blk, ksl])
        return jnp.concatenate(parts, axis=0).astype(BF16)

    for j in range(qb):
        kpos = (pl.program_id(1) * qb + j - 1) * blk + col
        in_seq = (kpos >= 0) & (kpos < seq)
        rows = slice(j * blk, (j + 1) * blk)
        for kh in range(n_kv):
            ksl = slice(kh * HEAD_DIM, (kh + 1) * HEAD_DIM)
            kband = band(kp_ref, kc_ref, kn_ref, ksl, j)
            vband = band(vp_ref, vc_ref, vn_ref, ksl, j)
            heads = [kh * KV_GROUP + g for g in range(KV_GROUP)]
            qs = jnp.concatenate([q_ref[0, rows, h * HEAD_DIM:(h + 1) * HEAD_DIM] for h in heads],
                                 axis=0).astype(BF16)
            s_ref[...] = lax.dot_general(qs, kband, (((1,), (1,)), ((), ())),
                                         preferred_element_type=F32)

            def softmax_rows(c, carry, kh=kh, in_seq=in_seq):
                r0 = pl.multiple_of(c * rc, rc)
                bias = bias_ref[pl.ds(kh * KV_GROUP * blk + r0, rc), :]
                s = jnp.where(in_seq, s_ref[pl.ds(r0, rc), :] * scale + bias, NEG_INF)
                sink = sink_ref[kh * KV_GROUP + c // (blk // rc)]
                m = jnp.maximum(jnp.max(s, axis=-1, keepdims=True), sink)
                p = jnp.exp(s - m)
                denom = jnp.sum(p, axis=-1, keepdims=True) + jnp.exp(sink - m)
                p_ref[pl.ds(r0, rc), :] = (p / denom).astype(BF16)
                return carry

            lax.fori_loop(0, KV_GROUP * blk // rc, softmax_rows, 0, unroll=True)
            o = jnp.dot(p_ref[...], vband, preferred_element_type=F32)
            for g, h in enumerate(heads):
                o_ref[0, rows, h * HEAD_DIM:(h + 1) * HEAD_DIM] = (
                    o[g * blk:(g + 1) * blk].astype(o_ref.dtype))


def windowed_attention(qkv, sink, bias, *, attn_w, kv_w, qb=4):
    b, s, _ = qkv.shape
    blk = ATTN_BLOCK
    nblk = s // blk
    qb = _tile(nblk, qb)
    n_kv = kv_w // HEAD_DIM
    kcol = attn_w // kv_w
    assert attn_w % kv_w == 0 and n_kv * KV_GROUP * HEAD_DIM == attn_w

    def edge_spec(col, first):
        return pl.BlockSpec(
            (1, blk, kv_w),
            lambda bi, n: (bi, jnp.clip(n * qb + (-1 if first else qb), 0, nblk - 1), col))

    def mid_spec(col):
        return pl.BlockSpec((1, qb * blk, kv_w), lambda bi, n: (bi, n, col))

    return pl.pallas_call(
        functools.partial(_attn_kernel, seq=s, n_kv=n_kv, qb=qb),
        grid=(b, nblk // qb),
        in_specs=[
            pl.BlockSpec(memory_space=pltpu.SMEM),
            pl.BlockSpec((1, qb * blk, attn_w), lambda bi, n: (bi, n, 0)),
            edge_spec(kcol, True), mid_spec(kcol), edge_spec(kcol, False),
            edge_spec(kcol + 1, True), mid_spec(kcol + 1), edge_spec(kcol + 1, False),
            pl.BlockSpec(bias.shape, lambda bi, n: (0, 0)),
        ],
        out_specs=pl.BlockSpec((1, qb * blk, attn_w), lambda bi, n: (bi, n, 0)),
        out_shape=jax.ShapeDtypeStruct((b, s, attn_w), BF16),
        scratch_shapes=[pltpu.VMEM((KV_GROUP * blk, 3 * blk), F32),
                        pltpu.VMEM((KV_GROUP * blk, 3 * blk), BF16)],
        compiler_params=_params(("parallel", "parallel")),
        name="windowed_attention",
    )(sink.astype(F32), qkv, qkv, qkv, qkv, qkv, qkv, qkv, bias)


def _s5_operands(a_re, a_im, log_dt, b_re, b_im, c_re, c_im, chunk, n_levels):
    ell = chunk
    p = a_re.shape[-1]
    dt = jnp.exp(log_dt)[..., None]
    mag = jnp.exp(a_re * dt)
    lb_re = mag * jnp.cos(a_im * dt)
    lb_im = mag * jnp.sin(a_im * dt)
    den = a_re * a_re + a_im * a_im
    nr = lb_re - 1.0
    coef_re = (nr * a_re + lb_im * a_im) / den
    coef_im = (lb_im * a_re - nr * a_im) / den
    bb_re = coef_re[..., None] * b_re - coef_im[..., None] * b_im
    bb_im = coef_re[..., None] * b_im + coef_im[..., None] * b_re
    bt_re, bt_im = bb_re.transpose(0, 1, 3, 2), bb_im.transpose(0, 1, 3, 2)

    def powers(tau):
        t = tau.astype(F32)[None, None, :, None]
        pm = jnp.exp((a_re * dt)[:, :, None, :] * t)
        ang = (a_im * dt)[:, :, None, :] * t
        return pm * jnp.cos(ang), pm * jnp.sin(ang)

    lp = lambda t: jnp.pad(t, [(0, 0)] * (t.ndim - 1) + [(0, LANES - p)])
    cat = lambda *ts: jnp.concatenate(ts, axis=-1)
    ii = jnp.arange(ell)
    dn_re, dn_im = powers(ell - 1 - ii)
    up_re, up_im = powers(ii)
    e1_re, e1_im = powers(ii + 1)
    e2_re, e2_im = powers(ell - ii)
    lv_re, lv_im = powers(ell * (2 ** jnp.arange(n_levels)))
    lev = SUBLANES * pl.cdiv(n_levels, SUBLANES)
    lvp = lambda t: jnp.pad(lp(t), ((0, 0), (0, lev - n_levels), (0, 0)))
    wide = jnp.concatenate([
        cat(lp(dn_re[0]), lp(dn_re[0]), lp(up_re[1]), lp(up_re[1])),
        cat(lp(dn_im[0]), lp(dn_im[0]), lp(up_im[1]), lp(up_im[1])),
        cat(lp(e1_re[0]), lp(e1_re[0]), lp(e2_re[1]), lp(e2_re[1])),
        cat(lp(e1_im[0]), lp(e1_im[0]), lp(e2_im[1]), lp(e2_im[1])),
        cat(lp(bt_re[0]), lp(bt_im[0]), lp(bt_re[1]), lp(bt_im[1])),
        cat(-lp(bt_im[0]), lp(bt_re[0]), -lp(bt_im[1]), lp(bt_re[1])),
        cat(lp(c_re[0]), -lp(c_im[0]), lp(c_re[1]), -lp(c_im[1])),
        cat(-lp(c_im[0]), -lp(c_re[0]), -lp(c_im[1]), -lp(c_re[1])),
        cat(lvp(lv_re[0]), lvp(lv_re[0]), lvp(lv_re[1]), lvp(lv_re[1])),
        cat(-lvp(lv_im[0]), lvp(lv_im[0]), -lvp(lv_im[1]), lvp(lv_im[1])),
    ], axis=1)

    def lag_rows(fwd, bwd):
        f = jnp.pad(lp(fwd), ((0, 0), (ell - 1, 1), (0, 0)))
        b = jnp.pad(lp(bwd), ((0, 0), (0, ell), (0, 0)))
        return cat(f, b)

    narrow = jnp.concatenate([
        lag_rows(up_re[0], dn_re[1]), lag_rows(up_im[0], dn_im[1]),
        cat(lp(c_re[0]), lp(c_re[1])), cat(lp(c_im[0]), lp(c_im[1])),
        cat(lp(bt_re[0]), lp(bt_re[1])), cat(lp(bt_im[0]), lp(bt_im[1])),
    ], axis=1)
    return wide, narrow


def _shift_rows(z, r):
    rows = z.shape[0]
    rolled = pltpu.roll(z, r % rows, 0)
    ridx = lax.broadcasted_iota(I32, z.shape, 0)
    keep = (ridx >= r) if r > 0 else (ridx < rows + r)
    return jnp.where(keep, rolled, 0.0)


def _rep_rows(x, reps):
    return jnp.concatenate([jnp.broadcast_to(x[j:j + 1, :], (reps, x.shape[1]))
                            for j in range(x.shape[0])], axis=0)


def _s5_kernel(u_ref, wide_ref, narrow_ref, o_ref, *, rows_per_chunk, n_levels, chunk, h):
    ell, lh = chunk, chunk * h
    lev = SUBLANES * pl.cdiv(n_levels, SUBLANES)
    nt = (((1,), (1,)), ((), ()))
    w = wide_ref[0]
    nr = narrow_ref[0]
    pwf_re, pwf_im, pwe_re, pwe_im = (w[k * ell:(k + 1) * ell] for k in range(4))
    o = 4 * ell
    bb1, bb2, cc1, cc2 = (w[o + k * h:o + (k + 1) * h] for k in range(4))
    o += 4 * h
    pa_all, pb_all = w[o:o + lev], w[o + lev:o + 2 * lev]
    tile = lambda t, n: jnp.concatenate([t] * n, axis=0)
    f_tab = (_rep_rows(pwf_re, h) * tile(bb1, ell) + _rep_rows(pwf_im, h) * tile(bb2, ell)).astype(BF16)
    et_tab = (_rep_rows(pwe_re, h) * tile(cc1, ell) + _rep_rows(pwe_im, h) * tile(cc2, ell)).astype(BF16)
    pwm_re, pwm_im = nr[:2 * ell], nr[2 * ell:4 * ell]
    o = 4 * ell
    cr, ci, br, bi = (nr[o + k * h:o + (k + 1) * h] for k in range(4))
    pr, pi = _rep_rows(pwm_re, h), _rep_rows(pwm_im, h)
    crt, cit = tile(cr, 2 * ell), tile(ci, 2 * ell)
    hi = lax.Precision.HIGHEST
    strip = (lax.dot_general(br, pr * crt - pi * cit, nt, precision=hi, preferred_element_type=F32)
             - lax.dot_general(bi, pr * cit + pi * crt, nt, precision=hi, preferred_element_type=F32))
    t_tab = jnp.concatenate([strip[:, (ell - 1 - j) * h:(ell - 1 - j) * h + lh] for j in range(ell)],
                            axis=0).astype(BF16)

    u = u_ref[0]
    z = jnp.dot(u, f_tab, preferred_element_type=F32)
    half = 2 * LANES

    def swap(x):
        return jnp.concatenate([x[:, LANES:], x[:, :LANES]], axis=1)

    zf, zb = z[:, :half], z[:, half:]
    for k in range(n_levels):
        r = rows_per_chunk * (2 ** k)
        pa = pa_all[k:k + 1, :]
        pb = pb_all[k:k + 1, :]
        sf = _shift_rows(zf, r)
        sb = _shift_rows(zb, -r)
        zf = zf + pa[:, :half] * sf + pb[:, :half] * swap(sf)
        zb = zb + pa[:, half:] * sb + pb[:, half:] * swap(sb)
    hin = jnp.concatenate([_shift_rows(zf, rows_per_chunk), _shift_rows(zb, -rows_per_chunk)],
                          axis=1).astype(BF16)
    y = jnp.dot(u, t_tab, preferred_element_type=F32)
    y = y + lax.dot_general(hin, et_tab, nt, preferred_element_type=F32)
    o_ref[0] = y


def s5_scan(u, operands, *, chunk, n_levels):
    wide, narrow = operands
    b, s, w = u.shape
    g = wide.shape[0]
    h = w // g
    n = s // chunk
    lh = chunk * h
    rows = n * b
    ut = u.astype(BF16).reshape(b, n, chunk, g, h).transpose(3, 1, 0, 2, 4).reshape(g, rows, lh)
    y = pl.pallas_call(
        functools.partial(_s5_kernel, rows_per_chunk=b, n_levels=n_levels, chunk=chunk, h=h),
        grid=(g,),
        in_specs=[
            pl.BlockSpec((1, rows, lh), lambda i: (i, 0, 0)),
            pl.BlockSpec((1,) + wide.shape[1:], lambda i: (i, 0, 0)),
            pl.BlockSpec((1,) + narrow.shape[1:], lambda i: (i, 0, 0)),
        ],
        out_specs=pl.BlockSpec((1, rows, lh), lambda i: (i, 0, 0)),
        out_shape=jax.ShapeDtypeStruct((g, rows, lh), F32),
        compiler_params=_params(("parallel",)),
        name="s5_scan",
    )(ut, wide, narrow)
    return y.reshape(g, n, b, chunk, h).transpose(2, 1, 3, 0, 4).reshape(b, s, w)


def _glu_kernel(y_ref, u_ref, d_ref, w_ref, b_ref, o_ref):
    y = y_ref[...] + d_ref[...] * u_ref[...]
    g = _gelu(y)
    z = jnp.dot(g.astype(BF16), w_ref[...], preferred_element_type=F32) + b_ref[...]
    o_ref[...] = (g * jax.nn.sigmoid(z)).astype(o_ref.dtype)


def s5_glu(y, u, d_skip, glu_w_bf16, glu_b, *, tm=512):
    m, w = y.shape
    tm = _tile(m, tm)
    return pl.pallas_call(
        _glu_kernel,
        grid=(m // tm,),
        in_specs=[
            pl.BlockSpec((tm, w), lambda i: (i, 0)),
            pl.BlockSpec((tm, w), lambda i: (i, 0)),
            pl.BlockSpec((1, w), lambda i: (0, 0)),
            pl.BlockSpec((w, w), lambda i: (0, 0)),
            pl.BlockSpec((1, w), lambda i: (0, 0)),
        ],
        out_specs=pl.BlockSpec((tm, w), lambda i: (i, 0)),
        out_shape=jax.ShapeDtypeStruct((m, w), BF16),
        compiler_params=_params(("parallel",)),
        name="s5_glu",
    )(y, u, d_skip.reshape(1, w), glu_w_bf16, glu_b.reshape(1, w))


def _gmlp_kernel(x_ref, ng_ref, w_ref, g_ref, b_ref, ws_ref, bs_ref, o_ref, *, width, n_heads, tm):
    hd = width // n_heads
    ck = GMLP_CHUNK
    hn = _rms(x_ref[...], ng_ref[...]).astype(BF16)
    z = _gelu(jnp.dot(hn, w_ref[...], preferred_element_type=F32))
    u, v = z[:, :width], z[:, width:]
    mu = jnp.mean(v, axis=-1, keepdims=True)
    var = jnp.mean(jnp.square(v - mu), axis=-1, keepdims=True)
    vn = ((v - mu) * lax.rsqrt(var + EPS) * g_ref[...] + b_ref[...]).astype(BF16)
    for c in range(tm // ck):
        rows = slice(c * ck, (c + 1) * ck)
        for h in range(n_heads):
            sl = slice(h * hd, (h + 1) * hd)
            mixed = jnp.dot(ws_ref[h], vn[rows, sl], preferred_element_type=F32) + bs_ref[:, h:h + 1]
            o_ref[rows, sl] = (u[rows, sl] * mixed).astype(o_ref.dtype)


def gmlp_gate(x, norm_g, w_in_bf16, ln_g, ln_b, w_s_bf16, b_s, *, tm=256):
    m, d = x.shape
    w2 = w_in_bf16.shape[1]
    width = w2 // 2
    n_heads = w_s_bf16.shape[0]
    ck = GMLP_CHUNK
    tm = max(ck, _tile(m, tm))
    return pl.pallas_call(
        functools.partial(_gmlp_kernel, width=width, n_heads=n_heads, tm=tm),
        grid=(m // tm,),
        in_specs=[
            pl.BlockSpec((tm, d), lambda i: (i, 0)),
            pl.BlockSpec((1, d), lambda i: (0, 0)),
            pl.BlockSpec((d, w2), lambda i: (0, 0)),
            pl.BlockSpec((1, width), lambda i: (0, 0)),
            pl.BlockSpec((1, width), lambda i: (0, 0)),
            pl.BlockSpec((n_heads, ck, ck), lambda i: (0, 0, 0)),
            pl.BlockSpec((ck, n_heads), lambda i: (0, 0)),
        ],
        out_specs=pl.BlockSpec((tm, width), lambda i: (i, 0)),
        out_shape=jax.ShapeDtypeStruct((m, width), BF16),
        compiler_params=_params(("parallel",)),
        name="gmlp_gate",
    )(x, norm_g.reshape(1, d), w_in_bf16, ln_g.reshape(1, width), ln_b.reshape(1, width), w_s_bf16,
      b_s.T.astype(F32))


def _slab_pitch(half):
    return SUBLANES * pl.cdiv(half // LANES, SUBLANES)


def _router_kernel(x_ref, g_ref, rt_ref, aff_ref, slab_ref, *, tm, half, pitch):
    h = _rms(x_ref[0], g_ref[...])
    nt = (((1,), (1,)), ((), ()))
    e = rt_ref.shape[0]
    h1 = h.astype(BF16)
    h1f = h1.astype(F32)
    h2 = (h - h1f).astype(BF16)
    r1 = rt_ref[...].astype(BF16)
    r2 = (rt_ref[...] - r1.astype(F32)).astype(BF16)
    lead = lax.dot_general(jnp.concatenate([r1, r2], axis=0), h1, nt, preferred_element_type=F32)
    logits = lead[:e] + lead[e:] + lax.dot_general(r1, h2, nt, preferred_element_type=F32)
    mx = jnp.max(logits, axis=0, keepdims=True)
    ex = jnp.exp(logits - mx)
    aff_ref[0] = ex / jnp.sum(ex, axis=0, keepdims=True)
    lo = lax.bitcast_convert_type(h1f[:, :half], U32) >> 16
    hi = lax.bitcast_convert_type(h1f[:, half:], U32) & jnp.uint32(0xFFFF0000)
    word = lo | hi
    for s in range(half // LANES):
        slab_ref[pl.ds(s, tm, stride=pitch), :] = word[:, s * LANES:(s + 1) * LANES]
    if pitch > half // LANES:
        for s in range(half // LANES, pitch):
            slab_ref[pl.ds(s, tm, stride=pitch), :] = jnp.zeros((tm, LANES), U32)


def moe_router(x, g, router, *, tm=512):
    b, s, d = x.shape
    e = router.shape[1]
    half = d // 2
    pitch = _slab_pitch(half)
    tm = _tile(s, tm)
    nt = s // tm
    return pl.pallas_call(
        functools.partial(_router_kernel, tm=tm, half=half, pitch=pitch),
        grid=(b, nt),
        in_specs=[
            pl.BlockSpec((1, tm, d), lambda bi, j: (bi, j, 0)),
            pl.BlockSpec((1, d), lambda bi, j: (0, 0)),
            pl.BlockSpec((e, d), lambda bi, j: (0, 0)),
        ],
        out_specs=[
            pl.BlockSpec((1, e, tm), lambda bi, j: (bi, 0, j)),
            pl.BlockSpec((tm * pitch, LANES), lambda bi, j: (bi * nt + j, 0)),
        ],
        out_shape=[jax.ShapeDtypeStruct((b, e, s), F32),
                   jax.ShapeDtypeStruct((b * s * pitch, LANES), U32)],
        compiler_params=_params(("parallel", "parallel")),
        name="moe_router",
    )(x, g.reshape(1, d), router.T.astype(F32))


def _lane_cumsum(m_bf16, tri):
    e, s = m_bf16.shape
    off = jnp.zeros((e, 1), F32)
    out = []
    for j in range(s // LANES):
        c = jnp.dot(m_bf16[:, j * LANES:(j + 1) * LANES], tri, preferred_element_type=F32) + off
        out.append(c)
        off = c[:, LANES - 1:LANES]
    return jnp.concatenate(out, axis=1)


def _topk_kernel(aff_ref, idx_ref, gate_ref, *, cap, kchunk):
    aff = aff_ref[0]
    e, s = aff.shape
    bits = lax.bitcast_convert_type(aff, I32)

    def count_ge(t):
        return jnp.sum((bits >= t).astype(F32), axis=1, keepdims=True)

    def body(_, carry):
        lo, hi = carry
        mid = lo + ((hi - lo) >> 1)
        ok = count_ge(mid) >= cap
        return jnp.where(ok, mid, lo), jnp.where(ok, hi, mid)

    lo0 = jnp.zeros((e, 1), I32)
    hi0 = jnp.full((e, 1), 0x7F800000, I32)
    thr, _ = lax.fori_loop(0, 32, body, (lo0, hi0))
    gt = bits > thr
    eq = bits == thr
    need = cap - jnp.sum(gt.astype(F32), axis=1, keepdims=True)
    tri = (lax.broadcasted_iota(I32, (LANES, LANES), 0)
           <= lax.broadcasted_iota(I32, (LANES, LANES), 1)).astype(BF16)
    eq_rank = _lane_cumsum(jnp.where(eq, 1.0, 0.0).astype(BF16), tri)
    sel = gt | (eq & (eq_rank <= need))
    pos = _lane_cumsum(jnp.where(sel, 1.0, 0.0).astype(BF16), tri)
    pos = jnp.where(sel, pos, 0.0)

    tok = lax.broadcasted_iota(I32, (1, s), 1)
    t_hi = (tok >> 6).astype(F32)
    t_lo = (tok & 63).astype(F32)
    slot = lax.broadcasted_iota(I32, (cap, kchunk), 0).astype(F32) + 1.0
    lrow = lax.broadcasted_iota(I32, (SUBLANES, s), 0)
    for ei in range(e):
        a = aff[ei:ei + 1, :]
        g1 = a.astype(BF16).astype(F32)
        g2 = (a - g1).astype(BF16).astype(F32)
        g3 = (a - g1) - g2
        lhs = jnp.where(lrow == 0, t_hi, jnp.where(lrow == 1, t_lo, jnp.where(
            lrow == 2, g1, jnp.where(lrow == 3, g2, jnp.where(lrow == 4, g3, 0.0)))))
        lhs = lhs.astype(BF16)
        acc = jnp.zeros((SUBLANES, cap), F32)
        for c0 in range(0, s, kchunk):
            onehot = jnp.where(pos[ei:ei + 1, c0:c0 + kchunk] == slot, 1.0, 0.0).astype(BF16)
            acc = acc + lax.dot_general(lhs[:, c0:c0 + kchunk], onehot, (((1,), (1,)), ((), ())),
                                        preferred_element_type=F32)
        idx_ref[0, ei:ei + 1, :] = (acc[0:1] * 64.0 + acc[1:2]).astype(I32)
        gate_ref[0, ei:ei + 1, :] = acc[2:3] + acc[3:4] + acc[4:5]


def moe_topk(aff, cap):
    b, e, s = aff.shape
    return pl.pallas_call(
        functools.partial(_topk_kernel, cap=cap, kchunk=_tile(s, 1024)),
        grid=(b,),
        in_specs=[pl.BlockSpec((1, e, s), lambda i: (i, 0, 0))],
        out_specs=[pl.BlockSpec((1, e, cap), lambda i: (i, 0, 0)),
                   pl.BlockSpec((1, e, cap), lambda i: (i, 0, 0))],
        out_shape=[jax.ShapeDtypeStruct((b, e, cap), I32),
                   jax.ShapeDtypeStruct((b, e, cap), F32)],
        compiler_params=_params(("parallel",)),
        name="moe_topk",
    )(aff)


def _gather_kernel(idx_ref, slab_ref, o_ref, rows_ref, *, cap, half, pitch, unroll):
    def body(j0, c):
        for u in range(unroll):
            j = j0 * unroll + u
            src = pl.multiple_of(idx_ref[0, 0, j] * pitch, pitch)
            rows_ref[pl.ds(pl.multiple_of(j * pitch, pitch), pitch), :] = slab_ref[pl.ds(src, pitch), :]
        return c

    lax.fori_loop(0, cap // unroll, body, 0)
    for s in range(half // LANES):
        word = rows_ref[pl.ds(s, cap, stride=pitch), :]
        lo = lax.bitcast_convert_type(word << 16, F32)
        hi = lax.bitcast_convert_type(word & jnp.uint32(0xFFFF0000), F32)
        o_ref[0, :, s * LANES:(s + 1) * LANES] = lo.astype(BF16)
        o_ref[0, :, half + s * LANES:half + (s + 1) * LANES] = hi.astype(BF16)


def moe_gather(idx, slabs, *, seq, d):
    b, e, cap = idx.shape
    half = d // 2
    pitch = _slab_pitch(half)
    return pl.pallas_call(
        functools.partial(_gather_kernel, cap=cap, half=half, pitch=pitch, unroll=8),
        grid=(b, e),
        in_specs=[
            pl.BlockSpec((1, 1, cap), lambda bi, ei: (bi * e + ei, 0, 0), memory_space=pltpu.SMEM),
            pl.BlockSpec((seq * pitch, LANES), lambda bi, ei: (bi, 0)),
        ],
        out_specs=pl.BlockSpec((1, cap, d), lambda bi, ei: (ei, bi, 0)),
        out_shape=jax.ShapeDtypeStruct((e, b * cap, d), BF16),
        scratch_shapes=[pltpu.VMEM((cap * pitch, LANES), U32)],
        compiler_params=_params(("parallel", "arbitrary")),
        name="moe_gather",
    )(idx.reshape(b * e, 1, cap), slabs)


def _ffn_kernel(xs_ref, w1_ref, w3_ref, w2_ref, o_ref, hid_ref, *, nt, tf):
    step = pl.program_id(1)

    @pl.when(step < nt)
    def _():
        xs = xs_ref[0]
        a = jnp.dot(xs, w1_ref[0, 0].astype(BF16), preferred_element_type=F32)
        g = jnp.dot(xs, w3_ref[0, 0].astype(BF16), preferred_element_type=F32)
        hid_ref[step] = (a * jax.nn.sigmoid(a) * g).astype(BF16)

    @pl.when(step >= nt)
    def _():
        acc = jnp.zeros(o_ref.shape[1:], F32)
        for k in range(nt):
            acc = acc + jnp.dot(hid_ref[k], w2_ref[0, 0, k * tf:(k + 1) * tf, :].astype(BF16),
                                preferred_element_type=F32)
        o_ref[0] = acc


def moe_ffn(xs, w1, w3, w2, layer, *, tf=256):
    e, m, d = xs.shape
    f = w1.shape[3]
    tf = _tile(f, tf)
    tn = _tile(d, tf)
    nt = f // tf
    nn = d // tn
    return pl.pallas_call(
        functools.partial(_ffn_kernel, nt=nt, tf=tf),
        grid=(e, nt + nn),
        in_specs=[
            pl.BlockSpec((1, m, d), lambda ei, t: (ei, 0, 0)),
            pl.BlockSpec((1, 1, d, tf), lambda ei, t: (layer, ei, 0, jnp.minimum(t, nt - 1))),
            pl.BlockSpec((1, 1, d, tf), lambda ei, t: (layer, ei, 0, jnp.minimum(t, nt - 1))),
            pl.BlockSpec((1, 1, f, tn), lambda ei, t: (layer, ei, 0, jnp.maximum(t - nt, 0))),
        ],
        out_specs=pl.BlockSpec((1, m, tn), lambda ei, t: (ei, 0, jnp.maximum(t - nt, 0))),
        out_shape=jax.ShapeDtypeStruct((e, m, d), F32),
        scratch_shapes=[pltpu.VMEM((nt, m, tf), BF16)],
        compiler_params=_params(("parallel", "arbitrary")),
        name="moe_ffn",
    )(xs, w1, w3, w2)


def _combine_kernel(idx_ref, gate_ref, ys_ref, o_ref, rows_ref, *, cap, half, pitch, unroll):
    ei = pl.program_id(2)

    @pl.when(ei == 0)
    def _():
        o_ref[...] = jnp.zeros(o_ref.shape, F32)

    for s in range(half // LANES):
        rows_ref[pl.ds(s, cap, stride=pitch), :] = ys_ref[0, :, s * LANES:(s + 1) * LANES]

    def body(j0, c):
        dsts, vals = [], []
        for u in range(unroll):
            j = j0 * unroll + u
            dst = pl.multiple_of(idx_ref[0, 0, j] * pitch, pitch)
            src = pl.multiple_of(j * pitch, pitch)
            vals.append(o_ref[0, 0, pl.ds(dst, pitch), :]
                        + gate_ref[0, 0, j] * rows_ref[pl.ds(src, pitch), :])
            dsts.append(dst)
        for dst, val in zip(dsts, vals):
            o_ref[0, 0, pl.ds(dst, pitch), :] = val
        return c

    lax.fori_loop(0, cap // unroll, body, 0)


def moe_combine(idx, gate, ys, *, seq, d):
    b, e, cap = idx.shape
    half = d // 2
    pitch = _slab_pitch(half)
    return pl.pallas_call(
        functools.partial(_combine_kernel, cap=cap, half=half, pitch=pitch, unroll=8),
        grid=(b, 2, e),
        in_specs=[
            pl.BlockSpec((1, 1, cap), lambda bi, dh, ei: (bi * e + ei, 0, 0), memory_space=pltpu.SMEM),
            pl.BlockSpec((1, 1, cap), lambda bi, dh, ei: (bi * e + ei, 0, 0), memory_space=pltpu.SMEM),
            pl.BlockSpec((1, cap, half), lambda bi, dh, ei: (ei, bi, dh)),
        ],
        out_specs=pl.BlockSpec((1, 1, seq * pitch, LANES), lambda bi, dh, ei: (bi, dh, 0, 0)),
        out_shape=jax.ShapeDtypeStruct((b, 2, seq * pitch, LANES), F32),
        scratch_shapes=[pltpu.VMEM((cap * pitch, LANES), F32)],
        compiler_params=_params(("parallel", "parallel", "arbitrary")),
        name="moe_combine",
    )(idx.reshape(b * e, 1, cap), gate.reshape(b * e, 1, cap), ys)


def _unslab_kernel(*refs, tm, half, pitch, final_norm):
    x_ref, s0_ref, s1_ref = refs[:3]
    o_ref = refs[-1]
    parts = []
    for s_ref in (s0_ref, s1_ref):
        for s in range(half // LANES):
            parts.append(s_ref[0, 0, pl.ds(s, tm, stride=pitch), :])
    y = x_ref[0] + jnp.concatenate(parts, axis=1)
    if final_norm:
        y = _rms(y, refs[3][...])
    o_ref[0] = y


def moe_residual(x, moe_slabs, final_g=None, *, tm=256):
    b, s, d = x.shape
    half = d // 2
    pitch = _slab_pitch(half)
    tm = _tile(s, tm)
    final_norm = final_g is not None
    in_specs = [
        pl.BlockSpec((1, tm, d), lambda bi, j: (bi, j, 0)),
        pl.BlockSpec((1, 1, tm * pitch, LANES), lambda bi, j: (bi, 0, j, 0)),
        pl.BlockSpec((1, 1, tm * pitch, LANES), lambda bi, j: (bi, 1, j, 0)),
    ]
    args = [x, moe_slabs, moe_slabs]
    if final_norm:
        in_specs.append(pl.BlockSpec((1, d), lambda bi, j: (0, 0)))
        args.append(final_g.reshape(1, d))
    return pl.pallas_call(
        functools.partial(_unslab_kernel, tm=tm, half=half, pitch=pitch, final_norm=final_norm),
        grid=(b, s // tm),
        in_specs=in_specs,
        out_specs=pl.BlockSpec((1, tm, d), lambda bi, j: (bi, j, 0)),
        out_shape=jax.ShapeDtypeStruct((b, s, d), F32),
        compiler_params=_params(("parallel", "parallel")),
        name="moe_residual",
    )(*args)


def expert_choice_layer(x, norm_g, router, w1, w3, w2, layer, final_g=None):
    b, s, d = x.shape
    cap = CAPACITY_FACTOR * s // N_EXPERTS
    aff, slabs = moe_router(x, norm_g, router)
    idx, gate = moe_topk(aff, cap)
    xs = moe_gather(idx, slabs, seq=s, d=d)
    ys = moe_ffn(xs, w1, w3, w2, layer)
    moe = moe_combine(idx, gate, ys, seq=s, d=d)
    return moe_residual(x, moe, final_g)


def kernel(x, rel_bias, mix_norm, ffn_norm, final_norm, even_w_in, attn_sink, ssm_a_re, ssm_a_im, ssm_log_dt, ssm_b_re, ssm_b_im, ssm_c_re, ssm_c_im, ssm_d, glu_w, glu_b, even_w_out, odd_w_in, sgu_ln_g, sgu_ln_b, sgu_w, sgu_b, odd_w_out, router, moe_w1, moe_w3, moe_w2):
    b, s, d = x.shape
    t = b * s
    depth = mix_norm.shape[0]
    ssm_w = ssm_d.shape[-1]
    attn_w = even_w_out.shape[1] - ssm_w
    kv_w = (even_w_in.shape[-1] - attn_w - ssm_w) // 2
    bias = attention_bias(rel_bias)

    for layer in range(depth):
        i = layer // 2
        x2 = x.reshape(t, d)
        if layer % 2 == 0:
            qkv, u = in_projection(x2, mix_norm[layer], even_w_in[i].astype(BF16), attn_w + 2 * kv_w)
            attn = windowed_attention(qkv.reshape(b, s, -1), attn_sink[i], bias,
                                      attn_w=attn_w, kv_w=kv_w)
            n_levels = max(1, (s // S5_CHUNK - 1).bit_length())
            operands = _s5_operands(ssm_a_re[i], ssm_a_im[i], ssm_log_dt[i], ssm_b_re[i], ssm_b_im[i],
                                    ssm_c_re[i], ssm_c_im[i], S5_CHUNK, n_levels)
            y = s5_scan(u.reshape(b, s, ssm_w), operands, chunk=S5_CHUNK, n_levels=n_levels)
            ssm = s5_glu(y.reshape(t, ssm_w), u, ssm_d[i], glu_w[i].astype(BF16), glu_b[i])
            x2 = matmul_residual([attn.reshape(t, attn_w), ssm], even_w_out[i].astype(BF16), x2)
        else:
            gated = gmlp_gate(x2, mix_norm[layer], odd_w_in[i].astype(BF16), sgu_ln_g[i], sgu_ln_b[i],
                              sgu_w[i].astype(BF16), sgu_b[i])
            x2 = matmul_residual([gated], odd_w_out[i].astype(BF16), x2)
        x = expert_choice_layer(x2.reshape(b, s, d), ffn_norm[layer], router[layer], moe_w1, moe_w3,
                                moe_w2, layer, final_norm if layer == depth - 1 else None)
    return x
```

```python
import functools
import math

import jax
import jax.numpy as jnp
from jax import lax
from jax.experimental import pallas as pl
from jax.experimental.pallas import tpu as pltpu

F32 = jnp.float32
BF16 = jnp.bfloat16
I32 = jnp.int32
U32 = jnp.uint32

EPS = 1e-6
NEG_INF = -1e30

LANES = 128
SUBLANES = 8
VMEM_LIMIT = 56 << 20

ATTN_BLOCK = 128
ATTN_ROW_CHUNK = 32
HEAD_DIM = 128
KV_GROUP = 4
REL_BUCKETS = 32
REL_MAX_DIST = 128
SSM_GROUP = 16
SSM_STATE = 64
S5_CHUNK = 32
GMLP_CHUNK = 128
N_EXPERTS = 16
CAPACITY_FACTOR = 2


def _params(sem, vmem=VMEM_LIMIT):
    return pltpu.CompilerParams(dimension_semantics=sem, vmem_limit_bytes=vmem)


def _tile(n, want):
    t = min(n, want)
    while n % t:
        t //= 2
    return t


def _rms(x, g):
    ms = jnp.mean(x * x, axis=-1, keepdims=True)
    return x * lax.rsqrt(ms + EPS) * g


def _gelu(x):
    c = math.sqrt(2.0 / math.pi)
    return x * (0.5 * (1.0 + jnp.tanh(c * (x + 0.044715 * (x * x * x)))))


def _in_proj_kernel(x_ref, g_ref, w_ref, qkv_ref, u_ref):
    hn = _rms(x_ref[...], g_ref[...]).astype(BF16)
    y = jnp.dot(hn, w_ref[...], preferred_element_type=F32)
    split = qkv_ref.shape[1]
    qkv_ref[...] = y[:, :split]
    u_ref[...] = y[:, split:]


def in_projection(x, g, w_bf16, split, *, tm=512):
    m, k = x.shape
    n = w_bf16.shape[1]
    tm = _tile(m, tm)
    return pl.pallas_call(
        _in_proj_kernel,
        grid=(m // tm,),
        in_specs=[
            pl.BlockSpec((tm, k), lambda i: (i, 0)),
            pl.BlockSpec((1, k), lambda i: (0, 0)),
            pl.BlockSpec((k, n), lambda i: (0, 0), pipeline_mode=pl.Buffered(1)),
        ],
        out_specs=[pl.BlockSpec((tm, split), lambda i: (i, 0)),
                   pl.BlockSpec((tm, n - split), lambda i: (i, 0))],
        out_shape=[jax.ShapeDtypeStruct((m, split), F32),
                   jax.ShapeDtypeStruct((m, n - split), F32)],
        compiler_params=_params(("parallel",)),
        name="in_projection",
    )(x, g.reshape(1, k), w_bf16)


def _mm_res_route_kernel(*refs, n_lhs, tm, half, pitch):
    lhs = refs[:n_lhs]
    ws = refs[n_lhs:2 * n_lhs]
    res_ref, g_ref, rt_ref, o_ref, aff_ref, slab_ref = refs[2 * n_lhs:]
    acc = res_ref[...]
    for l_ref, w_ref in zip(lhs, ws):
        acc = acc + jnp.dot(l_ref[...], w_ref[...], preferred_element_type=F32)
    o_ref[...] = acc
    _route_rows(acc, g_ref, rt_ref, aff_ref, slab_ref, tm=tm, half=half, pitch=pitch)


def matmul_residual_route(lhs_list, w_bf16, res, norm_g, router, *, seq, tm=512):
    m, d = res.shape
    kk = lhs_list[0].shape[1]
    n_lhs = len(lhs_list)
    e = router.shape[1]
    assert all(l.shape == (m, kk) for l in lhs_list) and w_bf16.shape == (n_lhs * kk, d)
    tm = _tile(seq, tm)
    nt = seq // tm
    half = d // 2
    pitch = _slab_pitch(half)
    in_specs = [pl.BlockSpec((tm, kk), lambda i: (i, 0)) for _ in lhs_list]
    in_specs += [pl.BlockSpec((kk, d), functools.partial(lambda i, r: (r, 0), r=r),
                              pipeline_mode=pl.Buffered(1)) for r in range(n_lhs)]
    in_specs += [pl.BlockSpec((tm, d), lambda i: (i, 0)),
                 pl.BlockSpec((1, d), lambda i: (0, 0)),
                 pl.BlockSpec((e, d), lambda i: (0, 0))]
    return pl.pallas_call(
        functools.partial(_mm_res_route_kernel, n_lhs=n_lhs, tm=tm, half=half, pitch=pitch),
        grid=(m // tm,),
        in_specs=in_specs,
        out_specs=[pl.BlockSpec((tm, d), lambda i: (i, 0)),
                   pl.BlockSpec((1, e, tm), lambda i: (i // nt, 0, i % nt)),
                   pl.BlockSpec((tm * pitch, LANES), lambda i: (i, 0))],
        out_shape=[jax.ShapeDtypeStruct((m, d), F32),
                   jax.ShapeDtypeStruct((m // seq, e, seq), F32),
                   jax.ShapeDtypeStruct((m * pitch, LANES), U32)],
        compiler_params=_params(("parallel",)),
        name="matmul_residual_route",
    )(*lhs_list, *([w_bf16] * n_lhs), res, norm_g.reshape(1, d), router.T.astype(F32))


def _t5_bucket(rel):
    nb = REL_BUCKETS // 2
    max_exact = nb // 2
    base = jnp.where(rel > 0, nb, 0)
    n = jnp.abs(rel)
    nf = jnp.maximum(n, 1).astype(F32)
    large = max_exact + (jnp.log(nf / max_exact) / math.log(REL_MAX_DIST / max_exact)
                         * (nb - max_exact)).astype(I32)
    large = jnp.minimum(large, nb - 1)
    return base + jnp.where(n < max_exact, n, large)


def _bias_kernel(rb_ref, bucket_ref, o_ref, *, n_heads):
    bkt = bucket_ref[...]
    row = lax.broadcasted_iota(I32, bkt.shape, 0)
    col = lax.broadcasted_iota(I32, bkt.shape, 1)
    in_window = jnp.abs(col - ATTN_BLOCK - row) <= ATTN_BLOCK
    for h in range(n_heads):
        acc = jnp.zeros(bkt.shape, F32)
        for k in range(REL_BUCKETS):
            acc = jnp.where(bkt == k, rb_ref[k, h], acc)
        o_ref[h * ATTN_BLOCK:(h + 1) * ATTN_BLOCK, :] = jnp.where(in_window, acc, NEG_INF)


def attention_bias(rel_bias):
    n_heads = rel_bias.shape[1]
    q_off = jnp.arange(ATTN_BLOCK, dtype=I32)
    c_off = jnp.arange(3 * ATTN_BLOCK, dtype=I32)
    bucket = _t5_bucket(c_off[None, :] - ATTN_BLOCK - q_off[:, None])
    return pl.pallas_call(
        functools.partial(_bias_kernel, n_heads=n_heads),
        in_specs=[pl.BlockSpec(memory_space=pltpu.SMEM),
                  pl.BlockSpec(memory_space=pltpu.VMEM)],
        out_specs=pl.BlockSpec(memory_space=pltpu.VMEM),
        out_shape=jax.ShapeDtypeStruct((n_heads * ATTN_BLOCK, 3 * ATTN_BLOCK), F32),
        name="attention_bias",
    )(rel_bias.astype(F32), bucket)


def _attn_kernel(sink_ref, q_ref, kp_ref, kc_ref, kn_ref, vp_ref, vc_ref, vn_ref, bias_ref,
                 o_ref, s_ref, p_ref, *, seq, n_kv, qb):
    blk = ATTN_BLOCK
    col = lax.broadcasted_iota(I32, (1, 3 * blk), 1)
    scale = HEAD_DIM ** -0.5
    rc = ATTN_ROW_CHUNK

    def band(p_ref, c_ref, n_ref, ksl, j):
        parts = []
        for i in (j - 1, j, j + 1):
            if i < 0:
                parts.append(p_ref[0, :, ksl])
            elif i >= qb:
                parts.append(n_ref[0, :, ksl])
            else:
                parts.append(c_ref[0, i * blk:(i + 1) * blk, ksl])
        return jnp.concatenate(parts, axis=0).astype(BF16)

    for j in range(qb):
        kpos = (pl.program_id(1) * qb + j - 1) * blk + col
        in_seq = (kpos >= 0) & (kpos < seq)
        rows = slice(j * blk, (j + 1) * blk)
        for kh in range(n_kv):
            ksl = slice(kh * HEAD_DIM, (kh + 1) * HEAD_DIM)
            kband = band(kp_ref, kc_ref, kn_ref, ksl, j)
            vband = band(vp_ref, vc_ref, vn_ref, ksl, j)
            heads = [kh * KV_GROUP + g for g in range(KV_GROUP)]
            qs = jnp.concatenate([q_ref[0, rows, h * HEAD_DIM:(h + 1) * HEAD_DIM] for h in heads],
                                 axis=0).astype(BF16)
            s_ref[...] = lax.dot_general(qs, kband, (((1,), (1,)), ((), ())),
                                         preferred_element_type=F32)

            def softmax_rows(c, carry, kh=kh, in_seq=in_seq):
                r0 = pl.multiple_of(c * rc, rc)
                bias = bias_ref[pl.ds(kh * KV_GROUP * blk + r0, rc), :]
                s = jnp.where(in_seq, s_ref[pl.ds(r0, rc), :] * scale + bias, NEG_INF)
                sink = sink_ref[kh * KV_GROUP + c // (blk // rc)]
                m = jnp.maximum(jnp.max(s, axis=-1, keepdims=True), sink)
                p = jnp.exp(s - m)
                denom = jnp.sum(p, axis=-1, keepdims=True) + jnp.exp(sink - m)
                p_ref[pl.ds(r0, rc), :] = (p / denom).astype(BF16)
                return carry

            lax.fori_loop(0, KV_GROUP * blk // rc, softmax_rows, 0, unroll=True)
            o = jnp.dot(p_ref[...], vband, preferred_element_type=F32)
            for g, h in enumerate(heads):
                o_ref[0, rows, h * HEAD_DIM:(h + 1) * HEAD_DIM] = (
                    o[g * blk:(g + 1) * blk].astype(o_ref.dtype))


def windowed_attention(qkv, sink, bias, *, attn_w, kv_w, qb=4):
    b, s, _ = qkv.shape
    blk = ATTN_BLOCK
    nblk = s // blk
    qb = _tile(nblk, qb)
    n_kv = kv_w // HEAD_DIM
    kcol = attn_w // kv_w
    assert attn_w % kv_w == 0 and n_kv * KV_GROUP * HEAD_DIM == attn_w

    def edge_spec(col, first):
        return pl.BlockSpec(
            (1, blk, kv_w),
            lambda bi, n: (bi, jnp.clip(n * qb + (-1 if first else qb), 0, nblk - 1), col))

    def mid_spec(col):
        return pl.BlockSpec((1, qb * blk, kv_w), lambda bi, n: (bi, n, col))

    return pl.pallas_call(
        functools.partial(_attn_kernel, seq=s, n_kv=n_kv, qb=qb),
        grid=(b, nblk // qb),
        in_specs=[
            pl.BlockSpec(memory_space=pltpu.SMEM),
            pl.BlockSpec((1, qb * blk, attn_w), lambda bi, n: (bi, n, 0)),
            edge_spec(kcol, True), mid_spec(kcol), edge_spec(kcol, False),
            edge_spec(kcol + 1, True), mid_spec(kcol + 1), edge_spec(kcol + 1, False),
            pl.BlockSpec(bias.shape, lambda bi, n: (0, 0)),
        ],
        out_specs=pl.BlockSpec((1, qb * blk, attn_w), lambda bi, n: (bi, n, 0)),
        out_shape=jax.ShapeDtypeStruct((b, s, attn_w), BF16),
        scratch_shapes=[pltpu.VMEM((KV_GROUP * blk, 3 * blk), F32),
                        pltpu.VMEM((KV_GROUP * blk, 3 * blk), BF16)],
        compiler_params=_params(("parallel", "parallel")),
        name="windowed_attention",
    )(sink.astype(F32), qkv, qkv, qkv, qkv, qkv, qkv, qkv, bias)


def _s5_operands(a_re, a_im, log_dt, b_re, b_im, c_re, c_im, chunk, n_levels):
    ell = chunk
    p = a_re.shape[-1]
    dt = jnp.exp(log_dt)[..., None]
    mag = jnp.exp(a_re * dt)
    lb_re = mag * jnp.cos(a_im * dt)
    lb_im = mag * jnp.sin(a_im * dt)
    den = a_re * a_re + a_im * a_im
    nr = lb_re - 1.0
    coef_re = (nr * a_re + lb_im * a_im) / den
    coef_im = (lb_im * a_re - nr * a_im) / den
    bb_re = coef_re[..., None] * b_re - coef_im[..., None] * b_im
    bb_im = coef_re[..., None] * b_im + coef_im[..., None] * b_re
    bt_re, bt_im = bb_re.transpose(0, 1, 3, 2), bb_im.transpose(0, 1, 3, 2)

    def powers(tau):
        t = tau.astype(F32)[None, None, :, None]
        pm = jnp.exp((a_re * dt)[:, :, None, :] * t)
        ang = (a_im * dt)[:, :, None, :] * t
        return pm * jnp.cos(ang), pm * jnp.sin(ang)

    lp = lambda t: jnp.pad(t, [(0, 0)] * (t.ndim - 1) + [(0, LANES - p)])
    cat = lambda *ts: jnp.concatenate(ts, axis=-1)
    ii = jnp.arange(ell)
    dn_re, dn_im = powers(ell - 1 - ii)
    up_re, up_im = powers(ii)
    e1_re, e1_im = powers(ii + 1)
    e2_re, e2_im = powers(ell - ii)
    lv_re, lv_im = powers(ell * (2 ** jnp.arange(n_levels)))
    lev = SUBLANES * pl.cdiv(n_levels, SUBLANES)
    lvp = lambda t: jnp.pad(lp(t), ((0, 0), (0, lev - n_levels), (0, 0)))
    wide = jnp.concatenate([
        cat(lp(dn_re[0]), lp(dn_re[0]), lp(up_re[1]), lp(up_re[1])),
        cat(lp(dn_im[0]), lp(dn_im[0]), lp(up_im[1]), lp(up_im[1])),
        cat(lp(e1_re[0]), lp(e1_re[0]), lp(e2_re[1]), lp(e2_re[1])),
        cat(lp(e1_im[0]), lp(e1_im[0]), lp(e2_im[1]), lp(e2_im[1])),
        cat(lp(bt_re[0]), lp(bt_im[0]), lp(bt_re[1]), lp(bt_im[1])),
        cat(-lp(bt_im[0]), lp(bt_re[0]), -lp(bt_im[1]), lp(bt_re[1])),
        cat(lp(c_re[0]), -lp(c_im[0]), lp(c_re[1]), -lp(c_im[1])),
        cat(-lp(c_im[0]), -lp(c_re[0]), -lp(c_im[1]), -lp(c_re[1])),
        cat(lvp(lv_re[0]), lvp(lv_re[0]), lvp(lv_re[1]), lvp(lv_re[1])),
        cat(-lvp(lv_im[0]), lvp(lv_im[0]), -lvp(lv_im[1]), lvp(lv_im[1])),
    ], axis=1)

    def lag_rows(fwd, bwd):
        f = jnp.pad(lp(fwd), ((0, 0), (ell - 1, 1), (0, 0)))
        b = jnp.pad(lp(bwd), ((0, 0), (0, ell), (0, 0)))
        return cat(f, b)

    narrow = jnp.concatenate([
        lag_rows(up_re[0], dn_re[1]), lag_rows(up_im[0], dn_im[1]),
        cat(lp(c_re[0]), lp(c_re[1])), cat(lp(c_im[0]), lp(c_im[1])),
        cat(lp(bt_re[0]), lp(bt_re[1])), cat(lp(bt_im[0]), lp(bt_im[1])),
    ], axis=1)
    return wide, narrow


def _shift_rows(z, r):
    rows = z.shape[0]
    rolled = pltpu.roll(z, r % rows, 0)
    ridx = lax.broadcasted_iota(I32, z.shape, 0)
    keep = (ridx >= r) if r > 0 else (ridx < rows + r)
    return jnp.where(keep, rolled, 0.0)


def _rep_rows(x, reps):
    return jnp.concatenate([jnp.broadcast_to(x[j:j + 1, :], (reps, x.shape[1]))
                            for j in range(x.shape[0])], axis=0)


def _s5_kernel(u_ref, wide_ref, narrow_ref, o_ref, *, rows_per_chunk, n_levels, chunk, h):
    ell, lh = chunk, chunk * h
    lev = SUBLANES * pl.cdiv(n_levels, SUBLANES)
    nt = (((1,), (1,)), ((), ()))
    w = wide_ref[0]
    nr = narrow_ref[0]
    pwf_re, pwf_im, pwe_re, pwe_im = (w[k * ell:(k + 1) * ell] for k in range(4))
    o = 4 * ell
    bb1, bb2, cc1, cc2 = (w[o + k * h:o + (k + 1) * h] for k in range(4))
    o += 4 * h
    pa_all, pb_all = w[o:o + lev], w[o + lev:o + 2 * lev]
    tile = lambda t, n: jnp.concatenate([t] * n, axis=0)
    f_tab = (_rep_rows(pwf_re, h) * tile(bb1, ell) + _rep_rows(pwf_im, h) * tile(bb2, ell)).astype(BF16)
    et_tab = (_rep_rows(pwe_re, h) * tile(cc1, ell) + _rep_rows(pwe_im, h) * tile(cc2, ell)).astype(BF16)
    pwm_re, pwm_im = nr[:2 * ell], nr[2 * ell:4 * ell]
    o = 4 * ell
    cr, ci, br, bi = (nr[o + k * h:o + (k + 1) * h] for k in range(4))
    pr, pi = _rep_rows(pwm_re, h), _rep_rows(pwm_im, h)
    crt, cit = tile(cr, 2 * ell), tile(ci, 2 * ell)
    hi = lax.Precision.HIGHEST
    strip = (lax.dot_general(br, pr * crt - pi * cit, nt, precision=hi, preferred_element_type=F32)
             - lax.dot_general(bi, pr * cit + pi * crt, nt, precision=hi, preferred_element_type=F32))
    t_tab = jnp.concatenate([strip[:, (ell - 1 - j) * h:(ell - 1 - j) * h + lh] for j in range(ell)],
                            axis=0).astype(BF16)

    u = u_ref[0]
    z = jnp.dot(u, f_tab, preferred_element_type=F32)
    half = 2 * LANES

    def swap(x):
        return jnp.concatenate([x[:, LANES:], x[:, :LANES]], axis=1)

    zf, zb = z[:, :half], z[:, half:]
    for k in range(n_levels):
        r = rows_per_chunk * (2 ** k)
        pa = pa_all[k:k + 1, :]
        pb = pb_all[k:k + 1, :]
        sf = _shift_rows(zf, r)
        sb = _shift_rows(zb, -r)
        zf = zf + pa[:, :half] * sf + pb[:, :half] * swap(sf)
        zb = zb + pa[:, half:] * sb + pb[:, half:] * swap(sb)
    hin = jnp.concatenate([_shift_rows(zf, rows_per_chunk), _shift_rows(zb, -rows_per_chunk)],
                          axis=1).astype(BF16)
    y = jnp.dot(u, t_tab, preferred_element_type=F32)
    y = y + lax.dot_general(hin, et_tab, nt, preferred_element_type=F32)
    o_ref[0] = y


def s5_scan(u, operands, *, chunk, n_levels):
    wide, narrow = operands
    b, s, w = u.shape
    g = wide.shape[0]
    h = w // g
    n = s // chunk
    lh = chunk * h
    rows = n * b
    ut = u.astype(BF16).reshape(b, n, chunk, g, h).transpose(3, 1, 0, 2, 4).reshape(g, rows, lh)
    y = pl.pallas_call(
        functools.partial(_s5_kernel, rows_per_chunk=b, n_levels=n_levels, chunk=chunk, h=h),
        grid=(g,),
        in_specs=[
            pl.BlockSpec((1, rows, lh), lambda i: (i, 0, 0)),
            pl.BlockSpec((1,) + wide.shape[1:], lambda i: (i, 0, 0)),
            pl.BlockSpec((1,) + narrow.shape[1:], lambda i: (i, 0, 0)),
        ],
        out_specs=pl.BlockSpec((1, rows, lh), lambda i: (i, 0, 0)),
        out_shape=jax.ShapeDtypeStruct((g, rows, lh), F32),
        compiler_params=_params(("parallel",)),
        name="s5_scan",
    )(ut, wide, narrow)
    return y.reshape(g, n, b, chunk, h).transpose(2, 1, 3, 0, 4).reshape(b, s, w)


def _glu_kernel(y_ref, u_ref, d_ref, w_ref, b_ref, o_ref):
    y = y_ref[...] + d_ref[...] * u_ref[...]
    g = _gelu(y)
    z = jnp.dot(g.astype(BF16), w_ref[...], preferred_element_type=F32) + b_ref[...]
    o_ref[...] = (g * jax.nn.sigmoid(z)).astype(o_ref.dtype)


def s5_glu(y, u, d_skip, glu_w_bf16, glu_b, *, tm=512):
    m, w = y.shape
    tm = _tile(m, tm)
    return pl.pallas_call(
        _glu_kernel,
        grid=(m // tm,),
        in_specs=[
            pl.BlockSpec((tm, w), lambda i: (i, 0)),
            pl.BlockSpec((tm, w), lambda i: (i, 0)),
            pl.BlockSpec((1, w), lambda i: (0, 0)),
            pl.BlockSpec((w, w), lambda i: (0, 0)),
            pl.BlockSpec((1, w), lambda i: (0, 0)),
        ],
        out_specs=pl.BlockSpec((tm, w), lambda i: (i, 0)),
        out_shape=jax.ShapeDtypeStruct((m, w), BF16),
        compiler_params=_params(("parallel",)),
        name="s5_glu",
    )(y, u, d_skip.reshape(1, w), glu_w_bf16, glu_b.reshape(1, w))


def _gmlp_kernel(x_ref, s0_ref, s1_ref, ng_ref, w_ref, g_ref, b_ref, ws_ref, bs_ref, o_ref, xn_ref, *,
                 width, n_heads, tm, half, pitch):
    hd = width // n_heads
    ck = GMLP_CHUNK
    x = x_ref[...] + _unslab(s0_ref, s1_ref, tm=tm, half=half, pitch=pitch)
    xn_ref[...] = x
    hn = _rms(x, ng_ref[...]).astype(BF16)
    z = _gelu(jnp.dot(hn, w_ref[...], preferred_element_type=F32))
    u, v = z[:, :width], z[:, width:]
    mu = jnp.mean(v, axis=-1, keepdims=True)
    var = jnp.mean(jnp.square(v - mu), axis=-1, keepdims=True)
    vn = ((v - mu) * lax.rsqrt(var + EPS) * g_ref[...] + b_ref[...]).astype(BF16)
    for c in range(tm // ck):
        rows = slice(c * ck, (c + 1) * ck)
        for h in range(n_heads):
            sl = slice(h * hd, (h + 1) * hd)
            mixed = jnp.dot(ws_ref[h], vn[rows, sl], preferred_element_type=F32) + bs_ref[:, h:h + 1]
            o_ref[rows, sl] = (u[rows, sl] * mixed).astype(o_ref.dtype)


def gmlp_gate(x, moe_slabs, norm_g, w_in_bf16, ln_g, ln_b, w_s_bf16, b_s, *, seq, tm=512):
    m, d = x.shape
    w2 = w_in_bf16.shape[1]
    width = w2 // 2
    n_heads = w_s_bf16.shape[0]
    ck = GMLP_CHUNK
    tm = max(ck, _tile(seq, tm))
    nt = seq // tm
    half = d // 2
    pitch = _slab_pitch(half)
    return pl.pallas_call(
        functools.partial(_gmlp_kernel, width=width, n_heads=n_heads, tm=tm, half=half, pitch=pitch),
        grid=(m // tm,),
        in_specs=[
            pl.BlockSpec((tm, d), lambda i: (i, 0)),
            pl.BlockSpec((1, 1, tm * pitch, LANES), lambda i: (i // nt, 0, i % nt, 0)),
            pl.BlockSpec((1, 1, tm * pitch, LANES), lambda i: (i // nt, 1, i % nt, 0)),
            pl.BlockSpec((1, d), lambda i: (0, 0)),
            pl.BlockSpec((d, w2), lambda i: (0, 0), pipeline_mode=pl.Buffered(1)),
            pl.BlockSpec((1, width), lambda i: (0, 0)),
            pl.BlockSpec((1, width), lambda i: (0, 0)),
            pl.BlockSpec((n_heads, ck, ck), lambda i: (0, 0, 0)),
            pl.BlockSpec((ck, n_heads), lambda i: (0, 0)),
        ],
        out_specs=[pl.BlockSpec((tm, width), lambda i: (i, 0)),
                   pl.BlockSpec((tm, d), lambda i: (i, 0))],
        out_shape=[jax.ShapeDtypeStruct((m, width), BF16),
                   jax.ShapeDtypeStruct((m, d), F32)],
        compiler_params=_params(("parallel",)),
        name="gmlp_gate",
    )(x, moe_slabs, moe_slabs, norm_g.reshape(1, d), w_in_bf16, ln_g.reshape(1, width),
      ln_b.reshape(1, width), w_s_bf16, b_s.T.astype(F32))


def _slab_pitch(half):
    return SUBLANES * pl.cdiv(half // LANES, SUBLANES)


def _route_rows(x, g_ref, rt_ref, aff_ref, slab_ref, *, tm, half, pitch):
    h = _rms(x, g_ref[...])
    nt = (((1,), (1,)), ((), ()))
    e = rt_ref.shape[0]
    h1 = h.astype(BF16)
    h1f = h1.astype(F32)
    h2 = (h - h1f).astype(BF16)
    r1 = rt_ref[...].astype(BF16)
    r2 = (rt_ref[...] - r1.astype(F32)).astype(BF16)
    lead = lax.dot_general(jnp.concatenate([r1, r2], axis=0), h1, nt, preferred_element_type=F32)
    logits = lead[:e] + lead[e:] + lax.dot_general(r1, h2, nt, preferred_element_type=F32)
    mx = jnp.max(logits, axis=0, keepdims=True)
    ex = jnp.exp(logits - mx)
    aff_ref[0] = ex / jnp.sum(ex, axis=0, keepdims=True)
    lo = lax.bitcast_convert_type(h1f[:, :half], U32) >> 16
    hi = lax.bitcast_convert_type(h1f[:, half:], U32) & jnp.uint32(0xFFFF0000)
    word = lo | hi
    for s in range(half // LANES):
        slab_ref[pl.ds(s, tm, stride=pitch), :] = word[:, s * LANES:(s + 1) * LANES]
    if pitch > half // LANES:
        for s in range(half // LANES, pitch):
            slab_ref[pl.ds(s, tm, stride=pitch), :] = jnp.zeros((tm, LANES), U32)


def _lane_cumsum(m_bf16, tri):
    e, s = m_bf16.shape
    off = jnp.zeros((e, 1), F32)
    out = []
    for j in range(s // LANES):
        c = jnp.dot(m_bf16[:, j * LANES:(j + 1) * LANES], tri, preferred_element_type=F32) + off
        out.append(c)
        off = c[:, LANES - 1:LANES]
    return jnp.concatenate(out, axis=1)


def _topk_kernel(aff_ref, idx_ref, gate_ref, *, cap, kchunk):
    aff = aff_ref[0]
    e, s = aff.shape
    bits = lax.bitcast_convert_type(aff, I32)

    def count_ge(t):
        return jnp.sum((bits >= t).astype(F32), axis=1, keepdims=True)

    def body(_, carry):
        lo, hi = carry
        mid = lo + ((hi - lo) >> 1)
        ok = count_ge(mid) >= cap
        return jnp.where(ok, mid, lo), jnp.where(ok, hi, mid)

    lo0 = jnp.zeros((e, 1), I32)
    hi0 = jnp.full((e, 1), 0x7F800000, I32)
    thr, _ = lax.fori_loop(0, 32, body, (lo0, hi0))
    gt = bits > thr
    eq = bits == thr
    need = cap - jnp.sum(gt.astype(F32), axis=1, keepdims=True)
    tri = (lax.broadcasted_iota(I32, (LANES, LANES), 0)
           <= lax.broadcasted_iota(I32, (LANES, LANES), 1)).astype(BF16)
    eq_rank = _lane_cumsum(jnp.where(eq, 1.0, 0.0).astype(BF16), tri)
    sel = gt | (eq & (eq_rank <= need))
    pos = _lane_cumsum(jnp.where(sel, 1.0, 0.0).astype(BF16), tri)
    pos = jnp.where(sel, pos, 0.0)

    tok = lax.broadcasted_iota(I32, (1, s), 1)
    t_hi = (tok >> 6).astype(F32)
    t_lo = (tok & 63).astype(F32)
    slot = lax.broadcasted_iota(I32, (cap, kchunk), 0).astype(F32) + 1.0
    lrow = lax.broadcasted_iota(I32, (SUBLANES, s), 0)
    for ei in range(e):
        a = aff[ei:ei + 1, :]
        g1 = a.astype(BF16).astype(F32)
        g2 = (a - g1).astype(BF16).astype(F32)
        g3 = (a - g1) - g2
        lhs = jnp.where(lrow == 0, t_hi, jnp.where(lrow == 1, t_lo, jnp.where(
            lrow == 2, g1, jnp.where(lrow == 3, g2, jnp.where(lrow == 4, g3, 0.0)))))
        lhs = lhs.astype(BF16)
        acc = jnp.zeros((SUBLANES, cap), F32)
        for c0 in range(0, s, kchunk):
            onehot = jnp.where(pos[ei:ei + 1, c0:c0 + kchunk] == slot, 1.0, 0.0).astype(BF16)
            acc = acc + lax.dot_general(lhs[:, c0:c0 + kchunk], onehot, (((1,), (1,)), ((), ())),
                                        preferred_element_type=F32)
        idx_ref[0, ei:ei + 1, :] = (acc[0:1] * 64.0 + acc[1:2]).astype(I32)
        gate_ref[0, ei:ei + 1, :] = acc[2:3] + acc[3:4] + acc[4:5]


def moe_topk(aff, cap):
    b, e, s = aff.shape
    return pl.pallas_call(
        functools.partial(_topk_kernel, cap=cap, kchunk=_tile(s, 1024)),
        grid=(b,),
        in_specs=[pl.BlockSpec((1, e, s), lambda i: (i, 0, 0))],
        out_specs=[pl.BlockSpec((1, e, cap), lambda i: (i, 0, 0)),
                   pl.BlockSpec((1, e, cap), lambda i: (i, 0, 0))],
        out_shape=[jax.ShapeDtypeStruct((b, e, cap), I32),
                   jax.ShapeDtypeStruct((b, e, cap), F32)],
        compiler_params=_params(("parallel",)),
        name="moe_topk",
    )(aff)


def _gather_kernel(idx_ref, slab_ref, o_ref, rows_ref, *, cap, half, pitch, unroll):
    def body(j0, c):
        for u in range(unroll):
            j = j0 * unroll + u
            src = pl.multiple_of(idx_ref[0, 0, j] * pitch, pitch)
            rows_ref[pl.ds(pl.multiple_of(j * pitch, pitch), pitch), :] = slab_ref[pl.ds(src, pitch), :]
        return c

    lax.fori_loop(0, cap // unroll, body, 0)
    for s in range(half // LANES):
        word = rows_ref[pl.ds(s, cap, stride=pitch), :]
        lo = lax.bitcast_convert_type(word << 16, F32)
        hi = lax.bitcast_convert_type(word & jnp.uint32(0xFFFF0000), F32)
        o_ref[0, :, s * LANES:(s + 1) * LANES] = lo.astype(BF16)
        o_ref[0, :, half + s * LANES:half + (s + 1) * LANES] = hi.astype(BF16)


def moe_gather(idx, slabs, *, seq, d):
    b, e, cap = idx.shape
    half = d // 2
    pitch = _slab_pitch(half)
    return pl.pallas_call(
        functools.partial(_gather_kernel, cap=cap, half=half, pitch=pitch, unroll=8),
        grid=(b, e),
        in_specs=[
            pl.BlockSpec((1, 1, cap), lambda bi, ei: (bi * e + ei, 0, 0), memory_space=pltpu.SMEM),
            pl.BlockSpec((seq * pitch, LANES), lambda bi, ei: (bi, 0)),
        ],
        out_specs=pl.BlockSpec((1, cap, d), lambda bi, ei: (ei, bi, 0)),
        out_shape=jax.ShapeDtypeStruct((e, b * cap, d), BF16),
        scratch_shapes=[pltpu.VMEM((cap * pitch, LANES), U32)],
        compiler_params=_params(("parallel", "arbitrary")),
        name="moe_gather",
    )(idx.reshape(b * e, 1, cap), slabs)


def _ffn_kernel(xs_ref, w1_ref, w3_ref, w2_ref, o_ref, hid_ref, *, nt, tf):
    step = pl.program_id(1)

    @pl.when(step < nt)
    def _():
        xs = xs_ref[0]
        a = jnp.dot(xs, w1_ref[0, 0].astype(BF16), preferred_element_type=F32)
        g = jnp.dot(xs, w3_ref[0, 0].astype(BF16), preferred_element_type=F32)
        hid_ref[step] = (a * jax.nn.sigmoid(a) * g).astype(BF16)

    @pl.when(step >= nt)
    def _():
        acc = jnp.zeros(o_ref.shape[1:], F32)
        for k in range(nt):
            acc = acc + jnp.dot(hid_ref[k], w2_ref[0, 0, k * tf:(k + 1) * tf, :].astype(BF16),
                                preferred_element_type=F32)
        o_ref[0] = acc


def moe_ffn(xs, w1, w3, w2, layer, *, tf=256):
    e, m, d = xs.shape
    f = w1.shape[3]
    tf = _tile(f, tf)
    tn = _tile(d, tf)
    nt = f // tf
    nn = d // tn
    return pl.pallas_call(
        functools.partial(_ffn_kernel, nt=nt, tf=tf),
        grid=(e, nt + nn),
        in_specs=[
            pl.BlockSpec((1, m, d), lambda ei, t: (ei, 0, 0)),
            pl.BlockSpec((1, 1, d, tf), lambda ei, t: (layer, ei, 0, jnp.minimum(t, nt - 1))),
            pl.BlockSpec((1, 1, d, tf), lambda ei, t: (layer, ei, 0, jnp.minimum(t, nt - 1))),
            pl.BlockSpec((1, 1, f, tn), lambda ei, t: (layer, ei, 0, jnp.maximum(t - nt, 0))),
        ],
        out_specs=pl.BlockSpec((1, m, tn), lambda ei, t: (ei, 0, jnp.maximum(t - nt, 0))),
        out_shape=jax.ShapeDtypeStruct((e, m, d), F32),
        scratch_shapes=[pltpu.VMEM((nt, m, tf), BF16)],
        compiler_params=_params(("parallel", "arbitrary")),
        name="moe_ffn",
    )(xs, w1, w3, w2)


def _combine_kernel(idx_ref, gate_ref, ys_ref, o_ref, rows_ref, *, cap, half, pitch, unroll):
    ei = pl.program_id(2)

    @pl.when(ei == 0)
    def _():
        o_ref[...] = jnp.zeros(o_ref.shape, F32)

    for s in range(half // LANES):
        rows_ref[pl.ds(s, cap, stride=pitch), :] = ys_ref[0, :, s * LANES:(s + 1) * LANES]

    def body(j0, c):
        dsts, vals = [], []
        for u in range(unroll):
            j = j0 * unroll + u
            dst = pl.multiple_of(idx_ref[0, 0, j] * pitch, pitch)
            src = pl.multiple_of(j * pitch, pitch)
            vals.append(o_ref[0, 0, pl.ds(dst, pitch), :]
                        + gate_ref[0, 0, j] * rows_ref[pl.ds(src, pitch), :])
            dsts.append(dst)
        for dst, val in zip(dsts, vals):
            o_ref[0, 0, pl.ds(dst, pitch), :] = val
        return c

    lax.fori_loop(0, cap // unroll, body, 0)


def moe_combine(idx, gate, ys, *, seq, d):
    b, e, cap = idx.shape
    half = d // 2
    pitch = _slab_pitch(half)
    return pl.pallas_call(
        functools.partial(_combine_kernel, cap=cap, half=half, pitch=pitch, unroll=8),
        grid=(b, 2, e),
        in_specs=[
            pl.BlockSpec((1, 1, cap), lambda bi, dh, ei: (bi * e + ei, 0, 0), memory_space=pltpu.SMEM),
            pl.BlockSpec((1, 1, cap), lambda bi, dh, ei: (bi * e + ei, 0, 0), memory_space=pltpu.SMEM),
            pl.BlockSpec((1, cap, half), lambda bi, dh, ei: (ei, bi, dh)),
        ],
        out_specs=pl.BlockSpec((1, 1, seq * pitch, LANES), lambda bi, dh, ei: (bi, dh, 0, 0)),
        out_shape=jax.ShapeDtypeStruct((b, 2, seq * pitch, LANES), F32),
        scratch_shapes=[pltpu.VMEM((cap * pitch, LANES), F32)],
        compiler_params=_params(("parallel", "parallel", "arbitrary")),
        name="moe_combine",
    )(idx.reshape(b * e, 1, cap), gate.reshape(b * e, 1, cap), ys)


def _unslab(s0_ref, s1_ref, *, tm, half, pitch):
    parts = []
    for s_ref in (s0_ref, s1_ref):
        for s in range(half // LANES):
            parts.append(s_ref[0, 0, pl.ds(s, tm, stride=pitch), :])
    return jnp.concatenate(parts, axis=1)


def _unslab_kernel(*refs, tm, half, pitch, final_norm):
    x_ref, s0_ref, s1_ref = refs[:3]
    o_ref = refs[-1]
    y = x_ref[0] + _unslab(s0_ref, s1_ref, tm=tm, half=half, pitch=pitch)
    if final_norm:
        y = _rms(y, refs[3][...])
    o_ref[0] = y


def moe_residual(x, moe_slabs, final_g=None, *, tm=256):
    b, s, d = x.shape
    half = d // 2
    pitch = _slab_pitch(half)
    tm = _tile(s, tm)
    final_norm = final_g is not None
    in_specs = [
        pl.BlockSpec((1, tm, d), lambda bi, j: (bi, j, 0)),
        pl.BlockSpec((1, 1, tm * pitch, LANES), lambda bi, j: (bi, 0, j, 0)),
        pl.BlockSpec((1, 1, tm * pitch, LANES), lambda bi, j: (bi, 1, j, 0)),
    ]
    args = [x, moe_slabs, moe_slabs]
    if final_norm:
        in_specs.append(pl.BlockSpec((1, d), lambda bi, j: (0, 0)))
        args.append(final_g.reshape(1, d))
    return pl.pallas_call(
        functools.partial(_unslab_kernel, tm=tm, half=half, pitch=pitch, final_norm=final_norm),
        grid=(b, s // tm),
        in_specs=in_specs,
        out_specs=pl.BlockSpec((1, tm, d), lambda bi, j: (bi, j, 0)),
        out_shape=jax.ShapeDtypeStruct((b, s, d), F32),
        compiler_params=_params(("parallel", "parallel")),
        name="moe_residual",
    )(*args)


def expert_choice_ffn(aff, slabs, w1, w3, w2, layer, *, seq, d):
    cap = CAPACITY_FACTOR * seq // N_EXPERTS
    idx, gate = moe_topk(aff, cap)
    xs = moe_gather(idx, slabs, seq=seq, d=d)
    ys = moe_ffn(xs, w1, w3, w2, layer)
    return moe_combine(idx, gate, ys, seq=seq, d=d)


def kernel(x, rel_bias, mix_norm, ffn_norm, final_norm, even_w_in, attn_sink, ssm_a_re, ssm_a_im, ssm_log_dt, ssm_b_re, ssm_b_im, ssm_c_re, ssm_c_im, ssm_d, glu_w, glu_b, even_w_out, odd_w_in, sgu_ln_g, sgu_ln_b, sgu_w, sgu_b, odd_w_out, router, moe_w1, moe_w3, moe_w2):
    b, s, d = x.shape
    t = b * s
    depth = mix_norm.shape[0]
    ssm_w = ssm_d.shape[-1]
    attn_w = even_w_out.shape[1] - ssm_w
    kv_w = (even_w_in.shape[-1] - attn_w - ssm_w) // 2
    bias = attention_bias(rel_bias)

    x2 = x.reshape(t, d)
    moe = None
    for layer in range(depth):
        i = layer // 2
        if layer % 2 == 0:
            if moe is not None:
                x2 = moe_residual(x2.reshape(b, s, d), moe).reshape(t, d)
            qkv, u = in_projection(x2, mix_norm[layer], even_w_in[i].astype(BF16), attn_w + 2 * kv_w)
            attn = windowed_attention(qkv.reshape(b, s, -1), attn_sink[i], bias,
                                      attn_w=attn_w, kv_w=kv_w)
            n_levels = max(1, (s // S5_CHUNK - 1).bit_length())
            operands = _s5_operands(ssm_a_re[i], ssm_a_im[i], ssm_log_dt[i], ssm_b_re[i], ssm_b_im[i],
                                    ssm_c_re[i], ssm_c_im[i], S5_CHUNK, n_levels)
            y = s5_scan(u.reshape(b, s, ssm_w), operands, chunk=S5_CHUNK, n_levels=n_levels)
            ssm = s5_glu(y.reshape(t, ssm_w), u, ssm_d[i], glu_w[i].astype(BF16), glu_b[i])
            mixed, w_out = [attn.reshape(t, attn_w), ssm], even_w_out[i]
        else:
            gated, x2 = gmlp_gate(x2, moe, mix_norm[layer], odd_w_in[i].astype(BF16), sgu_ln_g[i],
                                  sgu_ln_b[i], sgu_w[i].astype(BF16), sgu_b[i], seq=s)
            mixed, w_out = [gated], odd_w_out[i]
        x2, aff, slabs = matmul_residual_route(mixed, w_out.astype(BF16), x2, ffn_norm[layer],
                                               router[layer], seq=s)
        moe = expert_choice_ffn(aff, slabs, moe_w1, moe_w3, moe_w2, layer, seq=s, d=d)
    return moe_residual(x2.reshape(b, s, d), moe, final_norm)
```

```python
import functools
import math

import jax
import jax.numpy as jnp
from jax import lax
from jax.experimental import pallas as pl
from jax.experimental.pallas import tpu as pltpu

F32 = jnp.float32
BF16 = jnp.bfloat16
I32 = jnp.int32
U32 = jnp.uint32

EPS = 1e-6
NEG_INF = -1e30

LANES = 128
SUBLANES = 8
VMEM_LIMIT = 56 << 20

ATTN_BLOCK = 128
ATTN_ROW_CHUNK = 32
HEAD_DIM = 128
KV_GROUP = 4
REL_BUCKETS = 32
REL_MAX_DIST = 128
SSM_GROUP = 16
SSM_STATE = 64
S5_CHUNK = 32
S5_PITCH = 136
GMLP_CHUNK = 128
N_EXPERTS = 16
CAPACITY_FACTOR = 2


def _params(sem, vmem=VMEM_LIMIT):
    return pltpu.CompilerParams(dimension_semantics=sem, vmem_limit_bytes=vmem)


def _tile(n, want):
    t = min(n, want)
    while n % t:
        t //= 2
    return t


def _rms(x, g):
    ms = jnp.mean(x * x, axis=-1, keepdims=True)
    return x * lax.rsqrt(ms + EPS) * g


def _gelu(x):
    c = math.sqrt(2.0 / math.pi)
    return x * (0.5 * (1.0 + jnp.tanh(c * (x + 0.044715 * (x * x * x)))))


def _in_proj_kernel(x_ref, g_ref, w_ref, qkv_ref, u_ref):
    hn = _rms(x_ref[...], g_ref[...]).astype(BF16)
    y = jnp.dot(hn, w_ref[...], preferred_element_type=F32)
    split = qkv_ref.shape[1]
    qkv_ref[...] = y[:, :split]
    u_ref[...] = y[:, split:]


def in_projection(x, g, w_bf16, split, *, tm=512):
    m, k = x.shape
    n = w_bf16.shape[1]
    tm = _tile(m, tm)
    return pl.pallas_call(
        _in_proj_kernel,
        grid=(m // tm,),
        in_specs=[
            pl.BlockSpec((tm, k), lambda i: (i, 0)),
            pl.BlockSpec((1, k), lambda i: (0, 0)),
            pl.BlockSpec((k, n), lambda i: (0, 0), pipeline_mode=pl.Buffered(1)),
        ],
        out_specs=[pl.BlockSpec((tm, split), lambda i: (i, 0)),
                   pl.BlockSpec((tm, n - split), lambda i: (i, 0))],
        out_shape=[jax.ShapeDtypeStruct((m, split), F32),
                   jax.ShapeDtypeStruct((m, n - split), F32)],
        compiler_params=_params(("parallel",)),
        name="in_projection",
    )(x, g.reshape(1, k), w_bf16)


def _mm_res_route_kernel(*refs, n_lhs, tm, half, pitch):
    lhs = refs[:n_lhs]
    ws = refs[n_lhs:2 * n_lhs]
    res_ref, g_ref, rt_ref, o_ref, aff_ref, slab_ref = refs[2 * n_lhs:]
    acc = res_ref[...]
    for l_ref, w_ref in zip(lhs, ws):
        acc = acc + jnp.dot(l_ref[...], w_ref[...], preferred_element_type=F32)
    o_ref[...] = acc
    _route_rows(acc, g_ref, rt_ref, aff_ref, slab_ref, tm=tm, half=half, pitch=pitch)


def matmul_residual_route(lhs_list, w_bf16, res, norm_g, router, *, seq, tm=512):
    m, d = res.shape
    kk = lhs_list[0].shape[1]
    n_lhs = len(lhs_list)
    e = router.shape[1]
    assert all(l.shape == (m, kk) for l in lhs_list) and w_bf16.shape == (n_lhs * kk, d)
    tm = _tile(seq, tm)
    nt = seq // tm
    half = d // 2
    pitch = _slab_pitch(half)
    in_specs = [pl.BlockSpec((tm, kk), lambda i: (i, 0)) for _ in lhs_list]
    in_specs += [pl.BlockSpec((kk, d), functools.partial(lambda i, r: (r, 0), r=r),
                              pipeline_mode=pl.Buffered(1)) for r in range(n_lhs)]
    in_specs += [pl.BlockSpec((tm, d), lambda i: (i, 0)),
                 pl.BlockSpec((1, d), lambda i: (0, 0)),
                 pl.BlockSpec((e, d), lambda i: (0, 0))]
    return pl.pallas_call(
        functools.partial(_mm_res_route_kernel, n_lhs=n_lhs, tm=tm, half=half, pitch=pitch),
        grid=(m // tm,),
        in_specs=in_specs,
        out_specs=[pl.BlockSpec((tm, d), lambda i: (i, 0)),
                   pl.BlockSpec((1, e, tm), lambda i: (i // nt, 0, i % nt)),
                   pl.BlockSpec((tm * pitch, LANES), lambda i: (i, 0))],
        out_shape=[jax.ShapeDtypeStruct((m, d), F32),
                   jax.ShapeDtypeStruct((m // seq, e, seq), F32),
                   jax.ShapeDtypeStruct((m * pitch, LANES), U32)],
        compiler_params=_params(("parallel",)),
        name="matmul_residual_route",
    )(*lhs_list, *([w_bf16] * n_lhs), res, norm_g.reshape(1, d), router.T.astype(F32))


def _t5_bucket(rel):
    nb = REL_BUCKETS // 2
    max_exact = nb // 2
    base = jnp.where(rel > 0, nb, 0)
    n = jnp.abs(rel)
    nf = jnp.maximum(n, 1).astype(F32)
    large = max_exact + (jnp.log(nf / max_exact) / math.log(REL_MAX_DIST / max_exact)
                         * (nb - max_exact)).astype(I32)
    large = jnp.minimum(large, nb - 1)
    return base + jnp.where(n < max_exact, n, large)


def _bias_kernel(rb_ref, bucket_ref, o_ref, *, n_heads):
    bkt = bucket_ref[...]
    row = lax.broadcasted_iota(I32, bkt.shape, 0)
    col = lax.broadcasted_iota(I32, bkt.shape, 1)
    in_window = jnp.abs(col - ATTN_BLOCK - row) <= ATTN_BLOCK
    for h in range(n_heads):
        acc = jnp.zeros(bkt.shape, F32)
        for k in range(REL_BUCKETS):
            acc = jnp.where(bkt == k, rb_ref[k, h], acc)
        o_ref[h * ATTN_BLOCK:(h + 1) * ATTN_BLOCK, :] = jnp.where(in_window, acc, NEG_INF)


def attention_bias(rel_bias):
    n_heads = rel_bias.shape[1]
    q_off = jnp.arange(ATTN_BLOCK, dtype=I32)
    c_off = jnp.arange(3 * ATTN_BLOCK, dtype=I32)
    bucket = _t5_bucket(c_off[None, :] - ATTN_BLOCK - q_off[:, None])
    return pl.pallas_call(
        functools.partial(_bias_kernel, n_heads=n_heads),
        in_specs=[pl.BlockSpec(memory_space=pltpu.SMEM),
                  pl.BlockSpec(memory_space=pltpu.VMEM)],
        out_specs=pl.BlockSpec(memory_space=pltpu.VMEM),
        out_shape=jax.ShapeDtypeStruct((n_heads * ATTN_BLOCK, 3 * ATTN_BLOCK), F32),
        name="attention_bias",
    )(rel_bias.astype(F32), bucket)


def _attn_kernel(sink_ref, q_ref, kp_ref, kc_ref, kn_ref, vp_ref, vc_ref, vn_ref, bias_ref,
                 o_ref, s_ref, p_ref, *, seq, n_kv, qb):
    blk = ATTN_BLOCK
    col = lax.broadcasted_iota(I32, (1, 3 * blk), 1)
    scale = HEAD_DIM ** -0.5
    rc = ATTN_ROW_CHUNK

    def band(p_ref, c_ref, n_ref, ksl, j):
        parts = []
        for i in (j - 1, j, j + 1):
            if i < 0:
                parts.append(p_ref[0, :, ksl])
            elif i >= qb:
                parts.append(n_ref[0, :, ksl])
            else:
                parts.append(c_ref[0, i * blk:(i + 1) * blk, ksl])
        return jnp.concatenate(parts, axis=0).astype(BF16)

    for j in range(qb):
        kpos = (pl.program_id(1) * qb + j - 1) * blk + col
        in_seq = (kpos >= 0) & (kpos < seq)
        rows = slice(j * blk, (j + 1) * blk)
        for kh in range(n_kv):
            ksl = slice(kh * HEAD_DIM, (kh + 1) * HEAD_DIM)
            kband = band(kp_ref, kc_ref, kn_ref, ksl, j)
            vband = band(vp_ref, vc_ref, vn_ref, ksl, j)
            heads = [kh * KV_GROUP + g for g in range(KV_GROUP)]
            qs = jnp.concatenate([q_ref[0, rows, h * HEAD_DIM:(h + 1) * HEAD_DIM] for h in heads],
                                 axis=0).astype(BF16)
            s_ref[...] = lax.dot_general(qs, kband, (((1,), (1,)), ((), ())),
                                         preferred_element_type=F32)

            def softmax_rows(c, carry, kh=kh, in_seq=in_seq):
                r0 = pl.multiple_of(c * rc, rc)
                bias = bias_ref[pl.ds(kh * KV_GROUP * blk + r0, rc), :]
                s = jnp.where(in_seq, s_ref[pl.ds(r0, rc), :] * scale + bias, NEG_INF)
                sink = sink_ref[kh * KV_GROUP + c // (blk // rc)]
                m = jnp.maximum(jnp.max(s, axis=-1, keepdims=True), sink)
                p = jnp.exp(s - m)
                denom = jnp.sum(p, axis=-1, keepdims=True) + jnp.exp(sink - m)
                p_ref[pl.ds(r0, rc), :] = (p / denom).astype(BF16)
                return carry

            lax.fori_loop(0, KV_GROUP * blk // rc, softmax_rows, 0, unroll=True)
            o = jnp.dot(p_ref[...], vband, preferred_element_type=F32)
            for g, h in enumerate(heads):
                o_ref[0, rows, h * HEAD_DIM:(h + 1) * HEAD_DIM] = (
                    o[g * blk:(g + 1) * blk].astype(o_ref.dtype))


def windowed_attention(qkv, sink, bias, *, attn_w, kv_w, qb=4):
    b, s, _ = qkv.shape
    blk = ATTN_BLOCK
    nblk = s // blk
    qb = _tile(nblk, qb)
    n_kv = kv_w // HEAD_DIM
    kcol = attn_w // kv_w
    assert attn_w % kv_w == 0 and n_kv * KV_GROUP * HEAD_DIM == attn_w

    def edge_spec(col, first):
        return pl.BlockSpec(
            (1, blk, kv_w),
            lambda bi, n: (bi, jnp.clip(n * qb + (-1 if first else qb), 0, nblk - 1), col))

    def mid_spec(col):
        return pl.BlockSpec((1, qb * blk, kv_w), lambda bi, n: (bi, n, col))

    return pl.pallas_call(
        functools.partial(_attn_kernel, seq=s, n_kv=n_kv, qb=qb),
        grid=(b, nblk // qb),
        in_specs=[
            pl.BlockSpec(memory_space=pltpu.SMEM),
            pl.BlockSpec((1, qb * blk, attn_w), lambda bi, n: (bi, n, 0)),
            edge_spec(kcol, True), mid_spec(kcol), edge_spec(kcol, False),
            edge_spec(kcol + 1, True), mid_spec(kcol + 1), edge_spec(kcol + 1, False),
            pl.BlockSpec(bias.shape, lambda bi, n: (0, 0)),
        ],
        out_specs=pl.BlockSpec((1, qb * blk, attn_w), lambda bi, n: (bi, n, 0)),
        out_shape=jax.ShapeDtypeStruct((b, s, attn_w), BF16),
        scratch_shapes=[pltpu.VMEM((KV_GROUP * blk, 3 * blk), F32),
                        pltpu.VMEM((KV_GROUP * blk, 3 * blk), BF16)],
        compiler_params=_params(("parallel", "parallel")),
        name="windowed_attention",
    )(sink.astype(F32), qkv, qkv, qkv, qkv, qkv, qkv, qkv, bias)


def _s5_operands(a_re, a_im, log_dt, b_re, b_im, c_re, c_im, chunk, n_levels):
    ell = chunk
    p = a_re.shape[-1]
    dt = jnp.exp(log_dt)[..., None]
    mag = jnp.exp(a_re * dt)
    lb_re = mag * jnp.cos(a_im * dt)
    lb_im = mag * jnp.sin(a_im * dt)
    den = a_re * a_re + a_im * a_im
    nr = lb_re - 1.0
    coef_re = (nr * a_re + lb_im * a_im) / den
    coef_im = (lb_im * a_re - nr * a_im) / den
    bb_re = coef_re[..., None] * b_re - coef_im[..., None] * b_im
    bb_im = coef_re[..., None] * b_im + coef_im[..., None] * b_re
    bt_re, bt_im = bb_re.transpose(0, 1, 3, 2), bb_im.transpose(0, 1, 3, 2)

    def powers(tau):
        t = tau.astype(F32)[None, None, :, None]
        pm = jnp.exp((a_re * dt)[:, :, None, :] * t)
        ang = (a_im * dt)[:, :, None, :] * t
        return pm * jnp.cos(ang), pm * jnp.sin(ang)

    lp = lambda t: jnp.pad(t, [(0, 0)] * (t.ndim - 1) + [(0, LANES - p)])
    cat = lambda *ts: jnp.concatenate(ts, axis=-1)
    ii = jnp.arange(ell)
    dn_re, dn_im = powers(ell - 1 - ii)
    up_re, up_im = powers(ii)
    e1_re, e1_im = powers(ii + 1)
    e2_re, e2_im = powers(ell - ii)
    lv_re, lv_im = powers(ell * (2 ** jnp.arange(n_levels)))
    lev = SUBLANES * pl.cdiv(n_levels, SUBLANES)
    lvp = lambda t: jnp.pad(lp(t), ((0, 0), (0, lev - n_levels), (0, 0)))
    wide = jnp.concatenate([
        cat(lp(dn_re[0]), lp(dn_re[0]), lp(up_re[1]), lp(up_re[1])),
        cat(lp(dn_im[0]), lp(dn_im[0]), lp(up_im[1]), lp(up_im[1])),
        cat(lp(e1_re[0]), lp(e1_re[0]), lp(e2_re[1]), lp(e2_re[1])),
        cat(lp(e1_im[0]), lp(e1_im[0]), lp(e2_im[1]), lp(e2_im[1])),
        cat(lp(bt_re[0]), lp(bt_im[0]), lp(bt_re[1]), lp(bt_im[1])),
        cat(-lp(bt_im[0]), lp(bt_re[0]), -lp(bt_im[1]), lp(bt_re[1])),
        cat(lp(c_re[0]), -lp(c_im[0]), lp(c_re[1]), -lp(c_im[1])),
        cat(-lp(c_im[0]), -lp(c_re[0]), -lp(c_im[1]), -lp(c_re[1])),
        cat(lvp(lv_re[0]), lvp(lv_re[0]), lvp(lv_re[1]), lvp(lv_re[1])),
        cat(-lvp(lv_im[0]), lvp(lv_im[0]), -lvp(lv_im[1]), lvp(lv_im[1])),
    ], axis=1)

    def lag_rows(fwd, bwd):
        f = jnp.pad(lp(fwd), ((0, 0), (ell - 1, 1), (0, 0)))
        b = jnp.pad(lp(bwd), ((0, 0), (0, ell), (0, 0)))
        return cat(f, b)

    narrow = jnp.concatenate([
        lag_rows(up_re[0], dn_re[1]), lag_rows(up_im[0], dn_im[1]),
        cat(lp(c_re[0]), lp(c_re[1])), cat(lp(c_im[0]), lp(c_im[1])),
        cat(lp(bt_re[0]), lp(bt_re[1])), cat(lp(bt_im[0]), lp(bt_im[1])),
    ], axis=1)
    return wide, narrow


def _s5_pack_kernel(u_ref, o_ref, t_ref, *, n_lt, h, chunk):
    def transpose_block(lt, c):
        src = pl.multiple_of(lt * LANES, LANES)
        dst = pl.multiple_of(lt * S5_PITCH, SUBLANES)
        t_ref[pl.ds(dst, LANES), :] = u_ref[pl.ds(src, LANES), :].T
        return c

    lax.fori_loop(0, n_lt, transpose_block, 0, unroll=8)
    n_sub = LANES // chunk

    def pack_group(g8, c):
        for kt in range(h // n_sub):
            ms = [t_ref[pl.ds(g8 * h + kt * n_sub + kk, n_lt, stride=S5_PITCH), :] for kk in range(n_sub)]
            for c_lo in range(n_sub):
                tile = jnp.concatenate([m[:, c_lo * chunk:(c_lo + 1) * chunk] for m in ms], axis=1)
                o_ref[g8, c_lo * n_lt:(c_lo + 1) * n_lt, kt * LANES:(kt + 1) * LANES] = tile.astype(BF16)
        return c

    lax.fori_loop(0, LANES // h, pack_group, 0)


def _s5_unpack_kernel(y_ref, o_ref, t_ref, *, n_lt, h, chunk):
    n_sub = LANES // chunk

    def unpack_group(g8, c):
        for hh in range(h):
            lane0 = (hh // n_sub) * LANES + (hh % n_sub) * chunk
            m = jnp.concatenate([y_ref[g8, c_lo * n_lt:(c_lo + 1) * n_lt, lane0:lane0 + chunk]
                                 for c_lo in range(n_sub)], axis=1)
            t_ref[pl.ds(g8 * h + hh, n_lt, stride=S5_PITCH), :] = m
        return c

    lax.fori_loop(0, LANES // h, unpack_group, 0)

    def transpose_block(lt, c):
        src = pl.multiple_of(lt * S5_PITCH, SUBLANES)
        dst = pl.multiple_of(lt * LANES, LANES)
        o_ref[pl.ds(dst, LANES), :] = t_ref[pl.ds(src, LANES), :].T
        return c

    lax.fori_loop(0, n_lt, transpose_block, 0, unroll=8)


def _shift_rows(z, r, seg):
    if r == 0:
        return z
    rows = z.shape[0]
    rolled = pltpu.roll(z, r % rows, 0)
    assert seg & (seg - 1) == 0
    pos = lax.broadcasted_iota(I32, z.shape, 0) & (seg - 1)
    keep = (pos >= r) if r > 0 else (pos < seg + r)
    return jnp.where(keep, rolled, 0.0)


def _shift_chunks(blocks, d, seg):
    n_sub = len(blocks)
    out = []
    for c_lo in range(n_sub):
        e, s_lo = divmod(c_lo - d, n_sub)
        out.append(_shift_rows(blocks[s_lo], -e, seg))
    return out


def _s5_kernel(u_ref, wide_ref, narrow_ref, perm_ref, o_ref, *, seg, n_levels, chunk, h):
    ell, lh = chunk, chunk * h
    n_sub = LANES // ell
    lev = SUBLANES * pl.cdiv(n_levels, SUBLANES)
    nt = (((1,), (1,)), ((), ()))
    w = wide_ref[0]
    nr = narrow_ref[0]
    pwf_re, pwf_im, pwe_re, pwe_im = (w[k * ell:(k + 1) * ell] for k in range(4))
    o = 4 * ell
    bb1, bb2, cc1, cc2 = (w[o + k * h:o + (k + 1) * h] for k in range(4))
    o += 4 * h
    pa_all, pb_all = w[o:o + lev], w[o + lev:o + 2 * lev]
    f_tab = jnp.concatenate([pwf_re * bb1[k:k + 1, :] + pwf_im * bb2[k:k + 1, :] for k in range(h)],
                            axis=0).astype(BF16)
    et_tab = jnp.concatenate([pwe_re * cc1[k:k + 1, :] + pwe_im * cc2[k:k + 1, :] for k in range(h)],
                             axis=0).astype(BF16)
    pwm_re, pwm_im = nr[:2 * ell], nr[2 * ell:4 * ell]
    o = 4 * ell
    cr, ci, br, bi = (nr[o + k * h:o + (k + 1) * h] for k in range(4))
    rep = lambda x: jnp.concatenate([jnp.broadcast_to(x[j:j + 1, :], (h, x.shape[1]))
                                     for j in range(x.shape[0])], axis=0)
    tile = lambda t: jnp.concatenate([t] * (2 * ell), axis=0)
    pr, pi = rep(pwm_re), rep(pwm_im)
    crt, cit = tile(cr), tile(ci)
    hi = lax.Precision.HIGHEST
    strip = (lax.dot_general(br, pr * crt - pi * cit, nt, precision=hi, preferred_element_type=F32)
             - lax.dot_general(bi, pr * cit + pi * crt, nt, precision=hi, preferred_element_type=F32))
    t_jk = jnp.concatenate([strip[:, (ell - 1 - j) * h:(ell - 1 - j) * h + lh] for j in range(ell)],
                           axis=0).astype(BF16)
    perm = perm_ref[...]
    t_tab = lax.dot_general(jnp.dot(perm, t_jk, preferred_element_type=F32).astype(BF16), perm, nt,
                            preferred_element_type=F32).astype(BF16)

    u = u_ref[0]
    z = jnp.dot(u, f_tab, preferred_element_type=F32)
    n_lt = z.shape[0] // n_sub
    half = 2 * LANES

    def swap(x):
        return jnp.concatenate([x[:, LANES:], x[:, :LANES]], axis=1)

    zf = [z[c * n_lt:(c + 1) * n_lt, :half] for c in range(n_sub)]
    zb = [z[c * n_lt:(c + 1) * n_lt, half:] for c in range(n_sub)]
    for k in range(n_levels):
        pa = pa_all[k:k + 1, :]
        pb = pb_all[k:k + 1, :]
        sf = _shift_chunks(zf, 2 ** k, seg)
        sb = _shift_chunks(zb, -(2 ** k), seg)
        zf = [a + pa[:, :half] * s + pb[:, :half] * swap(s) for a, s in zip(zf, sf)]
        zb = [a + pa[:, half:] * s + pb[:, half:] * swap(s) for a, s in zip(zb, sb)]
    hin = jnp.concatenate([jnp.concatenate(_shift_chunks(zf, 1, seg), axis=0),
                           jnp.concatenate(_shift_chunks(zb, -1, seg), axis=0)], axis=1).astype(BF16)
    y = jnp.dot(u, t_tab, preferred_element_type=F32)
    y = y + lax.dot_general(hin, et_tab, nt, preferred_element_type=F32)
    o_ref[0] = y


def s5_scan(u, operands, *, seq, chunk, n_levels):
    wide, narrow = operands
    t, w = u.shape
    g = wide.shape[0]
    h = w // g
    lh = chunk * h
    n_lt = t // LANES
    rows = t // chunk
    gpt = LANES // h
    assert t % LANES == 0 and seq % LANES == 0 and LANES % chunk == 0 and h % (LANES // chunk) == 0
    scratch = pltpu.VMEM((n_lt * S5_PITCH, LANES), F32)
    ug = pl.pallas_call(
        functools.partial(_s5_pack_kernel, n_lt=n_lt, h=h, chunk=chunk),
        grid=(w // LANES,),
        in_specs=[pl.BlockSpec((t, LANES), lambda q: (0, q))],
        out_specs=pl.BlockSpec((gpt, rows, lh), lambda q: (q, 0, 0)),
        out_shape=jax.ShapeDtypeStruct((g, rows, lh), BF16),
        scratch_shapes=[scratch],
        compiler_params=_params(("parallel",)),
        name="s5_pack",
    )(u)
    kj = jnp.arange(lh)
    perm = (kj[:, None] // chunk + (kj[:, None] % chunk) * h == kj[None, :]).astype(BF16)
    yg = pl.pallas_call(
        functools.partial(_s5_kernel, seg=seq // LANES, n_levels=n_levels, chunk=chunk, h=h),
        grid=(g,),
        in_specs=[
            pl.BlockSpec((1, rows, lh), lambda i: (i, 0, 0)),
            pl.BlockSpec((1,) + wide.shape[1:], lambda i: (i, 0, 0)),
            pl.BlockSpec((1,) + narrow.shape[1:], lambda i: (i, 0, 0)),
            pl.BlockSpec((lh, lh), lambda i: (0, 0)),
        ],
        out_specs=pl.BlockSpec((1, rows, lh), lambda i: (i, 0, 0)),
        out_shape=jax.ShapeDtypeStruct((g, rows, lh), F32),
        compiler_params=_params(("parallel",)),
        name="s5_scan",
    )(ug, wide, narrow, perm)
    return pl.pallas_call(
        functools.partial(_s5_unpack_kernel, n_lt=n_lt, h=h, chunk=chunk),
        grid=(w // LANES,),
        in_specs=[pl.BlockSpec((gpt, rows, lh), lambda q: (q, 0, 0))],
        out_specs=pl.BlockSpec((t, LANES), lambda q: (0, q)),
        out_shape=jax.ShapeDtypeStruct((t, w), F32),
        scratch_shapes=[scratch],
        compiler_params=_params(("parallel",)),
        name="s5_unpack",
    )(yg)


def _glu_kernel(y_ref, u_ref, d_ref, w_ref, b_ref, o_ref):
    y = y_ref[...] + d_ref[...] * u_ref[...]
    g = _gelu(y)
    z = jnp.dot(g.astype(BF16), w_ref[...], preferred_element_type=F32) + b_ref[...]
    o_ref[...] = (g * jax.nn.sigmoid(z)).astype(o_ref.dtype)


def s5_glu(y, u, d_skip, glu_w_bf16, glu_b, *, tm=512):
    m, w = y.shape
    tm = _tile(m, tm)
    return pl.pallas_call(
        _glu_kernel,
        grid=(m // tm,),
        in_specs=[
            pl.BlockSpec((tm, w), lambda i: (i, 0)),
            pl.BlockSpec((tm, w), lambda i: (i, 0)),
            pl.BlockSpec((1, w), lambda i: (0, 0)),
            pl.BlockSpec((w, w), lambda i: (0, 0)),
            pl.BlockSpec((1, w), lambda i: (0, 0)),
        ],
        out_specs=pl.BlockSpec((tm, w), lambda i: (i, 0)),
        out_shape=jax.ShapeDtypeStruct((m, w), BF16),
        compiler_params=_params(("parallel",)),
        name="s5_glu",
    )(y, u, d_skip.reshape(1, w), glu_w_bf16, glu_b.reshape(1, w))


def _gmlp_kernel(x_ref, s0_ref, s1_ref, ng_ref, w_ref, g_ref, b_ref, ws_ref, bs_ref, o_ref, xn_ref, *,
                 width, n_heads, tm, half, pitch):
    hd = width // n_heads
    ck = GMLP_CHUNK
    x = x_ref[...] + _unslab(s0_ref, s1_ref, tm=tm, half=half, pitch=pitch)
    xn_ref[...] = x
    hn = _rms(x, ng_ref[...]).astype(BF16)
    z = _gelu(jnp.dot(hn, w_ref[...], preferred_element_type=F32))
    u, v = z[:, :width], z[:, width:]
    mu = jnp.mean(v, axis=-1, keepdims=True)
    var = jnp.mean(jnp.square(v - mu), axis=-1, keepdims=True)
    vn = ((v - mu) * lax.rsqrt(var + EPS) * g_ref[...] + b_ref[...]).astype(BF16)
    for c in range(tm // ck):
        rows = slice(c * ck, (c + 1) * ck)
        for h in range(n_heads):
            sl = slice(h * hd, (h + 1) * hd)
            mixed = jnp.dot(ws_ref[h], vn[rows, sl], preferred_element_type=F32) + bs_ref[:, h:h + 1]
            o_ref[rows, sl] = (u[rows, sl] * mixed).astype(o_ref.dtype)


def gmlp_gate(x, moe_slabs, norm_g, w_in_bf16, ln_g, ln_b, w_s_bf16, b_s, *, seq, tm=512):
    m, d = x.shape
    w2 = w_in_bf16.shape[1]
    width = w2 // 2
    n_heads = w_s_bf16.shape[0]
    ck = GMLP_CHUNK
    tm = max(ck, _tile(seq, tm))
    nt = seq // tm
    half = d // 2
    pitch = _slab_pitch(half)
    return pl.pallas_call(
        functools.partial(_gmlp_kernel, width=width, n_heads=n_heads, tm=tm, half=half, pitch=pitch),
        grid=(m // tm,),
        in_specs=[
            pl.BlockSpec((tm, d), lambda i: (i, 0)),
            pl.BlockSpec((1, 1, tm * pitch, LANES), lambda i: (i // nt, 0, i % nt, 0)),
            pl.BlockSpec((1, 1, tm * pitch, LANES), lambda i: (i // nt, 1, i % nt, 0)),
            pl.BlockSpec((1, d), lambda i: (0, 0)),
            pl.BlockSpec((d, w2), lambda i: (0, 0), pipeline_mode=pl.Buffered(1)),
            pl.BlockSpec((1, width), lambda i: (0, 0)),
            pl.BlockSpec((1, width), lambda i: (0, 0)),
            pl.BlockSpec((n_heads, ck, ck), lambda i: (0, 0, 0)),
            pl.BlockSpec((ck, n_heads), lambda i: (0, 0)),
        ],
        out_specs=[pl.BlockSpec((tm, width), lambda i: (i, 0)),
                   pl.BlockSpec((tm, d), lambda i: (i, 0))],
        out_shape=[jax.ShapeDtypeStruct((m, width), BF16),
                   jax.ShapeDtypeStruct((m, d), F32)],
        compiler_params=_params(("parallel",)),
        name="gmlp_gate",
    )(x, moe_slabs, moe_slabs, norm_g.reshape(1, d), w_in_bf16, ln_g.reshape(1, width),
      ln_b.reshape(1, width), w_s_bf16, b_s.T.astype(F32))


def _slab_pitch(half):
    return SUBLANES * pl.cdiv(half // LANES, SUBLANES)


def _route_rows(x, g_ref, rt_ref, aff_ref, slab_ref, *, tm, half, pitch):
    h = _rms(x, g_ref[...])
    nt = (((1,), (1,)), ((), ()))
    e = rt_ref.shape[0]
    h1 = h.astype(BF16)
    h1f = h1.astype(F32)
    h2 = (h - h1f).astype(BF16)
    r1 = rt_ref[...].astype(BF16)
    r2 = (rt_ref[...] - r1.astype(F32)).astype(BF16)
    lead = lax.dot_general(jnp.concatenate([r1, r2], axis=0), h1, nt, preferred_element_type=F32)
    logits = lead[:e] + lead[e:] + lax.dot_general(r1, h2, nt, preferred_element_type=F32)
    mx = jnp.max(logits, axis=0, keepdims=True)
    ex = jnp.exp(logits - mx)
    aff_ref[0] = ex / jnp.sum(ex, axis=0, keepdims=True)
    lo = lax.bitcast_convert_type(h1f[:, :half], U32) >> 16
    hi = lax.bitcast_convert_type(h1f[:, half:], U32) & jnp.uint32(0xFFFF0000)
    word = lo | hi
    for s in range(half // LANES):
        slab_ref[pl.ds(s, tm, stride=pitch), :] = word[:, s * LANES:(s + 1) * LANES]
    if pitch > half // LANES:
        for s in range(half // LANES, pitch):
            slab_ref[pl.ds(s, tm, stride=pitch), :] = jnp.zeros((tm, LANES), U32)


def _lane_cumsum(m_bf16, tri):
    e, s = m_bf16.shape
    off = jnp.zeros((e, 1), F32)
    out = []
    for j in range(s // LANES):
        c = jnp.dot(m_bf16[:, j * LANES:(j + 1) * LANES], tri, preferred_element_type=F32) + off
        out.append(c)
        off = c[:, LANES - 1:LANES]
    return jnp.concatenate(out, axis=1)


def _topk_kernel(aff_ref, idx_ref, gate_ref, *, cap, kchunk):
    aff = aff_ref[0]
    e, s = aff.shape
    bits = lax.bitcast_convert_type(aff, I32)

    def count_ge(t):
        return jnp.sum((bits >= t).astype(F32), axis=1, keepdims=True)

    def body(_, carry):
        lo, hi = carry
        mid = lo + ((hi - lo) >> 1)
        ok = count_ge(mid) >= cap
        return jnp.where(ok, mid, lo), jnp.where(ok, hi, mid)

    lo0 = jnp.zeros((e, 1), I32)
    hi0 = jnp.full((e, 1), 0x7F800000, I32)
    thr, _ = lax.fori_loop(0, 32, body, (lo0, hi0))
    gt = bits > thr
    eq = bits == thr
    need = cap - jnp.sum(gt.astype(F32), axis=1, keepdims=True)
    tri = (lax.broadcasted_iota(I32, (LANES, LANES), 0)
           <= lax.broadcasted_iota(I32, (LANES, LANES), 1)).astype(BF16)
    eq_rank = _lane_cumsum(jnp.where(eq, 1.0, 0.0).astype(BF16), tri)
    sel = gt | (eq & (eq_rank <= need))
    pos = _lane_cumsum(jnp.where(sel, 1.0, 0.0).astype(BF16), tri)
    pos = jnp.where(sel, pos, 0.0)

    tok = lax.broadcasted_iota(I32, (1, s), 1)
    t_hi = (tok >> 6).astype(F32)
    t_lo = (tok & 63).astype(F32)
    slot = lax.broadcasted_iota(I32, (cap, kchunk), 0).astype(F32) + 1.0
    lrow = lax.broadcasted_iota(I32, (SUBLANES, s), 0)
    for ei in range(e):
        a = aff[ei:ei + 1, :]
        g1 = a.astype(BF16).astype(F32)
        g2 = (a - g1).astype(BF16).astype(F32)
        g3 = (a - g1) - g2
        lhs = jnp.where(lrow == 0, t_hi, jnp.where(lrow == 1, t_lo, jnp.where(
            lrow == 2, g1, jnp.where(lrow == 3, g2, jnp.where(lrow == 4, g3, 0.0)))))
        lhs = lhs.astype(BF16)
        acc = jnp.zeros((SUBLANES, cap), F32)
        for c0 in range(0, s, kchunk):
            onehot = jnp.where(pos[ei:ei + 1, c0:c0 + kchunk] == slot, 1.0, 0.0).astype(BF16)
            acc = acc + lax.dot_general(lhs[:, c0:c0 + kchunk], onehot, (((1,), (1,)), ((), ())),
                                        preferred_element_type=F32)
        idx_ref[0, ei:ei + 1, :] = (acc[0:1] * 64.0 + acc[1:2]).astype(I32)
        gate_ref[0, ei:ei + 1, :] = acc[2:3] + acc[3:4] + acc[4:5]


def moe_topk(aff, cap):
    b, e, s = aff.shape
    return pl.pallas_call(
        functools.partial(_topk_kernel, cap=cap, kchunk=_tile(s, 1024)),
        grid=(b,),
        in_specs=[pl.BlockSpec((1, e, s), lambda i: (i, 0, 0))],
        out_specs=[pl.BlockSpec((1, e, cap), lambda i: (i, 0, 0)),
                   pl.BlockSpec((1, e, cap), lambda i: (i, 0, 0))],
        out_shape=[jax.ShapeDtypeStruct((b, e, cap), I32),
                   jax.ShapeDtypeStruct((b, e, cap), F32)],
        compiler_params=_params(("parallel",)),
        name="moe_topk",
    )(aff)


def _gather_kernel(idx_ref, slab_ref, o_ref, rows_ref, *, cap, half, pitch, unroll):
    def body(j0, c):
        for u in range(unroll):
            j = j0 * unroll + u
            src = pl.multiple_of(idx_ref[0, 0, j] * pitch, pitch)
            rows_ref[pl.ds(pl.multiple_of(j * pitch, pitch), pitch), :] = slab_ref[pl.ds(src, pitch), :]
        return c

    lax.fori_loop(0, cap // unroll, body, 0)
    for s in range(half // LANES):
        word = rows_ref[pl.ds(s, cap, stride=pitch), :]
        lo = lax.bitcast_convert_type(word << 16, F32)
        hi = lax.bitcast_convert_type(word & jnp.uint32(0xFFFF0000), F32)
        o_ref[0, :, s * LANES:(s + 1) * LANES] = lo.astype(BF16)
        o_ref[0, :, half + s * LANES:half + (s + 1) * LANES] = hi.astype(BF16)


def moe_gather(idx, slabs, *, seq, d):
    b, e, cap = idx.shape
    half = d // 2
    pitch = _slab_pitch(half)
    return pl.pallas_call(
        functools.partial(_gather_kernel, cap=cap, half=half, pitch=pitch, unroll=8),
        grid=(b, e),
        in_specs=[
            pl.BlockSpec((1, 1, cap), lambda bi, ei: (bi * e + ei, 0, 0), memory_space=pltpu.SMEM),
            pl.BlockSpec((seq * pitch, LANES), lambda bi, ei: (bi, 0)),
        ],
        out_specs=pl.BlockSpec((1, cap, d), lambda bi, ei: (ei, bi, 0)),
        out_shape=jax.ShapeDtypeStruct((e, b * cap, d), BF16),
        scratch_shapes=[pltpu.VMEM((cap * pitch, LANES), U32)],
        compiler_params=_params(("parallel", "arbitrary")),
        name="moe_gather",
    )(idx.reshape(b * e, 1, cap), slabs)


def _ffn_kernel(xs_ref, w1_ref, w3_ref, w2_ref, o_ref, hid_ref, *, nt, tf):
    step = pl.program_id(1)

    @pl.when(step < nt)
    def _():
        xs = xs_ref[0]
        a = jnp.dot(xs, w1_ref[0, 0].astype(BF16), preferred_element_type=F32)
        g = jnp.dot(xs, w3_ref[0, 0].astype(BF16), preferred_element_type=F32)
        hid_ref[step] = (a * jax.nn.sigmoid(a) * g).astype(BF16)

    @pl.when(step >= nt)
    def _():
        acc = jnp.zeros(o_ref.shape[1:], F32)
        for k in range(nt):
            acc = acc + jnp.dot(hid_ref[k], w2_ref[0, 0, k * tf:(k + 1) * tf, :].astype(BF16),
                                preferred_element_type=F32)
        o_ref[0] = acc


def moe_ffn(xs, w1, w3, w2, layer, *, tf=256):
    e, m, d = xs.shape
    f = w1.shape[3]
    tf = _tile(f, tf)
    tn = _tile(d, tf)
    nt = f // tf
    nn = d // tn
    return pl.pallas_call(
        functools.partial(_ffn_kernel, nt=nt, tf=tf),
        grid=(e, nt + nn),
        in_specs=[
            pl.BlockSpec((1, m, d), lambda ei, t: (ei, 0, 0)),
            pl.BlockSpec((1, 1, d, tf), lambda ei, t: (layer, ei, 0, jnp.minimum(t, nt - 1))),
            pl.BlockSpec((1, 1, d, tf), lambda ei, t: (layer, ei, 0, jnp.minimum(t, nt - 1))),
            pl.BlockSpec((1, 1, f, tn), lambda ei, t: (layer, ei, 0, jnp.maximum(t - nt, 0))),
        ],
        out_specs=pl.BlockSpec((1, m, tn), lambda ei, t: (ei, 0, jnp.maximum(t - nt, 0))),
        out_shape=jax.ShapeDtypeStruct((e, m, d), F32),
        scratch_shapes=[pltpu.VMEM((nt, m, tf), BF16)],
        compiler_params=_params(("parallel", "arbitrary")),
        name="moe_ffn",
    )(xs, w1, w3, w2)


def _combine_kernel(idx_ref, gate_ref, ys_ref, o_ref, rows_ref, *, cap, half, pitch, unroll):
    ei = pl.program_id(2)

    @pl.when(ei == 0)
    def _():
        o_ref[...] = jnp.zeros(o_ref.shape, F32)

    for s in range(half // LANES):
        rows_ref[pl.ds(s, cap, stride=pitch), :] = ys_ref[0, :, s * LANES:(s + 1) * LANES]

    def body(j0, c):
        dsts, vals = [], []
        for u in range(unroll):
            j = j0 * unroll + u
            dst = pl.multiple_of(idx_ref[0, 0, j] * pitch, pitch)
            src = pl.multiple_of(j * pitch, pitch)
            vals.append(o_ref[0, 0, pl.ds(dst, pitch), :]
                        + gate_ref[0, 0, j] * rows_ref[pl.ds(src, pitch), :])
            dsts.append(dst)
        for dst, val in zip(dsts, vals):
            o_ref[0, 0, pl.ds(dst, pitch), :] = val
        return c

    lax.fori_loop(0, cap // unroll, body, 0)


def moe_combine(idx, gate, ys, *, seq, d):
    b, e, cap = idx.shape
    half = d // 2
    pitch = _slab_pitch(half)
    return pl.pallas_call(
        functools.partial(_combine_kernel, cap=cap, half=half, pitch=pitch, unroll=8),
        grid=(b, 2, e),
        in_specs=[
            pl.BlockSpec((1, 1, cap), lambda bi, dh, ei: (bi * e + ei, 0, 0), memory_space=pltpu.SMEM),
            pl.BlockSpec((1, 1, cap), lambda bi, dh, ei: (bi * e + ei, 0, 0), memory_space=pltpu.SMEM),
            pl.BlockSpec((1, cap, half), lambda bi, dh, ei: (ei, bi, dh)),
        ],
        out_specs=pl.BlockSpec((1, 1, seq * pitch, LANES), lambda bi, dh, ei: (bi, dh, 0, 0)),
        out_shape=jax.ShapeDtypeStruct((b, 2, seq * pitch, LANES), F32),
        scratch_shapes=[pltpu.VMEM((cap * pitch, LANES), F32)],
        compiler_params=_params(("parallel", "parallel", "arbitrary")),
        name="moe_combine",
    )(idx.reshape(b * e, 1, cap), gate.reshape(b * e, 1, cap), ys)


def _unslab(s0_ref, s1_ref, *, tm, half, pitch):
    parts = []
    for s_ref in (s0_ref, s1_ref):
        for s in range(half // LANES):
            parts.append(s_ref[0, 0, pl.ds(s, tm, stride=pitch), :])
    return jnp.concatenate(parts, axis=1)


def _unslab_kernel(*refs, tm, half, pitch, final_norm):
    x_ref, s0_ref, s1_ref = refs[:3]
    o_ref = refs[-1]
    y = x_ref[0] + _unslab(s0_ref, s1_ref, tm=tm, half=half, pitch=pitch)
    if final_norm:
        y = _rms(y, refs[3][...])
    o_ref[0] = y


def moe_residual(x, moe_slabs, final_g=None, *, tm=256):
    b, s, d = x.shape
    half = d // 2
    pitch = _slab_pitch(half)
    tm = _tile(s, tm)
    final_norm = final_g is not None
    in_specs = [
        pl.BlockSpec((1, tm, d), lambda bi, j: (bi, j, 0)),
        pl.BlockSpec((1, 1, tm * pitch, LANES), lambda bi, j: (bi, 0, j, 0)),
        pl.BlockSpec((1, 1, tm * pitch, LANES), lambda bi, j: (bi, 1, j, 0)),
    ]
    args = [x, moe_slabs, moe_slabs]
    if final_norm:
        in_specs.append(pl.BlockSpec((1, d), lambda bi, j: (0, 0)))
        args.append(final_g.reshape(1, d))
    return pl.pallas_call(
        functools.partial(_unslab_kernel, tm=tm, half=half, pitch=pitch, final_norm=final_norm),
        grid=(b, s // tm),
        in_specs=in_specs,
        out_specs=pl.BlockSpec((1, tm, d), lambda bi, j: (bi, j, 0)),
        out_shape=jax.ShapeDtypeStruct((b, s, d), F32),
        compiler_params=_params(("parallel", "parallel")),
        name="moe_residual",
    )(*args)


def expert_choice_ffn(aff, slabs, w1, w3, w2, layer, *, seq, d):
    cap = CAPACITY_FACTOR * seq // N_EXPERTS
    idx, gate = moe_topk(aff, cap)
    xs = moe_gather(idx, slabs, seq=seq, d=d)
    ys = moe_ffn(xs, w1, w3, w2, layer)
    return moe_combine(idx, gate, ys, seq=seq, d=d)


def kernel(x, rel_bias, mix_norm, ffn_norm, final_norm, even_w_in, attn_sink, ssm_a_re, ssm_a_im, ssm_log_dt, ssm_b_re, ssm_b_im, ssm_c_re, ssm_c_im, ssm_d, glu_w, glu_b, even_w_out, odd_w_in, sgu_ln_g, sgu_ln_b, sgu_w, sgu_b, odd_w_out, router, moe_w1, moe_w3, moe_w2):
    b, s, d = x.shape
    t = b * s
    depth = mix_norm.shape[0]
    ssm_w = ssm_d.shape[-1]
    attn_w = even_w_out.shape[1] - ssm_w
    kv_w = (even_w_in.shape[-1] - attn_w - ssm_w) // 2
    bias = attention_bias(rel_bias)

    x2 = x.reshape(t, d)
    moe = None
    for layer in range(depth):
        i = layer // 2
        if layer % 2 == 0:
            if moe is not None:
                x2 = moe_residual(x2.reshape(b, s, d), moe).reshape(t, d)
            qkv, u = in_projection(x2, mix_norm[layer], even_w_in[i].astype(BF16), attn_w + 2 * kv_w)
            attn = windowed_attention(qkv.reshape(b, s, -1), attn_sink[i], bias,
                                      attn_w=attn_w, kv_w=kv_w)
            n_levels = max(1, (s // S5_CHUNK - 1).bit_length())
            operands = _s5_operands(ssm_a_re[i], ssm_a_im[i], ssm_log_dt[i], ssm_b_re[i], ssm_b_im[i],
                                    ssm_c_re[i], ssm_c_im[i], S5_CHUNK, n_levels)
            y = s5_scan(u, operands, seq=s, chunk=S5_CHUNK, n_levels=n_levels)
            ssm = s5_glu(y, u, ssm_d[i], glu_w[i].astype(BF16), glu_b[i])
            mixed, w_out = [attn.reshape(t, attn_w), ssm], even_w_out[i]
        else:
            gated, x2 = gmlp_gate(x2, moe, mix_norm[layer], odd_w_in[i].astype(BF16), sgu_ln_g[i],
                                  sgu_ln_b[i], sgu_w[i].astype(BF16), sgu_b[i], seq=s)
            mixed, w_out = [gated], odd_w_out[i]
        x2, aff, slabs = matmul_residual_route(mixed, w_out.astype(BF16), x2, ffn_norm[layer],
                                               router[layer], seq=s)
        moe = expert_choice_ffn(aff, slabs, moe_w1, moe_w3, moe_w2, layer, seq=s, d=d)
    return moe_residual(x2.reshape(b, s, d), moe, final_norm)
```

```python
import functools
import math

import jax
import jax.numpy as jnp
from jax import lax
from jax.experimental import pallas as pl
from jax.experimental.pallas import tpu as pltpu

F32 = jnp.float32
BF16 = jnp.bfloat16
I32 = jnp.int32
U32 = jnp.uint32

EPS = 1e-6
NEG_INF = -1e30

LANES = 128
SUBLANES = 8
VMEM_LIMIT = 56 << 20

ATTN_BLOCK = 128
ATTN_ROW_CHUNK = 32
HEAD_DIM = 128
KV_GROUP = 4
REL_BUCKETS = 32
REL_MAX_DIST = 128
SSM_GROUP = 16
SSM_STATE = 64
S5_CHUNK = 32
S5_PITCH = 136
GMLP_CHUNK = 128
N_EXPERTS = 16
CAPACITY_FACTOR = 2


def _params(sem, vmem=VMEM_LIMIT):
    return pltpu.CompilerParams(dimension_semantics=sem, vmem_limit_bytes=vmem)


def _tile(n, want):
    t = min(n, want)
    while n % t:
        t //= 2
    return t


def _rms(x, g):
    ms = jnp.mean(x * x, axis=-1, keepdims=True)
    return x * lax.rsqrt(ms + EPS) * g


def _gelu(x):
    c = math.sqrt(2.0 / math.pi)
    return x * (0.5 * (1.0 + jnp.tanh(c * (x + 0.044715 * (x * x * x)))))


def _in_proj_kernel(x_ref, g_ref, w_ref, qkv_ref, u_ref):
    hn = _rms(x_ref[...], g_ref[...]).astype(BF16)
    y = jnp.dot(hn, w_ref[...], preferred_element_type=F32)
    split = qkv_ref.shape[1]
    qkv_ref[...] = y[:, :split]
    u_ref[...] = y[:, split:]


def in_projection(x, g, w_bf16, split, *, tm=512):
    m, k = x.shape
    n = w_bf16.shape[1]
    tm = _tile(m, tm)
    return pl.pallas_call(
        _in_proj_kernel,
        grid=(m // tm,),
        in_specs=[
            pl.BlockSpec((tm, k), lambda i: (i, 0)),
            pl.BlockSpec((1, k), lambda i: (0, 0)),
            pl.BlockSpec((k, n), lambda i: (0, 0), pipeline_mode=pl.Buffered(1)),
        ],
        out_specs=[pl.BlockSpec((tm, split), lambda i: (i, 0)),
                   pl.BlockSpec((tm, n - split), lambda i: (i, 0))],
        out_shape=[jax.ShapeDtypeStruct((m, split), F32),
                   jax.ShapeDtypeStruct((m, n - split), F32)],
        compiler_params=_params(("parallel",)),
        name="in_projection",
    )(x, g.reshape(1, k), w_bf16)


def _mm_res_route_kernel(*refs, n_lhs, tm, half, pitch):
    lhs = refs[:n_lhs]
    ws = refs[n_lhs:2 * n_lhs]
    res_ref, g_ref, rt_ref, o_ref, aff_ref, slab_ref = refs[2 * n_lhs:]
    acc = res_ref[...]
    for l_ref, w_ref in zip(lhs, ws):
        acc = acc + jnp.dot(l_ref[...], w_ref[...], preferred_element_type=F32)
    o_ref[...] = acc
    _route_rows(acc, g_ref, rt_ref, aff_ref, slab_ref, tm=tm, half=half, pitch=pitch)


def matmul_residual_route(lhs_list, w_bf16, res, norm_g, router, *, seq, tm=512):
    m, d = res.shape
    kk = lhs_list[0].shape[1]
    n_lhs = len(lhs_list)
    e = router.shape[1]
    assert all(l.shape == (m, kk) for l in lhs_list) and w_bf16.shape == (n_lhs * kk, d)
    tm = _tile(seq, tm)
    nt = seq // tm
    half = d // 2
    pitch = _slab_pitch(half)
    in_specs = [pl.BlockSpec((tm, kk), lambda i: (i, 0)) for _ in lhs_list]
    in_specs += [pl.BlockSpec((kk, d), functools.partial(lambda i, r: (r, 0), r=r),
                              pipeline_mode=pl.Buffered(1)) for r in range(n_lhs)]
    in_specs += [pl.BlockSpec((tm, d), lambda i: (i, 0)),
                 pl.BlockSpec((1, d), lambda i: (0, 0)),
                 pl.BlockSpec((e, d), lambda i: (0, 0))]
    return pl.pallas_call(
        functools.partial(_mm_res_route_kernel, n_lhs=n_lhs, tm=tm, half=half, pitch=pitch),
        grid=(m // tm,),
        in_specs=in_specs,
        out_specs=[pl.BlockSpec((tm, d), lambda i: (i, 0)),
                   pl.BlockSpec((1, e, tm), lambda i: (i // nt, 0, i % nt)),
                   pl.BlockSpec((tm * pitch, LANES), lambda i: (i, 0))],
        out_shape=[jax.ShapeDtypeStruct((m, d), F32),
                   jax.ShapeDtypeStruct((m // seq, e, seq), F32),
                   jax.ShapeDtypeStruct((m * pitch, LANES), U32)],
        compiler_params=_params(("parallel",)),
        name="matmul_residual_route",
    )(*lhs_list, *([w_bf16] * n_lhs), res, norm_g.reshape(1, d), router.T.astype(F32))


def _t5_bucket(rel):
    nb = REL_BUCKETS // 2
    max_exact = nb // 2
    base = jnp.where(rel > 0, nb, 0)
    n = jnp.abs(rel)
    nf = jnp.maximum(n, 1).astype(F32)
    large = max_exact + (jnp.log(nf / max_exact) / math.log(REL_MAX_DIST / max_exact)
                         * (nb - max_exact)).astype(I32)
    large = jnp.minimum(large, nb - 1)
    return base + jnp.where(n < max_exact, n, large)


def _bias_kernel(rb_ref, bucket_ref, o_ref, *, n_heads):
    bkt = bucket_ref[...]
    row = lax.broadcasted_iota(I32, bkt.shape, 0)
    col = lax.broadcasted_iota(I32, bkt.shape, 1)
    in_window = jnp.abs(col - ATTN_BLOCK - row) <= ATTN_BLOCK
    for h in range(n_heads):
        acc = jnp.zeros(bkt.shape, F32)
        for k in range(REL_BUCKETS):
            acc = jnp.where(bkt == k, rb_ref[k, h], acc)
        o_ref[h * ATTN_BLOCK:(h + 1) * ATTN_BLOCK, :] = jnp.where(in_window, acc, NEG_INF)


def attention_bias(rel_bias):
    n_heads = rel_bias.shape[1]
    q_off = jnp.arange(ATTN_BLOCK, dtype=I32)
    c_off = jnp.arange(3 * ATTN_BLOCK, dtype=I32)
    bucket = _t5_bucket(c_off[None, :] - ATTN_BLOCK - q_off[:, None])
    return pl.pallas_call(
        functools.partial(_bias_kernel, n_heads=n_heads),
        in_specs=[pl.BlockSpec(memory_space=pltpu.SMEM),
                  pl.BlockSpec(memory_space=pltpu.VMEM)],
        out_specs=pl.BlockSpec(memory_space=pltpu.VMEM),
        out_shape=jax.ShapeDtypeStruct((n_heads * ATTN_BLOCK, 3 * ATTN_BLOCK), F32),
        name="attention_bias",
    )(rel_bias.astype(F32), bucket)


def _attn_kernel(sink_ref, q_ref, kp_ref, kc_ref, kn_ref, vp_ref, vc_ref, vn_ref, bias_ref,
                 o_ref, s_ref, p_ref, *, seq, n_kv, qb):
    blk = ATTN_BLOCK
    col = lax.broadcasted_iota(I32, (1, 3 * blk), 1)
    scale = HEAD_DIM ** -0.5
    rc = ATTN_ROW_CHUNK

    def band(p_ref, c_ref, n_ref, ksl, j):
        parts = []
        for i in (j - 1, j, j + 1):
            if i < 0:
                parts.append(p_ref[0, :, ksl])
            elif i >= qb:
                parts.append(n_ref[0, :, ksl])
            else:
                parts.append(c_ref[0, i * blk:(i + 1) * blk, ksl])
        return jnp.concatenate(parts, axis=0).astype(BF16)

    for j in range(qb):
        kpos = (pl.program_id(1) * qb + j - 1) * blk + col
        in_seq = (kpos >= 0) & (kpos < seq)
        rows = slice(j * blk, (j + 1) * blk)
        for kh in range(n_kv):
            ksl = slice(kh * HEAD_DIM, (kh + 1) * HEAD_DIM)
            kband = band(kp_ref, kc_ref, kn_ref, ksl, j)
            vband = band(vp_ref, vc_ref, vn_ref, ksl, j)
            heads = [kh * KV_GROUP + g for g in range(KV_GROUP)]
            qs = jnp.concatenate([q_ref[0, rows, h * HEAD_DIM:(h + 1) * HEAD_DIM] for h in heads],
                                 axis=0).astype(BF16)
            s_ref[...] = lax.dot_general(qs, kband, (((1,), (1,)), ((), ())),
                                         preferred_element_type=F32)

            def softmax_rows(c, carry, kh=kh, in_seq=in_seq):
                r0 = pl.multiple_of(c * rc, rc)
                bias = bias_ref[pl.ds(kh * KV_GROUP * blk + r0, rc), :]
                s = jnp.where(in_seq, s_ref[pl.ds(r0, rc), :] * scale + bias, NEG_INF)
                sink = sink_ref[kh * KV_GROUP + c // (blk // rc)]
                m = jnp.maximum(jnp.max(s, axis=-1, keepdims=True), sink)
                p = jnp.exp(s - m)
                denom = jnp.sum(p, axis=-1, keepdims=True) + jnp.exp(sink - m)
                p_ref[pl.ds(r0, rc), :] = (p / denom).astype(BF16)
                return carry

            lax.fori_loop(0, KV_GROUP * blk // rc, softmax_rows, 0, unroll=True)
            o = jnp.dot(p_ref[...], vband, preferred_element_type=F32)
            for g, h in enumerate(heads):
                o_ref[0, rows, h * HEAD_DIM:(h + 1) * HEAD_DIM] = (
                    o[g * blk:(g + 1) * blk].astype(o_ref.dtype))


def windowed_attention(qkv, sink, bias, *, attn_w, kv_w, qb=4):
    b, s, _ = qkv.shape
    blk = ATTN_BLOCK
    nblk = s // blk
    qb = _tile(nblk, qb)
    n_kv = kv_w // HEAD_DIM
    kcol = attn_w // kv_w
    assert attn_w % kv_w == 0 and n_kv * KV_GROUP * HEAD_DIM == attn_w

    def edge_spec(col, first):
        return pl.BlockSpec(
            (1, blk, kv_w),
            lambda bi, n: (bi, jnp.clip(n * qb + (-1 if first else qb), 0, nblk - 1), col))

    def mid_spec(col):
        return pl.BlockSpec((1, qb * blk, kv_w), lambda bi, n: (bi, n, col))

    return pl.pallas_call(
        functools.partial(_attn_kernel, seq=s, n_kv=n_kv, qb=qb),
        grid=(b, nblk // qb),
        in_specs=[
            pl.BlockSpec(memory_space=pltpu.SMEM),
            pl.BlockSpec((1, qb * blk, attn_w), lambda bi, n: (bi, n, 0)),
            edge_spec(kcol, True), mid_spec(kcol), edge_spec(kcol, False),
            edge_spec(kcol + 1, True), mid_spec(kcol + 1), edge_spec(kcol + 1, False),
            pl.BlockSpec(bias.shape, lambda bi, n: (0, 0)),
        ],
        out_specs=pl.BlockSpec((1, qb * blk, attn_w), lambda bi, n: (bi, n, 0)),
        out_shape=jax.ShapeDtypeStruct((b, s, attn_w), BF16),
        scratch_shapes=[pltpu.VMEM((KV_GROUP * blk, 3 * blk), F32),
                        pltpu.VMEM((KV_GROUP * blk, 3 * blk), BF16)],
        compiler_params=_params(("parallel", "parallel")),
        name="windowed_attention",
    )(sink.astype(F32), qkv, qkv, qkv, qkv, qkv, qkv, qkv, bias)


def _s5_operands(a_re, a_im, log_dt, b_re, b_im, c_re, c_im, chunk, n_levels):
    ell = chunk
    p = a_re.shape[-1]
    dt = jnp.exp(log_dt)[..., None]
    mag = jnp.exp(a_re * dt)
    lb_re = mag * jnp.cos(a_im * dt)
    lb_im = mag * jnp.sin(a_im * dt)
    den = a_re * a_re + a_im * a_im
    nr = lb_re - 1.0
    coef_re = (nr * a_re + lb_im * a_im) / den
    coef_im = (lb_im * a_re - nr * a_im) / den
    bb_re = coef_re[..., None] * b_re - coef_im[..., None] * b_im
    bb_im = coef_re[..., None] * b_im + coef_im[..., None] * b_re
    bt_re, bt_im = bb_re.transpose(0, 1, 3, 2), bb_im.transpose(0, 1, 3, 2)

    def powers(tau):
        t = tau.astype(F32)[None, None, :, None]
        pm = jnp.exp((a_re * dt)[:, :, None, :] * t)
        ang = (a_im * dt)[:, :, None, :] * t
        return pm * jnp.cos(ang), pm * jnp.sin(ang)

    g = a_re.shape[1]
    assert 2 * p == LANES and g % 2 == 0
    odd = (jnp.arange(g) % 2 == 1)[:, None, None]
    cat = lambda *ts: jnp.concatenate(ts, axis=-1)
    lp = lambda t: jnp.pad(t, [(0, 0)] * (t.ndim - 1) + [(0, LANES - p)])
    sp = lambda t: jnp.where(odd, cat(jnp.zeros_like(t), t), cat(t, jnp.zeros_like(t)))
    ii = jnp.arange(ell)
    dn_re, dn_im = powers(ell - 1 - ii)
    up_re, up_im = powers(ii)
    e1_re, e1_im = powers(ii + 1)
    e2_re, e2_im = powers(ell - ii)
    lv_re, lv_im = powers(ell * (2 ** jnp.arange(n_levels)))
    lev = SUBLANES * pl.cdiv(n_levels, SUBLANES)
    lvp = lambda t: jnp.pad(sp(t), ((0, 0), (0, lev - n_levels), (0, 0)))
    wide = jnp.concatenate([
        cat(sp(dn_re[0]), sp(dn_re[0]), sp(up_re[1]), sp(up_re[1])),
        cat(sp(dn_im[0]), sp(dn_im[0]), sp(up_im[1]), sp(up_im[1])),
        cat(sp(e1_re[0]), sp(e1_re[0]), sp(e2_re[1]), sp(e2_re[1])),
        cat(sp(e1_im[0]), sp(e1_im[0]), sp(e2_im[1]), sp(e2_im[1])),
        cat(sp(bt_re[0]), sp(bt_im[0]), sp(bt_re[1]), sp(bt_im[1])),
        cat(-sp(bt_im[0]), sp(bt_re[0]), -sp(bt_im[1]), sp(bt_re[1])),
        cat(sp(c_re[0]), -sp(c_im[0]), sp(c_re[1]), -sp(c_im[1])),
        cat(-sp(c_im[0]), -sp(c_re[0]), -sp(c_im[1]), -sp(c_re[1])),
        cat(lvp(lv_re[0]), lvp(lv_re[0]), lvp(lv_re[1]), lvp(lv_re[1])),
        cat(-lvp(lv_im[0]), lvp(lv_im[0]), -lvp(lv_im[1]), lvp(lv_im[1])),
    ], axis=1)

    def lag_rows(fwd, bwd):
        f = jnp.pad(lp(fwd), ((0, 0), (ell - 1, 1), (0, 0)))
        b = jnp.pad(lp(bwd), ((0, 0), (0, ell), (0, 0)))
        return cat(f, b)

    narrow = jnp.concatenate([
        lag_rows(up_re[0], dn_re[1]), lag_rows(up_im[0], dn_im[1]),
        cat(lp(c_re[0]), lp(c_re[1])), cat(lp(c_im[0]), lp(c_im[1])),
        cat(lp(bt_re[0]), lp(bt_re[1])), cat(lp(bt_im[0]), lp(bt_im[1])),
    ], axis=1)
    return wide, narrow


def _s5_pack_kernel(u_ref, o_ref, t_ref, *, n_lt, h, chunk):
    def transpose_block(lt, c):
        src = pl.multiple_of(lt * LANES, LANES)
        dst = pl.multiple_of(lt * S5_PITCH, SUBLANES)
        t_ref[pl.ds(dst, LANES), :] = u_ref[pl.ds(src, LANES), :].T
        return c

    lax.fori_loop(0, n_lt, transpose_block, 0, unroll=8)
    n_sub = LANES // chunk

    def pack_group(g8, c):
        for kt in range(h // n_sub):
            ms = [t_ref[pl.ds(g8 * h + kt * n_sub + kk, n_lt, stride=S5_PITCH), :] for kk in range(n_sub)]
            for c_lo in range(n_sub):
                tile = jnp.concatenate([m[:, c_lo * chunk:(c_lo + 1) * chunk] for m in ms], axis=1)
                o_ref[g8, c_lo * n_lt:(c_lo + 1) * n_lt, kt * LANES:(kt + 1) * LANES] = tile.astype(BF16)
        return c

    lax.fori_loop(0, LANES // h, pack_group, 0)


def _s5_unpack_kernel(y_ref, o_ref, t_ref, *, n_lt, h, chunk):
    n_sub = LANES // chunk

    def unpack_group(g8, c):
        for hh in range(h):
            lane0 = (hh // n_sub) * LANES + (hh % n_sub) * chunk
            m = jnp.concatenate([y_ref[g8, c_lo * n_lt:(c_lo + 1) * n_lt, lane0:lane0 + chunk]
                                 for c_lo in range(n_sub)], axis=1)
            t_ref[pl.ds(g8 * h + hh, n_lt, stride=S5_PITCH), :] = m
        return c

    lax.fori_loop(0, LANES // h, unpack_group, 0)

    def transpose_block(lt, c):
        src = pl.multiple_of(lt * S5_PITCH, SUBLANES)
        dst = pl.multiple_of(lt * LANES, LANES)
        o_ref[pl.ds(dst, LANES), :] = t_ref[pl.ds(src, LANES), :].T
        return c

    lax.fori_loop(0, n_lt, transpose_block, 0, unroll=8)


def _shift_rows(z, r, seg):
    if r == 0:
        return z
    rows = z.shape[0]
    rolled = pltpu.roll(z, r % rows, 0)
    assert seg & (seg - 1) == 0
    pos = lax.broadcasted_iota(I32, z.shape, 0) & (seg - 1)
    keep = (pos >= r) if r > 0 else (pos < seg + r)
    return jnp.where(keep, rolled, 0.0)


def _shift_chunks(blocks, d, seg):
    n_sub = len(blocks)
    out = []
    for c_lo in range(n_sub):
        e, s_lo = divmod(c_lo - d, n_sub)
        out.append(_shift_rows(blocks[s_lo], -e, seg))
    return out


def _s5_group_tables(w, nr, perm, *, n_levels, chunk, h):
    ell, lh = chunk, chunk * h
    lev = SUBLANES * pl.cdiv(n_levels, SUBLANES)
    nt = (((1,), (1,)), ((), ()))
    pwf_re, pwf_im, pwe_re, pwe_im = (w[k * ell:(k + 1) * ell] for k in range(4))
    o = 4 * ell
    bb1, bb2, cc1, cc2 = (w[o + k * h:o + (k + 1) * h] for k in range(4))
    o += 4 * h
    pa_all, pb_all = w[o:o + lev], w[o + lev:o + 2 * lev]
    f_tab = jnp.concatenate([pwf_re * bb1[k:k + 1, :] + pwf_im * bb2[k:k + 1, :] for k in range(h)],
                            axis=0).astype(BF16)
    et_tab = jnp.concatenate([pwe_re * cc1[k:k + 1, :] + pwe_im * cc2[k:k + 1, :] for k in range(h)],
                             axis=0).astype(BF16)
    pwm_re, pwm_im = nr[:2 * ell], nr[2 * ell:4 * ell]
    o = 4 * ell
    cr, ci, br, bi = (nr[o + k * h:o + (k + 1) * h] for k in range(4))
    rep = lambda x: jnp.concatenate([jnp.broadcast_to(x[j:j + 1, :], (h, x.shape[1]))
                                     for j in range(x.shape[0])], axis=0)
    tile = lambda t: jnp.concatenate([t] * (2 * ell), axis=0)
    pr, pi = rep(pwm_re), rep(pwm_im)
    crt, cit = tile(cr), tile(ci)
    a = jnp.concatenate([pr * crt - pi * cit, pr * cit + pi * crt], axis=1)
    b = jnp.concatenate([br, -bi], axis=1)
    a1 = a.astype(BF16)
    a2 = (a - a1.astype(F32)).astype(BF16)
    b1 = b.astype(BF16)
    b2 = (b - b1.astype(F32)).astype(BF16)
    lead = lax.dot_general(jnp.concatenate([b1, b2], axis=0), a1, nt, preferred_element_type=F32)
    strip = lead[:h] + lead[h:] + lax.dot_general(b1, a2, nt, preferred_element_type=F32)
    t_jk = jnp.concatenate([strip[:, (ell - 1 - j) * h:(ell - 1 - j) * h + lh] for j in range(ell)],
                           axis=0).astype(BF16)
    t_tab = lax.dot_general(jnp.dot(perm, t_jk, preferred_element_type=F32).astype(BF16), perm, nt,
                            preferred_element_type=F32).astype(BF16)
    return f_tab, et_tab, t_tab, pa_all, pb_all


def _s5_kernel(u_ref, wide_ref, narrow_ref, perm_ref, o_ref, *, seg, n_levels, chunk, h):
    n_sub = LANES // chunk
    nt = (((1,), (1,)), ((), ()))
    perm = perm_ref[...]
    tabs = [_s5_group_tables(wide_ref[gi], narrow_ref[gi], perm, n_levels=n_levels, chunk=chunk, h=h)
            for gi in range(2)]
    us = [u_ref[gi] for gi in range(2)]
    z = sum(jnp.dot(u, t[0], preferred_element_type=F32) for u, t in zip(us, tabs))
    pa_all = tabs[0][3] + tabs[1][3]
    pb_all = tabs[0][4] + tabs[1][4]
    n_lt = z.shape[0] // n_sub
    half = 2 * LANES

    def swap(x):
        return jnp.concatenate([x[:, LANES:], x[:, :LANES]], axis=1)

    zf = [z[c * n_lt:(c + 1) * n_lt, :half] for c in range(n_sub)]
    zb = [z[c * n_lt:(c + 1) * n_lt, half:] for c in range(n_sub)]
    for k in range(n_levels):
        pa = pa_all[k:k + 1, :]
        pb = pb_all[k:k + 1, :]
        sf = _shift_chunks(zf, 2 ** k, seg)
        sb = _shift_chunks(zb, -(2 ** k), seg)
        zf = [a + pa[:, :half] * s + pb[:, :half] * swap(s) for a, s in zip(zf, sf)]
        zb = [a + pa[:, half:] * s + pb[:, half:] * swap(s) for a, s in zip(zb, sb)]
    hin = jnp.concatenate([jnp.concatenate(_shift_chunks(zf, 1, seg), axis=0),
                           jnp.concatenate(_shift_chunks(zb, -1, seg), axis=0)], axis=1).astype(BF16)
    for gi in range(2):
        y = jnp.dot(us[gi], tabs[gi][2], preferred_element_type=F32)
        o_ref[gi] = y + lax.dot_general(hin, tabs[gi][1], nt, preferred_element_type=F32)


def s5_scan(u, operands, *, seq, chunk, n_levels):
    wide, narrow = operands
    t, w = u.shape
    g = wide.shape[0]
    h = w // g
    lh = chunk * h
    n_lt = t // LANES
    rows = t // chunk
    gpt = LANES // h
    assert t % LANES == 0 and seq % LANES == 0 and LANES % chunk == 0 and h % (LANES // chunk) == 0
    scratch = pltpu.VMEM((n_lt * S5_PITCH, LANES), F32)
    ug = pl.pallas_call(
        functools.partial(_s5_pack_kernel, n_lt=n_lt, h=h, chunk=chunk),
        grid=(w // LANES,),
        in_specs=[pl.BlockSpec((t, LANES), lambda q: (0, q))],
        out_specs=pl.BlockSpec((gpt, rows, lh), lambda q: (q, 0, 0)),
        out_shape=jax.ShapeDtypeStruct((g, rows, lh), BF16),
        scratch_shapes=[scratch],
        compiler_params=_params(("parallel",)),
        name="s5_pack",
    )(u)
    kj = jnp.arange(lh)
    perm = (kj[:, None] // chunk + (kj[:, None] % chunk) * h == kj[None, :]).astype(BF16)
    yg = pl.pallas_call(
        functools.partial(_s5_kernel, seg=seq // LANES, n_levels=n_levels, chunk=chunk, h=h),
        grid=(g // 2,),
        in_specs=[
            pl.BlockSpec((2, rows, lh), lambda i: (i, 0, 0)),
            pl.BlockSpec((2,) + wide.shape[1:], lambda i: (i, 0, 0)),
            pl.BlockSpec((2,) + narrow.shape[1:], lambda i: (i, 0, 0)),
            pl.BlockSpec((lh, lh), lambda i: (0, 0)),
        ],
        out_specs=pl.BlockSpec((2, rows, lh), lambda i: (i, 0, 0)),
        out_shape=jax.ShapeDtypeStruct((g, rows, lh), F32),
        compiler_params=_params(("parallel",)),
        name="s5_scan",
    )(ug, wide, narrow, perm)
    return pl.pallas_call(
        functools.partial(_s5_unpack_kernel, n_lt=n_lt, h=h, chunk=chunk),
        grid=(w // LANES,),
        in_specs=[pl.BlockSpec((gpt, rows, lh), lambda q: (q, 0, 0))],
        out_specs=pl.BlockSpec((t, LANES), lambda q: (0, q)),
        out_shape=jax.ShapeDtypeStruct((t, w), F32),
        scratch_shapes=[scratch],
        compiler_params=_params(("parallel",)),
        name="s5_unpack",
    )(yg)


def _glu_kernel(y_ref, u_ref, d_ref, w_ref, b_ref, o_ref):
    y = y_ref[...] + d_ref[...] * u_ref[...]
    g = _gelu(y)
    z = jnp.dot(g.astype(BF16), w_ref[...], preferred_element_type=F32) + b_ref[...]
    o_ref[...] = (g * jax.nn.sigmoid(z)).astype(o_ref.dtype)


def s5_glu(y, u, d_skip, glu_w_bf16, glu_b, *, tm=512):
    m, w = y.shape
    tm = _tile(m, tm)
    return pl.pallas_call(
        _glu_kernel,
        grid=(m // tm,),
        in_specs=[
            pl.BlockSpec((tm, w), lambda i: (i, 0)),
            pl.BlockSpec((tm, w), lambda i: (i, 0)),
            pl.BlockSpec((1, w), lambda i: (0, 0)),
            pl.BlockSpec((w, w), lambda i: (0, 0)),
            pl.BlockSpec((1, w), lambda i: (0, 0)),
        ],
        out_specs=pl.BlockSpec((tm, w), lambda i: (i, 0)),
        out_shape=jax.ShapeDtypeStruct((m, w), BF16),
        compiler_params=_params(("parallel",)),
        name="s5_glu",
    )(y, u, d_skip.reshape(1, w), glu_w_bf16, glu_b.reshape(1, w))


def _gmlp_kernel(x_ref, s0_ref, s1_ref, ng_ref, w_ref, g_ref, b_ref, ws_ref, bs_ref, o_ref, xn_ref, *,
                 width, n_heads, tm, half, pitch):
    hd = width // n_heads
    ck = GMLP_CHUNK
    x = x_ref[...] + _unslab(s0_ref, s1_ref, tm=tm, half=half, pitch=pitch)
    xn_ref[...] = x
    hn = _rms(x, ng_ref[...]).astype(BF16)
    z = _gelu(jnp.dot(hn, w_ref[...], preferred_element_type=F32))
    u, v = z[:, :width], z[:, width:]
    mu = jnp.mean(v, axis=-1, keepdims=True)
    var = jnp.mean(jnp.square(v - mu), axis=-1, keepdims=True)
    vn = ((v - mu) * lax.rsqrt(var + EPS) * g_ref[...] + b_ref[...]).astype(BF16)
    for c in range(tm // ck):
        rows = slice(c * ck, (c + 1) * ck)
        for h in range(n_heads):
            sl = slice(h * hd, (h + 1) * hd)
            mixed = jnp.dot(ws_ref[h], vn[rows, sl], preferred_element_type=F32) + bs_ref[:, h:h + 1]
            o_ref[rows, sl] = (u[rows, sl] * mixed).astype(o_ref.dtype)


def gmlp_gate(x, moe_slabs, norm_g, w_in_bf16, ln_g, ln_b, w_s_bf16, b_s, *, seq, tm=512):
    m, d = x.shape
    w2 = w_in_bf16.shape[1]
    width = w2 // 2
    n_heads = w_s_bf16.shape[0]
    ck = GMLP_CHUNK
    tm = max(ck, _tile(seq, tm))
    nt = seq // tm
    half = d // 2
    pitch = _slab_pitch(half)
    return pl.pallas_call(
        functools.partial(_gmlp_kernel, width=width, n_heads=n_heads, tm=tm, half=half, pitch=pitch),
        grid=(m // tm,),
        in_specs=[
            pl.BlockSpec((tm, d), lambda i: (i, 0)),
            pl.BlockSpec((1, 1, tm * pitch, LANES), lambda i: (i // nt, 0, i % nt, 0)),
            pl.BlockSpec((1, 1, tm * pitch, LANES), lambda i: (i // nt, 1, i % nt, 0)),
            pl.BlockSpec((1, d), lambda i: (0, 0)),
            pl.BlockSpec((d, w2), lambda i: (0, 0), pipeline_mode=pl.Buffered(1)),
            pl.BlockSpec((1, width), lambda i: (0, 0)),
            pl.BlockSpec((1, width), lambda i: (0, 0)),
            pl.BlockSpec((n_heads, ck, ck), lambda i: (0, 0, 0)),
            pl.BlockSpec((ck, n_heads), lambda i: (0, 0)),
        ],
        out_specs=[pl.BlockSpec((tm, width), lambda i: (i, 0)),
                   pl.BlockSpec((tm, d), lambda i: (i, 0))],
        out_shape=[jax.ShapeDtypeStruct((m, width), BF16),
                   jax.ShapeDtypeStruct((m, d), F32)],
        compiler_params=_params(("parallel",)),
        name="gmlp_gate",
    )(x, moe_slabs, moe_slabs, norm_g.reshape(1, d), w_in_bf16, ln_g.reshape(1, width),
      ln_b.reshape(1, width), w_s_bf16, b_s.T.astype(F32))


def _slab_pitch(half):
    return SUBLANES * pl.cdiv(half // LANES, SUBLANES)


def _route_rows(x, g_ref, rt_ref, aff_ref, slab_ref, *, tm, half, pitch):
    h = _rms(x, g_ref[...])
    nt = (((1,), (1,)), ((), ()))
    e = rt_ref.shape[0]
    h1 = h.astype(BF16)
    h1f = h1.astype(F32)
    h2 = (h - h1f).astype(BF16)
    r1 = rt_ref[...].astype(BF16)
    r2 = (rt_ref[...] - r1.astype(F32)).astype(BF16)
    lead = lax.dot_general(jnp.concatenate([r1, r2], axis=0), h1, nt, preferred_element_type=F32)
    logits = lead[:e] + lead[e:] + lax.dot_general(r1, h2, nt, preferred_element_type=F32)
    mx = jnp.max(logits, axis=0, keepdims=True)
    ex = jnp.exp(logits - mx)
    aff_ref[0] = ex / jnp.sum(ex, axis=0, keepdims=True)
    lo = lax.bitcast_convert_type(h1f[:, :half], U32) >> 16
    hi = lax.bitcast_convert_type(h1f[:, half:], U32) & jnp.uint32(0xFFFF0000)
    word = lo | hi
    for s in range(half // LANES):
        slab_ref[pl.ds(s, tm, stride=pitch), :] = word[:, s * LANES:(s + 1) * LANES]
    if pitch > half // LANES:
        for s in range(half // LANES, pitch):
            slab_ref[pl.ds(s, tm, stride=pitch), :] = jnp.zeros((tm, LANES), U32)


def _lane_cumsum(m_bf16, tri):
    e, s = m_bf16.shape
    off = jnp.zeros((e, 1), F32)
    out = []
    for j in range(s // LANES):
        c = jnp.dot(m_bf16[:, j * LANES:(j + 1) * LANES], tri, preferred_element_type=F32) + off
        out.append(c)
        off = c[:, LANES - 1:LANES]
    return jnp.concatenate(out, axis=1)


def _topk_kernel(aff_ref, idx_ref, gate_ref, *, cap, kchunk):
    aff = aff_ref[0]
    e, s = aff.shape
    bits = lax.bitcast_convert_type(aff, I32)

    def count_ge(t):
        return jnp.sum((bits >= t).astype(F32), axis=1, keepdims=True)

    def body(_, carry):
        lo, hi = carry
        mid = lo + ((hi - lo) >> 1)
        ok = count_ge(mid) >= cap
        return jnp.where(ok, mid, lo), jnp.where(ok, hi, mid)

    lo0 = jnp.zeros((e, 1), I32)
    hi0 = jnp.full((e, 1), 0x7F800000, I32)
    thr, _ = lax.fori_loop(0, 32, body, (lo0, hi0))
    gt = bits > thr
    eq = bits == thr
    need = cap - jnp.sum(gt.astype(F32), axis=1, keepdims=True)
    tri = (lax.broadcasted_iota(I32, (LANES, LANES), 0)
           <= lax.broadcasted_iota(I32, (LANES, LANES), 1)).astype(BF16)
    eq_rank = _lane_cumsum(jnp.where(eq, 1.0, 0.0).astype(BF16), tri)
    sel = gt | (eq & (eq_rank <= need))
    pos = _lane_cumsum(jnp.where(sel, 1.0, 0.0).astype(BF16), tri)
    pos = jnp.where(sel, pos, 0.0)

    tok = lax.broadcasted_iota(I32, (1, s), 1)
    t_hi = (tok >> 6).astype(F32)
    t_lo = (tok & 63).astype(F32)
    slot = lax.broadcasted_iota(I32, (cap, kchunk), 0).astype(F32) + 1.0
    lrow = lax.broadcasted_iota(I32, (SUBLANES, s), 0)
    for ei in range(e):
        a = aff[ei:ei + 1, :]
        g1 = a.astype(BF16).astype(F32)
        g2 = (a - g1).astype(BF16).astype(F32)
        g3 = (a - g1) - g2
        lhs = jnp.where(lrow == 0, t_hi, jnp.where(lrow == 1, t_lo, jnp.where(
            lrow == 2, g1, jnp.where(lrow == 3, g2, jnp.where(lrow == 4, g3, 0.0)))))
        lhs = lhs.astype(BF16)
        acc = jnp.zeros((SUBLANES, cap), F32)
        for c0 in range(0, s, kchunk):
            onehot = jnp.where(pos[ei:ei + 1, c0:c0 + kchunk] == slot, 1.0, 0.0).astype(BF16)
            acc = acc + lax.dot_general(lhs[:, c0:c0 + kchunk], onehot, (((1,), (1,)), ((), ())),
                                        preferred_element_type=F32)
        idx_ref[0, ei:ei + 1, :] = (acc[0:1] * 64.0 + acc[1:2]).astype(I32)
        gate_ref[0, ei:ei + 1, :] = acc[2:3] + acc[3:4] + acc[4:5]


def moe_topk(aff, cap):
    b, e, s = aff.shape
    return pl.pallas_call(
        functools.partial(_topk_kernel, cap=cap, kchunk=_tile(s, 1024)),
        grid=(b,),
        in_specs=[pl.BlockSpec((1, e, s), lambda i: (i, 0, 0))],
        out_specs=[pl.BlockSpec((1, e, cap), lambda i: (i, 0, 0)),
                   pl.BlockSpec((1, e, cap), lambda i: (i, 0, 0))],
        out_shape=[jax.ShapeDtypeStruct((b, e, cap), I32),
                   jax.ShapeDtypeStruct((b, e, cap), F32)],
        compiler_params=_params(("parallel",)),
        name="moe_topk",
    )(aff)


def _gather_kernel(idx_ref, slab_ref, o_ref, rows_ref, *, cap, half, pitch, unroll):
    def body(j0, c):
        for u in range(unroll):
            j = j0 * unroll + u
            src = pl.multiple_of(idx_ref[0, 0, j] * pitch, pitch)
            rows_ref[pl.ds(pl.multiple_of(j * pitch, pitch), pitch), :] = slab_ref[pl.ds(src, pitch), :]
        return c

    lax.fori_loop(0, cap // unroll, body, 0)
    for s in range(half // LANES):
        word = rows_ref[pl.ds(s, cap, stride=pitch), :]
        lo = lax.bitcast_convert_type(word << 16, F32)
        hi = lax.bitcast_convert_type(word & jnp.uint32(0xFFFF0000), F32)
        o_ref[0, :, s * LANES:(s + 1) * LANES] = lo.astype(BF16)
        o_ref[0, :, half + s * LANES:half + (s + 1) * LANES] = hi.astype(BF16)


def moe_gather(idx, slabs, *, seq, d):
    b, e, cap = idx.shape
    half = d // 2
    pitch = _slab_pitch(half)
    return pl.pallas_call(
        functools.partial(_gather_kernel, cap=cap, half=half, pitch=pitch, unroll=8),
        grid=(b, e),
        in_specs=[
            pl.BlockSpec((1, 1, cap), lambda bi, ei: (bi * e + ei, 0, 0), memory_space=pltpu.SMEM),
            pl.BlockSpec((seq * pitch, LANES), lambda bi, ei: (bi, 0)),
        ],
        out_specs=pl.BlockSpec((1, cap, d), lambda bi, ei: (ei, bi, 0)),
        out_shape=jax.ShapeDtypeStruct((e, b * cap, d), BF16),
        scratch_shapes=[pltpu.VMEM((cap * pitch, LANES), U32)],
        compiler_params=_params(("parallel", "arbitrary")),
        name="moe_gather",
    )(idx.reshape(b * e, 1, cap), slabs)


def _ffn_kernel(xs_ref, w1_ref, w3_ref, w2_ref, o_ref, hid_ref, *, nt, tf):
    step = pl.program_id(1)

    @pl.when(step < nt)
    def _():
        xs = xs_ref[0]
        a = jnp.dot(xs, w1_ref[0, 0].astype(BF16), preferred_element_type=F32)
        g = jnp.dot(xs, w3_ref[0, 0].astype(BF16), preferred_element_type=F32)
        hid_ref[step] = (a * jax.nn.sigmoid(a) * g).astype(BF16)

    @pl.when(step >= nt)
    def _():
        acc = jnp.zeros(o_ref.shape[1:], F32)
        for k in range(nt):
            acc = acc + jnp.dot(hid_ref[k], w2_ref[0, 0, k * tf:(k + 1) * tf, :].astype(BF16),
                                preferred_element_type=F32)
        o_ref[0] = acc


def moe_ffn(xs, w1, w3, w2, layer, *, tf=256):
    e, m, d = xs.shape
    f = w1.shape[3]
    tf = _tile(f, tf)
    tn = _tile(d, tf)
    nt = f // tf
    nn = d // tn
    return pl.pallas_call(
        functools.partial(_ffn_kernel, nt=nt, tf=tf),
        grid=(e, nt + nn),
        in_specs=[
            pl.BlockSpec((1, m, d), lambda ei, t: (ei, 0, 0)),
            pl.BlockSpec((1, 1, d, tf), lambda ei, t: (layer, ei, 0, jnp.minimum(t, nt - 1))),
            pl.BlockSpec((1, 1, d, tf), lambda ei, t: (layer, ei, 0, jnp.minimum(t, nt - 1))),
            pl.BlockSpec((1, 1, f, tn), lambda ei, t: (layer, ei, 0, jnp.maximum(t - nt, 0))),
        ],
        out_specs=pl.BlockSpec((1, m, tn), lambda ei, t: (ei, 0, jnp.maximum(t - nt, 0))),
        out_shape=jax.ShapeDtypeStruct((e, m, d), F32),
        scratch_shapes=[pltpu.VMEM((nt, m, tf), BF16)],
        compiler_params=_params(("parallel", "arbitrary")),
        name="moe_ffn",
    )(xs, w1, w3, w2)


def _combine_kernel(idx_ref, gate_ref, ys_ref, o_ref, rows_ref, *, cap, half, pitch, unroll):
    ei = pl.program_id(2)

    @pl.when(ei == 0)
    def _():
        o_ref[...] = jnp.zeros(o_ref.shape, F32)

    for s in range(half // LANES):
        rows_ref[pl.ds(s, cap, stride=pitch), :] = ys_ref[0, :, s * LANES:(s + 1) * LANES]

    def body(j0, c):
        dsts, vals = [], []
        for u in range(unroll):
            j = j0 * unroll + u
            dst = pl.multiple_of(idx_ref[0, 0, j] * pitch, pitch)
            src = pl.multiple_of(j * pitch, pitch)
            vals.append(o_ref[0, 0, pl.ds(dst, pitch), :]
                        + gate_ref[0, 0, j] * rows_ref[pl.ds(src, pitch), :])
            dsts.append(dst)
        for dst, val in zip(dsts, vals):
            o_ref[0, 0, pl.ds(dst, pitch), :] = val
        return c

    lax.fori_loop(0, cap // unroll, body, 0)


def moe_combine(idx, gate, ys, *, seq, d):
    b, e, cap = idx.shape
    half = d // 2
    pitch = _slab_pitch(half)
    return pl.pallas_call(
        functools.partial(_combine_kernel, cap=cap, half=half, pitch=pitch, unroll=8),
        grid=(b, 2, e),
        in_specs=[
            pl.BlockSpec((1, 1, cap), lambda bi, dh, ei: (bi * e + ei, 0, 0), memory_space=pltpu.SMEM),
            pl.BlockSpec((1, 1, cap), lambda bi, dh, ei: (bi * e + ei, 0, 0), memory_space=pltpu.SMEM),
            pl.BlockSpec((1, cap, half), lambda bi, dh, ei: (ei, bi, dh)),
        ],
        out_specs=pl.BlockSpec((1, 1, seq * pitch, LANES), lambda bi, dh, ei: (bi, dh, 0, 0)),
        out_shape=jax.ShapeDtypeStruct((b, 2, seq * pitch, LANES), F32),
        scratch_shapes=[pltpu.VMEM((cap * pitch, LANES), F32)],
        compiler_params=_params(("parallel", "parallel", "arbitrary")),
        name="moe_combine",
    )(idx.reshape(b * e, 1, cap), gate.reshape(b * e, 1, cap), ys)


def _unslab(s0_ref, s1_ref, *, tm, half, pitch):
    parts = []
    for s_ref in (s0_ref, s1_ref):
        for s in range(half // LANES):
            parts.append(s_ref[0, 0, pl.ds(s, tm, stride=pitch), :])
    return jnp.concatenate(parts, axis=1)


def _unslab_kernel(*refs, tm, half, pitch, final_norm):
    x_ref, s0_ref, s1_ref = refs[:3]
    o_ref = refs[-1]
    y = x_ref[0] + _unslab(s0_ref, s1_ref, tm=tm, half=half, pitch=pitch)
    if final_norm:
        y = _rms(y, refs[3][...])
    o_ref[0] = y


def moe_residual(x, moe_slabs, final_g=None, *, tm=256):
    b, s, d = x.shape
    half = d // 2
    pitch = _slab_pitch(half)
    tm = _tile(s, tm)
    final_norm = final_g is not None
    in_specs = [
        pl.BlockSpec((1, tm, d), lambda bi, j: (bi, j, 0)),
        pl.BlockSpec((1, 1, tm * pitch, LANES), lambda bi, j: (bi, 0, j, 0)),
        pl.BlockSpec((1, 1, tm * pitch, LANES), lambda bi, j: (bi, 1, j, 0)),
    ]
    args = [x, moe_slabs, moe_slabs]
    if final_norm:
        in_specs.append(pl.BlockSpec((1, d), lambda bi, j: (0, 0)))
        args.append(final_g.reshape(1, d))
    return pl.pallas_call(
        functools.partial(_unslab_kernel, tm=tm, half=half, pitch=pitch, final_norm=final_norm),
        grid=(b, s // tm),
        in_specs=in_specs,
        out_specs=pl.BlockSpec((1, tm, d), lambda bi, j: (bi, j, 0)),
        out_shape=jax.ShapeDtypeStruct((b, s, d), F32),
        compiler_params=_params(("parallel", "parallel")),
        name="moe_residual",
    )(*args)


def expert_choice_ffn(aff, slabs, w1, w3, w2, layer, *, seq, d):
    cap = CAPACITY_FACTOR * seq // N_EXPERTS
    idx, gate = moe_topk(aff, cap)
    xs = moe_gather(idx, slabs, seq=seq, d=d)
    ys = moe_ffn(xs, w1, w3, w2, layer)
    return moe_combine(idx, gate, ys, seq=seq, d=d)


def kernel(x, rel_bias, mix_norm, ffn_norm, final_norm, even_w_in, attn_sink, ssm_a_re, ssm_a_im, ssm_log_dt, ssm_b_re, ssm_b_im, ssm_c_re, ssm_c_im, ssm_d, glu_w, glu_b, even_w_out, odd_w_in, sgu_ln_g, sgu_ln_b, sgu_w, sgu_b, odd_w_out, router, moe_w1, moe_w3, moe_w2):
    b, s, d = x.shape
    t = b * s
    depth = mix_norm.shape[0]
    ssm_w = ssm_d.shape[-1]
    attn_w = even_w_out.shape[1] - ssm_w
    kv_w = (even_w_in.shape[-1] - attn_w - ssm_w) // 2
    bias = attention_bias(rel_bias)

    x2 = x.reshape(t, d)
    moe = None
    for layer in range(depth):
        i = layer // 2
        if layer % 2 == 0:
            if moe is not None:
                x2 = moe_residual(x2.reshape(b, s, d), moe).reshape(t, d)
            qkv, u = in_projection(x2, mix_norm[layer], even_w_in[i].astype(BF16), attn_w + 2 * kv_w)
            attn = windowed_attention(qkv.reshape(b, s, -1), attn_sink[i], bias,
                                      attn_w=attn_w, kv_w=kv_w)
            n_levels = max(1, (s // S5_CHUNK - 1).bit_length())
            operands = _s5_operands(ssm_a_re[i], ssm_a_im[i], ssm_log_dt[i], ssm_b_re[i], ssm_b_im[i],
                                    ssm_c_re[i], ssm_c_im[i], S5_CHUNK, n_levels)
            y = s5_scan(u, operands, seq=s, chunk=S5_CHUNK, n_levels=n_levels)
            ssm = s5_glu(y, u, ssm_d[i], glu_w[i].astype(BF16), glu_b[i])
            mixed, w_out = [attn.reshape(t, attn_w), ssm], even_w_out[i]
        else:
            gated, x2 = gmlp_gate(x2, moe, mix_norm[layer], odd_w_in[i].astype(BF16), sgu_ln_g[i],
                                  sgu_ln_b[i], sgu_w[i].astype(BF16), sgu_b[i], seq=s)
            mixed, w_out = [gated], odd_w_out[i]
        x2, aff, slabs = matmul_residual_route(mixed, w_out.astype(BF16), x2, ffn_norm[layer],
                                               router[layer], seq=s)
        moe = expert_choice_ffn(aff, slabs, moe_w1, moe_w3, moe_w2, layer, seq=s, d=d)
    return moe_residual(x2.reshape(b, s, d), moe, final_norm)
```

```python
import functools
import math

import jax
import jax.numpy as jnp
from jax import lax
from jax.experimental import pallas as pl
from jax.experimental.pallas import tpu as pltpu

F32 = jnp.float32
BF16 = jnp.bfloat16
I32 = jnp.int32
U32 = jnp.uint32

EPS = 1e-6
NEG_INF = -1e30

LANES = 128
SUBLANES = 8
VMEM_LIMIT = 56 << 20

ATTN_BLOCK = 128
ATTN_ROW_CHUNK = 32
HEAD_DIM = 128
KV_GROUP = 4
REL_BUCKETS = 32
REL_MAX_DIST = 128
SSM_GROUP = 16
SSM_STATE = 64
S5_CHUNK = 32
S5_PITCH = 136
GMLP_CHUNK = 128
N_EXPERTS = 16
CAPACITY_FACTOR = 2
MOE_EXPERTS_PER_STEP = 2


def _params(sem, vmem=VMEM_LIMIT):
    return pltpu.CompilerParams(dimension_semantics=sem, vmem_limit_bytes=vmem)


def _tile(n, want):
    t = min(n, want)
    while n % t:
        t //= 2
    return t


def _rms(x, g):
    ms = jnp.mean(x * x, axis=-1, keepdims=True)
    return x * lax.rsqrt(ms + EPS) * g


def _gelu(x):
    c = math.sqrt(2.0 / math.pi)
    return x * (0.5 * (1.0 + jnp.tanh(c * (x + 0.044715 * (x * x * x)))))


def _in_proj_kernel(x_ref, g_ref, w_ref, qkv_ref, u_ref):
    hn = _rms(x_ref[...], g_ref[...]).astype(BF16)
    y = jnp.dot(hn, w_ref[...], preferred_element_type=F32)
    split = qkv_ref.shape[1]
    qkv_ref[...] = y[:, :split]
    u_ref[...] = y[:, split:]


def in_projection(x, g, w_bf16, split, *, tm=512):
    m, k = x.shape
    n = w_bf16.shape[1]
    tm = _tile(m, tm)
    return pl.pallas_call(
        _in_proj_kernel,
        grid=(m // tm,),
        in_specs=[
            pl.BlockSpec((tm, k), lambda i: (i, 0)),
            pl.BlockSpec((1, k), lambda i: (0, 0)),
            pl.BlockSpec((k, n), lambda i: (0, 0), pipeline_mode=pl.Buffered(1)),
        ],
        out_specs=[pl.BlockSpec((tm, split), lambda i: (i, 0)),
                   pl.BlockSpec((tm, n - split), lambda i: (i, 0))],
        out_shape=[jax.ShapeDtypeStruct((m, split), F32),
                   jax.ShapeDtypeStruct((m, n - split), F32)],
        compiler_params=_params(("parallel",)),
        name="in_projection",
    )(x, g.reshape(1, k), w_bf16)


def _mm_res_route_kernel(*refs, n_lhs, tm, half, pitch):
    lhs = refs[:n_lhs]
    ws = refs[n_lhs:2 * n_lhs]
    res_ref, g_ref, rt_ref, o_ref, aff_ref, slab_ref = refs[2 * n_lhs:]
    acc = res_ref[...]
    for l_ref, w_ref in zip(lhs, ws):
        acc = acc + jnp.dot(l_ref[...], w_ref[...], preferred_element_type=F32)
    o_ref[...] = acc
    _route_rows(acc, g_ref, rt_ref, aff_ref, slab_ref, tm=tm, half=half, pitch=pitch)


def matmul_residual_route(lhs_list, w_bf16, res, norm_g, router, *, seq, tm=512):
    m, d = res.shape
    kk = lhs_list[0].shape[1]
    n_lhs = len(lhs_list)
    e = router.shape[1]
    assert all(l.shape == (m, kk) for l in lhs_list) and w_bf16.shape == (n_lhs * kk, d)
    tm = _tile(seq, tm)
    nt = seq // tm
    half = d // 2
    pitch = _slab_pitch(half)
    in_specs = [pl.BlockSpec((tm, kk), lambda i: (i, 0)) for _ in lhs_list]
    in_specs += [pl.BlockSpec((kk, d), functools.partial(lambda i, r: (r, 0), r=r),
                              pipeline_mode=pl.Buffered(1)) for r in range(n_lhs)]
    in_specs += [pl.BlockSpec((tm, d), lambda i: (i, 0)),
                 pl.BlockSpec((1, d), lambda i: (0, 0)),
                 pl.BlockSpec((e, d), lambda i: (0, 0))]
    return pl.pallas_call(
        functools.partial(_mm_res_route_kernel, n_lhs=n_lhs, tm=tm, half=half, pitch=pitch),
        grid=(m // tm,),
        in_specs=in_specs,
        out_specs=[pl.BlockSpec((tm, d), lambda i: (i, 0)),
                   pl.BlockSpec((1, e, tm), lambda i: (i // nt, 0, i % nt)),
                   pl.BlockSpec((tm * pitch, LANES), lambda i: (i, 0))],
        out_shape=[jax.ShapeDtypeStruct((m, d), F32),
                   jax.ShapeDtypeStruct((m // seq, e, seq), F32),
                   jax.ShapeDtypeStruct((m * pitch, LANES), U32)],
        compiler_params=_params(("parallel",)),
        name="matmul_residual_route",
    )(*lhs_list, *([w_bf16] * n_lhs), res, norm_g.reshape(1, d), router.T.astype(F32))


def _t5_bucket(rel):
    nb = REL_BUCKETS // 2
    max_exact = nb // 2
    base = jnp.where(rel > 0, nb, 0)
    n = jnp.abs(rel)
    nf = jnp.maximum(n, 1).astype(F32)
    large = max_exact + (jnp.log(nf / max_exact) / math.log(REL_MAX_DIST / max_exact)
                         * (nb - max_exact)).astype(I32)
    large = jnp.minimum(large, nb - 1)
    return base + jnp.where(n < max_exact, n, large)


def _bias_kernel(rb_ref, bucket_ref, o_ref, *, n_heads):
    bkt = bucket_ref[...]
    row = lax.broadcasted_iota(I32, bkt.shape, 0)
    col = lax.broadcasted_iota(I32, bkt.shape, 1)
    in_window = jnp.abs(col - ATTN_BLOCK - row) <= ATTN_BLOCK
    for h in range(n_heads):
        acc = jnp.zeros(bkt.shape, F32)
        for k in range(REL_BUCKETS):
            acc = jnp.where(bkt == k, rb_ref[k, h], acc)
        o_ref[h * ATTN_BLOCK:(h + 1) * ATTN_BLOCK, :] = jnp.where(in_window, acc, NEG_INF)


def attention_bias(rel_bias):
    n_heads = rel_bias.shape[1]
    q_off = jnp.arange(ATTN_BLOCK, dtype=I32)
    c_off = jnp.arange(3 * ATTN_BLOCK, dtype=I32)
    bucket = _t5_bucket(c_off[None, :] - ATTN_BLOCK - q_off[:, None])
    return pl.pallas_call(
        functools.partial(_bias_kernel, n_heads=n_heads),
        in_specs=[pl.BlockSpec(memory_space=pltpu.SMEM),
                  pl.BlockSpec(memory_space=pltpu.VMEM)],
        out_specs=pl.BlockSpec(memory_space=pltpu.VMEM),
        out_shape=jax.ShapeDtypeStruct((n_heads * ATTN_BLOCK, 3 * ATTN_BLOCK), F32),
        name="attention_bias",
    )(rel_bias.astype(F32), bucket)


def _attn_kernel(sink_ref, q_ref, kp_ref, kc_ref, kn_ref, vp_ref, vc_ref, vn_ref, bias_ref,
                 o_ref, s_ref, p_ref, *, seq, n_kv, qb):
    blk = ATTN_BLOCK
    col = lax.broadcasted_iota(I32, (1, 3 * blk), 1)
    scale = HEAD_DIM ** -0.5
    rc = ATTN_ROW_CHUNK

    def band(p_ref, c_ref, n_ref, ksl, j):
        parts = []
        for i in (j - 1, j, j + 1):
            if i < 0:
                parts.append(p_ref[0, :, ksl])
            elif i >= qb:
                parts.append(n_ref[0, :, ksl])
            else:
                parts.append(c_ref[0, i * blk:(i + 1) * blk, ksl])
        return jnp.concatenate(parts, axis=0).astype(BF16)

    for j in range(qb):
        kpos = (pl.program_id(1) * qb + j - 1) * blk + col
        in_seq = (kpos >= 0) & (kpos < seq)
        rows = slice(j * blk, (j + 1) * blk)
        for kh in range(n_kv):
            ksl = slice(kh * HEAD_DIM, (kh + 1) * HEAD_DIM)
            kband = band(kp_ref, kc_ref, kn_ref, ksl, j)
            vband = band(vp_ref, vc_ref, vn_ref, ksl, j)
            heads = [kh * KV_GROUP + g for g in range(KV_GROUP)]
            qs = jnp.concatenate([q_ref[0, rows, h * HEAD_DIM:(h + 1) * HEAD_DIM] for h in heads],
                                 axis=0).astype(BF16)
            s_ref[...] = lax.dot_general(qs, kband, (((1,), (1,)), ((), ())),
                                         preferred_element_type=F32)

            def softmax_rows(c, carry, kh=kh, in_seq=in_seq):
                r0 = pl.multiple_of(c * rc, rc)
                bias = bias_ref[pl.ds(kh * KV_GROUP * blk + r0, rc), :]
                s = jnp.where(in_seq, s_ref[pl.ds(r0, rc), :] * scale + bias, NEG_INF)
                sink = sink_ref[kh * KV_GROUP + c // (blk // rc)]
                m = jnp.maximum(jnp.max(s, axis=-1, keepdims=True), sink)
                p = jnp.exp(s - m)
                denom = jnp.sum(p, axis=-1, keepdims=True) + jnp.exp(sink - m)
                p_ref[pl.ds(r0, rc), :] = (p / denom).astype(BF16)
                return carry

            lax.fori_loop(0, KV_GROUP * blk // rc, softmax_rows, 0, unroll=True)
            o = jnp.dot(p_ref[...], vband, preferred_element_type=F32)
            for g, h in enumerate(heads):
                o_ref[0, rows, h * HEAD_DIM:(h + 1) * HEAD_DIM] = (
                    o[g * blk:(g + 1) * blk].astype(o_ref.dtype))


def windowed_attention(qkv, sink, bias, *, attn_w, kv_w, qb=4):
    b, s, _ = qkv.shape
    blk = ATTN_BLOCK
    nblk = s // blk
    qb = _tile(nblk, qb)
    n_kv = kv_w // HEAD_DIM
    kcol = attn_w // kv_w
    assert attn_w % kv_w == 0 and n_kv * KV_GROUP * HEAD_DIM == attn_w

    def edge_spec(col, first):
        return pl.BlockSpec(
            (1, blk, kv_w),
            lambda bi, n: (bi, jnp.clip(n * qb + (-1 if first else qb), 0, nblk - 1), col))

    def mid_spec(col):
        return pl.BlockSpec((1, qb * blk, kv_w), lambda bi, n: (bi, n, col))

    return pl.pallas_call(
        functools.partial(_attn_kernel, seq=s, n_kv=n_kv, qb=qb),
        grid=(b, nblk // qb),
        in_specs=[
            pl.BlockSpec(memory_space=pltpu.SMEM),
            pl.BlockSpec((1, qb * blk, attn_w), lambda bi, n: (bi, n, 0)),
            edge_spec(kcol, True), mid_spec(kcol), edge_spec(kcol, False),
            edge_spec(kcol + 1, True), mid_spec(kcol + 1), edge_spec(kcol + 1, False),
            pl.BlockSpec(bias.shape, lambda bi, n: (0, 0)),
        ],
        out_specs=pl.BlockSpec((1, qb * blk, attn_w), lambda bi, n: (bi, n, 0)),
        out_shape=jax.ShapeDtypeStruct((b, s, attn_w), BF16),
        scratch_shapes=[pltpu.VMEM((KV_GROUP * blk, 3 * blk), F32),
                        pltpu.VMEM((KV_GROUP * blk, 3 * blk), BF16)],
        compiler_params=_params(("parallel", "parallel")),
        name="windowed_attention",
    )(sink.astype(F32), qkv, qkv, qkv, qkv, qkv, qkv, qkv, bias)


def _s5_operands(a_re, a_im, log_dt, b_re, b_im, c_re, c_im, chunk, n_levels):
    ell = chunk
    p = a_re.shape[-1]
    dt = jnp.exp(log_dt)[..., None]
    mag = jnp.exp(a_re * dt)
    lb_re = mag * jnp.cos(a_im * dt)
    lb_im = mag * jnp.sin(a_im * dt)
    den = a_re * a_re + a_im * a_im
    nr = lb_re - 1.0
    coef_re = (nr * a_re + lb_im * a_im) / den
    coef_im = (lb_im * a_re - nr * a_im) / den
    bb_re = coef_re[..., None] * b_re - coef_im[..., None] * b_im
    bb_im = coef_re[..., None] * b_im + coef_im[..., None] * b_re
    bt_re, bt_im = bb_re.transpose(0, 1, 3, 2), bb_im.transpose(0, 1, 3, 2)

    def powers(tau):
        t = tau.astype(F32)[None, None, :, None]
        pm = jnp.exp((a_re * dt)[:, :, None, :] * t)
        ang = (a_im * dt)[:, :, None, :] * t
        return pm * jnp.cos(ang), pm * jnp.sin(ang)

    g = a_re.shape[1]
    assert 2 * p == LANES and g % 2 == 0
    odd = (jnp.arange(g) % 2 == 1)[:, None, None]
    cat = lambda *ts: jnp.concatenate(ts, axis=-1)
    lp = lambda t: jnp.pad(t, [(0, 0)] * (t.ndim - 1) + [(0, LANES - p)])
    sp = lambda t: jnp.where(odd, cat(jnp.zeros_like(t), t), cat(t, jnp.zeros_like(t)))
    ii = jnp.arange(ell)
    dn_re, dn_im = powers(ell - 1 - ii)
    up_re, up_im = powers(ii)
    e1_re, e1_im = powers(ii + 1)
    e2_re, e2_im = powers(ell - ii)
    lv_re, lv_im = powers(ell * (2 ** jnp.arange(n_levels)))
    lev = SUBLANES * pl.cdiv(n_levels, SUBLANES)
    lvp = lambda t: jnp.pad(sp(t), ((0, 0), (0, lev - n_levels), (0, 0)))
    wide = jnp.concatenate([
        cat(sp(dn_re[0]), sp(dn_re[0]), sp(up_re[1]), sp(up_re[1])),
        cat(sp(dn_im[0]), sp(dn_im[0]), sp(up_im[1]), sp(up_im[1])),
        cat(sp(e1_re[0]), sp(e1_re[0]), sp(e2_re[1]), sp(e2_re[1])),
        cat(sp(e1_im[0]), sp(e1_im[0]), sp(e2_im[1]), sp(e2_im[1])),
        cat(sp(bt_re[0]), sp(bt_im[0]), sp(bt_re[1]), sp(bt_im[1])),
        cat(-sp(bt_im[0]), sp(bt_re[0]), -sp(bt_im[1]), sp(bt_re[1])),
        cat(sp(c_re[0]), -sp(c_im[0]), sp(c_re[1]), -sp(c_im[1])),
        cat(-sp(c_im[0]), -sp(c_re[0]), -sp(c_im[1]), -sp(c_re[1])),
        cat(lvp(lv_re[0]), lvp(lv_re[0]), lvp(lv_re[1]), lvp(lv_re[1])),
        cat(-lvp(lv_im[0]), lvp(lv_im[0]), -lvp(lv_im[1]), lvp(lv_im[1])),
    ], axis=1)

    def lag_rows(fwd, bwd):
        f = jnp.pad(lp(fwd), ((0, 0), (ell - 1, 1), (0, 0)))
        b = jnp.pad(lp(bwd), ((0, 0), (0, ell), (0, 0)))
        return cat(f, b)

    narrow = jnp.concatenate([
        lag_rows(up_re[0], dn_re[1]), lag_rows(up_im[0], dn_im[1]),
        cat(lp(c_re[0]), lp(c_re[1])), cat(lp(c_im[0]), lp(c_im[1])),
        cat(lp(bt_re[0]), lp(bt_re[1])), cat(lp(bt_im[0]), lp(bt_im[1])),
    ], axis=1)
    return wide, narrow


def _s5_pack_kernel(u_ref, o_ref, t_ref, *, n_lt, h, chunk):
    def transpose_block(lt, c):
        src = pl.multiple_of(lt * LANES, LANES)
        dst = pl.multiple_of(lt * S5_PITCH, SUBLANES)
        t_ref[pl.ds(dst, LANES), :] = u_ref[pl.ds(src, LANES), :].T
        return c

    lax.fori_loop(0, n_lt, transpose_block, 0, unroll=8)
    n_sub = LANES // chunk

    def pack_group(g8, c):
        for kt in range(h // n_sub):
            ms = [t_ref[pl.ds(g8 * h + kt * n_sub + kk, n_lt, stride=S5_PITCH), :] for kk in range(n_sub)]
            for c_lo in range(n_sub):
                tile = jnp.concatenate([m[:, c_lo * chunk:(c_lo + 1) * chunk] for m in ms], axis=1)
                o_ref[g8, c_lo * n_lt:(c_lo + 1) * n_lt, kt * LANES:(kt + 1) * LANES] = tile.astype(BF16)
        return c

    lax.fori_loop(0, LANES // h, pack_group, 0)


def _s5_unpack_kernel(y_ref, o_ref, t_ref, *, n_lt, h, chunk):
    n_sub = LANES // chunk

    def unpack_group(g8, c):
        for hh in range(h):
            lane0 = (hh // n_sub) * LANES + (hh % n_sub) * chunk
            m = jnp.concatenate([y_ref[g8, c_lo * n_lt:(c_lo + 1) * n_lt, lane0:lane0 + chunk]
                                 for c_lo in range(n_sub)], axis=1)
            t_ref[pl.ds(g8 * h + hh, n_lt, stride=S5_PITCH), :] = m
        return c

    lax.fori_loop(0, LANES // h, unpack_group, 0)

    def transpose_block(lt, c):
        src = pl.multiple_of(lt * S5_PITCH, SUBLANES)
        dst = pl.multiple_of(lt * LANES, LANES)
        o_ref[pl.ds(dst, LANES), :] = t_ref[pl.ds(src, LANES), :].T
        return c

    lax.fori_loop(0, n_lt, transpose_block, 0, unroll=8)


def _shift_rows(z, r, seg):
    if r == 0:
        return z
    rows = z.shape[0]
    rolled = pltpu.roll(z, r % rows, 0)
    assert seg & (seg - 1) == 0
    pos = lax.broadcasted_iota(I32, z.shape, 0) & (seg - 1)
    keep = (pos >= r) if r > 0 else (pos < seg + r)
    return jnp.where(keep, rolled, 0.0)


def _shift_chunks(blocks, d, seg):
    n_sub = len(blocks)
    out = []
    for c_lo in range(n_sub):
        e, s_lo = divmod(c_lo - d, n_sub)
        out.append(_shift_rows(blocks[s_lo], -e, seg))
    return out


def _s5_group_tables(w, nr, perm, *, n_levels, chunk, h):
    ell, lh = chunk, chunk * h
    lev = SUBLANES * pl.cdiv(n_levels, SUBLANES)
    nt = (((1,), (1,)), ((), ()))
    pwf_re, pwf_im, pwe_re, pwe_im = (w[k * ell:(k + 1) * ell] for k in range(4))
    o = 4 * ell
    bb1, bb2, cc1, cc2 = (w[o + k * h:o + (k + 1) * h] for k in range(4))
    o += 4 * h
    pa_all, pb_all = w[o:o + lev], w[o + lev:o + 2 * lev]
    f_tab = jnp.concatenate([pwf_re * bb1[k:k + 1, :] + pwf_im * bb2[k:k + 1, :] for k in range(h)],
                            axis=0).astype(BF16)
    et_tab = jnp.concatenate([pwe_re * cc1[k:k + 1, :] + pwe_im * cc2[k:k + 1, :] for k in range(h)],
                             axis=0).astype(BF16)
    pwm_re, pwm_im = nr[:2 * ell], nr[2 * ell:4 * ell]
    o = 4 * ell
    cr, ci, br, bi = (nr[o + k * h:o + (k + 1) * h] for k in range(4))
    rep = lambda x: jnp.concatenate([jnp.broadcast_to(x[j:j + 1, :], (h, x.shape[1]))
                                     for j in range(x.shape[0])], axis=0)
    tile = lambda t: jnp.concatenate([t] * (2 * ell), axis=0)
    pr, pi = rep(pwm_re), rep(pwm_im)
    crt, cit = tile(cr), tile(ci)
    a = jnp.concatenate([pr * crt - pi * cit, pr * cit + pi * crt], axis=1)
    b = jnp.concatenate([br, -bi], axis=1)
    a1 = a.astype(BF16)
    a2 = (a - a1.astype(F32)).astype(BF16)
    b1 = b.astype(BF16)
    b2 = (b - b1.astype(F32)).astype(BF16)
    lead = lax.dot_general(jnp.concatenate([b1, b2], axis=0), a1, nt, preferred_element_type=F32)
    strip = lead[:h] + lead[h:] + lax.dot_general(b1, a2, nt, preferred_element_type=F32)
    t_jk = jnp.concatenate([strip[:, (ell - 1 - j) * h:(ell - 1 - j) * h + lh] for j in range(ell)],
                           axis=0).astype(BF16)
    t_tab = lax.dot_general(jnp.dot(perm, t_jk, preferred_element_type=F32).astype(BF16), perm, nt,
                            preferred_element_type=F32).astype(BF16)
    return f_tab, et_tab, t_tab, pa_all, pb_all


def _s5_kernel(u_ref, wide_ref, narrow_ref, perm_ref, o_ref, *, seg, n_levels, chunk, h):
    n_sub = LANES // chunk
    nt = (((1,), (1,)), ((), ()))
    perm = perm_ref[...]
    tabs = [_s5_group_tables(wide_ref[gi], narrow_ref[gi], perm, n_levels=n_levels, chunk=chunk, h=h)
            for gi in range(2)]
    us = [u_ref[gi] for gi in range(2)]
    z = sum(jnp.dot(u, t[0], preferred_element_type=F32) for u, t in zip(us, tabs))
    pa_all = tabs[0][3] + tabs[1][3]
    pb_all = tabs[0][4] + tabs[1][4]
    n_lt = z.shape[0] // n_sub
    half = 2 * LANES

    def swap(x):
        return jnp.concatenate([x[:, LANES:], x[:, :LANES]], axis=1)

    zf = [z[c * n_lt:(c + 1) * n_lt, :half] for c in range(n_sub)]
    zb = [z[c * n_lt:(c + 1) * n_lt, half:] for c in range(n_sub)]
    for k in range(n_levels):
        pa = pa_all[k:k + 1, :]
        pb = pb_all[k:k + 1, :]
        sf = _shift_chunks(zf, 2 ** k, seg)
        sb = _shift_chunks(zb, -(2 ** k), seg)
        zf = [a + pa[:, :half] * s + pb[:, :half] * swap(s) for a, s in zip(zf, sf)]
        zb = [a + pa[:, half:] * s + pb[:, half:] * swap(s) for a, s in zip(zb, sb)]
    hin = jnp.concatenate([jnp.concatenate(_shift_chunks(zf, 1, seg), axis=0),
                           jnp.concatenate(_shift_chunks(zb, -1, seg), axis=0)], axis=1).astype(BF16)
    for gi in range(2):
        y = jnp.dot(us[gi], tabs[gi][2], preferred_element_type=F32)
        o_ref[gi] = y + lax.dot_general(hin, tabs[gi][1], nt, preferred_element_type=F32)


def s5_scan(u, operands, *, seq, chunk, n_levels):
    wide, narrow = operands
    t, w = u.shape
    g = wide.shape[0]
    h = w // g
    lh = chunk * h
    n_lt = t // LANES
    rows = t // chunk
    gpt = LANES // h
    assert t % LANES == 0 and seq % LANES == 0 and LANES % chunk == 0 and h % (LANES // chunk) == 0
    scratch = pltpu.VMEM((n_lt * S5_PITCH, LANES), F32)
    ug = pl.pallas_call(
        functools.partial(_s5_pack_kernel, n_lt=n_lt, h=h, chunk=chunk),
        grid=(w // LANES,),
        in_specs=[pl.BlockSpec((t, LANES), lambda q: (0, q))],
        out_specs=pl.BlockSpec((gpt, rows, lh), lambda q: (q, 0, 0)),
        out_shape=jax.ShapeDtypeStruct((g, rows, lh), BF16),
        scratch_shapes=[scratch],
        compiler_params=_params(("parallel",)),
        name="s5_pack",
    )(u)
    kj = jnp.arange(lh)
    perm = (kj[:, None] // chunk + (kj[:, None] % chunk) * h == kj[None, :]).astype(BF16)
    yg = pl.pallas_call(
        functools.partial(_s5_kernel, seg=seq // LANES, n_levels=n_levels, chunk=chunk, h=h),
        grid=(g // 2,),
        in_specs=[
            pl.BlockSpec((2, rows, lh), lambda i: (i, 0, 0)),
            pl.BlockSpec((2,) + wide.shape[1:], lambda i: (i, 0, 0)),
            pl.BlockSpec((2,) + narrow.shape[1:], lambda i: (i, 0, 0)),
            pl.BlockSpec((lh, lh), lambda i: (0, 0)),
        ],
        out_specs=pl.BlockSpec((2, rows, lh), lambda i: (i, 0, 0)),
        out_shape=jax.ShapeDtypeStruct((g, rows, lh), F32),
        compiler_params=_params(("parallel",)),
        name="s5_scan",
    )(ug, wide, narrow, perm)
    return pl.pallas_call(
        functools.partial(_s5_unpack_kernel, n_lt=n_lt, h=h, chunk=chunk),
        grid=(w // LANES,),
        in_specs=[pl.BlockSpec((gpt, rows, lh), lambda q: (q, 0, 0))],
        out_specs=pl.BlockSpec((t, LANES), lambda q: (0, q)),
        out_shape=jax.ShapeDtypeStruct((t, w), F32),
        scratch_shapes=[scratch],
        compiler_params=_params(("parallel",)),
        name="s5_unpack",
    )(yg)


def _glu_kernel(y_ref, u_ref, d_ref, w_ref, b_ref, o_ref):
    y = y_ref[...] + d_ref[...] * u_ref[...]
    g = _gelu(y)
    z = jnp.dot(g.astype(BF16), w_ref[...], preferred_element_type=F32) + b_ref[...]
    o_ref[...] = (g * jax.nn.sigmoid(z)).astype(o_ref.dtype)


def s5_glu(y, u, d_skip, glu_w_bf16, glu_b, *, tm=512):
    m, w = y.shape
    tm = _tile(m, tm)
    return pl.pallas_call(
        _glu_kernel,
        grid=(m // tm,),
        in_specs=[
            pl.BlockSpec((tm, w), lambda i: (i, 0)),
            pl.BlockSpec((tm, w), lambda i: (i, 0)),
            pl.BlockSpec((1, w), lambda i: (0, 0)),
            pl.BlockSpec((w, w), lambda i: (0, 0)),
            pl.BlockSpec((1, w), lambda i: (0, 0)),
        ],
        out_specs=pl.BlockSpec((tm, w), lambda i: (i, 0)),
        out_shape=jax.ShapeDtypeStruct((m, w), BF16),
        compiler_params=_params(("parallel",)),
        name="s5_glu",
    )(y, u, d_skip.reshape(1, w), glu_w_bf16, glu_b.reshape(1, w))


def _gmlp_kernel(x_ref, s0_ref, s1_ref, ng_ref, w_ref, g_ref, b_ref, ws_ref, bs_ref, o_ref, xn_ref, *,
                 width, n_heads, tm, half, pitch):
    hd = width // n_heads
    ck = GMLP_CHUNK
    x = x_ref[...] + _unslab(s0_ref, s1_ref, tm=tm, half=half, pitch=pitch)
    xn_ref[...] = x
    hn = _rms(x, ng_ref[...]).astype(BF16)
    z = _gelu(jnp.dot(hn, w_ref[...], preferred_element_type=F32))
    u, v = z[:, :width], z[:, width:]
    mu = jnp.mean(v, axis=-1, keepdims=True)
    var = jnp.mean(jnp.square(v - mu), axis=-1, keepdims=True)
    vn = ((v - mu) * lax.rsqrt(var + EPS) * g_ref[...] + b_ref[...]).astype(BF16)
    for c in range(tm // ck):
        rows = slice(c * ck, (c + 1) * ck)
        for h in range(n_heads):
            sl = slice(h * hd, (h + 1) * hd)
            mixed = jnp.dot(ws_ref[h], vn[rows, sl], preferred_element_type=F32) + bs_ref[:, h:h + 1]
            o_ref[rows, sl] = (u[rows, sl] * mixed).astype(o_ref.dtype)


def gmlp_gate(x, moe_slabs, norm_g, w_in_bf16, ln_g, ln_b, w_s_bf16, b_s, *, seq, tm=512):
    m, d = x.shape
    w2 = w_in_bf16.shape[1]
    width = w2 // 2
    n_heads = w_s_bf16.shape[0]
    ck = GMLP_CHUNK
    tm = max(ck, _tile(seq, tm))
    nt = seq // tm
    half = d // 2
    pitch = _slab_pitch(half)
    return pl.pallas_call(
        functools.partial(_gmlp_kernel, width=width, n_heads=n_heads, tm=tm, half=half, pitch=pitch),
        grid=(m // tm,),
        in_specs=[
            pl.BlockSpec((tm, d), lambda i: (i, 0)),
            pl.BlockSpec((1, 1, tm * pitch, LANES), lambda i: (i // nt, 0, i % nt, 0)),
            pl.BlockSpec((1, 1, tm * pitch, LANES), lambda i: (i // nt, 1, i % nt, 0)),
            pl.BlockSpec((1, d), lambda i: (0, 0)),
            pl.BlockSpec((d, w2), lambda i: (0, 0), pipeline_mode=pl.Buffered(1)),
            pl.BlockSpec((1, width), lambda i: (0, 0)),
            pl.BlockSpec((1, width), lambda i: (0, 0)),
            pl.BlockSpec((n_heads, ck, ck), lambda i: (0, 0, 0)),
            pl.BlockSpec((ck, n_heads), lambda i: (0, 0)),
        ],
        out_specs=[pl.BlockSpec((tm, width), lambda i: (i, 0)),
                   pl.BlockSpec((tm, d), lambda i: (i, 0))],
        out_shape=[jax.ShapeDtypeStruct((m, width), BF16),
                   jax.ShapeDtypeStruct((m, d), F32)],
        compiler_params=_params(("parallel",)),
        name="gmlp_gate",
    )(x, moe_slabs, moe_slabs, norm_g.reshape(1, d), w_in_bf16, ln_g.reshape(1, width),
      ln_b.reshape(1, width), w_s_bf16, b_s.T.astype(F32))


def _slab_pitch(half):
    return SUBLANES * pl.cdiv(half // LANES, SUBLANES)


def _route_rows(x, g_ref, rt_ref, aff_ref, slab_ref, *, tm, half, pitch):
    h = _rms(x, g_ref[...])
    nt = (((1,), (1,)), ((), ()))
    e = rt_ref.shape[0]
    h1 = h.astype(BF16)
    h1f = h1.astype(F32)
    h2 = (h - h1f).astype(BF16)
    r1 = rt_ref[...].astype(BF16)
    r2 = (rt_ref[...] - r1.astype(F32)).astype(BF16)
    lead = lax.dot_general(jnp.concatenate([r1, r2], axis=0), h1, nt, preferred_element_type=F32)
    logits = lead[:e] + lead[e:] + lax.dot_general(r1, h2, nt, preferred_element_type=F32)
    mx = jnp.max(logits, axis=0, keepdims=True)
    ex = jnp.exp(logits - mx)
    aff_ref[0] = ex / jnp.sum(ex, axis=0, keepdims=True)
    lo = lax.bitcast_convert_type(h1f[:, :half], U32) >> 16
    hi = lax.bitcast_convert_type(h1f[:, half:], U32) & jnp.uint32(0xFFFF0000)
    word = lo | hi
    for s in range(half // LANES):
        slab_ref[pl.ds(s, tm, stride=pitch), :] = word[:, s * LANES:(s + 1) * LANES]
    if pitch > half // LANES:
        for s in range(half // LANES, pitch):
            slab_ref[pl.ds(s, tm, stride=pitch), :] = jnp.zeros((tm, LANES), U32)


def _lane_cumsum(m_bf16, tri):
    e, s = m_bf16.shape
    off = jnp.zeros((e, 1), F32)
    out = []
    for j in range(s // LANES):
        c = jnp.dot(m_bf16[:, j * LANES:(j + 1) * LANES], tri, preferred_element_type=F32) + off
        out.append(c)
        off = c[:, LANES - 1:LANES]
    return jnp.concatenate(out, axis=1)


def _topk_kernel(aff_ref, idx_ref, gate_ref, *, cap, kchunk):
    aff = aff_ref[0]
    e, s = aff.shape
    bits = lax.bitcast_convert_type(aff, I32)

    def count_ge(t):
        return jnp.sum((bits >= t).astype(F32), axis=1, keepdims=True)

    def body(_, carry):
        lo, hi = carry
        mid = lo + ((hi - lo) >> 1)
        ok = count_ge(mid) >= cap
        return jnp.where(ok, mid, lo), jnp.where(ok, hi, mid)

    lo0 = jnp.zeros((e, 1), I32)
    hi0 = jnp.full((e, 1), 0x7F800000, I32)
    thr, _ = lax.fori_loop(0, 32, body, (lo0, hi0))
    gt = bits > thr
    eq = bits == thr
    need = cap - jnp.sum(gt.astype(F32), axis=1, keepdims=True)
    tri = (lax.broadcasted_iota(I32, (LANES, LANES), 0)
           <= lax.broadcasted_iota(I32, (LANES, LANES), 1)).astype(BF16)
    eq_rank = _lane_cumsum(jnp.where(eq, 1.0, 0.0).astype(BF16), tri)
    sel = gt | (eq & (eq_rank <= need))
    pos = _lane_cumsum(jnp.where(sel, 1.0, 0.0).astype(BF16), tri)
    pos = jnp.where(sel, pos, 0.0)

    tok = lax.broadcasted_iota(I32, (1, s), 1)
    t_hi = (tok >> 6).astype(F32)
    t_lo = (tok & 63).astype(F32)
    slot = lax.broadcasted_iota(I32, (cap, kchunk), 0).astype(F32) + 1.0
    lrow = lax.broadcasted_iota(I32, (SUBLANES, s), 0)
    for ei in range(e):
        a = aff[ei:ei + 1, :]
        g1 = a.astype(BF16).astype(F32)
        g2 = (a - g1).astype(BF16).astype(F32)
        g3 = (a - g1) - g2
        lhs = jnp.where(lrow == 0, t_hi, jnp.where(lrow == 1, t_lo, jnp.where(
            lrow == 2, g1, jnp.where(lrow == 3, g2, jnp.where(lrow == 4, g3, 0.0)))))
        lhs = lhs.astype(BF16)
        acc = jnp.zeros((SUBLANES, cap), F32)
        for c0 in range(0, s, kchunk):
            onehot = jnp.where(pos[ei:ei + 1, c0:c0 + kchunk] == slot, 1.0, 0.0).astype(BF16)
            acc = acc + lax.dot_general(lhs[:, c0:c0 + kchunk], onehot, (((1,), (1,)), ((), ())),
                                        preferred_element_type=F32)
        idx_ref[0, ei:ei + 1, :] = (acc[0:1] * 64.0 + acc[1:2]).astype(I32)
        gate_ref[0, ei:ei + 1, :] = acc[2:3] + acc[3:4] + acc[4:5]


def moe_topk(aff, cap):
    b, e, s = aff.shape
    return pl.pallas_call(
        functools.partial(_topk_kernel, cap=cap, kchunk=_tile(s, 1024)),
        grid=(b,),
        in_specs=[pl.BlockSpec((1, e, s), lambda i: (i, 0, 0))],
        out_specs=[pl.BlockSpec((1, e, cap), lambda i: (i, 0, 0)),
                   pl.BlockSpec((1, e, cap), lambda i: (i, 0, 0))],
        out_shape=[jax.ShapeDtypeStruct((b, e, cap), I32),
                   jax.ShapeDtypeStruct((b, e, cap), F32)],
        compiler_params=_params(("parallel",)),
        name="moe_topk",
    )(aff)


def _gather_kernel(idx_ref, slab_ref, o_ref, rows_ref, *, cap, half, pitch, unroll):
    for x in range(MOE_EXPERTS_PER_STEP):
        def body(j0, c, x=x):
            for u in range(unroll):
                j = j0 * unroll + u
                src = pl.multiple_of(idx_ref[x, 0, j] * pitch, pitch)
                rows_ref[pl.ds(pl.multiple_of(j * pitch, pitch), pitch), :] = slab_ref[pl.ds(src, pitch), :]
            return c

        lax.fori_loop(0, cap // unroll, body, 0)
        for s in range(half // LANES):
            word = rows_ref[pl.ds(s, cap, stride=pitch), :]
            lo = lax.bitcast_convert_type(word << 16, F32)
            hi = lax.bitcast_convert_type(word & jnp.uint32(0xFFFF0000), F32)
            o_ref[x, :, s * LANES:(s + 1) * LANES] = lo.astype(BF16)
            o_ref[x, :, half + s * LANES:half + (s + 1) * LANES] = hi.astype(BF16)


def moe_gather(idx, slabs, *, seq, d):
    b, e, cap = idx.shape
    half = d // 2
    pitch = _slab_pitch(half)
    eps = MOE_EXPERTS_PER_STEP
    ne = e // eps
    return pl.pallas_call(
        functools.partial(_gather_kernel, cap=cap, half=half, pitch=pitch, unroll=8),
        grid=(b, ne),
        in_specs=[
            pl.BlockSpec((eps, 1, cap), lambda bi, ei: (bi * ne + ei, 0, 0), memory_space=pltpu.SMEM),
            pl.BlockSpec((seq * pitch, LANES), lambda bi, ei: (bi, 0)),
        ],
        out_specs=pl.BlockSpec((eps, cap, d), lambda bi, ei: (ei, bi, 0)),
        out_shape=jax.ShapeDtypeStruct((e, b * cap, d), BF16),
        scratch_shapes=[pltpu.VMEM((cap * pitch, LANES), U32)],
        compiler_params=_params(("parallel", "arbitrary")),
        name="moe_gather",
    )(idx.reshape(b * e, 1, cap), slabs)


def _ffn_kernel(xs_ref, w1_ref, w3_ref, w2_ref, o_ref, hid_ref, *, nt, tf):
    step = pl.program_id(1)

    @pl.when(step < nt)
    def _():
        xs = xs_ref[0]
        a = jnp.dot(xs, w1_ref[0, 0].astype(BF16), preferred_element_type=F32)
        g = jnp.dot(xs, w3_ref[0, 0].astype(BF16), preferred_element_type=F32)
        hid_ref[step] = (a * jax.nn.sigmoid(a) * g).astype(BF16)

    @pl.when(step >= nt)
    def _():
        acc = jnp.zeros(o_ref.shape[1:], F32)
        for k in range(nt):
            acc = acc + jnp.dot(hid_ref[k], w2_ref[0, 0, k * tf:(k + 1) * tf, :].astype(BF16),
                                preferred_element_type=F32)
        o_ref[0] = acc


def moe_ffn(xs, w1, w3, w2, layer, *, tf=256):
    e, m, d = xs.shape
    f = w1.shape[3]
    tf = _tile(f, tf)
    tn = _tile(d, tf)
    nt = f // tf
    nn = d // tn
    return pl.pallas_call(
        functools.partial(_ffn_kernel, nt=nt, tf=tf),
        grid=(e, nt + nn),
        in_specs=[
            pl.BlockSpec((1, m, d), lambda ei, t: (ei, 0, 0)),
            pl.BlockSpec((1, 1, d, tf), lambda ei, t: (layer, ei, 0, jnp.minimum(t, nt - 1))),
            pl.BlockSpec((1, 1, d, tf), lambda ei, t: (layer, ei, 0, jnp.minimum(t, nt - 1))),
            pl.BlockSpec((1, 1, f, tn), lambda ei, t: (layer, ei, 0, jnp.maximum(t - nt, 0))),
        ],
        out_specs=pl.BlockSpec((1, m, tn), lambda ei, t: (ei, 0, jnp.maximum(t - nt, 0))),
        out_shape=jax.ShapeDtypeStruct((e, m, d), F32),
        scratch_shapes=[pltpu.VMEM((nt, m, tf), BF16)],
        compiler_params=_params(("parallel", "arbitrary")),
        name="moe_ffn",
    )(xs, w1, w3, w2)


def _combine_kernel(idx_ref, gate_ref, ys_ref, o_ref, rows_ref, *, cap, half, pitch, unroll):
    ei = pl.program_id(2)

    @pl.when(ei == 0)
    def _():
        o_ref[...] = jnp.zeros(o_ref.shape, F32)

    for x in range(MOE_EXPERTS_PER_STEP):
        for s in range(half // LANES):
            rows_ref[pl.ds(s, cap, stride=pitch), :] = ys_ref[x, :, s * LANES:(s + 1) * LANES]

        def body(j0, c, x=x):
            dsts, vals = [], []
            for u in range(unroll):
                j = j0 * unroll + u
                dst = pl.multiple_of(idx_ref[x, 0, j] * pitch, pitch)
                src = pl.multiple_of(j * pitch, pitch)
                vals.append(o_ref[0, 0, pl.ds(dst, pitch), :]
                            + gate_ref[x, 0, j] * rows_ref[pl.ds(src, pitch), :])
                dsts.append(dst)
            for dst, val in zip(dsts, vals):
                o_ref[0, 0, pl.ds(dst, pitch), :] = val
            return c

        lax.fori_loop(0, cap // unroll, body, 0)


def moe_combine(idx, gate, ys, *, seq, d):
    b, e, cap = idx.shape
    half = d // 2
    pitch = _slab_pitch(half)
    eps = MOE_EXPERTS_PER_STEP
    ne = e // eps
    return pl.pallas_call(
        functools.partial(_combine_kernel, cap=cap, half=half, pitch=pitch, unroll=8),
        grid=(b, 2, ne),
        in_specs=[
            pl.BlockSpec((eps, 1, cap), lambda bi, dh, ei: (bi * ne + ei, 0, 0), memory_space=pltpu.SMEM),
            pl.BlockSpec((eps, 1, cap), lambda bi, dh, ei: (bi * ne + ei, 0, 0), memory_space=pltpu.SMEM),
            pl.BlockSpec((eps, cap, half), lambda bi, dh, ei: (ei, bi, dh)),
        ],
        out_specs=pl.BlockSpec((1, 1, seq * pitch, LANES), lambda bi, dh, ei: (bi, dh, 0, 0)),
        out_shape=jax.ShapeDtypeStruct((b, 2, seq * pitch, LANES), F32),
        scratch_shapes=[pltpu.VMEM((cap * pitch, LANES), F32)],
        compiler_params=_params(("parallel", "parallel", "arbitrary")),
        name="moe_combine",
    )(idx.reshape(b * e, 1, cap), gate.reshape(b * e, 1, cap), ys)


def _unslab(s0_ref, s1_ref, *, tm, half, pitch):
    parts = []
    for s_ref in (s0_ref, s1_ref):
        for s in range(half // LANES):
            parts.append(s_ref[0, 0, pl.ds(s, tm, stride=pitch), :])
    return jnp.concatenate(parts, axis=1)


def _unslab_kernel(*refs, tm, half, pitch, final_norm):
    x_ref, s0_ref, s1_ref = refs[:3]
    o_ref = refs[-1]
    y = x_ref[0] + _unslab(s0_ref, s1_ref, tm=tm, half=half, pitch=pitch)
    if final_norm:
        y = _rms(y, refs[3][...])
    o_ref[0] = y


def moe_residual(x, moe_slabs, final_g=None, *, tm=256):
    b, s, d = x.shape
    half = d // 2
    pitch = _slab_pitch(half)
    tm = _tile(s, tm)
    final_norm = final_g is not None
    in_specs = [
        pl.BlockSpec((1, tm, d), lambda bi, j: (bi, j, 0)),
        pl.BlockSpec((1, 1, tm * pitch, LANES), lambda bi, j: (bi, 0, j, 0)),
        pl.BlockSpec((1, 1, tm * pitch, LANES), lambda bi, j: (bi, 1, j, 0)),
    ]
    args = [x, moe_slabs, moe_slabs]
    if final_norm:
        in_specs.append(pl.BlockSpec((1, d), lambda bi, j: (0, 0)))
        args.append(final_g.reshape(1, d))
    return pl.pallas_call(
        functools.partial(_unslab_kernel, tm=tm, half=half, pitch=pitch, final_norm=final_norm),
        grid=(b, s // tm),
        in_specs=in_specs,
        out_specs=pl.BlockSpec((1, tm, d), lambda bi, j: (bi, j, 0)),
        out_shape=jax.ShapeDtypeStruct((b, s, d), F32),
        compiler_params=_params(("parallel", "parallel")),
        name="moe_residual",
    )(*args)


def expert_choice_ffn(aff, slabs, w1, w3, w2, layer, *, seq, d):
    cap = CAPACITY_FACTOR * seq // N_EXPERTS
    idx, gate = moe_topk(aff, cap)
    xs = moe_gather(idx, slabs, seq=seq, d=d)
    ys = moe_ffn(xs, w1, w3, w2, layer)
    return moe_combine(idx, gate, ys, seq=seq, d=d)


def kernel(x, rel_bias, mix_norm, ffn_norm, final_norm, even_w_in, attn_sink, ssm_a_re, ssm_a_im, ssm_log_dt, ssm_b_re, ssm_b_im, ssm_c_re, ssm_c_im, ssm_d, glu_w, glu_b, even_w_out, odd_w_in, sgu_ln_g, sgu_ln_b, sgu_w, sgu_b, odd_w_out, router, moe_w1, moe_w3, moe_w2):
    b, s, d = x.shape
    t = b * s
    depth = mix_norm.shape[0]
    ssm_w = ssm_d.shape[-1]
    attn_w = even_w_out.shape[1] - ssm_w
    kv_w = (even_w_in.shape[-1] - attn_w - ssm_w) // 2
    bias = attention_bias(rel_bias)

    x2 = x.reshape(t, d)
    moe = None
    for layer in range(depth):
        i = layer // 2
        if layer % 2 == 0:
            if moe is not None:
                x2 = moe_residual(x2.reshape(b, s, d), moe).reshape(t, d)
            qkv, u = in_projection(x2, mix_norm[layer], even_w_in[i].astype(BF16), attn_w + 2 * kv_w)
            attn = windowed_attention(qkv.reshape(b, s, -1), attn_sink[i], bias,
                                      attn_w=attn_w, kv_w=kv_w)
            n_levels = max(1, (s // S5_CHUNK - 1).bit_length())
            operands = _s5_operands(ssm_a_re[i], ssm_a_im[i], ssm_log_dt[i], ssm_b_re[i], ssm_b_im[i],
                                    ssm_c_re[i], ssm_c_im[i], S5_CHUNK, n_levels)
            y = s5_scan(u, operands, seq=s, chunk=S5_CHUNK, n_levels=n_levels)
            ssm = s5_glu(y, u, ssm_d[i], glu_w[i].astype(BF16), glu_b[i])
            mixed, w_out = [attn.reshape(t, attn_w), ssm], even_w_out[i]
        else:
            gated, x2 = gmlp_gate(x2, moe, mix_norm[layer], odd_w_in[i].astype(BF16), sgu_ln_g[i],
                                  sgu_ln_b[i], sgu_w[i].astype(BF16), sgu_b[i], seq=s)
            mixed, w_out = [gated], odd_w_out[i]
        x2, aff, slabs = matmul_residual_route(mixed, w_out.astype(BF16), x2, ffn_norm[layer],
                                               router[layer], seq=s)
        moe = expert_choice_ffn(aff, slabs, moe_w1, moe_w3, moe_w2, layer, seq=s, d=d)
    return moe_residual(x2.reshape(b, s, d), moe, final_norm)
```

```python
import functools
import math

import jax
import jax.numpy as jnp
from jax import lax
from jax.experimental import pallas as pl
from jax.experimental.pallas import tpu as pltpu

F32 = jnp.float32
BF16 = jnp.bfloat16
I32 = jnp.int32
U32 = jnp.uint32

EPS = 1e-6
NEG_INF = -1e30

LANES = 128
SUBLANES = 8
VMEM_LIMIT = 56 << 20

ATTN_BLOCK = 128
ATTN_ROW_CHUNK = 32
HEAD_DIM = 128
KV_GROUP = 4
REL_BUCKETS = 32
REL_MAX_DIST = 128
SSM_GROUP = 16
SSM_STATE = 64
S5_CHUNK = 32
S5_PITCH = 136
GMLP_CHUNK = 128
N_EXPERTS = 16
CAPACITY_FACTOR = 2


def _params(sem, vmem=VMEM_LIMIT):
    return pltpu.CompilerParams(dimension_semantics=sem, vmem_limit_bytes=vmem)


def _tile(n, want):
    t = min(n, want)
    while n % t:
        t //= 2
    return t


def _rms(x, g):
    ms = jnp.mean(x * x, axis=-1, keepdims=True)
    return x * lax.rsqrt(ms + EPS) * g


def _gelu(x):
    c = math.sqrt(2.0 / math.pi)
    return x * (0.5 * (1.0 + jnp.tanh(c * (x + 0.044715 * (x * x * x)))))


def _in_proj_kernel(x_ref, g_ref, w_ref, qkv_ref, u_ref):
    hn = _rms(x_ref[...], g_ref[...]).astype(BF16)
    y = jnp.dot(hn, w_ref[...], preferred_element_type=F32)
    split = qkv_ref.shape[1]
    qkv_ref[...] = y[:, :split]
    u_ref[...] = y[:, split:]


def in_projection(x, g, w_bf16, split, *, tm=512):
    m, k = x.shape
    n = w_bf16.shape[1]
    tm = _tile(m, tm)
    return pl.pallas_call(
        _in_proj_kernel,
        grid=(m // tm,),
        in_specs=[
            pl.BlockSpec((tm, k), lambda i: (i, 0)),
            pl.BlockSpec((1, k), lambda i: (0, 0)),
            pl.BlockSpec((k, n), lambda i: (0, 0), pipeline_mode=pl.Buffered(1)),
        ],
        out_specs=[pl.BlockSpec((tm, split), lambda i: (i, 0)),
                   pl.BlockSpec((tm, n - split), lambda i: (i, 0))],
        out_shape=[jax.ShapeDtypeStruct((m, split), F32),
                   jax.ShapeDtypeStruct((m, n - split), F32)],
        compiler_params=_params(("parallel",)),
        name="in_projection",
    )(x, g.reshape(1, k), w_bf16)


def _mm_res_route_kernel(*refs, n_lhs, tm, half, pitch):
    lhs = refs[:n_lhs]
    ws = refs[n_lhs:2 * n_lhs]
    res_ref, g_ref, rt_ref, o_ref, aff_ref, slab_ref = refs[2 * n_lhs:]
    acc = res_ref[...]
    for l_ref, w_ref in zip(lhs, ws):
        acc = acc + jnp.dot(l_ref[...], w_ref[...], preferred_element_type=F32)
    o_ref[...] = acc
    _route_rows(acc, g_ref, rt_ref, aff_ref, slab_ref, tm=tm, half=half, pitch=pitch)


def matmul_residual_route(lhs_list, w_bf16, res, norm_g, router, *, seq, tm=512):
    m, d = res.shape
    kk = lhs_list[0].shape[1]
    n_lhs = len(lhs_list)
    e = router.shape[1]
    assert all(l.shape == (m, kk) for l in lhs_list) and w_bf16.shape == (n_lhs * kk, d)
    tm = _tile(seq, tm)
    nt = seq // tm
    half = d // 2
    pitch = _slab_pitch(half)
    in_specs = [pl.BlockSpec((tm, kk), lambda i: (i, 0)) for _ in lhs_list]
    in_specs += [pl.BlockSpec((kk, d), functools.partial(lambda i, r: (r, 0), r=r),
                              pipeline_mode=pl.Buffered(1)) for r in range(n_lhs)]
    in_specs += [pl.BlockSpec((tm, d), lambda i: (i, 0)),
                 pl.BlockSpec((1, d), lambda i: (0, 0)),
                 pl.BlockSpec((e, d), lambda i: (0, 0))]
    return pl.pallas_call(
        functools.partial(_mm_res_route_kernel, n_lhs=n_lhs, tm=tm, half=half, pitch=pitch),
        grid=(m // tm,),
        in_specs=in_specs,
        out_specs=[pl.BlockSpec((tm, d), lambda i: (i, 0)),
                   pl.BlockSpec((1, e, tm), lambda i: (i // nt, 0, i % nt)),
                   pl.BlockSpec((tm * pitch, LANES), lambda i: (i, 0))],
        out_shape=[jax.ShapeDtypeStruct((m, d), F32),
                   jax.ShapeDtypeStruct((m // seq, e, seq), F32),
                   jax.ShapeDtypeStruct((m * pitch, LANES), U32)],
        compiler_params=_params(("parallel",)),
        name="matmul_residual_route",
    )(*lhs_list, *([w_bf16] * n_lhs), res, norm_g.reshape(1, d), router.T.astype(F32))


def _t5_bucket(rel):
    nb = REL_BUCKETS // 2
    max_exact = nb // 2
    base = jnp.where(rel > 0, nb, 0)
    n = jnp.abs(rel)
    nf = jnp.maximum(n, 1).astype(F32)
    large = max_exact + (jnp.log(nf / max_exact) / math.log(REL_MAX_DIST / max_exact)
                         * (nb - max_exact)).astype(I32)
    large = jnp.minimum(large, nb - 1)
    return base + jnp.where(n < max_exact, n, large)


def _bias_kernel(rb_ref, bucket_ref, o_ref, *, n_heads):
    bkt = bucket_ref[...]
    row = lax.broadcasted_iota(I32, bkt.shape, 0)
    col = lax.broadcasted_iota(I32, bkt.shape, 1)
    in_window = jnp.abs(col - ATTN_BLOCK - row) <= ATTN_BLOCK
    for h in range(n_heads):
        acc = jnp.zeros(bkt.shape, F32)
        for k in range(REL_BUCKETS):
            acc = jnp.where(bkt == k, rb_ref[k, h], acc)
        o_ref[h * ATTN_BLOCK:(h + 1) * ATTN_BLOCK, :] = jnp.where(in_window, acc, NEG_INF)


def attention_bias(rel_bias):
    n_heads = rel_bias.shape[1]
    q_off = jnp.arange(ATTN_BLOCK, dtype=I32)
    c_off = jnp.arange(3 * ATTN_BLOCK, dtype=I32)
    bucket = _t5_bucket(c_off[None, :] - ATTN_BLOCK - q_off[:, None])
    return pl.pallas_call(
        functools.partial(_bias_kernel, n_heads=n_heads),
        in_specs=[pl.BlockSpec(memory_space=pltpu.SMEM),
                  pl.BlockSpec(memory_space=pltpu.VMEM)],
        out_specs=pl.BlockSpec(memory_space=pltpu.VMEM),
        out_shape=jax.ShapeDtypeStruct((n_heads * ATTN_BLOCK, 3 * ATTN_BLOCK), F32),
        name="attention_bias",
    )(rel_bias.astype(F32), bucket)


def _attn_kernel(sink_ref, q_ref, kp_ref, kc_ref, kn_ref, vp_ref, vc_ref, vn_ref, bias_ref,
                 o_ref, s_ref, p_ref, *, seq, n_kv, qb):
    blk = ATTN_BLOCK
    col = lax.broadcasted_iota(I32, (1, 3 * blk), 1)
    scale = HEAD_DIM ** -0.5
    rc = ATTN_ROW_CHUNK

    def band(p_ref, c_ref, n_ref, ksl, j):
        parts = []
        for i in (j - 1, j, j + 1):
            if i < 0:
                parts.append(p_ref[0, :, ksl])
            elif i >= qb:
                parts.append(n_ref[0, :, ksl])
            else:
                parts.append(c_ref[0, i * blk:(i + 1) * blk, ksl])
        return jnp.concatenate(parts, axis=0).astype(BF16)

    for j in range(qb):
        kpos = (pl.program_id(1) * qb + j - 1) * blk + col
        in_seq = (kpos >= 0) & (kpos < seq)
        rows = slice(j * blk, (j + 1) * blk)
        for kh in range(n_kv):
            ksl = slice(kh * HEAD_DIM, (kh + 1) * HEAD_DIM)
            kband = band(kp_ref, kc_ref, kn_ref, ksl, j)
            vband = band(vp_ref, vc_ref, vn_ref, ksl, j)
            heads = [kh * KV_GROUP + g for g in range(KV_GROUP)]
            qs = jnp.concatenate([q_ref[0, rows, h * HEAD_DIM:(h + 1) * HEAD_DIM] for h in heads],
                                 axis=0).astype(BF16)
            s_ref[...] = lax.dot_general(qs, kband, (((1,), (1,)), ((), ())),
                                         preferred_element_type=F32)

            def softmax_rows(c, carry, kh=kh, in_seq=in_seq):
                r0 = pl.multiple_of(c * rc, rc)
                bias = bias_ref[pl.ds(kh * KV_GROUP * blk + r0, rc), :]
                s = jnp.where(in_seq, s_ref[pl.ds(r0, rc), :] * scale + bias, NEG_INF)
                sink = sink_ref[kh * KV_GROUP + c // (blk // rc)]
                m = jnp.maximum(jnp.max(s, axis=-1, keepdims=True), sink)
                p = jnp.exp(s - m)
                denom = jnp.sum(p, axis=-1, keepdims=True) + jnp.exp(sink - m)
                p_ref[pl.ds(r0, rc), :] = (p / denom).astype(BF16)
                return carry

            lax.fori_loop(0, KV_GROUP * blk // rc, softmax_rows, 0, unroll=True)
            o = jnp.dot(p_ref[...], vband, preferred_element_type=F32)
            for g, h in enumerate(heads):
                o_ref[0, rows, h * HEAD_DIM:(h + 1) * HEAD_DIM] = (
                    o[g * blk:(g + 1) * blk].astype(o_ref.dtype))


def windowed_attention(qkv, sink, bias, *, attn_w, kv_w, qb=4):
    b, s, _ = qkv.shape
    blk = ATTN_BLOCK
    nblk = s // blk
    qb = _tile(nblk, qb)
    n_kv = kv_w // HEAD_DIM
    kcol = attn_w // kv_w
    assert attn_w % kv_w == 0 and n_kv * KV_GROUP * HEAD_DIM == attn_w

    def edge_spec(col, first):
        return pl.BlockSpec(
            (1, blk, kv_w),
            lambda bi, n: (bi, jnp.clip(n * qb + (-1 if first else qb), 0, nblk - 1), col))

    def mid_spec(col):
        return pl.BlockSpec((1, qb * blk, kv_w), lambda bi, n: (bi, n, col))

    return pl.pallas_call(
        functools.partial(_attn_kernel, seq=s, n_kv=n_kv, qb=qb),
        grid=(b, nblk // qb),
        in_specs=[
            pl.BlockSpec(memory_space=pltpu.SMEM),
            pl.BlockSpec((1, qb * blk, attn_w), lambda bi, n: (bi, n, 0)),
            edge_spec(kcol, True), mid_spec(kcol), edge_spec(kcol, False),
            edge_spec(kcol + 1, True), mid_spec(kcol + 1), edge_spec(kcol + 1, False),
            pl.BlockSpec(bias.shape, lambda bi, n: (0, 0)),
        ],
        out_specs=pl.BlockSpec((1, qb * blk, attn_w), lambda bi, n: (bi, n, 0)),
        out_shape=jax.ShapeDtypeStruct((b, s, attn_w), BF16),
        scratch_shapes=[pltpu.VMEM((KV_GROUP * blk, 3 * blk), F32),
                        pltpu.VMEM((KV_GROUP * blk, 3 * blk), BF16)],
        compiler_params=_params(("parallel", "parallel")),
        name="windowed_attention",
    )(sink.astype(F32), qkv, qkv, qkv, qkv, qkv, qkv, qkv, bias)


def _s5_operands(a_re, a_im, log_dt, b_re, b_im, c_re, c_im, chunk, n_levels):
    ell = chunk
    p = a_re.shape[-1]
    dt = jnp.exp(log_dt)[..., None]
    mag = jnp.exp(a_re * dt)
    lb_re = mag * jnp.cos(a_im * dt)
    lb_im = mag * jnp.sin(a_im * dt)
    den = a_re * a_re + a_im * a_im
    nr = lb_re - 1.0
    coef_re = (nr * a_re + lb_im * a_im) / den
    coef_im = (lb_im * a_re - nr * a_im) / den
    bb_re = coef_re[..., None] * b_re - coef_im[..., None] * b_im
    bb_im = coef_re[..., None] * b_im + coef_im[..., None] * b_re
    bt_re, bt_im = bb_re.transpose(0, 1, 3, 2), bb_im.transpose(0, 1, 3, 2)

    def powers(tau):
        t = tau.astype(F32)[None, None, :, None]
        pm = jnp.exp((a_re * dt)[:, :, None, :] * t)
        ang = (a_im * dt)[:, :, None, :] * t
        return pm * jnp.cos(ang), pm * jnp.sin(ang)

    g = a_re.shape[1]
    assert 2 * p == LANES and g % 2 == 0
    odd = (jnp.arange(g) % 2 == 1)[:, None, None]
    cat = lambda *ts: jnp.concatenate(ts, axis=-1)
    lp = lambda t: jnp.pad(t, [(0, 0)] * (t.ndim - 1) + [(0, LANES - p)])
    sp = lambda t: jnp.where(odd, cat(jnp.zeros_like(t), t), cat(t, jnp.zeros_like(t)))
    ii = jnp.arange(ell)
    dn_re, dn_im = powers(ell - 1 - ii)
    up_re, up_im = powers(ii)
    e1_re, e1_im = powers(ii + 1)
    e2_re, e2_im = powers(ell - ii)
    lv_re, lv_im = powers(ell * (2 ** jnp.arange(n_levels)))
    lev = SUBLANES * pl.cdiv(n_levels, SUBLANES)
    lvp = lambda t: jnp.pad(sp(t), ((0, 0), (0, lev - n_levels), (0, 0)))
    wide = jnp.concatenate([
        cat(sp(dn_re[0]), sp(dn_re[0]), sp(up_re[1]), sp(up_re[1])),
        cat(sp(dn_im[0]), sp(dn_im[0]), sp(up_im[1]), sp(up_im[1])),
        cat(sp(e1_re[0]), sp(e1_re[0]), sp(e2_re[1]), sp(e2_re[1])),
        cat(sp(e1_im[0]), sp(e1_im[0]), sp(e2_im[1]), sp(e2_im[1])),
        cat(sp(bt_re[0]), sp(bt_im[0]), sp(bt_re[1]), sp(bt_im[1])),
        cat(-sp(bt_im[0]), sp(bt_re[0]), -sp(bt_im[1]), sp(bt_re[1])),
        cat(sp(c_re[0]), -sp(c_im[0]), sp(c_re[1]), -sp(c_im[1])),
        cat(-sp(c_im[0]), -sp(c_re[0]), -sp(c_im[1]), -sp(c_re[1])),
        cat(lvp(lv_re[0]), lvp(lv_re[0]), lvp(lv_re[1]), lvp(lv_re[1])),
        cat(-lvp(lv_im[0]), lvp(lv_im[0]), -lvp(lv_im[1]), lvp(lv_im[1])),
    ], axis=1)

    def lag_rows(fwd, bwd):
        f = jnp.pad(lp(fwd), ((0, 0), (ell - 1, 1), (0, 0)))
        b = jnp.pad(lp(bwd), ((0, 0), (0, ell), (0, 0)))
        return cat(f, b)

    narrow = jnp.concatenate([
        lag_rows(up_re[0], dn_re[1]), lag_rows(up_im[0], dn_im[1]),
        cat(lp(c_re[0]), lp(c_re[1])), cat(lp(c_im[0]), lp(c_im[1])),
        cat(lp(bt_re[0]), lp(bt_re[1])), cat(lp(bt_im[0]), lp(bt_im[1])),
    ], axis=1)
    return wide, narrow


def _s5_pack_kernel(u_ref, o_ref, t_ref, *, n_lt, h, chunk):
    def transpose_block(lt, c):
        src = pl.multiple_of(lt * LANES, LANES)
        dst = pl.multiple_of(lt * S5_PITCH, SUBLANES)
        t_ref[pl.ds(dst, LANES), :] = u_ref[pl.ds(src, LANES), :].T
        return c

    lax.fori_loop(0, n_lt, transpose_block, 0, unroll=8)
    n_sub = LANES // chunk

    def pack_group(g8, c):
        for kt in range(h // n_sub):
            ms = [t_ref[pl.ds(g8 * h + kt * n_sub + kk, n_lt, stride=S5_PITCH), :] for kk in range(n_sub)]
            for c_lo in range(n_sub):
                tile = jnp.concatenate([m[:, c_lo * chunk:(c_lo + 1) * chunk] for m in ms], axis=1)
                o_ref[g8, c_lo * n_lt:(c_lo + 1) * n_lt, kt * LANES:(kt + 1) * LANES] = tile.astype(BF16)
        return c

    lax.fori_loop(0, LANES // h, pack_group, 0)


def _s5_unpack_kernel(y_ref, o_ref, t_ref, *, n_lt, h, chunk):
    n_sub = LANES // chunk

    def unpack_group(g8, c):
        for hh in range(h):
            lane0 = (hh // n_sub) * LANES + (hh % n_sub) * chunk
            m = jnp.concatenate([y_ref[g8, c_lo * n_lt:(c_lo + 1) * n_lt, lane0:lane0 + chunk]
                                 for c_lo in range(n_sub)], axis=1)
            t_ref[pl.ds(g8 * h + hh, n_lt, stride=S5_PITCH), :] = m
        return c

    lax.fori_loop(0, LANES // h, unpack_group, 0)

    def transpose_block(lt, c):
        src = pl.multiple_of(lt * S5_PITCH, SUBLANES)
        dst = pl.multiple_of(lt * LANES, LANES)
        o_ref[pl.ds(dst, LANES), :] = t_ref[pl.ds(src, LANES), :].T
        return c

    lax.fori_loop(0, n_lt, transpose_block, 0, unroll=8)


def _shift_rows(z, r, seg):
    if r == 0:
        return z
    rows = z.shape[0]
    rolled = pltpu.roll(z, r % rows, 0)
    assert seg & (seg - 1) == 0
    pos = lax.broadcasted_iota(I32, z.shape, 0) & (seg - 1)
    keep = (pos >= r) if r > 0 else (pos < seg + r)
    return jnp.where(keep, rolled, 0.0)


def _shift_chunks(blocks, d, seg):
    n_sub = len(blocks)
    out = []
    for c_lo in range(n_sub):
        e, s_lo = divmod(c_lo - d, n_sub)
        out.append(_shift_rows(blocks[s_lo], -e, seg))
    return out


def _s5_group_tables(w, nr, perm, *, n_levels, chunk, h):
    ell, lh = chunk, chunk * h
    lev = SUBLANES * pl.cdiv(n_levels, SUBLANES)
    nt = (((1,), (1,)), ((), ()))
    pwf_re, pwf_im, pwe_re, pwe_im = (w[k * ell:(k + 1) * ell] for k in range(4))
    o = 4 * ell
    bb1, bb2, cc1, cc2 = (w[o + k * h:o + (k + 1) * h] for k in range(4))
    o += 4 * h
    pa_all, pb_all = w[o:o + lev], w[o + lev:o + 2 * lev]
    f_tab = jnp.concatenate([pwf_re * bb1[k:k + 1, :] + pwf_im * bb2[k:k + 1, :] for k in range(h)],
                            axis=0).astype(BF16)
    et_tab = jnp.concatenate([pwe_re * cc1[k:k + 1, :] + pwe_im * cc2[k:k + 1, :] for k in range(h)],
                             axis=0).astype(BF16)
    pwm_re, pwm_im = nr[:2 * ell], nr[2 * ell:4 * ell]
    o = 4 * ell
    cr, ci, br, bi = (nr[o + k * h:o + (k + 1) * h] for k in range(4))
    rep = lambda x: jnp.concatenate([jnp.broadcast_to(x[j:j + 1, :], (h, x.shape[1]))
                                     for j in range(x.shape[0])], axis=0)
    tile = lambda t: jnp.concatenate([t] * (2 * ell), axis=0)
    pr, pi = rep(pwm_re), rep(pwm_im)
    crt, cit = tile(cr), tile(ci)
    a = jnp.concatenate([pr * crt - pi * cit, pr * cit + pi * crt], axis=1)
    b = jnp.concatenate([br, -bi], axis=1)
    a1 = a.astype(BF16)
    a2 = (a - a1.astype(F32)).astype(BF16)
    b1 = b.astype(BF16)
    b2 = (b - b1.astype(F32)).astype(BF16)
    lead = lax.dot_general(jnp.concatenate([b1, b2], axis=0), a1, nt, preferred_element_type=F32)
    strip = lead[:h] + lead[h:] + lax.dot_general(b1, a2, nt, preferred_element_type=F32)
    t_jk = jnp.concatenate([strip[:, (ell - 1 - j) * h:(ell - 1 - j) * h + lh] for j in range(ell)],
                           axis=0).astype(BF16)
    t_tab = lax.dot_general(jnp.dot(perm, t_jk, preferred_element_type=F32).astype(BF16), perm, nt,
                            preferred_element_type=F32).astype(BF16)
    return f_tab, et_tab, t_tab, pa_all, pb_all


def _s5_kernel(u_ref, wide_ref, narrow_ref, perm_ref, o_ref, *, seg, n_levels, chunk, h):
    n_sub = LANES // chunk
    nt = (((1,), (1,)), ((), ()))
    perm = perm_ref[...]
    tabs = [_s5_group_tables(wide_ref[gi], narrow_ref[gi], perm, n_levels=n_levels, chunk=chunk, h=h)
            for gi in range(2)]
    us = [u_ref[gi] for gi in range(2)]
    z = sum(jnp.dot(u, t[0], preferred_element_type=F32) for u, t in zip(us, tabs))
    pa_all = tabs[0][3] + tabs[1][3]
    pb_all = tabs[0][4] + tabs[1][4]
    n_lt = z.shape[0] // n_sub
    half = 2 * LANES

    def swap(x):
        return jnp.concatenate([x[:, LANES:], x[:, :LANES]], axis=1)

    zf = [z[c * n_lt:(c + 1) * n_lt, :half] for c in range(n_sub)]
    zb = [z[c * n_lt:(c + 1) * n_lt, half:] for c in range(n_sub)]
    for k in range(n_levels):
        pa = pa_all[k:k + 1, :]
        pb = pb_all[k:k + 1, :]
        sf = _shift_chunks(zf, 2 ** k, seg)
        sb = _shift_chunks(zb, -(2 ** k), seg)
        zf = [a + pa[:, :half] * s + pb[:, :half] * swap(s) for a, s in zip(zf, sf)]
        zb = [a + pa[:, half:] * s + pb[:, half:] * swap(s) for a, s in zip(zb, sb)]
    hin = jnp.concatenate([jnp.concatenate(_shift_chunks(zf, 1, seg), axis=0),
                           jnp.concatenate(_shift_chunks(zb, -1, seg), axis=0)], axis=1).astype(BF16)
    for gi in range(2):
        y = jnp.dot(us[gi], tabs[gi][2], preferred_element_type=F32)
        o_ref[gi] = y + lax.dot_general(hin, tabs[gi][1], nt, preferred_element_type=F32)


def s5_scan(u, operands, *, seq, chunk, n_levels):
    wide, narrow = operands
    t, w = u.shape
    g = wide.shape[0]
    h = w // g
    lh = chunk * h
    n_lt = t // LANES
    rows = t // chunk
    gpt = LANES // h
    assert t % LANES == 0 and seq % LANES == 0 and LANES % chunk == 0 and h % (LANES // chunk) == 0
    scratch = pltpu.VMEM((n_lt * S5_PITCH, LANES), F32)
    ug = pl.pallas_call(
        functools.partial(_s5_pack_kernel, n_lt=n_lt, h=h, chunk=chunk),
        grid=(w // LANES,),
        in_specs=[pl.BlockSpec((t, LANES), lambda q: (0, q))],
        out_specs=pl.BlockSpec((gpt, rows, lh), lambda q: (q, 0, 0)),
        out_shape=jax.ShapeDtypeStruct((g, rows, lh), BF16),
        scratch_shapes=[scratch],
        compiler_params=_params(("parallel",)),
        name="s5_pack",
    )(u)
    kj = jnp.arange(lh)
    perm = (kj[:, None] // chunk + (kj[:, None] % chunk) * h == kj[None, :]).astype(BF16)
    yg = pl.pallas_call(
        functools.partial(_s5_kernel, seg=seq // LANES, n_levels=n_levels, chunk=chunk, h=h),
        grid=(g // 2,),
        in_specs=[
            pl.BlockSpec((2, rows, lh), lambda i: (i, 0, 0)),
            pl.BlockSpec((2,) + wide.shape[1:], lambda i: (i, 0, 0)),
            pl.BlockSpec((2,) + narrow.shape[1:], lambda i: (i, 0, 0)),
            pl.BlockSpec((lh, lh), lambda i: (0, 0)),
        ],
        out_specs=pl.BlockSpec((2, rows, lh), lambda i: (i, 0, 0)),
        out_shape=jax.ShapeDtypeStruct((g, rows, lh), F32),
        compiler_params=_params(("parallel",)),
        name="s5_scan",
    )(ug, wide, narrow, perm)
    return pl.pallas_call(
        functools.partial(_s5_unpack_kernel, n_lt=n_lt, h=h, chunk=chunk),
        grid=(w // LANES,),
        in_specs=[pl.BlockSpec((gpt, rows, lh), lambda q: (q, 0, 0))],
        out_specs=pl.BlockSpec((t, LANES), lambda q: (0, q)),
        out_shape=jax.ShapeDtypeStruct((t, w), F32),
        scratch_shapes=[scratch],
        compiler_params=_params(("parallel",)),
        name="s5_unpack",
    )(yg)


def _glu_kernel(y_ref, u_ref, d_ref, w_ref, b_ref, o_ref):
    y = y_ref[...] + d_ref[...] * u_ref[...]
    g = _gelu(y)
    z = jnp.dot(g.astype(BF16), w_ref[...], preferred_element_type=F32) + b_ref[...]
    o_ref[...] = (g * jax.nn.sigmoid(z)).astype(o_ref.dtype)


def s5_glu(y, u, d_skip, glu_w_bf16, glu_b, *, tm=512):
    m, w = y.shape
    tm = _tile(m, tm)
    return pl.pallas_call(
        _glu_kernel,
        grid=(m // tm,),
        in_specs=[
            pl.BlockSpec((tm, w), lambda i: (i, 0)),
            pl.BlockSpec((tm, w), lambda i: (i, 0)),
            pl.BlockSpec((1, w), lambda i: (0, 0)),
            pl.BlockSpec((w, w), lambda i: (0, 0)),
            pl.BlockSpec((1, w), lambda i: (0, 0)),
        ],
        out_specs=pl.BlockSpec((tm, w), lambda i: (i, 0)),
        out_shape=jax.ShapeDtypeStruct((m, w), BF16),
        compiler_params=_params(("parallel",)),
        name="s5_glu",
    )(y, u, d_skip.reshape(1, w), glu_w_bf16, glu_b.reshape(1, w))


def _gmlp_kernel(x_ref, s0_ref, s1_ref, ng_ref, w_ref, g_ref, b_ref, ws_ref, bs_ref, o_ref, xn_ref, *,
                 width, n_heads, tm, half, pitch):
    hd = width // n_heads
    ck = GMLP_CHUNK
    x = x_ref[...] + _unslab(s0_ref, s1_ref, tm=tm, half=half, pitch=pitch)
    xn_ref[...] = x
    hn = _rms(x, ng_ref[...]).astype(BF16)
    z = _gelu(jnp.dot(hn, w_ref[...], preferred_element_type=F32))
    u, v = z[:, :width], z[:, width:]
    mu = jnp.mean(v, axis=-1, keepdims=True)
    var = jnp.mean(jnp.square(v - mu), axis=-1, keepdims=True)
    vn = ((v - mu) * lax.rsqrt(var + EPS) * g_ref[...] + b_ref[...]).astype(BF16)
    for c in range(tm // ck):
        rows = slice(c * ck, (c + 1) * ck)
        for h in range(n_heads):
            sl = slice(h * hd, (h + 1) * hd)
            mixed = jnp.dot(ws_ref[h], vn[rows, sl], preferred_element_type=F32) + bs_ref[:, h:h + 1]
            o_ref[rows, sl] = (u[rows, sl] * mixed).astype(o_ref.dtype)


def gmlp_gate(x, moe_slabs, norm_g, w_in_bf16, ln_g, ln_b, w_s_bf16, b_s, *, seq, tm=512):
    m, d = x.shape
    w2 = w_in_bf16.shape[1]
    width = w2 // 2
    n_heads = w_s_bf16.shape[0]
    ck = GMLP_CHUNK
    tm = max(ck, _tile(seq, tm))
    nt = seq // tm
    half = d // 2
    pitch = _slab_pitch(half)
    return pl.pallas_call(
        functools.partial(_gmlp_kernel, width=width, n_heads=n_heads, tm=tm, half=half, pitch=pitch),
        grid=(m // tm,),
        in_specs=[
            pl.BlockSpec((tm, d), lambda i: (i, 0)),
            pl.BlockSpec((1, 1, tm * pitch, LANES), lambda i: (i // nt, 0, i % nt, 0)),
            pl.BlockSpec((1, 1, tm * pitch, LANES), lambda i: (i // nt, 1, i % nt, 0)),
            pl.BlockSpec((1, d), lambda i: (0, 0)),
            pl.BlockSpec((d, w2), lambda i: (0, 0), pipeline_mode=pl.Buffered(1)),
            pl.BlockSpec((1, width), lambda i: (0, 0)),
            pl.BlockSpec((1, width), lambda i: (0, 0)),
            pl.BlockSpec((n_heads, ck, ck), lambda i: (0, 0, 0)),
            pl.BlockSpec((ck, n_heads), lambda i: (0, 0)),
        ],
        out_specs=[pl.BlockSpec((tm, width), lambda i: (i, 0)),
                   pl.BlockSpec((tm, d), lambda i: (i, 0))],
        out_shape=[jax.ShapeDtypeStruct((m, width), BF16),
                   jax.ShapeDtypeStruct((m, d), F32)],
        compiler_params=_params(("parallel",)),
        name="gmlp_gate",
    )(x, moe_slabs, moe_slabs, norm_g.reshape(1, d), w_in_bf16, ln_g.reshape(1, width),
      ln_b.reshape(1, width), w_s_bf16, b_s.T.astype(F32))


def _slab_pitch(half):
    return SUBLANES * pl.cdiv(half // LANES, SUBLANES)


def _route_rows(x, g_ref, rt_ref, aff_ref, slab_ref, *, tm, half, pitch):
    h = _rms(x, g_ref[...])
    nt = (((1,), (1,)), ((), ()))
    e = rt_ref.shape[0]
    h1 = h.astype(BF16)
    h1f = h1.astype(F32)
    h2 = (h - h1f).astype(BF16)
    r1 = rt_ref[...].astype(BF16)
    r2 = (rt_ref[...] - r1.astype(F32)).astype(BF16)
    lead = lax.dot_general(jnp.concatenate([r1, r2], axis=0), h1, nt, preferred_element_type=F32)
    logits = lead[:e] + lead[e:] + lax.dot_general(r1, h2, nt, preferred_element_type=F32)
    mx = jnp.max(logits, axis=0, keepdims=True)
    ex = jnp.exp(logits - mx)
    aff_ref[0] = ex / jnp.sum(ex, axis=0, keepdims=True)
    lo = lax.bitcast_convert_type(h1f[:, :half], U32) >> 16
    hi = lax.bitcast_convert_type(h1f[:, half:], U32) & jnp.uint32(0xFFFF0000)
    word = lo | hi
    for s in range(half // LANES):
        slab_ref[pl.ds(s, tm, stride=pitch), :] = word[:, s * LANES:(s + 1) * LANES]
    if pitch > half // LANES:
        for s in range(half // LANES, pitch):
            slab_ref[pl.ds(s, tm, stride=pitch), :] = jnp.zeros((tm, LANES), U32)


def _lane_cumsum(m_bf16, tri):
    e, s = m_bf16.shape
    off = jnp.zeros((e, 1), F32)
    out = []
    for j in range(s // LANES):
        c = jnp.dot(m_bf16[:, j * LANES:(j + 1) * LANES], tri, preferred_element_type=F32) + off
        out.append(c)
        off = c[:, LANES - 1:LANES]
    return jnp.concatenate(out, axis=1)


def _topk_kernel(aff_ref, idx_ref, gate_ref, *, cap, kchunk):
    aff = aff_ref[0]
    e, s = aff.shape
    bits = lax.bitcast_convert_type(aff, I32)

    def count_ge(t):
        return jnp.sum((bits >= t).astype(F32), axis=1, keepdims=True)

    def body(_, carry):
        lo, hi = carry
        mid = lo + ((hi - lo) >> 1)
        ok = count_ge(mid) >= cap
        return jnp.where(ok, mid, lo), jnp.where(ok, hi, mid)

    lo0 = jnp.zeros((e, 1), I32)
    hi0 = jnp.full((e, 1), 0x7F800000, I32)
    thr, _ = lax.fori_loop(0, 32, body, (lo0, hi0))
    gt = bits > thr
    eq = bits == thr
    need = cap - jnp.sum(gt.astype(F32), axis=1, keepdims=True)
    tri = (lax.broadcasted_iota(I32, (LANES, LANES), 0)
           <= lax.broadcasted_iota(I32, (LANES, LANES), 1)).astype(BF16)
    eq_rank = _lane_cumsum(jnp.where(eq, 1.0, 0.0).astype(BF16), tri)
    sel = gt | (eq & (eq_rank <= need))
    pos = _lane_cumsum(jnp.where(sel, 1.0, 0.0).astype(BF16), tri)
    pos = jnp.where(sel, pos, 0.0)

    tok = lax.broadcasted_iota(I32, (1, s), 1)
    t_hi = (tok >> 6).astype(F32)
    t_lo = (tok & 63).astype(F32)
    slot = lax.broadcasted_iota(I32, (cap, kchunk), 0).astype(F32) + 1.0
    lrow = lax.broadcasted_iota(I32, (SUBLANES, s), 0)
    for ei in range(e):
        a = aff[ei:ei + 1, :]
        g1 = a.astype(BF16).astype(F32)
        g2 = (a - g1).astype(BF16).astype(F32)
        g3 = (a - g1) - g2
        lhs = jnp.where(lrow == 0, t_hi, jnp.where(lrow == 1, t_lo, jnp.where(
            lrow == 2, g1, jnp.where(lrow == 3, g2, jnp.where(lrow == 4, g3, 0.0)))))
        lhs = lhs.astype(BF16)
        acc = jnp.zeros((SUBLANES, cap), F32)
        for c0 in range(0, s, kchunk):
            onehot = jnp.where(pos[ei:ei + 1, c0:c0 + kchunk] == slot, 1.0, 0.0).astype(BF16)
            acc = acc + lax.dot_general(lhs[:, c0:c0 + kchunk], onehot, (((1,), (1,)), ((), ())),
                                        preferred_element_type=F32)
        idx_ref[0, ei:ei + 1, :] = (acc[0:1] * 64.0 + acc[1:2]).astype(I32)
        gate_ref[0, ei:ei + 1, :] = acc[2:3] + acc[3:4] + acc[4:5]


def moe_topk(aff, cap):
    b, e, s = aff.shape
    return pl.pallas_call(
        functools.partial(_topk_kernel, cap=cap, kchunk=_tile(s, 1024)),
        grid=(b,),
        in_specs=[pl.BlockSpec((1, e, s), lambda i: (i, 0, 0))],
        out_specs=[pl.BlockSpec((1, e, cap), lambda i: (i, 0, 0)),
                   pl.BlockSpec((1, e, cap), lambda i: (i, 0, 0))],
        out_shape=[jax.ShapeDtypeStruct((b, e, cap), I32),
                   jax.ShapeDtypeStruct((b, e, cap), F32)],
        compiler_params=_params(("parallel",)),
        name="moe_topk",
    )(aff)


def _gather_kernel(idx_ref, slab_ref, o_ref, rows_ref, *, cap, half, pitch, unroll):
    def body(j0, c):
        for u in range(unroll):
            j = j0 * unroll + u
            src = pl.multiple_of(idx_ref[0, 0, j] * pitch, pitch)
            rows_ref[pl.ds(pl.multiple_of(j * pitch, pitch), pitch), :] = slab_ref[pl.ds(src, pitch), :]
        return c

    lax.fori_loop(0, cap // unroll, body, 0)
    for s in range(half // LANES):
        word = rows_ref[pl.ds(s, cap, stride=pitch), :]
        lo = lax.bitcast_convert_type(word << 16, F32)
        hi = lax.bitcast_convert_type(word & jnp.uint32(0xFFFF0000), F32)
        o_ref[0, :, s * LANES:(s + 1) * LANES] = lo.astype(BF16)
        o_ref[0, :, half + s * LANES:half + (s + 1) * LANES] = hi.astype(BF16)


def moe_gather(idx, slabs, *, seq, d):
    b, e, cap = idx.shape
    half = d // 2
    pitch = _slab_pitch(half)
    return pl.pallas_call(
        functools.partial(_gather_kernel, cap=cap, half=half, pitch=pitch, unroll=8),
        grid=(b, e),
        in_specs=[
            pl.BlockSpec((1, 1, cap), lambda bi, ei: (bi * e + ei, 0, 0), memory_space=pltpu.SMEM),
            pl.BlockSpec((seq * pitch, LANES), lambda bi, ei: (bi, 0)),
        ],
        out_specs=pl.BlockSpec((1, cap, d), lambda bi, ei: (ei, bi, 0)),
        out_shape=jax.ShapeDtypeStruct((e, b * cap, d), BF16),
        scratch_shapes=[pltpu.VMEM((cap * pitch, LANES), U32)],
        compiler_params=_params(("parallel", "arbitrary")),
        name="moe_gather",
    )(idx.reshape(b * e, 1, cap), slabs)


def _ffn_kernel(xs_ref, w1_ref, w3_ref, w2_ref, o_ref, hid_ref, *, nt, tf):
    step = pl.program_id(1)

    @pl.when(step < nt)
    def _():
        xs = xs_ref[0]
        a = jnp.dot(xs, w1_ref[0, 0].astype(BF16), preferred_element_type=F32)
        g = jnp.dot(xs, w3_ref[0, 0].astype(BF16), preferred_element_type=F32)
        hid_ref[step] = (a * jax.nn.sigmoid(a) * g).astype(BF16)

    @pl.when(step >= nt)
    def _():
        hid = jnp.concatenate([hid_ref[k] for k in range(nt)], axis=1)
        o_ref[0] = jnp.dot(hid, w2_ref[0, 0].astype(BF16), preferred_element_type=F32)


def moe_ffn(xs, w1, w3, w2, layer, *, tf=256, tn=512):
    e, m, d = xs.shape
    f = w1.shape[3]
    tf = _tile(f, tf)
    tn = _tile(d, tn)
    nt = f // tf
    nn = d // tn
    return pl.pallas_call(
        functools.partial(_ffn_kernel, nt=nt, tf=tf),
        grid=(e, nt + nn),
        in_specs=[
            pl.BlockSpec((1, m, d), lambda ei, t: (ei, 0, 0)),
            pl.BlockSpec((1, 1, d, tf), lambda ei, t: (layer, ei, 0, jnp.minimum(t, nt - 1))),
            pl.BlockSpec((1, 1, d, tf), lambda ei, t: (layer, ei, 0, jnp.minimum(t, nt - 1))),
            pl.BlockSpec((1, 1, f, tn), lambda ei, t: (layer, ei, 0, jnp.maximum(t - nt, 0))),
        ],
        out_specs=pl.BlockSpec((1, m, tn), lambda ei, t: (ei, 0, jnp.maximum(t - nt, 0))),
        out_shape=jax.ShapeDtypeStruct((e, m, d), F32),
        scratch_shapes=[pltpu.VMEM((nt, m, tf), BF16)],
        compiler_params=_params(("parallel", "arbitrary")),
        name="moe_ffn",
    )(xs, w1, w3, w2)


def _combine_kernel(idx_ref, gate_ref, ys_ref, o_ref, rows_ref, *, cap, half, pitch, unroll):
    ei = pl.program_id(2)

    @pl.when(ei == 0)
    def _():
        o_ref[...] = jnp.zeros(o_ref.shape, F32)

    for s in range(half // LANES):
        rows_ref[pl.ds(s, cap, stride=pitch), :] = ys_ref[0, :, s * LANES:(s + 1) * LANES]

    def body(j0, c):
        dsts, vals = [], []
        for u in range(unroll):
            j = j0 * unroll + u
            dst = pl.multiple_of(idx_ref[0, 0, j] * pitch, pitch)
            src = pl.multiple_of(j * pitch, pitch)
            vals.append(o_ref[0, 0, pl.ds(dst, pitch), :]
                        + gate_ref[0, 0, j] * rows_ref[pl.ds(src, pitch), :])
            dsts.append(dst)
        for dst, val in zip(dsts, vals):
            o_ref[0, 0, pl.ds(dst, pitch), :] = val
        return c

    lax.fori_loop(0, cap // unroll, body, 0)


def moe_combine(idx, gate, ys, *, seq, d):
    b, e, cap = idx.shape
    half = d // 2
    pitch = _slab_pitch(half)
    return pl.pallas_call(
        functools.partial(_combine_kernel, cap=cap, half=half, pitch=pitch, unroll=8),
        grid=(b, 2, e),
        in_specs=[
            pl.BlockSpec((1, 1, cap), lambda bi, dh, ei: (bi * e + ei, 0, 0), memory_space=pltpu.SMEM),
            pl.BlockSpec((1, 1, cap), lambda bi, dh, ei: (bi * e + ei, 0, 0), memory_space=pltpu.SMEM),
            pl.BlockSpec((1, cap, half), lambda bi, dh, ei: (ei, bi, dh)),
        ],
        out_specs=pl.BlockSpec((1, 1, seq * pitch, LANES), lambda bi, dh, ei: (bi, dh, 0, 0)),
        out_shape=jax.ShapeDtypeStruct((b, 2, seq * pitch, LANES), F32),
        scratch_shapes=[pltpu.VMEM((cap * pitch, LANES), F32)],
        compiler_params=_params(("parallel", "parallel", "arbitrary")),
        name="moe_combine",
    )(idx.reshape(b * e, 1, cap), gate.reshape(b * e, 1, cap), ys)


def _unslab(s0_ref, s1_ref, *, tm, half, pitch):
    parts = []
    for s_ref in (s0_ref, s1_ref):
        for s in range(half // LANES):
            parts.append(s_ref[0, 0, pl.ds(s, tm, stride=pitch), :])
    return jnp.concatenate(parts, axis=1)


def _unslab_kernel(*refs, tm, half, pitch, final_norm):
    x_ref, s0_ref, s1_ref = refs[:3]
    o_ref = refs[-1]
    y = x_ref[0] + _unslab(s0_ref, s1_ref, tm=tm, half=half, pitch=pitch)
    if final_norm:
        y = _rms(y, refs[3][...])
    o_ref[0] = y


def moe_residual(x, moe_slabs, final_g=None, *, tm=256):
    b, s, d = x.shape
    half = d // 2
    pitch = _slab_pitch(half)
    tm = _tile(s, tm)
    final_norm = final_g is not None
    in_specs = [
        pl.BlockSpec((1, tm, d), lambda bi, j: (bi, j, 0)),
        pl.BlockSpec((1, 1, tm * pitch, LANES), lambda bi, j: (bi, 0, j, 0)),
        pl.BlockSpec((1, 1, tm * pitch, LANES), lambda bi, j: (bi, 1, j, 0)),
    ]
    args = [x, moe_slabs, moe_slabs]
    if final_norm:
        in_specs.append(pl.BlockSpec((1, d), lambda bi, j: (0, 0)))
        args.append(final_g.reshape(1, d))
    return pl.pallas_call(
        functools.partial(_unslab_kernel, tm=tm, half=half, pitch=pitch, final_norm=final_norm),
        grid=(b, s // tm),
        in_specs=in_specs,
        out_specs=pl.BlockSpec((1, tm, d), lambda bi, j: (bi, j, 0)),
        out_shape=jax.ShapeDtypeStruct((b, s, d), F32),
        compiler_params=_params(("parallel", "parallel")),
        name="moe_residual",
    )(*args)


def expert_choice_ffn(aff, slabs, w1, w3, w2, layer, *, seq, d):
    cap = CAPACITY_FACTOR * seq // N_EXPERTS
    idx, gate = moe_topk(aff, cap)
    xs = moe_gather(idx, slabs, seq=seq, d=d)
    ys = moe_ffn(xs, w1, w3, w2, layer)
    return moe_combine(idx, gate, ys, seq=seq, d=d)


def kernel(x, rel_bias, mix_norm, ffn_norm, final_norm, even_w_in, attn_sink, ssm_a_re, ssm_a_im, ssm_log_dt, ssm_b_re, ssm_b_im, ssm_c_re, ssm_c_im, ssm_d, glu_w, glu_b, even_w_out, odd_w_in, sgu_ln_g, sgu_ln_b, sgu_w, sgu_b, odd_w_out, router, moe_w1, moe_w3, moe_w2):
    b, s, d = x.shape
    t = b * s
    depth = mix_norm.shape[0]
    ssm_w = ssm_d.shape[-1]
    attn_w = even_w_out.shape[1] - ssm_w
    kv_w = (even_w_in.shape[-1] - attn_w - ssm_w) // 2
    bias = attention_bias(rel_bias)

    x2 = x.reshape(t, d)
    moe = None
    for layer in range(depth):
        i = layer // 2
        if layer % 2 == 0:
            if moe is not None:
                x2 = moe_residual(x2.reshape(b, s, d), moe).reshape(t, d)
            qkv, u = in_projection(x2, mix_norm[layer], even_w_in[i].astype(BF16), attn_w + 2 * kv_w)
            attn = windowed_attention(qkv.reshape(b, s, -1), attn_sink[i], bias,
                                      attn_w=attn_w, kv_w=kv_w)
            n_levels = max(1, (s // S5_CHUNK - 1).bit_length())
            operands = _s5_operands(ssm_a_re[i], ssm_a_im[i], ssm_log_dt[i], ssm_b_re[i], ssm_b_im[i],
                                    ssm_c_re[i], ssm_c_im[i], S5_CHUNK, n_levels)
            y = s5_scan(u, operands, seq=s, chunk=S5_CHUNK, n_levels=n_levels)
            ssm = s5_glu(y, u, ssm_d[i], glu_w[i].astype(BF16), glu_b[i])
            mixed, w_out = [attn.reshape(t, attn_w), ssm], even_w_out[i]
        else:
            gated, x2 = gmlp_gate(x2, moe, mix_norm[layer], odd_w_in[i].astype(BF16), sgu_ln_g[i],
                                  sgu_ln_b[i], sgu_w[i].astype(BF16), sgu_b[i], seq=s)
            mixed, w_out = [gated], odd_w_out[i]
        x2, aff, slabs = matmul_residual_route(mixed, w_out.astype(BF16), x2, ffn_norm[layer],
                                               router[layer], seq=s)
        moe = expert_choice_ffn(aff, slabs, moe_w1, moe_w3, moe_w2, layer, seq=s, d=d)
    return moe_residual(x2.reshape(b, s, d), moe, final_norm)
```

```python
import functools
import math

import jax
import jax.numpy as jnp
from jax import lax
from jax.experimental import pallas as pl
from jax.experimental.pallas import tpu as pltpu

F32 = jnp.float32
BF16 = jnp.bfloat16
I32 = jnp.int32
U32 = jnp.uint32

EPS = 1e-6
NEG_INF = -1e30

LANES = 128
SUBLANES = 8
VMEM_LIMIT = 56 << 20

ATTN_BLOCK = 128
ATTN_ROW_CHUNK = 32
HEAD_DIM = 128
KV_GROUP = 4
REL_BUCKETS = 32
REL_MAX_DIST = 128
SSM_GROUP = 16
SSM_STATE = 64
S5_CHUNK = 32
S5_PITCH = 136
GMLP_CHUNK = 128
N_EXPERTS = 16
CAPACITY_FACTOR = 2


def _params(sem, vmem=VMEM_LIMIT):
    return pltpu.CompilerParams(dimension_semantics=sem, vmem_limit_bytes=vmem)


def _tile(n, want):
    t = min(n, want)
    while n % t:
        t //= 2
    return t


def _rms(x, g):
    ms = jnp.mean(x * x, axis=-1, keepdims=True)
    return x * lax.rsqrt(ms + EPS) * g


def _gelu(x):
    c = math.sqrt(2.0 / math.pi)
    return x * (0.5 * (1.0 + jnp.tanh(c * (x + 0.044715 * (x * x * x)))))


def _in_proj_kernel(x_ref, g_ref, w_ref, qkv_ref, u_ref):
    hn = _rms(x_ref[...], g_ref[...]).astype(BF16)
    y = jnp.dot(hn, w_ref[...], preferred_element_type=F32)
    split = qkv_ref.shape[1]
    qkv_ref[...] = y[:, :split]
    u_ref[...] = y[:, split:]


def in_projection(x, g, w_bf16, split, *, tm=512):
    m, k = x.shape
    n = w_bf16.shape[1]
    tm = _tile(m, tm)
    return pl.pallas_call(
        _in_proj_kernel,
        grid=(m // tm,),
        in_specs=[
            pl.BlockSpec((tm, k), lambda i: (i, 0)),
            pl.BlockSpec((1, k), lambda i: (0, 0)),
            pl.BlockSpec((k, n), lambda i: (0, 0), pipeline_mode=pl.Buffered(1)),
        ],
        out_specs=[pl.BlockSpec((tm, split), lambda i: (i, 0)),
                   pl.BlockSpec((tm, n - split), lambda i: (i, 0))],
        out_shape=[jax.ShapeDtypeStruct((m, split), F32),
                   jax.ShapeDtypeStruct((m, n - split), F32)],
        compiler_params=_params(("parallel",)),
        name="in_projection",
    )(x, g.reshape(1, k), w_bf16)


def _mm_res_route_kernel(*refs, n_lhs, tm, half, pitch):
    lhs = refs[:n_lhs]
    ws = refs[n_lhs:2 * n_lhs]
    res_ref, g_ref, rt_ref, o_ref, aff_ref, slab_ref = refs[2 * n_lhs:]
    acc = res_ref[...]
    for l_ref, w_ref in zip(lhs, ws):
        acc = acc + jnp.dot(l_ref[...], w_ref[...], preferred_element_type=F32)
    o_ref[...] = acc
    _route_rows(acc, g_ref, rt_ref, aff_ref, slab_ref, tm=tm, half=half, pitch=pitch)


def matmul_residual_route(lhs_list, w_bf16, res, norm_g, router, *, seq, tm=512):
    m, d = res.shape
    kk = lhs_list[0].shape[1]
    n_lhs = len(lhs_list)
    e = router.shape[1]
    assert all(l.shape == (m, kk) for l in lhs_list) and w_bf16.shape == (n_lhs * kk, d)
    tm = _tile(seq, tm)
    nt = seq // tm
    half = d // 2
    pitch = _slab_pitch(half)
    in_specs = [pl.BlockSpec((tm, kk), lambda i: (i, 0)) for _ in lhs_list]
    in_specs += [pl.BlockSpec((kk, d), functools.partial(lambda i, r: (r, 0), r=r),
                              pipeline_mode=pl.Buffered(1)) for r in range(n_lhs)]
    in_specs += [pl.BlockSpec((tm, d), lambda i: (i, 0)),
                 pl.BlockSpec((1, d), lambda i: (0, 0)),
                 pl.BlockSpec((e, d), lambda i: (0, 0))]
    return pl.pallas_call(
        functools.partial(_mm_res_route_kernel, n_lhs=n_lhs, tm=tm, half=half, pitch=pitch),
        grid=(m // tm,),
        in_specs=in_specs,
        out_specs=[pl.BlockSpec((tm, d), lambda i: (i, 0)),
                   pl.BlockSpec((1, e, tm), lambda i: (i // nt, 0, i % nt)),
                   pl.BlockSpec((tm * pitch, LANES), lambda i: (i, 0))],
        out_shape=[jax.ShapeDtypeStruct((m, d), F32),
                   jax.ShapeDtypeStruct((m // seq, e, seq), F32),
                   jax.ShapeDtypeStruct((m * pitch, LANES), U32)],
        compiler_params=_params(("parallel",)),
        name="matmul_residual_route",
    )(*lhs_list, *([w_bf16] * n_lhs), res, norm_g.reshape(1, d), router.T.astype(F32))


def _t5_bucket(rel):
    nb = REL_BUCKETS // 2
    max_exact = nb // 2
    base = jnp.where(rel > 0, nb, 0)
    n = jnp.abs(rel)
    nf = jnp.maximum(n, 1).astype(F32)
    large = max_exact + (jnp.log(nf / max_exact) / math.log(REL_MAX_DIST / max_exact)
                         * (nb - max_exact)).astype(I32)
    large = jnp.minimum(large, nb - 1)
    return base + jnp.where(n < max_exact, n, large)


def _bias_kernel(rb_ref, bucket_ref, o_ref, *, n_heads):
    bkt = bucket_ref[...]
    row = lax.broadcasted_iota(I32, bkt.shape, 0)
    col = lax.broadcasted_iota(I32, bkt.shape, 1)
    in_window = jnp.abs(col - ATTN_BLOCK - row) <= ATTN_BLOCK
    for h in range(n_heads):
        acc = jnp.zeros(bkt.shape, F32)
        for k in range(REL_BUCKETS):
            acc = jnp.where(bkt == k, rb_ref[k, h], acc)
        o_ref[h * ATTN_BLOCK:(h + 1) * ATTN_BLOCK, :] = jnp.where(in_window, acc, NEG_INF)


def attention_bias(rel_bias):
    n_heads = rel_bias.shape[1]
    q_off = jnp.arange(ATTN_BLOCK, dtype=I32)
    c_off = jnp.arange(3 * ATTN_BLOCK, dtype=I32)
    bucket = _t5_bucket(c_off[None, :] - ATTN_BLOCK - q_off[:, None])
    return pl.pallas_call(
        functools.partial(_bias_kernel, n_heads=n_heads),
        in_specs=[pl.BlockSpec(memory_space=pltpu.SMEM),
                  pl.BlockSpec(memory_space=pltpu.VMEM)],
        out_specs=pl.BlockSpec(memory_space=pltpu.VMEM),
        out_shape=jax.ShapeDtypeStruct((n_heads * ATTN_BLOCK, 3 * ATTN_BLOCK), F32),
        name="attention_bias",
    )(rel_bias.astype(F32), bucket)


def _attn_kernel(sink_ref, q_ref, kp_ref, kc_ref, kn_ref, vp_ref, vc_ref, vn_ref, bias_ref,
                 o_ref, s_ref, p_ref, *, seq, n_kv, qb):
    blk = ATTN_BLOCK
    col = lax.broadcasted_iota(I32, (1, 3 * blk), 1)
    scale = HEAD_DIM ** -0.5
    rc = ATTN_ROW_CHUNK

    def band(p_ref, c_ref, n_ref, ksl, j):
        parts = []
        for i in (j - 1, j, j + 1):
            if i < 0:
                parts.append(p_ref[0, :, ksl])
            elif i >= qb:
                parts.append(n_ref[0, :, ksl])
            else:
                parts.append(c_ref[0, i * blk:(i + 1) * blk, ksl])
        return jnp.concatenate(parts, axis=0).astype(BF16)

    for j in range(qb):
        kpos = (pl.program_id(1) * qb + j - 1) * blk + col
        in_seq = (kpos >= 0) & (kpos < seq)
        rows = slice(j * blk, (j + 1) * blk)
        for kh in range(n_kv):
            ksl = slice(kh * HEAD_DIM, (kh + 1) * HEAD_DIM)
            kband = band(kp_ref, kc_ref, kn_ref, ksl, j)
            vband = band(vp_ref, vc_ref, vn_ref, ksl, j)
            heads = [kh * KV_GROUP + g for g in range(KV_GROUP)]
            qs = jnp.concatenate([q_ref[0, rows, h * HEAD_DIM:(h + 1) * HEAD_DIM] for h in heads],
                                 axis=0).astype(BF16)
            s_ref[...] = lax.dot_general(qs, kband, (((1,), (1,)), ((), ())),
                                         preferred_element_type=F32)

            def softmax_rows(c, carry, kh=kh, in_seq=in_seq):
                r0 = pl.multiple_of(c * rc, rc)
                bias = bias_ref[pl.ds(kh * KV_GROUP * blk + r0, rc), :]
                s = jnp.where(in_seq, s_ref[pl.ds(r0, rc), :] * scale + bias, NEG_INF)
                sink = sink_ref[kh * KV_GROUP + c // (blk // rc)]
                m = jnp.maximum(jnp.max(s, axis=-1, keepdims=True), sink)
                p = jnp.exp(s - m)
                denom = jnp.sum(p, axis=-1, keepdims=True) + jnp.exp(sink - m)
                p_ref[pl.ds(r0, rc), :] = (p / denom).astype(BF16)
                return carry

            lax.fori_loop(0, KV_GROUP * blk // rc, softmax_rows, 0, unroll=True)
            o = jnp.dot(p_ref[...], vband, preferred_element_type=F32)
            for g, h in enumerate(heads):
                o_ref[0, rows, h * HEAD_DIM:(h + 1) * HEAD_DIM] = (
                    o[g * blk:(g + 1) * blk].astype(o_ref.dtype))


def windowed_attention(qkv, sink, bias, *, attn_w, kv_w, qb=4):
    b, s, _ = qkv.shape
    blk = ATTN_BLOCK
    nblk = s // blk
    qb = _tile(nblk, qb)
    n_kv = kv_w // HEAD_DIM
    kcol = attn_w // kv_w
    assert attn_w % kv_w == 0 and n_kv * KV_GROUP * HEAD_DIM == attn_w

    def edge_spec(col, first):
        return pl.BlockSpec(
            (1, blk, kv_w),
            lambda bi, n: (bi, jnp.clip(n * qb + (-1 if first else qb), 0, nblk - 1), col))

    def mid_spec(col):
        return pl.BlockSpec((1, qb * blk, kv_w), lambda bi, n: (bi, n, col))

    return pl.pallas_call(
        functools.partial(_attn_kernel, seq=s, n_kv=n_kv, qb=qb),
        grid=(b, nblk // qb),
        in_specs=[
            pl.BlockSpec(memory_space=pltpu.SMEM),
            pl.BlockSpec((1, qb * blk, attn_w), lambda bi, n: (bi, n, 0)),
            edge_spec(kcol, True), mid_spec(kcol), edge_spec(kcol, False),
            edge_spec(kcol + 1, True), mid_spec(kcol + 1), edge_spec(kcol + 1, False),
            pl.BlockSpec(bias.shape, lambda bi, n: (0, 0)),
        ],
        out_specs=pl.BlockSpec((1, qb * blk, attn_w), lambda bi, n: (bi, n, 0)),
        out_shape=jax.ShapeDtypeStruct((b, s, attn_w), BF16),
        scratch_shapes=[pltpu.VMEM((KV_GROUP * blk, 3 * blk), F32),
                        pltpu.VMEM((KV_GROUP * blk, 3 * blk), BF16)],
        compiler_params=_params(("parallel", "parallel")),
        name="windowed_attention",
    )(sink.astype(F32), qkv, qkv, qkv, qkv, qkv, qkv, qkv, bias)


def _s5_operands(a_re, a_im, log_dt, b_re, b_im, c_re, c_im, chunk, n_levels):
    ell = chunk
    p = a_re.shape[-1]
    dt = jnp.exp(log_dt)[..., None]
    mag = jnp.exp(a_re * dt)
    lb_re = mag * jnp.cos(a_im * dt)
    lb_im = mag * jnp.sin(a_im * dt)
    den = a_re * a_re + a_im * a_im
    nr = lb_re - 1.0
    coef_re = (nr * a_re + lb_im * a_im) / den
    coef_im = (lb_im * a_re - nr * a_im) / den
    bb_re = coef_re[..., None] * b_re - coef_im[..., None] * b_im
    bb_im = coef_re[..., None] * b_im + coef_im[..., None] * b_re
    bt_re, bt_im = bb_re.transpose(0, 1, 3, 2), bb_im.transpose(0, 1, 3, 2)

    def powers(tau):
        t = tau.astype(F32)[None, None, :, None]
        pm = jnp.exp((a_re * dt)[:, :, None, :] * t)
        ang = (a_im * dt)[:, :, None, :] * t
        return pm * jnp.cos(ang), pm * jnp.sin(ang)

    g = a_re.shape[1]
    assert 2 * p == LANES and g % 2 == 0
    odd = (jnp.arange(g) % 2 == 1)[:, None, None]
    cat = lambda *ts: jnp.concatenate(ts, axis=-1)
    lp = lambda t: jnp.pad(t, [(0, 0)] * (t.ndim - 1) + [(0, LANES - p)])
    sp = lambda t: jnp.where(odd, cat(jnp.zeros_like(t), t), cat(t, jnp.zeros_like(t)))
    ii = jnp.arange(ell)
    dn_re, dn_im = powers(ell - 1 - ii)
    up_re, up_im = powers(ii)
    e1_re, e1_im = powers(ii + 1)
    e2_re, e2_im = powers(ell - ii)
    lv_re, lv_im = powers(ell * (2 ** jnp.arange(n_levels)))
    lev = SUBLANES * pl.cdiv(n_levels, SUBLANES)
    lvp = lambda t: jnp.pad(sp(t), ((0, 0), (0, lev - n_levels), (0, 0)))
    wide = jnp.concatenate([
        cat(sp(dn_re[0]), sp(dn_re[0]), sp(up_re[1]), sp(up_re[1])),
        cat(sp(dn_im[0]), sp(dn_im[0]), sp(up_im[1]), sp(up_im[1])),
        cat(sp(e1_re[0]), sp(e1_re[0]), sp(e2_re[1]), sp(e2_re[1])),
        cat(sp(e1_im[0]), sp(e1_im[0]), sp(e2_im[1]), sp(e2_im[1])),
        cat(sp(bt_re[0]), sp(bt_im[0]), sp(bt_re[1]), sp(bt_im[1])),
        cat(-sp(bt_im[0]), sp(bt_re[0]), -sp(bt_im[1]), sp(bt_re[1])),
        cat(sp(c_re[0]), -sp(c_im[0]), sp(c_re[1]), -sp(c_im[1])),
        cat(-sp(c_im[0]), -sp(c_re[0]), -sp(c_im[1]), -sp(c_re[1])),
        cat(lvp(lv_re[0]), lvp(lv_re[0]), lvp(lv_re[1]), lvp(lv_re[1])),
        cat(-lvp(lv_im[0]), lvp(lv_im[0]), -lvp(lv_im[1]), lvp(lv_im[1])),
    ], axis=1)

    def lag_rows(fwd, bwd):
        f = jnp.pad(lp(fwd), ((0, 0), (ell - 1, 1), (0, 0)))
        b = jnp.pad(lp(bwd), ((0, 0), (0, ell), (0, 0)))
        return cat(f, b)

    narrow = jnp.concatenate([
        lag_rows(up_re[0], dn_re[1]), lag_rows(up_im[0], dn_im[1]),
        cat(lp(c_re[0]), lp(c_re[1])), cat(lp(c_im[0]), lp(c_im[1])),
        cat(lp(bt_re[0]), lp(bt_re[1])), cat(lp(bt_im[0]), lp(bt_im[1])),
    ], axis=1)
    return wide, narrow


def _s5_pack_kernel(u_ref, o_ref, t_ref, *, n_lt, h, chunk):
    def transpose_block(lt, c):
        src = pl.multiple_of(lt * LANES, LANES)
        dst = pl.multiple_of(lt * S5_PITCH, SUBLANES)
        t_ref[pl.ds(dst, LANES), :] = u_ref[pl.ds(src, LANES), :].T
        return c

    lax.fori_loop(0, n_lt, transpose_block, 0, unroll=8)
    n_sub = LANES // chunk

    def pack_group(g8, c):
        for kt in range(h // n_sub):
            ms = [t_ref[pl.ds(g8 * h + kt * n_sub + kk, n_lt, stride=S5_PITCH), :] for kk in range(n_sub)]
            for c_lo in range(n_sub):
                tile = jnp.concatenate([m[:, c_lo * chunk:(c_lo + 1) * chunk] for m in ms], axis=1)
                o_ref[g8, c_lo * n_lt:(c_lo + 1) * n_lt, kt * LANES:(kt + 1) * LANES] = tile.astype(BF16)
        return c

    lax.fori_loop(0, LANES // h, pack_group, 0)


def _s5_unpack_kernel(y_ref, o_ref, t_ref, *, n_lt, h, chunk):
    n_sub = LANES // chunk

    def unpack_group(g8, c):
        for hh in range(h):
            lane0 = (hh // n_sub) * LANES + (hh % n_sub) * chunk
            m = jnp.concatenate([y_ref[g8, c_lo * n_lt:(c_lo + 1) * n_lt, lane0:lane0 + chunk]
                                 for c_lo in range(n_sub)], axis=1)
            t_ref[pl.ds(g8 * h + hh, n_lt, stride=S5_PITCH), :] = m
        return c

    lax.fori_loop(0, LANES // h, unpack_group, 0)

    def transpose_block(lt, c):
        src = pl.multiple_of(lt * S5_PITCH, SUBLANES)
        dst = pl.multiple_of(lt * LANES, LANES)
        o_ref[pl.ds(dst, LANES), :] = t_ref[pl.ds(src, LANES), :].T
        return c

    lax.fori_loop(0, n_lt, transpose_block, 0, unroll=8)


def _shift_rows(z, r, seg):
    if r == 0:
        return z
    rows = z.shape[0]
    rolled = pltpu.roll(z, r % rows, 0)
    assert seg & (seg - 1) == 0
    pos = lax.broadcasted_iota(I32, z.shape, 0) & (seg - 1)
    keep = (pos >= r) if r > 0 else (pos < seg + r)
    return jnp.where(keep, rolled, 0.0)


def _shift_chunks(blocks, d, seg):
    n_sub = len(blocks)
    out = []
    for c_lo in range(n_sub):
        e, s_lo = divmod(c_lo - d, n_sub)
        out.append(_shift_rows(blocks[s_lo], -e, seg))
    return out


def _s5_group_tables(w, nr, perm, *, n_levels, chunk, h):
    ell, lh = chunk, chunk * h
    lev = SUBLANES * pl.cdiv(n_levels, SUBLANES)
    nt = (((1,), (1,)), ((), ()))
    pwf_re, pwf_im, pwe_re, pwe_im = (w[k * ell:(k + 1) * ell] for k in range(4))
    o = 4 * ell
    bb1, bb2, cc1, cc2 = (w[o + k * h:o + (k + 1) * h] for k in range(4))
    o += 4 * h
    pa_all, pb_all = w[o:o + lev], w[o + lev:o + 2 * lev]
    f_tab = jnp.concatenate([pwf_re * bb1[k:k + 1, :] + pwf_im * bb2[k:k + 1, :] for k in range(h)],
                            axis=0).astype(BF16)
    et_tab = jnp.concatenate([pwe_re * cc1[k:k + 1, :] + pwe_im * cc2[k:k + 1, :] for k in range(h)],
                             axis=0).astype(BF16)
    pwm_re, pwm_im = nr[:2 * ell], nr[2 * ell:4 * ell]
    o = 4 * ell
    cr, ci, br, bi = (nr[o + k * h:o + (k + 1) * h] for k in range(4))
    rep = lambda x: jnp.concatenate([jnp.broadcast_to(x[j:j + 1, :], (h, x.shape[1]))
                                     for j in range(x.shape[0])], axis=0)
    tile = lambda t: jnp.concatenate([t] * (2 * ell), axis=0)
    pr, pi = rep(pwm_re), rep(pwm_im)
    crt, cit = tile(cr), tile(ci)
    a = jnp.concatenate([pr * crt - pi * cit, pr * cit + pi * crt], axis=1)
    b = jnp.concatenate([br, -bi], axis=1)
    a1 = a.astype(BF16)
    a2 = (a - a1.astype(F32)).astype(BF16)
    b1 = b.astype(BF16)
    b2 = (b - b1.astype(F32)).astype(BF16)
    lead = lax.dot_general(jnp.concatenate([b1, b2], axis=0), a1, nt, preferred_element_type=F32)
    strip = lead[:h] + lead[h:] + lax.dot_general(b1, a2, nt, preferred_element_type=F32)
    t_jk = jnp.concatenate([strip[:, (ell - 1 - j) * h:(ell - 1 - j) * h + lh] for j in range(ell)],
                           axis=0).astype(BF16)
    t_tab = lax.dot_general(jnp.dot(perm, t_jk, preferred_element_type=F32).astype(BF16), perm, nt,
                            preferred_element_type=F32).astype(BF16)
    return f_tab, et_tab, t_tab, pa_all, pb_all


def _s5_kernel(u_ref, wide_ref, narrow_ref, perm_ref, o_ref, *, seg, n_levels, chunk, h):
    n_sub = LANES // chunk
    nt = (((1,), (1,)), ((), ()))
    perm = perm_ref[...]
    tabs = [_s5_group_tables(wide_ref[gi], narrow_ref[gi], perm, n_levels=n_levels, chunk=chunk, h=h)
            for gi in range(2)]
    us = [u_ref[gi] for gi in range(2)]
    z = sum(jnp.dot(u, t[0], preferred_element_type=F32) for u, t in zip(us, tabs))
    pa_all = tabs[0][3] + tabs[1][3]
    pb_all = tabs[0][4] + tabs[1][4]
    n_lt = z.shape[0] // n_sub
    half = 2 * LANES

    def swap(x):
        return jnp.concatenate([x[:, LANES:], x[:, :LANES]], axis=1)

    zf = [z[c * n_lt:(c + 1) * n_lt, :half] for c in range(n_sub)]
    zb = [z[c * n_lt:(c + 1) * n_lt, half:] for c in range(n_sub)]
    for k in range(n_levels):
        pa = pa_all[k:k + 1, :]
        pb = pb_all[k:k + 1, :]
        sf = _shift_chunks(zf, 2 ** k, seg)
        sb = _shift_chunks(zb, -(2 ** k), seg)
        zf = [a + pa[:, :half] * s + pb[:, :half] * swap(s) for a, s in zip(zf, sf)]
        zb = [a + pa[:, half:] * s + pb[:, half:] * swap(s) for a, s in zip(zb, sb)]
    hin = jnp.concatenate([jnp.concatenate(_shift_chunks(zf, 1, seg), axis=0),
                           jnp.concatenate(_shift_chunks(zb, -1, seg), axis=0)], axis=1).astype(BF16)
    for gi in range(2):
        y = jnp.dot(us[gi], tabs[gi][2], preferred_element_type=F32)
        o_ref[gi] = y + lax.dot_general(hin, tabs[gi][1], nt, preferred_element_type=F32)


def s5_scan(u, operands, *, seq, chunk, n_levels):
    wide, narrow = operands
    t, w = u.shape
    g = wide.shape[0]
    h = w // g
    lh = chunk * h
    n_lt = t // LANES
    rows = t // chunk
    gpt = LANES // h
    assert t % LANES == 0 and seq % LANES == 0 and LANES % chunk == 0 and h % (LANES // chunk) == 0
    scratch = pltpu.VMEM((n_lt * S5_PITCH, LANES), F32)
    ug = pl.pallas_call(
        functools.partial(_s5_pack_kernel, n_lt=n_lt, h=h, chunk=chunk),
        grid=(w // LANES,),
        in_specs=[pl.BlockSpec((t, LANES), lambda q: (0, q))],
        out_specs=pl.BlockSpec((gpt, rows, lh), lambda q: (q, 0, 0)),
        out_shape=jax.ShapeDtypeStruct((g, rows, lh), BF16),
        scratch_shapes=[scratch],
        compiler_params=_params(("parallel",)),
        name="s5_pack",
    )(u)
    kj = jnp.arange(lh)
    perm = (kj[:, None] // chunk + (kj[:, None] % chunk) * h == kj[None, :]).astype(BF16)
    yg = pl.pallas_call(
        functools.partial(_s5_kernel, seg=seq // LANES, n_levels=n_levels, chunk=chunk, h=h),
        grid=(g // 2,),
        in_specs=[
            pl.BlockSpec((2, rows, lh), lambda i: (i, 0, 0)),
            pl.BlockSpec((2,) + wide.shape[1:], lambda i: (i, 0, 0)),
            pl.BlockSpec((2,) + narrow.shape[1:], lambda i: (i, 0, 0)),
            pl.BlockSpec((lh, lh), lambda i: (0, 0)),
        ],
        out_specs=pl.BlockSpec((2, rows, lh), lambda i: (i, 0, 0)),
        out_shape=jax.ShapeDtypeStruct((g, rows, lh), F32),
        compiler_params=_params(("parallel",)),
        name="s5_scan",
    )(ug, wide, narrow, perm)
    return pl.pallas_call(
        functools.partial(_s5_unpack_kernel, n_lt=n_lt, h=h, chunk=chunk),
        grid=(w // LANES,),
        in_specs=[pl.BlockSpec((gpt, rows, lh), lambda q: (q, 0, 0))],
        out_specs=pl.BlockSpec((t, LANES), lambda q: (0, q)),
        out_shape=jax.ShapeDtypeStruct((t, w), F32),
        scratch_shapes=[scratch],
        compiler_params=_params(("parallel",)),
        name="s5_unpack",
    )(yg)


def _glu_kernel(y_ref, u_ref, d_ref, w_ref, b_ref, o_ref):
    y = y_ref[...] + d_ref[...] * u_ref[...]
    g = _gelu(y)
    z = jnp.dot(g.astype(BF16), w_ref[...], preferred_element_type=F32) + b_ref[...]
    o_ref[...] = (g * jax.nn.sigmoid(z)).astype(o_ref.dtype)


def s5_glu(y, u, d_skip, glu_w_bf16, glu_b, *, tm=512):
    m, w = y.shape
    tm = _tile(m, tm)
    return pl.pallas_call(
        _glu_kernel,
        grid=(m // tm,),
        in_specs=[
            pl.BlockSpec((tm, w), lambda i: (i, 0)),
            pl.BlockSpec((tm, w), lambda i: (i, 0)),
            pl.BlockSpec((1, w), lambda i: (0, 0)),
            pl.BlockSpec((w, w), lambda i: (0, 0)),
            pl.BlockSpec((1, w), lambda i: (0, 0)),
        ],
        out_specs=pl.BlockSpec((tm, w), lambda i: (i, 0)),
        out_shape=jax.ShapeDtypeStruct((m, w), BF16),
        compiler_params=_params(("parallel",)),
        name="s5_glu",
    )(y, u, d_skip.reshape(1, w), glu_w_bf16, glu_b.reshape(1, w))


def _gmlp_kernel(x_ref, s0_ref, s1_ref, ng_ref, w_ref, g_ref, b_ref, ws_ref, bs_ref, o_ref, xn_ref, *,
                 width, n_heads, tm, half, pitch):
    hd = width // n_heads
    ck = GMLP_CHUNK
    x = x_ref[...] + _unslab(s0_ref, s1_ref, tm=tm, half=half, pitch=pitch)
    xn_ref[...] = x
    hn = _rms(x, ng_ref[...]).astype(BF16)
    z = _gelu(jnp.dot(hn, w_ref[...], preferred_element_type=F32))
    u, v = z[:, :width], z[:, width:]
    mu = jnp.mean(v, axis=-1, keepdims=True)
    var = jnp.mean(jnp.square(v - mu), axis=-1, keepdims=True)
    vn = ((v - mu) * lax.rsqrt(var + EPS) * g_ref[...] + b_ref[...]).astype(BF16)
    for c in range(tm // ck):
        rows = slice(c * ck, (c + 1) * ck)
        for h in range(n_heads):
            sl = slice(h * hd, (h + 1) * hd)
            mixed = jnp.dot(ws_ref[h], vn[rows, sl], preferred_element_type=F32) + bs_ref[:, h:h + 1]
            o_ref[rows, sl] = (u[rows, sl] * mixed).astype(o_ref.dtype)


def gmlp_gate(x, moe_slabs, norm_g, w_in_bf16, ln_g, ln_b, w_s_bf16, b_s, *, seq, tm=512):
    m, d = x.shape
    w2 = w_in_bf16.shape[1]
    width = w2 // 2
    n_heads = w_s_bf16.shape[0]
    ck = GMLP_CHUNK
    tm = max(ck, _tile(seq, tm))
    nt = seq // tm
    half = d // 2
    pitch = _slab_pitch(half)
    return pl.pallas_call(
        functools.partial(_gmlp_kernel, width=width, n_heads=n_heads, tm=tm, half=half, pitch=pitch),
        grid=(m // tm,),
        in_specs=[
            pl.BlockSpec((tm, d), lambda i: (i, 0)),
            pl.BlockSpec((1, 1, tm * pitch, LANES), lambda i: (i // nt, 0, i % nt, 0)),
            pl.BlockSpec((1, 1, tm * pitch, LANES), lambda i: (i // nt, 1, i % nt, 0)),
            pl.BlockSpec((1, d), lambda i: (0, 0)),
            pl.BlockSpec((d, w2), lambda i: (0, 0), pipeline_mode=pl.Buffered(1)),
            pl.BlockSpec((1, width), lambda i: (0, 0)),
            pl.BlockSpec((1, width), lambda i: (0, 0)),
            pl.BlockSpec((n_heads, ck, ck), lambda i: (0, 0, 0)),
            pl.BlockSpec((ck, n_heads), lambda i: (0, 0)),
        ],
        out_specs=[pl.BlockSpec((tm, width), lambda i: (i, 0)),
                   pl.BlockSpec((tm, d), lambda i: (i, 0))],
        out_shape=[jax.ShapeDtypeStruct((m, width), BF16),
                   jax.ShapeDtypeStruct((m, d), F32)],
        compiler_params=_params(("parallel",)),
        name="gmlp_gate",
    )(x, moe_slabs, moe_slabs, norm_g.reshape(1, d), w_in_bf16, ln_g.reshape(1, width),
      ln_b.reshape(1, width), w_s_bf16, b_s.T.astype(F32))


def _slab_pitch(half):
    return SUBLANES * pl.cdiv(half // LANES, SUBLANES)


def _route_rows(x, g_ref, rt_ref, aff_ref, slab_ref, *, tm, half, pitch):
    h = _rms(x, g_ref[...])
    nt = (((1,), (1,)), ((), ()))
    e = rt_ref.shape[0]
    h1 = h.astype(BF16)
    h1f = h1.astype(F32)
    h2 = (h - h1f).astype(BF16)
    r1 = rt_ref[...].astype(BF16)
    r2 = (rt_ref[...] - r1.astype(F32)).astype(BF16)
    lead = lax.dot_general(jnp.concatenate([r1, r2], axis=0), h1, nt, preferred_element_type=F32)
    logits = lead[:e] + lead[e:] + lax.dot_general(r1, h2, nt, preferred_element_type=F32)
    mx = jnp.max(logits, axis=0, keepdims=True)
    ex = jnp.exp(logits - mx)
    aff_ref[0] = ex / jnp.sum(ex, axis=0, keepdims=True)
    lo = lax.bitcast_convert_type(h1f[:, :half], U32) >> 16
    hi = lax.bitcast_convert_type(h1f[:, half:], U32) & jnp.uint32(0xFFFF0000)
    word = lo | hi
    for s in range(half // LANES):
        slab_ref[pl.ds(s, tm, stride=pitch), :] = word[:, s * LANES:(s + 1) * LANES]
    if pitch > half // LANES:
        for s in range(half // LANES, pitch):
            slab_ref[pl.ds(s, tm, stride=pitch), :] = jnp.zeros((tm, LANES), U32)


def _lane_cumsum(m_bf16, tri):
    e, s = m_bf16.shape
    off = jnp.zeros((e, 1), F32)
    out = []
    for j in range(s // LANES):
        c = jnp.dot(m_bf16[:, j * LANES:(j + 1) * LANES], tri, preferred_element_type=F32) + off
        out.append(c)
        off = c[:, LANES - 1:LANES]
    return jnp.concatenate(out, axis=1)


def _topk_kernel(aff_ref, idx_ref, gate_ref, *, cap, kchunk):
    aff = aff_ref[0]
    e, s = aff.shape
    bits = lax.bitcast_convert_type(aff, I32)

    def count_ge(t):
        return jnp.sum((bits >= t).astype(F32), axis=1, keepdims=True)

    def body(_, carry):
        lo, hi = carry
        mid = lo + ((hi - lo) >> 1)
        ok = count_ge(mid) >= cap
        return jnp.where(ok, mid, lo), jnp.where(ok, hi, mid)

    lo0 = jnp.zeros((e, 1), I32)
    hi0 = jnp.full((e, 1), 0x7F800000, I32)
    thr, _ = lax.fori_loop(0, 32, body, (lo0, hi0))
    gt = bits > thr
    eq = bits == thr
    need = cap - jnp.sum(gt.astype(F32), axis=1, keepdims=True)
    tri = (lax.broadcasted_iota(I32, (LANES, LANES), 0)
           <= lax.broadcasted_iota(I32, (LANES, LANES), 1)).astype(BF16)
    eq_rank = _lane_cumsum(jnp.where(eq, 1.0, 0.0).astype(BF16), tri)
    sel = gt | (eq & (eq_rank <= need))
    pos = _lane_cumsum(jnp.where(sel, 1.0, 0.0).astype(BF16), tri)
    pos = jnp.where(sel, pos, 0.0)

    tok = lax.broadcasted_iota(I32, (1, s), 1)
    t_hi = (tok >> 6).astype(F32)
    t_lo = (tok & 63).astype(F32)
    slot = lax.broadcasted_iota(I32, (cap, kchunk), 0).astype(F32) + 1.0
    lrow = lax.broadcasted_iota(I32, (SUBLANES, s), 0)
    for ei in range(e):
        a = aff[ei:ei + 1, :]
        g1 = a.astype(BF16).astype(F32)
        g2 = (a - g1).astype(BF16).astype(F32)
        g3 = (a - g1) - g2
        lhs = jnp.where(lrow == 0, t_hi, jnp.where(lrow == 1, t_lo, jnp.where(
            lrow == 2, g1, jnp.where(lrow == 3, g2, jnp.where(lrow == 4, g3, 0.0)))))
        lhs = lhs.astype(BF16)
        acc = jnp.zeros((SUBLANES, cap), F32)
        for c0 in range(0, s, kchunk):
            onehot = jnp.where(pos[ei:ei + 1, c0:c0 + kchunk] == slot, 1.0, 0.0).astype(BF16)
            acc = acc + lax.dot_general(lhs[:, c0:c0 + kchunk], onehot, (((1,), (1,)), ((), ())),
                                        preferred_element_type=F32)
        idx_ref[0, ei:ei + 1, :] = (acc[0:1] * 64.0 + acc[1:2]).astype(I32)
        gate_ref[0, ei:ei + 1, :] = acc[2:3] + acc[3:4] + acc[4:5]


def moe_topk(aff, cap):
    b, e, s = aff.shape
    return pl.pallas_call(
        functools.partial(_topk_kernel, cap=cap, kchunk=_tile(s, 1024)),
        grid=(b,),
        in_specs=[pl.BlockSpec((1, e, s), lambda i: (i, 0, 0))],
        out_specs=[pl.BlockSpec((1, e, cap), lambda i: (i, 0, 0)),
                   pl.BlockSpec((1, e, cap), lambda i: (i, 0, 0))],
        out_shape=[jax.ShapeDtypeStruct((b, e, cap), I32),
                   jax.ShapeDtypeStruct((b, e, cap), F32)],
        compiler_params=_params(("parallel",)),
        name="moe_topk",
    )(aff)


def _gather_kernel(idx_ref, slab_ref, o_ref, rows_ref, *, cap, half, pitch, unroll):
    def body(j0, c):
        for u in range(unroll):
            j = j0 * unroll + u
            src = pl.multiple_of(idx_ref[0, 0, j] * pitch, pitch)
            rows_ref[pl.ds(pl.multiple_of(j * pitch, pitch), pitch), :] = slab_ref[pl.ds(src, pitch), :]
        return c

    lax.fori_loop(0, cap // unroll, body, 0)
    for s in range(half // LANES):
        word = rows_ref[pl.ds(s, cap, stride=pitch), :]
        lo = lax.bitcast_convert_type(word << 16, F32)
        hi = lax.bitcast_convert_type(word & jnp.uint32(0xFFFF0000), F32)
        o_ref[0, :, s * LANES:(s + 1) * LANES] = lo.astype(BF16)
        o_ref[0, :, half + s * LANES:half + (s + 1) * LANES] = hi.astype(BF16)


def moe_gather(idx, slabs, *, seq, d):
    b, e, cap = idx.shape
    half = d // 2
    pitch = _slab_pitch(half)
    return pl.pallas_call(
        functools.partial(_gather_kernel, cap=cap, half=half, pitch=pitch, unroll=_tile(cap, 32)),
        grid=(b, e),
        in_specs=[
            pl.BlockSpec((1, 1, cap), lambda bi, ei: (bi * e + ei, 0, 0), memory_space=pltpu.SMEM),
            pl.BlockSpec((seq * pitch, LANES), lambda bi, ei: (bi, 0)),
        ],
        out_specs=pl.BlockSpec((1, cap, d), lambda bi, ei: (ei, bi, 0)),
        out_shape=jax.ShapeDtypeStruct((e, b * cap, d), BF16),
        scratch_shapes=[pltpu.VMEM((cap * pitch, LANES), U32)],
        compiler_params=_params(("parallel", "arbitrary")),
        name="moe_gather",
    )(idx.reshape(b * e, 1, cap), slabs)


def _ffn_kernel(xs_ref, w1_ref, w3_ref, w2_ref, gate_ref, o_ref, hid_ref, *, nt, tf):
    step = pl.program_id(1)

    @pl.when(step < nt)
    def _():
        xs = xs_ref[0]
        a = jnp.dot(xs, w1_ref[0, 0].astype(BF16), preferred_element_type=F32)
        g = jnp.dot(xs, w3_ref[0, 0].astype(BF16), preferred_element_type=F32)
        hid_ref[step] = (a * jax.nn.sigmoid(a) * g).astype(BF16)

    @pl.when(step >= nt)
    def _():
        hid = jnp.concatenate([hid_ref[k] for k in range(nt)], axis=1)
        o_ref[0] = jnp.dot(hid, w2_ref[0, 0].astype(BF16), preferred_element_type=F32) * gate_ref[0]


def moe_ffn(xs, gate, w1, w3, w2, layer, *, tf=256, tn=512):
    e, m, d = xs.shape
    f = w1.shape[3]
    tf = _tile(f, tf)
    tn = _tile(d, tn)
    nt = f // tf
    nn = d // tn
    return pl.pallas_call(
        functools.partial(_ffn_kernel, nt=nt, tf=tf),
        grid=(e, nt + nn),
        in_specs=[
            pl.BlockSpec((1, m, d), lambda ei, t: (ei, 0, 0)),
            pl.BlockSpec((1, 1, d, tf), lambda ei, t: (layer, ei, 0, jnp.minimum(t, nt - 1))),
            pl.BlockSpec((1, 1, d, tf), lambda ei, t: (layer, ei, 0, jnp.minimum(t, nt - 1))),
            pl.BlockSpec((1, 1, f, tn), lambda ei, t: (layer, ei, 0, jnp.maximum(t - nt, 0))),
            pl.BlockSpec((1, m, 1), lambda ei, t: (ei, 0, 0)),
        ],
        out_specs=pl.BlockSpec((1, m, tn), lambda ei, t: (ei, 0, jnp.maximum(t - nt, 0))),
        out_shape=jax.ShapeDtypeStruct((e, m, d), F32),
        scratch_shapes=[pltpu.VMEM((nt, m, tf), BF16)],
        compiler_params=_params(("parallel", "arbitrary")),
        name="moe_ffn",
    )(xs, w1, w3, w2, gate)


def _combine_kernel(idx_ref, ys_ref, o_ref, rows_ref, *, cap, half, pitch, unroll):
    ei = pl.program_id(2)

    @pl.when(ei == 0)
    def _():
        o_ref[...] = jnp.zeros(o_ref.shape, F32)

    for s in range(half // LANES):
        rows_ref[pl.ds(s, cap, stride=pitch), :] = ys_ref[0, :, s * LANES:(s + 1) * LANES]

    def body(j0, c):
        dsts, vals = [], []
        for u in range(unroll):
            j = j0 * unroll + u
            dst = pl.multiple_of(idx_ref[0, 0, j] * pitch, pitch)
            src = pl.multiple_of(j * pitch, pitch)
            vals.append(o_ref[0, 0, pl.ds(dst, pitch), :] + rows_ref[pl.ds(src, pitch), :])
            dsts.append(dst)
        for dst, val in zip(dsts, vals):
            o_ref[0, 0, pl.ds(dst, pitch), :] = val
        return c

    lax.fori_loop(0, cap // unroll, body, 0)


def moe_combine(idx, ys, *, seq, d):
    b, e, cap = idx.shape
    half = d // 2
    pitch = _slab_pitch(half)
    return pl.pallas_call(
        functools.partial(_combine_kernel, cap=cap, half=half, pitch=pitch, unroll=_tile(cap, 16)),
        grid=(b, 2, e),
        in_specs=[
            pl.BlockSpec((1, 1, cap), lambda bi, dh, ei: (bi * e + ei, 0, 0), memory_space=pltpu.SMEM),
            pl.BlockSpec((1, cap, half), lambda bi, dh, ei: (ei, bi, dh)),
        ],
        out_specs=pl.BlockSpec((1, 1, seq * pitch, LANES), lambda bi, dh, ei: (bi, dh, 0, 0)),
        out_shape=jax.ShapeDtypeStruct((b, 2, seq * pitch, LANES), F32),
        scratch_shapes=[pltpu.VMEM((cap * pitch, LANES), F32)],
        compiler_params=_params(("parallel", "parallel", "arbitrary")),
        name="moe_combine",
    )(idx.reshape(b * e, 1, cap), ys)


def _unslab(s0_ref, s1_ref, *, tm, half, pitch):
    parts = []
    for s_ref in (s0_ref, s1_ref):
        for s in range(half // LANES):
            parts.append(s_ref[0, 0, pl.ds(s, tm, stride=pitch), :])
    return jnp.concatenate(parts, axis=1)


def _unslab_kernel(*refs, tm, half, pitch, final_norm):
    x_ref, s0_ref, s1_ref = refs[:3]
    o_ref = refs[-1]
    y = x_ref[0] + _unslab(s0_ref, s1_ref, tm=tm, half=half, pitch=pitch)
    if final_norm:
        y = _rms(y, refs[3][...])
    o_ref[0] = y


def moe_residual(x, moe_slabs, final_g=None, *, tm=256):
    b, s, d = x.shape
    half = d // 2
    pitch = _slab_pitch(half)
    tm = _tile(s, tm)
    final_norm = final_g is not None
    in_specs = [
        pl.BlockSpec((1, tm, d), lambda bi, j: (bi, j, 0)),
        pl.BlockSpec((1, 1, tm * pitch, LANES), lambda bi, j: (bi, 0, j, 0)),
        pl.BlockSpec((1, 1, tm * pitch, LANES), lambda bi, j: (bi, 1, j, 0)),
    ]
    args = [x, moe_slabs, moe_slabs]
    if final_norm:
        in_specs.append(pl.BlockSpec((1, d), lambda bi, j: (0, 0)))
        args.append(final_g.reshape(1, d))
    return pl.pallas_call(
        functools.partial(_unslab_kernel, tm=tm, half=half, pitch=pitch, final_norm=final_norm),
        grid=(b, s // tm),
        in_specs=in_specs,
        out_specs=pl.BlockSpec((1, tm, d), lambda bi, j: (bi, j, 0)),
        out_shape=jax.ShapeDtypeStruct((b, s, d), F32),
        compiler_params=_params(("parallel", "parallel")),
        name="moe_residual",
    )(*args)


def expert_choice_ffn(aff, slabs, w1, w3, w2, layer, *, seq, d):
    cap = CAPACITY_FACTOR * seq // N_EXPERTS
    idx, gate = moe_topk(aff, cap)
    b, e, _ = idx.shape
    xs = moe_gather(idx, slabs, seq=seq, d=d)
    ys = moe_ffn(xs, gate.transpose(1, 0, 2).reshape(e, b * cap, 1), w1, w3, w2, layer)
    return moe_combine(idx, ys, seq=seq, d=d)


def kernel(x, rel_bias, mix_norm, ffn_norm, final_norm, even_w_in, attn_sink, ssm_a_re, ssm_a_im, ssm_log_dt, ssm_b_re, ssm_b_im, ssm_c_re, ssm_c_im, ssm_d, glu_w, glu_b, even_w_out, odd_w_in, sgu_ln_g, sgu_ln_b, sgu_w, sgu_b, odd_w_out, router, moe_w1, moe_w3, moe_w2):
    b, s, d = x.shape
    t = b * s
    depth = mix_norm.shape[0]
    ssm_w = ssm_d.shape[-1]
    attn_w = even_w_out.shape[1] - ssm_w
    kv_w = (even_w_in.shape[-1] - attn_w - ssm_w) // 2
    bias = attention_bias(rel_bias)

    x2 = x.reshape(t, d)
    moe = None
    for layer in range(depth):
        i = layer // 2
        if layer % 2 == 0:
            if moe is not None:
                x2 = moe_residual(x2.reshape(b, s, d), moe).reshape(t, d)
            qkv, u = in_projection(x2, mix_norm[layer], even_w_in[i].astype(BF16), attn_w + 2 * kv_w)
            attn = windowed_attention(qkv.reshape(b, s, -1), attn_sink[i], bias,
                                      attn_w=attn_w, kv_w=kv_w)
            n_levels = max(1, (s // S5_CHUNK - 1).bit_length())
            operands = _s5_operands(ssm_a_re[i], ssm_a_im[i], ssm_log_dt[i], ssm_b_re[i], ssm_b_im[i],
                                    ssm_c_re[i], ssm_c_im[i], S5_CHUNK, n_levels)
            y = s5_scan(u, operands, seq=s, chunk=S5_CHUNK, n_levels=n_levels)
            ssm = s5_glu(y, u, ssm_d[i], glu_w[i].astype(BF16), glu_b[i])
            mixed, w_out = [attn.reshape(t, attn_w), ssm], even_w_out[i]
        else:
            gated, x2 = gmlp_gate(x2, moe, mix_norm[layer], odd_w_in[i].astype(BF16), sgu_ln_g[i],
                                  sgu_ln_b[i], sgu_w[i].astype(BF16), sgu_b[i], seq=s)
            mixed, w_out = [gated], odd_w_out[i]
        x2, aff, slabs = matmul_residual_route(mixed, w_out.astype(BF16), x2, ffn_norm[layer],
                                               router[layer], seq=s)
        moe = expert_choice_ffn(aff, slabs, moe_w1, moe_w3, moe_w2, layer, seq=s, d=d)
    return moe_residual(x2.reshape(b, s, d), moe, final_norm)
```

```python
import functools
import math

import jax
import jax.numpy as jnp
from jax import lax
from jax.experimental import pallas as pl
from jax.experimental.pallas import tpu as pltpu

F32 = jnp.float32
BF16 = jnp.bfloat16
I32 = jnp.int32
U32 = jnp.uint32

EPS = 1e-6
NEG_INF = -1e30

LANES = 128
SUBLANES = 8
VMEM_LIMIT = 56 << 20

ATTN_BLOCK = 128
ATTN_ROW_CHUNK = 32
HEAD_DIM = 128
KV_GROUP = 4
REL_BUCKETS = 32
REL_MAX_DIST = 128
SSM_GROUP = 16
SSM_STATE = 64
S5_CHUNK = 32
S5_PITCH = 136
GMLP_CHUNK = 128
N_EXPERTS = 16
CAPACITY_FACTOR = 2


def _params(sem, vmem=VMEM_LIMIT):
    return pltpu.CompilerParams(dimension_semantics=sem, vmem_limit_bytes=vmem)


def _tile(n, want):
    t = min(n, want)
    while n % t:
        t //= 2
    return t


def _rms(x, g):
    ms = jnp.mean(x * x, axis=-1, keepdims=True)
    return x * lax.rsqrt(ms + EPS) * g


def _gelu(x):
    c = math.sqrt(2.0 / math.pi)
    return x * (0.5 * (1.0 + jnp.tanh(c * (x + 0.044715 * (x * x * x)))))


def _in_proj_kernel(x_ref, g_ref, w_ref, qkv_ref, u_ref):
    hn = _rms(x_ref[...], g_ref[...]).astype(BF16)
    y = jnp.dot(hn, w_ref[...], preferred_element_type=F32)
    split = qkv_ref.shape[1]
    qkv_ref[...] = y[:, :split]
    u_ref[...] = y[:, split:]


def in_projection(x, g, w_bf16, split, *, tm=512):
    m, k = x.shape
    n = w_bf16.shape[1]
    tm = _tile(m, tm)
    return pl.pallas_call(
        _in_proj_kernel,
        grid=(m // tm,),
        in_specs=[
            pl.BlockSpec((tm, k), lambda i: (i, 0)),
            pl.BlockSpec((1, k), lambda i: (0, 0)),
            pl.BlockSpec((k, n), lambda i: (0, 0), pipeline_mode=pl.Buffered(1)),
        ],
        out_specs=[pl.BlockSpec((tm, split), lambda i: (i, 0)),
                   pl.BlockSpec((tm, n - split), lambda i: (i, 0))],
        out_shape=[jax.ShapeDtypeStruct((m, split), F32),
                   jax.ShapeDtypeStruct((m, n - split), F32)],
        compiler_params=_params(("parallel",)),
        name="in_projection",
    )(x, g.reshape(1, k), w_bf16)


def _mm_res_route_kernel(*refs, n_lhs, tm, half, pitch):
    lhs = refs[:n_lhs]
    ws = refs[n_lhs:2 * n_lhs]
    res_ref, g_ref, rt_ref, o_ref, aff_ref, slab_ref = refs[2 * n_lhs:]
    acc = res_ref[...]
    for l_ref, w_ref in zip(lhs, ws):
        acc = acc + jnp.dot(l_ref[...], w_ref[...], preferred_element_type=F32)
    o_ref[...] = acc
    _route_rows(acc, g_ref, rt_ref, aff_ref, slab_ref, tm=tm, half=half, pitch=pitch)


def matmul_residual_route(lhs_list, w_bf16, res, norm_g, router, *, seq, tm=512):
    m, d = res.shape
    kk = lhs_list[0].shape[1]
    n_lhs = len(lhs_list)
    e = router.shape[1]
    assert all(l.shape == (m, kk) for l in lhs_list) and w_bf16.shape == (n_lhs * kk, d)
    tm = _tile(seq, tm)
    nt = seq // tm
    half = d // 2
    pitch = _slab_pitch(half)
    in_specs = [pl.BlockSpec((tm, kk), lambda i: (i, 0)) for _ in lhs_list]
    in_specs += [pl.BlockSpec((kk, d), functools.partial(lambda i, r: (r, 0), r=r),
                              pipeline_mode=pl.Buffered(1)) for r in range(n_lhs)]
    in_specs += [pl.BlockSpec((tm, d), lambda i: (i, 0)),
                 pl.BlockSpec((1, d), lambda i: (0, 0)),
                 pl.BlockSpec((e, d), lambda i: (0, 0))]
    return pl.pallas_call(
        functools.partial(_mm_res_route_kernel, n_lhs=n_lhs, tm=tm, half=half, pitch=pitch),
        grid=(m // tm,),
        in_specs=in_specs,
        out_specs=[pl.BlockSpec((tm, d), lambda i: (i, 0)),
                   pl.BlockSpec((1, e, tm), lambda i: (i // nt, 0, i % nt)),
                   pl.BlockSpec((tm * pitch, LANES), lambda i: (i, 0))],
        out_shape=[jax.ShapeDtypeStruct((m, d), F32),
                   jax.ShapeDtypeStruct((m // seq, e, seq), F32),
                   jax.ShapeDtypeStruct((m * pitch, LANES), U32)],
        compiler_params=_params(("parallel",)),
        name="matmul_residual_route",
    )(*lhs_list, *([w_bf16] * n_lhs), res, norm_g.reshape(1, d), router.T.astype(F32))


def _t5_bucket(rel):
    nb = REL_BUCKETS // 2
    max_exact = nb // 2
    base = jnp.where(rel > 0, nb, 0)
    n = jnp.abs(rel)
    nf = jnp.maximum(n, 1).astype(F32)
    large = max_exact + (jnp.log(nf / max_exact) / math.log(REL_MAX_DIST / max_exact)
                         * (nb - max_exact)).astype(I32)
    large = jnp.minimum(large, nb - 1)
    return base + jnp.where(n < max_exact, n, large)


def _bias_kernel(rb_ref, bucket_ref, o_ref, *, n_heads):
    bkt = bucket_ref[...]
    row = lax.broadcasted_iota(I32, bkt.shape, 0)
    col = lax.broadcasted_iota(I32, bkt.shape, 1)
    in_window = jnp.abs(col - ATTN_BLOCK - row) <= ATTN_BLOCK
    for h in range(n_heads):
        acc = jnp.zeros(bkt.shape, F32)
        for k in range(REL_BUCKETS):
            acc = jnp.where(bkt == k, rb_ref[k, h], acc)
        o_ref[h * ATTN_BLOCK:(h + 1) * ATTN_BLOCK, :] = jnp.where(in_window, acc, NEG_INF)


def attention_bias(rel_bias):
    n_heads = rel_bias.shape[1]
    q_off = jnp.arange(ATTN_BLOCK, dtype=I32)
    c_off = jnp.arange(3 * ATTN_BLOCK, dtype=I32)
    bucket = _t5_bucket(c_off[None, :] - ATTN_BLOCK - q_off[:, None])
    return pl.pallas_call(
        functools.partial(_bias_kernel, n_heads=n_heads),
        in_specs=[pl.BlockSpec(memory_space=pltpu.SMEM),
                  pl.BlockSpec(memory_space=pltpu.VMEM)],
        out_specs=pl.BlockSpec(memory_space=pltpu.VMEM),
        out_shape=jax.ShapeDtypeStruct((n_heads * ATTN_BLOCK, 3 * ATTN_BLOCK), F32),
        name="attention_bias",
    )(rel_bias.astype(F32), bucket)


def _attn_kernel(sink_ref, q_ref, kp_ref, kc_ref, kn_ref, vp_ref, vc_ref, vn_ref, bias_ref,
                 o_ref, s_ref, p_ref, *, seq, n_kv, qb):
    blk = ATTN_BLOCK
    col = lax.broadcasted_iota(I32, (1, 3 * blk), 1)
    scale = HEAD_DIM ** -0.5
    rc = ATTN_ROW_CHUNK

    def band(p_ref, c_ref, n_ref, ksl, j):
        parts = []
        for i in (j - 1, j, j + 1):
            if i < 0:
                parts.append(p_ref[0, :, ksl])
            elif i >= qb:
                parts.append(n_ref[0, :, ksl])
            else:
                parts.append(c_ref[0, i * blk:(i + 1) * blk, ksl])
        return jnp.concatenate(parts, axis=0).astype(BF16)

    for j in range(qb):
        kpos = (pl.program_id(1) * qb + j - 1) * blk + col
        in_seq = (kpos >= 0) & (kpos < seq)
        rows = slice(j * blk, (j + 1) * blk)
        for kh in range(n_kv):
            ksl = slice(kh * HEAD_DIM, (kh + 1) * HEAD_DIM)
            kband = band(kp_ref, kc_ref, kn_ref, ksl, j)
            vband = band(vp_ref, vc_ref, vn_ref, ksl, j)
            heads = [kh * KV_GROUP + g for g in range(KV_GROUP)]
            qs = jnp.concatenate([q_ref[0, rows, h * HEAD_DIM:(h + 1) * HEAD_DIM] for h in heads],
                                 axis=0).astype(BF16)
            s_ref[...] = lax.dot_general(qs, kband, (((1,), (1,)), ((), ())),
                                         preferred_element_type=F32)

            def softmax_rows(c, carry, kh=kh, in_seq=in_seq):
                r0 = pl.multiple_of(c * rc, rc)
                bias = bias_ref[pl.ds(kh * KV_GROUP * blk + r0, rc), :]
                s = jnp.where(in_seq, s_ref[pl.ds(r0, rc), :] * scale + bias, NEG_INF)
                sink = sink_ref[kh * KV_GROUP + c // (blk // rc)]
                m = jnp.maximum(jnp.max(s, axis=-1, keepdims=True), sink)
                p = jnp.exp(s - m)
                denom = jnp.sum(p, axis=-1, keepdims=True) + jnp.exp(sink - m)
                p_ref[pl.ds(r0, rc), :] = (p / denom).astype(BF16)
                return carry

            lax.fori_loop(0, KV_GROUP * blk // rc, softmax_rows, 0, unroll=True)
            o = jnp.dot(p_ref[...], vband, preferred_element_type=F32)
            for g, h in enumerate(heads):
                o_ref[0, rows, h * HEAD_DIM:(h + 1) * HEAD_DIM] = (
                    o[g * blk:(g + 1) * blk].astype(o_ref.dtype))


def windowed_attention(qkv, sink, bias, *, attn_w, kv_w, qb=4):
    b, s, _ = qkv.shape
    blk = ATTN_BLOCK
    nblk = s // blk
    qb = _tile(nblk, qb)
    n_kv = kv_w // HEAD_DIM
    kcol = attn_w // kv_w
    assert attn_w % kv_w == 0 and n_kv * KV_GROUP * HEAD_DIM == attn_w

    def edge_spec(col, first):
        return pl.BlockSpec(
            (1, blk, kv_w),
            lambda bi, n: (bi, jnp.clip(n * qb + (-1 if first else qb), 0, nblk - 1), col))

    def mid_spec(col):
        return pl.BlockSpec((1, qb * blk, kv_w), lambda bi, n: (bi, n, col))

    return pl.pallas_call(
        functools.partial(_attn_kernel, seq=s, n_kv=n_kv, qb=qb),
        grid=(b, nblk // qb),
        in_specs=[
            pl.BlockSpec(memory_space=pltpu.SMEM),
            pl.BlockSpec((1, qb * blk, attn_w), lambda bi, n: (bi, n, 0)),
            edge_spec(kcol, True), mid_spec(kcol), edge_spec(kcol, False),
            edge_spec(kcol + 1, True), mid_spec(kcol + 1), edge_spec(kcol + 1, False),
            pl.BlockSpec(bias.shape, lambda bi, n: (0, 0)),
        ],
        out_specs=pl.BlockSpec((1, qb * blk, attn_w), lambda bi, n: (bi, n, 0)),
        out_shape=jax.ShapeDtypeStruct((b, s, attn_w), BF16),
        scratch_shapes=[pltpu.VMEM((KV_GROUP * blk, 3 * blk), F32),
                        pltpu.VMEM((KV_GROUP * blk, 3 * blk), BF16)],
        compiler_params=_params(("parallel", "parallel")),
        name="windowed_attention",
    )(sink.astype(F32), qkv, qkv, qkv, qkv, qkv, qkv, qkv, bias)


def _s5_operands(a_re, a_im, log_dt, b_re, b_im, c_re, c_im, chunk, n_levels):
    ell = chunk
    p = a_re.shape[-1]
    dt = jnp.exp(log_dt)[..., None]
    mag = jnp.exp(a_re * dt)
    lb_re = mag * jnp.cos(a_im * dt)
    lb_im = mag * jnp.sin(a_im * dt)
    den = a_re * a_re + a_im * a_im
    nr = lb_re - 1.0
    coef_re = (nr * a_re + lb_im * a_im) / den
    coef_im = (lb_im * a_re - nr * a_im) / den
    bb_re = coef_re[..., None] * b_re - coef_im[..., None] * b_im
    bb_im = coef_re[..., None] * b_im + coef_im[..., None] * b_re
    bt_re, bt_im = bb_re.transpose(0, 1, 3, 2), bb_im.transpose(0, 1, 3, 2)

    def powers(tau):
        t = tau.astype(F32)[None, None, :, None]
        pm = jnp.exp((a_re * dt)[:, :, None, :] * t)
        ang = (a_im * dt)[:, :, None, :] * t
        return pm * jnp.cos(ang), pm * jnp.sin(ang)

    g = a_re.shape[1]
    assert 2 * p == LANES and g % 2 == 0
    odd = (jnp.arange(g) % 2 == 1)[:, None, None]
    cat = lambda *ts: jnp.concatenate(ts, axis=-1)
    lp = lambda t: jnp.pad(t, [(0, 0)] * (t.ndim - 1) + [(0, LANES - p)])
    sp = lambda t: jnp.where(odd, cat(jnp.zeros_like(t), t), cat(t, jnp.zeros_like(t)))
    ii = jnp.arange(ell)
    dn_re, dn_im = powers(ell - 1 - ii)
    up_re, up_im = powers(ii)
    e1_re, e1_im = powers(ii + 1)
    e2_re, e2_im = powers(ell - ii)
    lv_re, lv_im = powers(ell * (2 ** jnp.arange(n_levels)))
    lev = SUBLANES * pl.cdiv(n_levels, SUBLANES)
    lvp = lambda t: jnp.pad(sp(t), ((0, 0), (0, lev - n_levels), (0, 0)))
    wide = jnp.concatenate([
        cat(sp(dn_re[0]), sp(dn_re[0]), sp(up_re[1]), sp(up_re[1])),
        cat(sp(dn_im[0]), sp(dn_im[0]), sp(up_im[1]), sp(up_im[1])),
        cat(sp(e1_re[0]), sp(e1_re[0]), sp(e2_re[1]), sp(e2_re[1])),
        cat(sp(e1_im[0]), sp(e1_im[0]), sp(e2_im[1]), sp(e2_im[1])),
        cat(sp(bt_re[0]), sp(bt_im[0]), sp(bt_re[1]), sp(bt_im[1])),
        cat(-sp(bt_im[0]), sp(bt_re[0]), -sp(bt_im[1]), sp(bt_re[1])),
        cat(sp(c_re[0]), -sp(c_im[0]), sp(c_re[1]), -sp(c_im[1])),
        cat(-sp(c_im[0]), -sp(c_re[0]), -sp(c_im[1]), -sp(c_re[1])),
        cat(lvp(lv_re[0]), lvp(lv_re[0]), lvp(lv_re[1]), lvp(lv_re[1])),
        cat(-lvp(lv_im[0]), lvp(lv_im[0]), -lvp(lv_im[1]), lvp(lv_im[1])),
    ], axis=1)

    def lag_rows(fwd, bwd):
        f = jnp.pad(lp(fwd), ((0, 0), (ell - 1, 1), (0, 0)))
        b = jnp.pad(lp(bwd), ((0, 0), (0, ell), (0, 0)))
        return cat(f, b)

    narrow = jnp.concatenate([
        lag_rows(up_re[0], dn_re[1]), lag_rows(up_im[0], dn_im[1]),
        cat(lp(c_re[0]), lp(c_re[1])), cat(lp(c_im[0]), lp(c_im[1])),
        cat(lp(bt_re[0]), lp(bt_re[1])), cat(lp(bt_im[0]), lp(bt_im[1])),
    ], axis=1)
    return wide, narrow


def _s5_pack_kernel(u_ref, o_ref, t_ref, *, n_lt, h, chunk):
    def transpose_block(lt, c):
        src = pl.multiple_of(lt * LANES, LANES)
        dst = pl.multiple_of(lt * S5_PITCH, SUBLANES)
        t_ref[pl.ds(dst, LANES), :] = u_ref[pl.ds(src, LANES), :].T
        return c

    lax.fori_loop(0, n_lt, transpose_block, 0, unroll=8)
    n_sub = LANES // chunk

    def pack_group(g8, c):
        for kt in range(h // n_sub):
            ms = [t_ref[pl.ds(g8 * h + kt * n_sub + kk, n_lt, stride=S5_PITCH), :] for kk in range(n_sub)]
            for c_lo in range(n_sub):
                tile = jnp.concatenate([m[:, c_lo * chunk:(c_lo + 1) * chunk] for m in ms], axis=1)
                o_ref[g8, c_lo * n_lt:(c_lo + 1) * n_lt, kt * LANES:(kt + 1) * LANES] = tile.astype(BF16)
        return c

    lax.fori_loop(0, LANES // h, pack_group, 0)


def _s5_unpack_kernel(y_ref, o_ref, t_ref, *, n_lt, h, chunk):
    n_sub = LANES // chunk

    def unpack_group(g8, c):
        for hh in range(h):
            lane0 = (hh // n_sub) * LANES + (hh % n_sub) * chunk
            m = jnp.concatenate([y_ref[g8, c_lo * n_lt:(c_lo + 1) * n_lt, lane0:lane0 + chunk]
                                 for c_lo in range(n_sub)], axis=1)
            t_ref[pl.ds(g8 * h + hh, n_lt, stride=S5_PITCH), :] = m
        return c

    lax.fori_loop(0, LANES // h, unpack_group, 0)

    def transpose_block(lt, c):
        src = pl.multiple_of(lt * S5_PITCH, SUBLANES)
        dst = pl.multiple_of(lt * LANES, LANES)
        o_ref[pl.ds(dst, LANES), :] = t_ref[pl.ds(src, LANES), :].T
        return c

    lax.fori_loop(0, n_lt, transpose_block, 0, unroll=8)


def _shift_rows(z, r, seg):
    if r == 0:
        return z
    rows = z.shape[0]
    rolled = pltpu.roll(z, r % rows, 0)
    assert seg & (seg - 1) == 0
    pos = lax.broadcasted_iota(I32, z.shape, 0) & (seg - 1)
    keep = (pos >= r) if r > 0 else (pos < seg + r)
    return jnp.where(keep, rolled, 0.0)


def _shift_chunks(blocks, d, seg):
    n_sub = len(blocks)
    out = []
    for c_lo in range(n_sub):
        e, s_lo = divmod(c_lo - d, n_sub)
        out.append(_shift_rows(blocks[s_lo], -e, seg))
    return out


def _s5_group_tables(w, nr, perm, *, n_levels, chunk, h):
    ell, lh = chunk, chunk * h
    lev = SUBLANES * pl.cdiv(n_levels, SUBLANES)
    nt = (((1,), (1,)), ((), ()))
    pwf_re, pwf_im, pwe_re, pwe_im = (w[k * ell:(k + 1) * ell] for k in range(4))
    o = 4 * ell
    bb1, bb2, cc1, cc2 = (w[o + k * h:o + (k + 1) * h] for k in range(4))
    o += 4 * h
    pa_all, pb_all = w[o:o + lev], w[o + lev:o + 2 * lev]
    f_tab = jnp.concatenate([pwf_re * bb1[k:k + 1, :] + pwf_im * bb2[k:k + 1, :] for k in range(h)],
                            axis=0).astype(BF16)
    et_tab = jnp.concatenate([pwe_re * cc1[k:k + 1, :] + pwe_im * cc2[k:k + 1, :] for k in range(h)],
                             axis=0).astype(BF16)
    pwm_re, pwm_im = nr[:2 * ell], nr[2 * ell:4 * ell]
    o = 4 * ell
    cr, ci, br, bi = (nr[o + k * h:o + (k + 1) * h] for k in range(4))
    rep = lambda x: jnp.concatenate([jnp.broadcast_to(x[j:j + 1, :], (h, x.shape[1]))
                                     for j in range(x.shape[0])], axis=0)
    tile = lambda t: jnp.concatenate([t] * (2 * ell), axis=0)
    pr, pi = rep(pwm_re), rep(pwm_im)
    crt, cit = tile(cr), tile(ci)
    a = jnp.concatenate([pr * crt - pi * cit, pr * cit + pi * crt], axis=1)
    b = jnp.concatenate([br, -bi], axis=1)
    a1 = a.astype(BF16)
    a2 = (a - a1.astype(F32)).astype(BF16)
    b1 = b.astype(BF16)
    b2 = (b - b1.astype(F32)).astype(BF16)
    lead = lax.dot_general(jnp.concatenate([b1, b2], axis=0), a1, nt, preferred_element_type=F32)
    strip = lead[:h] + lead[h:] + lax.dot_general(b1, a2, nt, preferred_element_type=F32)
    t_jk = jnp.concatenate([strip[:, (ell - 1 - j) * h:(ell - 1 - j) * h + lh] for j in range(ell)],
                           axis=0).astype(BF16)
    t_tab = lax.dot_general(jnp.dot(perm, t_jk, preferred_element_type=F32).astype(BF16), perm, nt,
                            preferred_element_type=F32).astype(BF16)
    return f_tab, et_tab, t_tab, pa_all, pb_all


def _s5_kernel(u_ref, wide_ref, narrow_ref, perm_ref, o_ref, *, seg, n_levels, chunk, h):
    n_sub = LANES // chunk
    nt = (((1,), (1,)), ((), ()))
    perm = perm_ref[...]
    tabs = [_s5_group_tables(wide_ref[gi], narrow_ref[gi], perm, n_levels=n_levels, chunk=chunk, h=h)
            for gi in range(2)]
    us = [u_ref[gi] for gi in range(2)]
    z = sum(jnp.dot(u, t[0], preferred_element_type=F32) for u, t in zip(us, tabs))
    pa_all = tabs[0][3] + tabs[1][3]
    pb_all = tabs[0][4] + tabs[1][4]
    n_lt = z.shape[0] // n_sub
    half = 2 * LANES

    def swap(x):
        return jnp.concatenate([x[:, LANES:], x[:, :LANES]], axis=1)

    zf = [z[c * n_lt:(c + 1) * n_lt, :half] for c in range(n_sub)]
    zb = [z[c * n_lt:(c + 1) * n_lt, half:] for c in range(n_sub)]
    for k in range(n_levels):
        pa = pa_all[k:k + 1, :]
        pb = pb_all[k:k + 1, :]
        sf = _shift_chunks(zf, 2 ** k, seg)
        sb = _shift_chunks(zb, -(2 ** k), seg)
        zf = [a + pa[:, :half] * s + pb[:, :half] * swap(s) for a, s in zip(zf, sf)]
        zb = [a + pa[:, half:] * s + pb[:, half:] * swap(s) for a, s in zip(zb, sb)]
    hin = jnp.concatenate([jnp.concatenate(_shift_chunks(zf, 1, seg), axis=0),
                           jnp.concatenate(_shift_chunks(zb, -1, seg), axis=0)], axis=1).astype(BF16)
    for gi in range(2):
        y = jnp.dot(us[gi], tabs[gi][2], preferred_element_type=F32)
        o_ref[gi] = y + lax.dot_general(hin, tabs[gi][1], nt, preferred_element_type=F32)


def s5_scan(u, operands, *, seq, chunk, n_levels):
    wide, narrow = operands
    t, w = u.shape
    g = wide.shape[0]
    h = w // g
    lh = chunk * h
    n_lt = t // LANES
    rows = t // chunk
    gpt = LANES // h
    assert t % LANES == 0 and seq % LANES == 0 and LANES % chunk == 0 and h % (LANES // chunk) == 0
    scratch = pltpu.VMEM((n_lt * S5_PITCH, LANES), F32)
    ug = pl.pallas_call(
        functools.partial(_s5_pack_kernel, n_lt=n_lt, h=h, chunk=chunk),
        grid=(w // LANES,),
        in_specs=[pl.BlockSpec((t, LANES), lambda q: (0, q))],
        out_specs=pl.BlockSpec((gpt, rows, lh), lambda q: (q, 0, 0)),
        out_shape=jax.ShapeDtypeStruct((g, rows, lh), BF16),
        scratch_shapes=[scratch],
        compiler_params=_params(("parallel",)),
        name="s5_pack",
    )(u)
    kj = jnp.arange(lh)
    perm = (kj[:, None] // chunk + (kj[:, None] % chunk) * h == kj[None, :]).astype(BF16)
    yg = pl.pallas_call(
        functools.partial(_s5_kernel, seg=seq // LANES, n_levels=n_levels, chunk=chunk, h=h),
        grid=(g // 2,),
        in_specs=[
            pl.BlockSpec((2, rows, lh), lambda i: (i, 0, 0)),
            pl.BlockSpec((2,) + wide.shape[1:], lambda i: (i, 0, 0)),
            pl.BlockSpec((2,) + narrow.shape[1:], lambda i: (i, 0, 0)),
            pl.BlockSpec((lh, lh), lambda i: (0, 0)),
        ],
        out_specs=pl.BlockSpec((2, rows, lh), lambda i: (i, 0, 0)),
        out_shape=jax.ShapeDtypeStruct((g, rows, lh), F32),
        compiler_params=_params(("parallel",)),
        name="s5_scan",
    )(ug, wide, narrow, perm)
    return pl.pallas_call(
        functools.partial(_s5_unpack_kernel, n_lt=n_lt, h=h, chunk=chunk),
        grid=(w // LANES,),
        in_specs=[pl.BlockSpec((gpt, rows, lh), lambda q: (q, 0, 0))],
        out_specs=pl.BlockSpec((t, LANES), lambda q: (0, q)),
        out_shape=jax.ShapeDtypeStruct((t, w), F32),
        scratch_shapes=[scratch],
        compiler_params=_params(("parallel",)),
        name="s5_unpack",
    )(yg)


def _glu_kernel(y_ref, u_ref, d_ref, w_ref, b_ref, o_ref):
    y = y_ref[...] + d_ref[...] * u_ref[...]
    g = _gelu(y)
    z = jnp.dot(g.astype(BF16), w_ref[...], preferred_element_type=F32) + b_ref[...]
    o_ref[...] = (g * jax.nn.sigmoid(z)).astype(o_ref.dtype)


def s5_glu(y, u, d_skip, glu_w_bf16, glu_b, *, tm=512):
    m, w = y.shape
    tm = _tile(m, tm)
    return pl.pallas_call(
        _glu_kernel,
        grid=(m // tm,),
        in_specs=[
            pl.BlockSpec((tm, w), lambda i: (i, 0)),
            pl.BlockSpec((tm, w), lambda i: (i, 0)),
            pl.BlockSpec((1, w), lambda i: (0, 0)),
            pl.BlockSpec((w, w), lambda i: (0, 0)),
            pl.BlockSpec((1, w), lambda i: (0, 0)),
        ],
        out_specs=pl.BlockSpec((tm, w), lambda i: (i, 0)),
        out_shape=jax.ShapeDtypeStruct((m, w), BF16),
        compiler_params=_params(("parallel",)),
        name="s5_glu",
    )(y, u, d_skip.reshape(1, w), glu_w_bf16, glu_b.reshape(1, w))


def _gmlp_kernel(x_ref, s0_ref, s1_ref, ng_ref, w_ref, g_ref, b_ref, ws_ref, bs_ref, o_ref, xn_ref, *,
                 width, n_heads, tm, half, pitch):
    hd = width // n_heads
    ck = GMLP_CHUNK
    x = x_ref[...] + _unslab(s0_ref, s1_ref, tm=tm, half=half, pitch=pitch)
    xn_ref[...] = x
    hn = _rms(x, ng_ref[...]).astype(BF16)
    z = _gelu(jnp.dot(hn, w_ref[...], preferred_element_type=F32))
    u, v = z[:, :width], z[:, width:]
    mu = jnp.mean(v, axis=-1, keepdims=True)
    var = jnp.mean(jnp.square(v - mu), axis=-1, keepdims=True)
    vn = ((v - mu) * lax.rsqrt(var + EPS) * g_ref[...] + b_ref[...]).astype(BF16)
    for c in range(tm // ck):
        rows = slice(c * ck, (c + 1) * ck)
        for h in range(n_heads):
            sl = slice(h * hd, (h + 1) * hd)
            mixed = jnp.dot(ws_ref[h], vn[rows, sl], preferred_element_type=F32) + bs_ref[:, h:h + 1]
            o_ref[rows, sl] = (u[rows, sl] * mixed).astype(o_ref.dtype)


def gmlp_gate(x, moe_slabs, norm_g, w_in_bf16, ln_g, ln_b, w_s_bf16, b_s, *, seq, tm=512):
    m, d = x.shape
    w2 = w_in_bf16.shape[1]
    width = w2 // 2
    n_heads = w_s_bf16.shape[0]
    ck = GMLP_CHUNK
    tm = max(ck, _tile(seq, tm))
    nt = seq // tm
    half = d // 2
    pitch = _slab_pitch(half)
    return pl.pallas_call(
        functools.partial(_gmlp_kernel, width=width, n_heads=n_heads, tm=tm, half=half, pitch=pitch),
        grid=(m // tm,),
        in_specs=[
            pl.BlockSpec((tm, d), lambda i: (i, 0)),
            pl.BlockSpec((1, 1, tm * pitch, LANES), lambda i: (i // nt, 0, i % nt, 0)),
            pl.BlockSpec((1, 1, tm * pitch, LANES), lambda i: (i // nt, 1, i % nt, 0)),
            pl.BlockSpec((1, d), lambda i: (0, 0)),
            pl.BlockSpec((d, w2), lambda i: (0, 0), pipeline_mode=pl.Buffered(1)),
            pl.BlockSpec((1, width), lambda i: (0, 0)),
            pl.BlockSpec((1, width), lambda i: (0, 0)),
            pl.BlockSpec((n_heads, ck, ck), lambda i: (0, 0, 0)),
            pl.BlockSpec((ck, n_heads), lambda i: (0, 0)),
        ],
        out_specs=[pl.BlockSpec((tm, width), lambda i: (i, 0)),
                   pl.BlockSpec((tm, d), lambda i: (i, 0))],
        out_shape=[jax.ShapeDtypeStruct((m, width), BF16),
                   jax.ShapeDtypeStruct((m, d), F32)],
        compiler_params=_params(("parallel",)),
        name="gmlp_gate",
    )(x, moe_slabs, moe_slabs, norm_g.reshape(1, d), w_in_bf16, ln_g.reshape(1, width),
      ln_b.reshape(1, width), w_s_bf16, b_s.T.astype(F32))


def _slab_pitch(half):
    return SUBLANES * pl.cdiv(half // LANES, SUBLANES)


def _route_rows(x, g_ref, rt_ref, aff_ref, slab_ref, *, tm, half, pitch):
    h = _rms(x, g_ref[...])
    nt = (((1,), (1,)), ((), ()))
    e = rt_ref.shape[0]
    h1 = h.astype(BF16)
    h1f = h1.astype(F32)
    h2 = (h - h1f).astype(BF16)
    r1 = rt_ref[...].astype(BF16)
    r2 = (rt_ref[...] - r1.astype(F32)).astype(BF16)
    lead = lax.dot_general(jnp.concatenate([r1, r2], axis=0), h1, nt, preferred_element_type=F32)
    logits = lead[:e] + lead[e:] + lax.dot_general(r1, h2, nt, preferred_element_type=F32)
    mx = jnp.max(logits, axis=0, keepdims=True)
    ex = jnp.exp(logits - mx)
    aff_ref[0] = ex / jnp.sum(ex, axis=0, keepdims=True)
    lo = lax.bitcast_convert_type(h1f[:, :half], U32) >> 16
    hi = lax.bitcast_convert_type(h1f[:, half:], U32) & jnp.uint32(0xFFFF0000)
    word = lo | hi
    for s in range(half // LANES):
        slab_ref[pl.ds(s, tm, stride=pitch), :] = word[:, s * LANES:(s + 1) * LANES]
    if pitch > half // LANES:
        for s in range(half // LANES, pitch):
            slab_ref[pl.ds(s, tm, stride=pitch), :] = jnp.zeros((tm, LANES), U32)


def _topk_kernel(aff_ref, idx_ref, gate_ref, blk_ref, *, cap):
    aff = aff_ref[0]
    e, s = aff.shape
    nb = s // LANES
    assert e & (e - 1) == 0
    bits = lax.bitcast_convert_type(aff, I32)

    def count_ge(t):
        return jnp.sum((bits >= t).astype(F32), axis=1, keepdims=True)

    def body(_, carry):
        lo, hi = carry
        mid = lo + ((hi - lo) >> 1)
        ok = count_ge(mid) >= cap
        return jnp.where(ok, mid, lo), jnp.where(ok, hi, mid)

    lo0 = jnp.zeros((e, 1), I32)
    hi0 = jnp.full((e, 1), 0x7F800000, I32)
    thr, _ = lax.fori_loop(0, 32, body, (lo0, hi0))
    gt = bits > thr
    eq = bits == thr
    need = cap - jnp.sum(gt.astype(F32), axis=1, keepdims=True)
    rows = nb * e
    by_block = lambda x: jnp.concatenate([x[:, k * LANES:(k + 1) * LANES] for k in range(nb)], axis=0)
    tri = (lax.broadcasted_iota(I32, (LANES, LANES), 0)
           <= lax.broadcasted_iota(I32, (LANES, LANES), 1)).astype(BF16)
    ri = lax.broadcasted_iota(I32, (rows, rows), 0)
    ci = lax.broadcasted_iota(I32, (rows, rows), 1)
    earlier = (((ci & (e - 1)) == (ri & (e - 1))) & (ci < ri)).astype(BF16)

    def counts(mask):
        within = jnp.dot(jnp.where(mask, 1.0, 0.0).astype(BF16), tri, preferred_element_type=F32)
        total = jnp.broadcast_to(within[:, LANES - 1:LANES], (rows, LANES)).astype(BF16)
        before = jnp.dot(earlier, total, preferred_element_type=F32)[:, :1]
        return within, before

    gt_r, eq_r = by_block(gt), by_block(eq)
    eq_in, eq_before = counts(eq_r)
    need_r = jnp.concatenate([need] * nb, axis=0)
    sel_r = gt_r | (eq_r & (eq_in + eq_before <= need_r))
    sel_in, sel_before = counts(sel_r)
    lane = lax.broadcasted_iota(I32, (rows, LANES), 1)
    blk_ref[0] = jnp.where(sel_r, sel_in, 0.0)
    blk_ref[1] = by_block(aff)
    blk_ref[2] = jnp.where(lane == 0, sel_before,
                           jnp.where(lane == 1, sel_before + sel_in[:, LANES - 1:LANES], 0.0))

    jrow = lax.broadcasted_iota(I32, (1, cap), 1).astype(F32)
    b_id = lax.broadcasted_iota(I32, (nb, cap), 0).astype(F32)
    l_id = lax.broadcasted_iota(I32, (LANES, cap), 0).astype(F32)
    for ei in range(e):
        rank_in = blk_ref[0, pl.ds(ei, nb, stride=e), :]
        aff_b = blk_ref[1, pl.ds(ei, nb, stride=e), :]
        cnt = blk_ref[2, pl.ds(ei, nb, stride=e), :]
        before, upto = cnt[:, 0:1], cnt[:, 1:2]
        blk = jnp.sum(jnp.where(upto <= jrow, 1.0, 0.0), axis=0, keepdims=True)
        onehot = b_id == blk
        rank = jrow - jnp.sum(jnp.where(onehot, before, 0.0), axis=0, keepdims=True) + 1.0
        oh = jnp.where(onehot, 1.0, 0.0).astype(BF16)
        ranks_t = jnp.dot(rank_in.T.astype(BF16), oh, preferred_element_type=F32)
        hit = ranks_t == rank
        a_t = aff_b.T
        g1 = a_t.astype(BF16)
        g2 = (a_t - g1.astype(F32)).astype(BF16)
        g3 = ((a_t - g1.astype(F32)) - g2.astype(F32)).astype(BF16)
        aff_t = (jnp.dot(g1, oh, preferred_element_type=F32) + jnp.dot(g2, oh, preferred_element_type=F32)
                 + jnp.dot(g3, oh, preferred_element_type=F32))
        local = jnp.sum(jnp.where(hit, l_id, 0.0), axis=0, keepdims=True)
        idx_ref[0, ei:ei + 1, :] = (blk * LANES + local).astype(I32)
        gate_ref[0, ei:ei + 1, :] = jnp.sum(jnp.where(hit, aff_t, 0.0), axis=0, keepdims=True)


def moe_topk(aff, cap):
    b, e, s = aff.shape
    return pl.pallas_call(
        functools.partial(_topk_kernel, cap=cap),
        grid=(b,),
        in_specs=[pl.BlockSpec((1, e, s), lambda i: (i, 0, 0))],
        out_specs=[pl.BlockSpec((1, e, cap), lambda i: (i, 0, 0)),
                   pl.BlockSpec((1, e, cap), lambda i: (i, 0, 0))],
        out_shape=[jax.ShapeDtypeStruct((b, e, cap), I32),
                   jax.ShapeDtypeStruct((b, e, cap), F32)],
        scratch_shapes=[pltpu.VMEM((3, s // LANES * e, LANES), F32)],
        compiler_params=_params(("parallel",)),
        name="moe_topk",
    )(aff)


def _gather_kernel(idx_ref, slab_ref, o_ref, rows_ref, *, cap, half, pitch, unroll):
    def body(j0, c):
        for u in range(unroll):
            j = j0 * unroll + u
            src = pl.multiple_of(idx_ref[0, 0, j] * pitch, pitch)
            rows_ref[pl.ds(pl.multiple_of(j * pitch, pitch), pitch), :] = slab_ref[pl.ds(src, pitch), :]
        return c

    lax.fori_loop(0, cap // unroll, body, 0)
    for s in range(half // LANES):
        word = rows_ref[pl.ds(s, cap, stride=pitch), :]
        lo = lax.bitcast_convert_type(word << 16, F32)
        hi = lax.bitcast_convert_type(word & jnp.uint32(0xFFFF0000), F32)
        o_ref[0, :, s * LANES:(s + 1) * LANES] = lo.astype(BF16)
        o_ref[0, :, half + s * LANES:half + (s + 1) * LANES] = hi.astype(BF16)


def moe_gather(idx, slabs, *, seq, d):
    b, e, cap = idx.shape
    half = d // 2
    pitch = _slab_pitch(half)
    return pl.pallas_call(
        functools.partial(_gather_kernel, cap=cap, half=half, pitch=pitch, unroll=_tile(cap, 32)),
        grid=(b, e),
        in_specs=[
            pl.BlockSpec((1, 1, cap), lambda bi, ei: (bi * e + ei, 0, 0), memory_space=pltpu.SMEM),
            pl.BlockSpec((seq * pitch, LANES), lambda bi, ei: (bi, 0)),
        ],
        out_specs=pl.BlockSpec((1, cap, d), lambda bi, ei: (ei, bi, 0)),
        out_shape=jax.ShapeDtypeStruct((e, b * cap, d), BF16),
        scratch_shapes=[pltpu.VMEM((cap * pitch, LANES), U32)],
        compiler_params=_params(("parallel", "arbitrary")),
        name="moe_gather",
    )(idx.reshape(b * e, 1, cap), slabs)


def _ffn_kernel(xs_ref, w1_ref, w3_ref, w2_ref, gate_ref, o_ref, hid_ref, *, nt, tf):
    step = pl.program_id(1)

    @pl.when(step < nt)
    def _():
        xs = xs_ref[0]
        a = jnp.dot(xs, w1_ref[0, 0].astype(BF16), preferred_element_type=F32)
        g = jnp.dot(xs, w3_ref[0, 0].astype(BF16), preferred_element_type=F32)
        hid_ref[step] = (a * jax.nn.sigmoid(a) * g).astype(BF16)

    @pl.when(step >= nt)
    def _():
        hid = jnp.concatenate([hid_ref[k] for k in range(nt)], axis=1)
        o_ref[0] = jnp.dot(hid, w2_ref[0, 0].astype(BF16), preferred_element_type=F32) * gate_ref[0]


def moe_ffn(xs, gate, w1, w3, w2, layer, *, tf=256, tn=512):
    e, m, d = xs.shape
    f = w1.shape[3]
    tf = _tile(f, tf)
    tn = _tile(d, tn)
    nt = f // tf
    nn = d // tn
    return pl.pallas_call(
        functools.partial(_ffn_kernel, nt=nt, tf=tf),
        grid=(e, nt + nn),
        in_specs=[
            pl.BlockSpec((1, m, d), lambda ei, t: (ei, 0, 0)),
            pl.BlockSpec((1, 1, d, tf), lambda ei, t: (layer, ei, 0, jnp.minimum(t, nt - 1))),
            pl.BlockSpec((1, 1, d, tf), lambda ei, t: (layer, ei, 0, jnp.minimum(t, nt - 1))),
            pl.BlockSpec((1, 1, f, tn), lambda ei, t: (layer, ei, 0, jnp.maximum(t - nt, 0))),
            pl.BlockSpec((1, m, 1), lambda ei, t: (ei, 0, 0)),
        ],
        out_specs=pl.BlockSpec((1, m, tn), lambda ei, t: (ei, 0, jnp.maximum(t - nt, 0))),
        out_shape=jax.ShapeDtypeStruct((e, m, d), F32),
        scratch_shapes=[pltpu.VMEM((nt, m, tf), BF16)],
        compiler_params=_params(("parallel", "arbitrary")),
        name="moe_ffn",
    )(xs, w1, w3, w2, gate)


def _combine_kernel(idx_ref, ys_ref, o_ref, rows_ref, *, cap, half, pitch, unroll):
    ei = pl.program_id(2)

    @pl.when(ei == 0)
    def _():
        o_ref[...] = jnp.zeros(o_ref.shape, F32)

    for s in range(half // LANES):
        rows_ref[pl.ds(s, cap, stride=pitch), :] = ys_ref[0, :, s * LANES:(s + 1) * LANES]

    def body(j0, c):
        dsts, vals = [], []
        for u in range(unroll):
            j = j0 * unroll + u
            dst = pl.multiple_of(idx_ref[0, 0, j] * pitch, pitch)
            src = pl.multiple_of(j * pitch, pitch)
            vals.append(o_ref[0, 0, pl.ds(dst, pitch), :] + rows_ref[pl.ds(src, pitch), :])
            dsts.append(dst)
        for dst, val in zip(dsts, vals):
            o_ref[0, 0, pl.ds(dst, pitch), :] = val
        return c

    lax.fori_loop(0, cap // unroll, body, 0)


def moe_combine(idx, ys, *, seq, d):
    b, e, cap = idx.shape
    half = d // 2
    pitch = _slab_pitch(half)
    return pl.pallas_call(
        functools.partial(_combine_kernel, cap=cap, half=half, pitch=pitch, unroll=_tile(cap, 16)),
        grid=(b, 2, e),
        in_specs=[
            pl.BlockSpec((1, 1, cap), lambda bi, dh, ei: (bi * e + ei, 0, 0), memory_space=pltpu.SMEM),
            pl.BlockSpec((1, cap, half), lambda bi, dh, ei: (ei, bi, dh)),
        ],
        out_specs=pl.BlockSpec((1, 1, seq * pitch, LANES), lambda bi, dh, ei: (bi, dh, 0, 0)),
        out_shape=jax.ShapeDtypeStruct((b, 2, seq * pitch, LANES), F32),
        scratch_shapes=[pltpu.VMEM((cap * pitch, LANES), F32)],
        compiler_params=_params(("parallel", "parallel", "arbitrary")),
        name="moe_combine",
    )(idx.reshape(b * e, 1, cap), ys)


def _unslab(s0_ref, s1_ref, *, tm, half, pitch):
    parts = []
    for s_ref in (s0_ref, s1_ref):
        for s in range(half // LANES):
            parts.append(s_ref[0, 0, pl.ds(s, tm, stride=pitch), :])
    return jnp.concatenate(parts, axis=1)


def _unslab_kernel(*refs, tm, half, pitch, final_norm):
    x_ref, s0_ref, s1_ref = refs[:3]
    o_ref = refs[-1]
    y = x_ref[0] + _unslab(s0_ref, s1_ref, tm=tm, half=half, pitch=pitch)
    if final_norm:
        y = _rms(y, refs[3][...])
    o_ref[0] = y


def moe_residual(x, moe_slabs, final_g=None, *, tm=256):
    b, s, d = x.shape
    half = d // 2
    pitch = _slab_pitch(half)
    tm = _tile(s, tm)
    final_norm = final_g is not None
    in_specs = [
        pl.BlockSpec((1, tm, d), lambda bi, j: (bi, j, 0)),
        pl.BlockSpec((1, 1, tm * pitch, LANES), lambda bi, j: (bi, 0, j, 0)),
        pl.BlockSpec((1, 1, tm * pitch, LANES), lambda bi, j: (bi, 1, j, 0)),
    ]
    args = [x, moe_slabs, moe_slabs]
    if final_norm:
        in_specs.append(pl.BlockSpec((1, d), lambda bi, j: (0, 0)))
        args.append(final_g.reshape(1, d))
    return pl.pallas_call(
        functools.partial(_unslab_kernel, tm=tm, half=half, pitch=pitch, final_norm=final_norm),
        grid=(b, s // tm),
        in_specs=in_specs,
        out_specs=pl.BlockSpec((1, tm, d), lambda bi, j: (bi, j, 0)),
        out_shape=jax.ShapeDtypeStruct((b, s, d), F32),
        compiler_params=_params(("parallel", "parallel")),
        name="moe_residual",
    )(*args)


def expert_choice_ffn(aff, slabs, w1, w3, w2, layer, *, seq, d):
    cap = CAPACITY_FACTOR * seq // N_EXPERTS
    idx, gate = moe_topk(aff, cap)
    b, e, _ = idx.shape
    xs = moe_gather(idx, slabs, seq=seq, d=d)
    ys = moe_ffn(xs, gate.transpose(1, 0, 2).reshape(e, b * cap, 1), w1, w3, w2, layer)
    return moe_combine(idx, ys, seq=seq, d=d)


def kernel(x, rel_bias, mix_norm, ffn_norm, final_norm, even_w_in, attn_sink, ssm_a_re, ssm_a_im, ssm_log_dt, ssm_b_re, ssm_b_im, ssm_c_re, ssm_c_im, ssm_d, glu_w, glu_b, even_w_out, odd_w_in, sgu_ln_g, sgu_ln_b, sgu_w, sgu_b, odd_w_out, router, moe_w1, moe_w3, moe_w2):
    b, s, d = x.shape
    t = b * s
    depth = mix_norm.shape[0]
    ssm_w = ssm_d.shape[-1]
    attn_w = even_w_out.shape[1] - ssm_w
    kv_w = (even_w_in.shape[-1] - attn_w - ssm_w) // 2
    bias = attention_bias(rel_bias)

    x2 = x.reshape(t, d)
    moe = None
    for layer in range(depth):
        i = layer // 2
        if layer % 2 == 0:
            if moe is not None:
                x2 = moe_residual(x2.reshape(b, s, d), moe).reshape(t, d)
            qkv, u = in_projection(x2, mix_norm[layer], even_w_in[i].astype(BF16), attn_w + 2 * kv_w)
            attn = windowed_attention(qkv.reshape(b, s, -1), attn_sink[i], bias,
                                      attn_w=attn_w, kv_w=kv_w)
            n_levels = max(1, (s // S5_CHUNK - 1).bit_length())
            operands = _s5_operands(ssm_a_re[i], ssm_a_im[i], ssm_log_dt[i], ssm_b_re[i], ssm_b_im[i],
                                    ssm_c_re[i], ssm_c_im[i], S5_CHUNK, n_levels)
            y = s5_scan(u, operands, seq=s, chunk=S5_CHUNK, n_levels=n_levels)
            ssm = s5_glu(y, u, ssm_d[i], glu_w[i].astype(BF16), glu_b[i])
            mixed, w_out = [attn.reshape(t, attn_w), ssm], even_w_out[i]
        else:
            gated, x2 = gmlp_gate(x2, moe, mix_norm[layer], odd_w_in[i].astype(BF16), sgu_ln_g[i],
                                  sgu_ln_b[i], sgu_w[i].astype(BF16), sgu_b[i], seq=s)
            mixed, w_out = [gated], odd_w_out[i]
        x2, aff, slabs = matmul_residual_route(mixed, w_out.astype(BF16), x2, ffn_norm[layer],
                                               router[layer], seq=s)
        moe = expert_choice_ffn(aff, slabs, moe_w1, moe_w3, moe_w2, layer, seq=s, d=d)
    return moe_residual(x2.reshape(b, s, d), moe, final_norm)
```

```python
import functools
import math

import jax
import jax.numpy as jnp
from jax import lax
from jax.experimental import pallas as pl
from jax.experimental.pallas import tpu as pltpu

F32 = jnp.float32
BF16 = jnp.bfloat16
I32 = jnp.int32
U32 = jnp.uint32

EPS = 1e-6
NEG_INF = -1e30

LANES = 128
SUBLANES = 8
VMEM_LIMIT = 56 << 20

ATTN_BLOCK = 128
ATTN_ROW_CHUNK = 32
HEAD_DIM = 128
KV_GROUP = 4
REL_BUCKETS = 32
REL_MAX_DIST = 128
S5_CHUNK = 32
S5_PITCH = 136
GMLP_CHUNK = 128
N_EXPERTS = 16
CAPACITY_FACTOR = 2


def _params(sem, vmem=VMEM_LIMIT):
    return pltpu.CompilerParams(dimension_semantics=sem, vmem_limit_bytes=vmem)


def _tile(n, want):
    t = min(n, want)
    while n % t:
        t //= 2
    return t


def _rms(x, g):
    ms = jnp.mean(x * x, axis=-1, keepdims=True)
    return x * lax.rsqrt(ms + EPS) * g


def _gelu(x):
    c = math.sqrt(2.0 / math.pi)
    return x * (0.5 * (1.0 + jnp.tanh(c * (x + 0.044715 * (x * x * x)))))


def _in_proj_kernel(x_ref, g_ref, w_ref, qkv_ref, u_ref):
    hn = _rms(x_ref[...], g_ref[...]).astype(BF16)
    y = jnp.dot(hn, w_ref[...], preferred_element_type=F32)
    split = qkv_ref.shape[1]
    qkv_ref[...] = y[:, :split]
    u_ref[...] = y[:, split:]


def in_projection(x, g, w_bf16, split, *, tm=512):
    m, k = x.shape
    n = w_bf16.shape[1]
    tm = _tile(m, tm)
    return pl.pallas_call(
        _in_proj_kernel,
        grid=(m // tm,),
        in_specs=[
            pl.BlockSpec((tm, k), lambda i: (i, 0)),
            pl.BlockSpec((1, k), lambda i: (0, 0)),
            pl.BlockSpec((k, n), lambda i: (0, 0), pipeline_mode=pl.Buffered(1)),
        ],
        out_specs=[pl.BlockSpec((tm, split), lambda i: (i, 0)),
                   pl.BlockSpec((tm, n - split), lambda i: (i, 0))],
        out_shape=[jax.ShapeDtypeStruct((m, split), F32),
                   jax.ShapeDtypeStruct((m, n - split), F32)],
        compiler_params=_params(("parallel",)),
        name="in_projection",
    )(x, g.reshape(1, k), w_bf16)


def _mm_res_route_kernel(*refs, n_lhs, tm, half, pitch):
    lhs = refs[:n_lhs]
    ws = refs[n_lhs:2 * n_lhs]
    res_ref, g_ref, rt_ref, o_ref, aff_ref, slab_ref = refs[2 * n_lhs:]
    acc = res_ref[...]
    for l_ref, w_ref in zip(lhs, ws):
        acc = acc + jnp.dot(l_ref[...], w_ref[...], preferred_element_type=F32)
    o_ref[...] = acc
    _route_rows(acc, g_ref, rt_ref, aff_ref, slab_ref, tm=tm, half=half, pitch=pitch)


def matmul_residual_route(lhs_list, w_bf16, res, norm_g, router, *, seq, tm=512):
    m, d = res.shape
    kk = lhs_list[0].shape[1]
    n_lhs = len(lhs_list)
    e = router.shape[1]
    assert all(l.shape == (m, kk) for l in lhs_list) and w_bf16.shape == (n_lhs * kk, d)
    tm = _tile(seq, tm)
    nt = seq // tm
    half = d // 2
    pitch = _slab_pitch(half)
    in_specs = [pl.BlockSpec((tm, kk), lambda i: (i, 0)) for _ in lhs_list]
    in_specs += [pl.BlockSpec((kk, d), functools.partial(lambda i, r: (r, 0), r=r),
                              pipeline_mode=pl.Buffered(1)) for r in range(n_lhs)]
    in_specs += [pl.BlockSpec((tm, d), lambda i: (i, 0)),
                 pl.BlockSpec((1, d), lambda i: (0, 0)),
                 pl.BlockSpec((e, d), lambda i: (0, 0))]
    return pl.pallas_call(
        functools.partial(_mm_res_route_kernel, n_lhs=n_lhs, tm=tm, half=half, pitch=pitch),
        grid=(m // tm,),
        in_specs=in_specs,
        out_specs=[pl.BlockSpec((tm, d), lambda i: (i, 0)),
                   pl.BlockSpec((1, e, tm), lambda i: (i // nt, 0, i % nt)),
                   pl.BlockSpec((tm * pitch, LANES), lambda i: (i, 0))],
        out_shape=[jax.ShapeDtypeStruct((m, d), F32),
                   jax.ShapeDtypeStruct((m // seq, e, seq), F32),
                   jax.ShapeDtypeStruct((m * pitch, LANES), U32)],
        compiler_params=_params(("parallel",)),
        name="matmul_residual_route",
    )(*lhs_list, *([w_bf16] * n_lhs), res, norm_g.reshape(1, d), router.T.astype(F32))


def _t5_bucket(rel):
    nb = REL_BUCKETS // 2
    max_exact = nb // 2
    base = jnp.where(rel > 0, nb, 0)
    n = jnp.abs(rel)
    nf = jnp.maximum(n, 1).astype(F32)
    large = max_exact + (jnp.log(nf / max_exact) / math.log(REL_MAX_DIST / max_exact)
                         * (nb - max_exact)).astype(I32)
    large = jnp.minimum(large, nb - 1)
    return base + jnp.where(n < max_exact, n, large)


def _bias_kernel(rb_ref, bucket_ref, o_ref, *, n_heads):
    bkt = bucket_ref[...]
    row = lax.broadcasted_iota(I32, bkt.shape, 0)
    col = lax.broadcasted_iota(I32, bkt.shape, 1)
    in_window = jnp.abs(col - ATTN_BLOCK - row) <= ATTN_BLOCK
    for h in range(n_heads):
        acc = jnp.zeros(bkt.shape, F32)
        for k in range(REL_BUCKETS):
            acc = jnp.where(bkt == k, rb_ref[k, h], acc)
        o_ref[h * ATTN_BLOCK:(h + 1) * ATTN_BLOCK, :] = jnp.where(in_window, acc, NEG_INF)


def attention_bias(rel_bias):
    n_heads = rel_bias.shape[1]
    q_off = jnp.arange(ATTN_BLOCK, dtype=I32)
    c_off = jnp.arange(3 * ATTN_BLOCK, dtype=I32)
    bucket = _t5_bucket(c_off[None, :] - ATTN_BLOCK - q_off[:, None])
    return pl.pallas_call(
        functools.partial(_bias_kernel, n_heads=n_heads),
        in_specs=[pl.BlockSpec(memory_space=pltpu.SMEM),
                  pl.BlockSpec(memory_space=pltpu.VMEM)],
        out_specs=pl.BlockSpec(memory_space=pltpu.VMEM),
        out_shape=jax.ShapeDtypeStruct((n_heads * ATTN_BLOCK, 3 * ATTN_BLOCK), F32),
        name="attention_bias",
    )(rel_bias.astype(F32), bucket)


def _attn_kernel(sink_ref, q_ref, kp_ref, kc_ref, kn_ref, vp_ref, vc_ref, vn_ref, bias_ref,
                 o_ref, s_ref, p_ref, *, seq, n_kv, qb):
    blk = ATTN_BLOCK
    col = lax.broadcasted_iota(I32, (1, 3 * blk), 1)
    scale = HEAD_DIM ** -0.5
    rc = ATTN_ROW_CHUNK

    def band(p_ref, c_ref, n_ref, ksl, j):
        parts = []
        for i in (j - 1, j, j + 1):
            if i < 0:
                parts.append(p_ref[0, :, ksl])
            elif i >= qb:
                parts.append(n_ref[0, :, ksl])
            else:
                parts.append(c_ref[0, i * blk:(i + 1) * blk, ksl])
        return jnp.concatenate(parts, axis=0).astype(BF16)

    for j in range(qb):
        kpos = (pl.program_id(1) * qb + j - 1) * blk + col
        in_seq = (kpos >= 0) & (kpos < seq)
        rows = slice(j * blk, (j + 1) * blk)
        for kh in range(n_kv):
            ksl = slice(kh * HEAD_DIM, (kh + 1) * HEAD_DIM)
            kband = band(kp_ref, kc_ref, kn_ref, ksl, j)
            vband = band(vp_ref, vc_ref, vn_ref, ksl, j)
            heads = [kh * KV_GROUP + g for g in range(KV_GROUP)]
            qs = jnp.concatenate([q_ref[0, rows, h * HEAD_DIM:(h + 1) * HEAD_DIM] for h in heads],
                                 axis=0).astype(BF16)
            s_ref[...] = lax.dot_general(qs, kband, (((1,), (1,)), ((), ())),
                                         preferred_element_type=F32)

            def softmax_rows(c, carry, kh=kh, in_seq=in_seq):
                r0 = pl.multiple_of(c * rc, rc)
                bias = bias_ref[pl.ds(kh * KV_GROUP * blk + r0, rc), :]
                s = jnp.where(in_seq, s_ref[pl.ds(r0, rc), :] * scale + bias, NEG_INF)
                sink = sink_ref[kh * KV_GROUP + c // (blk // rc)]
                m = jnp.maximum(jnp.max(s, axis=-1, keepdims=True), sink)
                p = jnp.exp(s - m)
                denom = jnp.sum(p, axis=-1, keepdims=True) + jnp.exp(sink - m)
                p_ref[pl.ds(r0, rc), :] = (p / denom).astype(BF16)
                return carry

            lax.fori_loop(0, KV_GROUP * blk // rc, softmax_rows, 0, unroll=True)
            o = jnp.dot(p_ref[...], vband, preferred_element_type=F32)
            for g, h in enumerate(heads):
                o_ref[0, rows, h * HEAD_DIM:(h + 1) * HEAD_DIM] = (
                    o[g * blk:(g + 1) * blk].astype(o_ref.dtype))


def windowed_attention(qkv, sink, bias, *, attn_w, kv_w, qb=4):
    b, s, _ = qkv.shape
    blk = ATTN_BLOCK
    nblk = s // blk
    qb = _tile(nblk, qb)
    n_kv = kv_w // HEAD_DIM
    kcol = attn_w // kv_w
    assert attn_w % kv_w == 0 and n_kv * KV_GROUP * HEAD_DIM == attn_w

    def edge_spec(col, first):
        return pl.BlockSpec(
            (1, blk, kv_w),
            lambda bi, n: (bi, jnp.clip(n * qb + (-1 if first else qb), 0, nblk - 1), col))

    def mid_spec(col):
        return pl.BlockSpec((1, qb * blk, kv_w), lambda bi, n: (bi, n, col))

    return pl.pallas_call(
        functools.partial(_attn_kernel, seq=s, n_kv=n_kv, qb=qb),
        grid=(b, nblk // qb),
        in_specs=[
            pl.BlockSpec(memory_space=pltpu.SMEM),
            pl.BlockSpec((1, qb * blk, attn_w), lambda bi, n: (bi, n, 0)),
            edge_spec(kcol, True), mid_spec(kcol), edge_spec(kcol, False),
            edge_spec(kcol + 1, True), mid_spec(kcol + 1), edge_spec(kcol + 1, False),
            pl.BlockSpec(bias.shape, lambda bi, n: (0, 0)),
        ],
        out_specs=pl.BlockSpec((1, qb * blk, attn_w), lambda bi, n: (bi, n, 0)),
        out_shape=jax.ShapeDtypeStruct((b, s, attn_w), BF16),
        scratch_shapes=[pltpu.VMEM((KV_GROUP * blk, 3 * blk), F32),
                        pltpu.VMEM((KV_GROUP * blk, 3 * blk), BF16)],
        compiler_params=_params(("parallel", "parallel")),
        name="windowed_attention",
    )(sink.astype(F32), qkv, qkv, qkv, qkv, qkv, qkv, qkv, bias)


def _s5_operands(a_re, a_im, log_dt, b_re, b_im, c_re, c_im, chunk, n_levels):
    ell = chunk
    p = a_re.shape[-1]
    dt = jnp.exp(log_dt)[..., None]
    mag = jnp.exp(a_re * dt)
    lb_re = mag * jnp.cos(a_im * dt)
    lb_im = mag * jnp.sin(a_im * dt)
    den = a_re * a_re + a_im * a_im
    nr = lb_re - 1.0
    coef_re = (nr * a_re + lb_im * a_im) / den
    coef_im = (lb_im * a_re - nr * a_im) / den
    bb_re = coef_re[..., None] * b_re - coef_im[..., None] * b_im
    bb_im = coef_re[..., None] * b_im + coef_im[..., None] * b_re
    bt_re, bt_im = bb_re.transpose(0, 1, 3, 2), bb_im.transpose(0, 1, 3, 2)

    def powers(tau):
        t = tau.astype(F32)[None, None, :, None]
        pm = jnp.exp((a_re * dt)[:, :, None, :] * t)
        ang = (a_im * dt)[:, :, None, :] * t
        return pm * jnp.cos(ang), pm * jnp.sin(ang)

    g = a_re.shape[1]
    assert 2 * p == LANES and g % 2 == 0
    odd = (jnp.arange(g) % 2 == 1)[:, None, None]
    cat = lambda *ts: jnp.concatenate(ts, axis=-1)
    lp = lambda t: jnp.pad(t, [(0, 0)] * (t.ndim - 1) + [(0, LANES - p)])
    sp = lambda t: jnp.where(odd, cat(jnp.zeros_like(t), t), cat(t, jnp.zeros_like(t)))
    ii = jnp.arange(ell)
    dn_re, dn_im = powers(ell - 1 - ii)
    up_re, up_im = powers(ii)
    e1_re, e1_im = powers(ii + 1)
    e2_re, e2_im = powers(ell - ii)
    lv_re, lv_im = powers(ell * (2 ** jnp.arange(n_levels)))
    lev = SUBLANES * pl.cdiv(n_levels, SUBLANES)
    lvp = lambda t: jnp.pad(sp(t), ((0, 0), (0, lev - n_levels), (0, 0)))
    wide = jnp.concatenate([
        cat(sp(dn_re[0]), sp(dn_re[0]), sp(up_re[1]), sp(up_re[1])),
        cat(sp(dn_im[0]), sp(dn_im[0]), sp(up_im[1]), sp(up_im[1])),
        cat(sp(e1_re[0]), sp(e1_re[0]), sp(e2_re[1]), sp(e2_re[1])),
        cat(sp(e1_im[0]), sp(e1_im[0]), sp(e2_im[1]), sp(e2_im[1])),
        cat(sp(bt_re[0]), sp(bt_im[0]), sp(bt_re[1]), sp(bt_im[1])),
        cat(-sp(bt_im[0]), sp(bt_re[0]), -sp(bt_im[1]), sp(bt_re[1])),
        cat(sp(c_re[0]), -sp(c_im[0]), sp(c_re[1]), -sp(c_im[1])),
        cat(-sp(c_im[0]), -sp(c_re[0]), -sp(c_im[1]), -sp(c_re[1])),
        cat(lvp(lv_re[0]), lvp(lv_re[0]), lvp(lv_re[1]), lvp(lv_re[1])),
        cat(-lvp(lv_im[0]), lvp(lv_im[0]), -lvp(lv_im[1]), lvp(lv_im[1])),
    ], axis=1)

    def lag_rows(fwd, bwd):
        f = jnp.pad(lp(fwd), ((0, 0), (ell - 1, 1), (0, 0)))
        b = jnp.pad(lp(bwd), ((0, 0), (0, ell), (0, 0)))
        return cat(f, b)

    narrow = jnp.concatenate([
        lag_rows(up_re[0], dn_re[1]), lag_rows(up_im[0], dn_im[1]),
        cat(lp(c_re[0]), lp(c_re[1])), cat(lp(c_im[0]), lp(c_im[1])),
        cat(lp(bt_re[0]), lp(bt_re[1])), cat(lp(bt_im[0]), lp(bt_im[1])),
    ], axis=1)
    return wide, narrow


def _s5_pack_kernel(u_ref, o_ref, t_ref, *, n_lt, h, chunk):
    def transpose_block(lt, c):
        src = pl.multiple_of(lt * LANES, LANES)
        dst = pl.multiple_of(lt * S5_PITCH, SUBLANES)
        t_ref[pl.ds(dst, LANES), :] = u_ref[pl.ds(src, LANES), :].T
        return c

    lax.fori_loop(0, n_lt, transpose_block, 0, unroll=8)
    n_sub = LANES // chunk

    def pack_group(g8, c):
        for kt in range(h // n_sub):
            ms = [t_ref[pl.ds(g8 * h + kt * n_sub + kk, n_lt, stride=S5_PITCH), :] for kk in range(n_sub)]
            for c_lo in range(n_sub):
                tile = jnp.concatenate([m[:, c_lo * chunk:(c_lo + 1) * chunk] for m in ms], axis=1)
                o_ref[g8, c_lo * n_lt:(c_lo + 1) * n_lt, kt * LANES:(kt + 1) * LANES] = tile.astype(BF16)
        return c

    lax.fori_loop(0, LANES // h, pack_group, 0)


def _s5_unpack_kernel(y_ref, o_ref, t_ref, *, n_lt, h, chunk):
    n_sub = LANES // chunk

    def unpack_group(g8, c):
        for hh in range(h):
            lane0 = (hh // n_sub) * LANES + (hh % n_sub) * chunk
            m = jnp.concatenate([y_ref[g8, c_lo * n_lt:(c_lo + 1) * n_lt, lane0:lane0 + chunk]
                                 for c_lo in range(n_sub)], axis=1)
            t_ref[pl.ds(g8 * h + hh, n_lt, stride=S5_PITCH), :] = m
        return c

    lax.fori_loop(0, LANES // h, unpack_group, 0)

    def transpose_block(lt, c):
        src = pl.multiple_of(lt * S5_PITCH, SUBLANES)
        dst = pl.multiple_of(lt * LANES, LANES)
        o_ref[pl.ds(dst, LANES), :] = t_ref[pl.ds(src, LANES), :].T
        return c

    lax.fori_loop(0, n_lt, transpose_block, 0, unroll=8)


def _shift_rows(z, r, seg):
    if r == 0:
        return z
    rows = z.shape[0]
    rolled = pltpu.roll(z, r % rows, 0)
    assert seg & (seg - 1) == 0
    pos = lax.broadcasted_iota(I32, z.shape, 0) & (seg - 1)
    keep = (pos >= r) if r > 0 else (pos < seg + r)
    return jnp.where(keep, rolled, 0.0)


def _shift_chunks(blocks, d, seg):
    n_sub = len(blocks)
    out = []
    for c_lo in range(n_sub):
        e, s_lo = divmod(c_lo - d, n_sub)
        out.append(_shift_rows(blocks[s_lo], -e, seg))
    return out


def _s5_group_tables(w, nr, perm, *, n_levels, chunk, h):
    ell, lh = chunk, chunk * h
    lev = SUBLANES * pl.cdiv(n_levels, SUBLANES)
    nt = (((1,), (1,)), ((), ()))
    pwf_re, pwf_im, pwe_re, pwe_im = (w[k * ell:(k + 1) * ell] for k in range(4))
    o = 4 * ell
    bb1, bb2, cc1, cc2 = (w[o + k * h:o + (k + 1) * h] for k in range(4))
    o += 4 * h
    pa_all, pb_all = w[o:o + lev], w[o + lev:o + 2 * lev]
    f_tab = jnp.concatenate([pwf_re * bb1[k:k + 1, :] + pwf_im * bb2[k:k + 1, :] for k in range(h)],
                            axis=0).astype(BF16)
    et_tab = jnp.concatenate([pwe_re * cc1[k:k + 1, :] + pwe_im * cc2[k:k + 1, :] for k in range(h)],
                             axis=0).astype(BF16)
    pwm_re, pwm_im = nr[:2 * ell], nr[2 * ell:4 * ell]
    o = 4 * ell
    cr, ci, br, bi = (nr[o + k * h:o + (k + 1) * h] for k in range(4))
    rep = lambda x: jnp.concatenate([jnp.broadcast_to(x[j:j + 1, :], (h, x.shape[1]))
                                     for j in range(x.shape[0])], axis=0)
    tile = lambda t: jnp.concatenate([t] * (2 * ell), axis=0)
    pr, pi = rep(pwm_re), rep(pwm_im)
    crt, cit = tile(cr), tile(ci)
    a = jnp.concatenate([pr * crt - pi * cit, pr * cit + pi * crt], axis=1)
    b = jnp.concatenate([br, -bi], axis=1)
    a1 = a.astype(BF16)
    a2 = (a - a1.astype(F32)).astype(BF16)
    b1 = b.astype(BF16)
    b2 = (b - b1.astype(F32)).astype(BF16)
    lead = lax.dot_general(jnp.concatenate([b1, b2], axis=0), a1, nt, preferred_element_type=F32)
    strip = lead[:h] + lead[h:] + lax.dot_general(b1, a2, nt, preferred_element_type=F32)
    t_jk = jnp.concatenate([strip[:, (ell - 1 - j) * h:(ell - 1 - j) * h + lh] for j in range(ell)],
                           axis=0).astype(BF16)
    t_tab = lax.dot_general(jnp.dot(perm, t_jk, preferred_element_type=F32).astype(BF16), perm, nt,
                            preferred_element_type=F32).astype(BF16)
    return f_tab, et_tab, t_tab, pa_all, pb_all


def _s5_kernel(u_ref, wide_ref, narrow_ref, perm_ref, o_ref, *, seg, n_levels, chunk, h):
    n_sub = LANES // chunk
    nt = (((1,), (1,)), ((), ()))
    perm = perm_ref[...]
    tabs = [_s5_group_tables(wide_ref[gi], narrow_ref[gi], perm, n_levels=n_levels, chunk=chunk, h=h)
            for gi in range(2)]
    us = [u_ref[gi] for gi in range(2)]
    z = sum(jnp.dot(u, t[0], preferred_element_type=F32) for u, t in zip(us, tabs))
    pa_all = tabs[0][3] + tabs[1][3]
    pb_all = tabs[0][4] + tabs[1][4]
    n_lt = z.shape[0] // n_sub
    half = 2 * LANES

    def swap(x):
        return jnp.concatenate([x[:, LANES:], x[:, :LANES]], axis=1)

    zf = [z[c * n_lt:(c + 1) * n_lt, :half] for c in range(n_sub)]
    zb = [z[c * n_lt:(c + 1) * n_lt, half:] for c in range(n_sub)]
    for k in range(n_levels):
        pa = pa_all[k:k + 1, :]
        pb = pb_all[k:k + 1, :]
        sf = _shift_chunks(zf, 2 ** k, seg)
        sb = _shift_chunks(zb, -(2 ** k), seg)
        zf = [a + pa[:, :half] * s + pb[:, :half] * swap(s) for a, s in zip(zf, sf)]
        zb = [a + pa[:, half:] * s + pb[:, half:] * swap(s) for a, s in zip(zb, sb)]
    hin = jnp.concatenate([jnp.concatenate(_shift_chunks(zf, 1, seg), axis=0),
                           jnp.concatenate(_shift_chunks(zb, -1, seg), axis=0)], axis=1).astype(BF16)
    for gi in range(2):
        y = jnp.dot(us[gi], tabs[gi][2], preferred_element_type=F32)
        o_ref[gi] = y + lax.dot_general(hin, tabs[gi][1], nt, preferred_element_type=F32)


def s5_scan(u, operands, *, seq, chunk, n_levels):
    wide, narrow = operands
    t, w = u.shape
    g = wide.shape[0]
    h = w // g
    lh = chunk * h
    n_lt = t // LANES
    rows = t // chunk
    gpt = LANES // h
    assert t % LANES == 0 and seq % LANES == 0 and LANES % chunk == 0 and h % (LANES // chunk) == 0
    scratch = pltpu.VMEM((n_lt * S5_PITCH, LANES), F32)
    ug = pl.pallas_call(
        functools.partial(_s5_pack_kernel, n_lt=n_lt, h=h, chunk=chunk),
        grid=(w // LANES,),
        in_specs=[pl.BlockSpec((t, LANES), lambda q: (0, q))],
        out_specs=pl.BlockSpec((gpt, rows, lh), lambda q: (q, 0, 0)),
        out_shape=jax.ShapeDtypeStruct((g, rows, lh), BF16),
        scratch_shapes=[scratch],
        compiler_params=_params(("parallel",)),
        name="s5_pack",
    )(u)
    kj = jnp.arange(lh)
    perm = (kj[:, None] // chunk + (kj[:, None] % chunk) * h == kj[None, :]).astype(BF16)
    yg = pl.pallas_call(
        functools.partial(_s5_kernel, seg=seq // LANES, n_levels=n_levels, chunk=chunk, h=h),
        grid=(g // 2,),
        in_specs=[
            pl.BlockSpec((2, rows, lh), lambda i: (i, 0, 0)),
            pl.BlockSpec((2,) + wide.shape[1:], lambda i: (i, 0, 0)),
            pl.BlockSpec((2,) + narrow.shape[1:], lambda i: (i, 0, 0)),
            pl.BlockSpec((lh, lh), lambda i: (0, 0)),
        ],
        out_specs=pl.BlockSpec((2, rows, lh), lambda i: (i, 0, 0)),
        out_shape=jax.ShapeDtypeStruct((g, rows, lh), F32),
        compiler_params=_params(("parallel",)),
        name="s5_scan",
    )(ug, wide, narrow, perm)
    return pl.pallas_call(
        functools.partial(_s5_unpack_kernel, n_lt=n_lt, h=h, chunk=chunk),
        grid=(w // LANES,),
        in_specs=[pl.BlockSpec((gpt, rows, lh), lambda q: (q, 0, 0))],
        out_specs=pl.BlockSpec((t, LANES), lambda q: (0, q)),
        out_shape=jax.ShapeDtypeStruct((t, w), F32),
        scratch_shapes=[scratch],
        compiler_params=_params(("parallel",)),
        name="s5_unpack",
    )(yg)


def _glu_kernel(y_ref, u_ref, d_ref, w_ref, b_ref, o_ref):
    y = y_ref[...] + d_ref[...] * u_ref[...]
    g = _gelu(y)
    z = jnp.dot(g.astype(BF16), w_ref[...], preferred_element_type=F32) + b_ref[...]
    o_ref[...] = (g * jax.nn.sigmoid(z)).astype(o_ref.dtype)


def s5_glu(y, u, d_skip, glu_w_bf16, glu_b, *, tm=512):
    m, w = y.shape
    tm = _tile(m, tm)
    return pl.pallas_call(
        _glu_kernel,
        grid=(m // tm,),
        in_specs=[
            pl.BlockSpec((tm, w), lambda i: (i, 0)),
            pl.BlockSpec((tm, w), lambda i: (i, 0)),
            pl.BlockSpec((1, w), lambda i: (0, 0)),
            pl.BlockSpec((w, w), lambda i: (0, 0)),
            pl.BlockSpec((1, w), lambda i: (0, 0)),
        ],
        out_specs=pl.BlockSpec((tm, w), lambda i: (i, 0)),
        out_shape=jax.ShapeDtypeStruct((m, w), BF16),
        compiler_params=_params(("parallel",)),
        name="s5_glu",
    )(y, u, d_skip.reshape(1, w), glu_w_bf16, glu_b.reshape(1, w))


def _gmlp_kernel(x_ref, s0_ref, s1_ref, ng_ref, w_ref, g_ref, b_ref, ws_ref, bs_ref, o_ref, xn_ref, *,
                 width, n_heads, tm, half, pitch):
    hd = width // n_heads
    ck = GMLP_CHUNK
    x = x_ref[...] + _unslab(s0_ref, s1_ref, tm=tm, half=half, pitch=pitch)
    xn_ref[...] = x
    hn = _rms(x, ng_ref[...]).astype(BF16)
    z = _gelu(jnp.dot(hn, w_ref[...], preferred_element_type=F32))
    u, v = z[:, :width], z[:, width:]
    mu = jnp.mean(v, axis=-1, keepdims=True)
    var = jnp.mean(jnp.square(v - mu), axis=-1, keepdims=True)
    vn = ((v - mu) * lax.rsqrt(var + EPS) * g_ref[...] + b_ref[...]).astype(BF16)
    for c in range(tm // ck):
        rows = slice(c * ck, (c + 1) * ck)
        for h in range(n_heads):
            sl = slice(h * hd, (h + 1) * hd)
            mixed = jnp.dot(ws_ref[h], vn[rows, sl], preferred_element_type=F32) + bs_ref[:, h:h + 1]
            o_ref[rows, sl] = (u[rows, sl] * mixed).astype(o_ref.dtype)


def gmlp_gate(x, moe_slabs, norm_g, w_in_bf16, ln_g, ln_b, w_s_bf16, b_s, *, seq, tm=512):
    m, d = x.shape
    w2 = w_in_bf16.shape[1]
    width = w2 // 2
    n_heads = w_s_bf16.shape[0]
    ck = GMLP_CHUNK
    tm = max(ck, _tile(seq, tm))
    nt = seq // tm
    half = d // 2
    pitch = _slab_pitch(half)
    return pl.pallas_call(
        functools.partial(_gmlp_kernel, width=width, n_heads=n_heads, tm=tm, half=half, pitch=pitch),
        grid=(m // tm,),
        in_specs=[
            pl.BlockSpec((tm, d), lambda i: (i, 0)),
            pl.BlockSpec((1, 1, tm * pitch, LANES), lambda i: (i // nt, 0, i % nt, 0)),
            pl.BlockSpec((1, 1, tm * pitch, LANES), lambda i: (i // nt, 1, i % nt, 0)),
            pl.BlockSpec((1, d), lambda i: (0, 0)),
            pl.BlockSpec((d, w2), lambda i: (0, 0), pipeline_mode=pl.Buffered(1)),
            pl.BlockSpec((1, width), lambda i: (0, 0)),
            pl.BlockSpec((1, width), lambda i: (0, 0)),
            pl.BlockSpec((n_heads, ck, ck), lambda i: (0, 0, 0)),
            pl.BlockSpec((ck, n_heads), lambda i: (0, 0)),
        ],
        out_specs=[pl.BlockSpec((tm, width), lambda i: (i, 0)),
                   pl.BlockSpec((tm, d), lambda i: (i, 0))],
        out_shape=[jax.ShapeDtypeStruct((m, width), BF16),
                   jax.ShapeDtypeStruct((m, d), F32)],
        compiler_params=_params(("parallel",)),
        name="gmlp_gate",
    )(x, moe_slabs, moe_slabs, norm_g.reshape(1, d), w_in_bf16, ln_g.reshape(1, width),
      ln_b.reshape(1, width), w_s_bf16, b_s.T.astype(F32))


def _slab_pitch(half):
    return SUBLANES * pl.cdiv(half // LANES, SUBLANES)


def _route_rows(x, g_ref, rt_ref, aff_ref, slab_ref, *, tm, half, pitch):
    h = _rms(x, g_ref[...])
    nt = (((1,), (1,)), ((), ()))
    e = rt_ref.shape[0]
    h1 = h.astype(BF16)
    h1f = h1.astype(F32)
    h2 = (h - h1f).astype(BF16)
    r1 = rt_ref[...].astype(BF16)
    r2 = (rt_ref[...] - r1.astype(F32)).astype(BF16)
    lead = lax.dot_general(jnp.concatenate([r1, r2], axis=0), h1, nt, preferred_element_type=F32)
    logits = lead[:e] + lead[e:] + lax.dot_general(r1, h2, nt, preferred_element_type=F32)
    mx = jnp.max(logits, axis=0, keepdims=True)
    ex = jnp.exp(logits - mx)
    aff_ref[0] = ex / jnp.sum(ex, axis=0, keepdims=True)
    lo = lax.bitcast_convert_type(h1f[:, :half], U32) >> 16
    hi = lax.bitcast_convert_type(h1f[:, half:], U32) & jnp.uint32(0xFFFF0000)
    word = lo | hi
    for s in range(half // LANES):
        slab_ref[pl.ds(s, tm, stride=pitch), :] = word[:, s * LANES:(s + 1) * LANES]
    if pitch > half // LANES:
        for s in range(half // LANES, pitch):
            slab_ref[pl.ds(s, tm, stride=pitch), :] = jnp.zeros((tm, LANES), U32)


def _topk_kernel(aff_ref, idx_ref, gate_ref, blk_ref, *, cap):
    aff = aff_ref[0]
    e, s = aff.shape
    nb = s // LANES
    assert e & (e - 1) == 0
    bits = lax.bitcast_convert_type(aff, I32)

    def count_ge(t):
        return jnp.sum((bits >= t).astype(F32), axis=1, keepdims=True)

    def body(_, carry):
        lo, hi = carry
        mid = lo + ((hi - lo) >> 1)
        ok = count_ge(mid) >= cap
        return jnp.where(ok, mid, lo), jnp.where(ok, hi, mid)

    lo0 = jnp.zeros((e, 1), I32)
    hi0 = jnp.full((e, 1), 0x7F800000, I32)
    thr, _ = lax.fori_loop(0, 32, body, (lo0, hi0))
    gt = bits > thr
    eq = bits == thr
    need = cap - jnp.sum(gt.astype(F32), axis=1, keepdims=True)
    rows = nb * e
    by_block = lambda x: jnp.concatenate([x[:, k * LANES:(k + 1) * LANES] for k in range(nb)], axis=0)
    tri = (lax.broadcasted_iota(I32, (LANES, LANES), 0)
           <= lax.broadcasted_iota(I32, (LANES, LANES), 1)).astype(BF16)
    ri = lax.broadcasted_iota(I32, (rows, rows), 0)
    ci = lax.broadcasted_iota(I32, (rows, rows), 1)
    earlier = (((ci & (e - 1)) == (ri & (e - 1))) & (ci < ri)).astype(BF16)

    def counts(mask):
        within = jnp.dot(jnp.where(mask, 1.0, 0.0).astype(BF16), tri, preferred_element_type=F32)
        total = jnp.broadcast_to(within[:, LANES - 1:LANES], (rows, LANES)).astype(BF16)
        before = jnp.dot(earlier, total, preferred_element_type=F32)[:, :1]
        return within, before

    gt_r, eq_r = by_block(gt), by_block(eq)
    eq_in, eq_before = counts(eq_r)
    need_r = jnp.concatenate([need] * nb, axis=0)
    sel_r = gt_r | (eq_r & (eq_in + eq_before <= need_r))
    sel_in, sel_before = counts(sel_r)
    lane = lax.broadcasted_iota(I32, (rows, LANES), 1)
    blk_ref[0] = jnp.where(sel_r, sel_in, 0.0)
    blk_ref[1] = by_block(aff)
    blk_ref[2] = jnp.where(lane == 0, sel_before,
                           jnp.where(lane == 1, sel_before + sel_in[:, LANES - 1:LANES], 0.0))

    jrow = lax.broadcasted_iota(I32, (1, cap), 1).astype(F32)
    b_id = lax.broadcasted_iota(I32, (nb, cap), 0).astype(F32)
    l_id = lax.broadcasted_iota(I32, (LANES, cap), 0).astype(F32)
    for ei in range(e):
        rank_in = blk_ref[0, pl.ds(ei, nb, stride=e), :]
        aff_b = blk_ref[1, pl.ds(ei, nb, stride=e), :]
        cnt = blk_ref[2, pl.ds(ei, nb, stride=e), :]
        before, upto = cnt[:, 0:1], cnt[:, 1:2]
        blk = jnp.sum(jnp.where(upto <= jrow, 1.0, 0.0), axis=0, keepdims=True)
        onehot = b_id == blk
        rank = jrow - jnp.sum(jnp.where(onehot, before, 0.0), axis=0, keepdims=True) + 1.0
        oh = jnp.where(onehot, 1.0, 0.0).astype(BF16)
        ranks_t = jnp.dot(rank_in.T.astype(BF16), oh, preferred_element_type=F32)
        hit = ranks_t == rank
        a_t = aff_b.T
        g1 = a_t.astype(BF16)
        g2 = (a_t - g1.astype(F32)).astype(BF16)
        g3 = ((a_t - g1.astype(F32)) - g2.astype(F32)).astype(BF16)
        aff_t = (jnp.dot(g1, oh, preferred_element_type=F32) + jnp.dot(g2, oh, preferred_element_type=F32)
                 + jnp.dot(g3, oh, preferred_element_type=F32))
        local = jnp.sum(jnp.where(hit, l_id, 0.0), axis=0, keepdims=True)
        idx_ref[0, ei:ei + 1, :] = (blk * LANES + local).astype(I32)
        gate_ref[0, ei:ei + 1, :] = jnp.sum(jnp.where(hit, aff_t, 0.0), axis=0, keepdims=True)


def moe_topk(aff, cap):
    b, e, s = aff.shape
    return pl.pallas_call(
        functools.partial(_topk_kernel, cap=cap),
        grid=(b,),
        in_specs=[pl.BlockSpec((1, e, s), lambda i: (i, 0, 0))],
        out_specs=[pl.BlockSpec((1, e, cap), lambda i: (i, 0, 0)),
                   pl.BlockSpec((1, e, cap), lambda i: (i, 0, 0))],
        out_shape=[jax.ShapeDtypeStruct((b, e, cap), I32),
                   jax.ShapeDtypeStruct((b, e, cap), F32)],
        scratch_shapes=[pltpu.VMEM((3, s // LANES * e, LANES), F32)],
        compiler_params=_params(("parallel",)),
        name="moe_topk",
    )(aff)


def _gather_kernel(idx_ref, slab_ref, o_ref, rows_ref, *, cap, half, pitch, unroll):
    def body(j0, c):
        for u in range(unroll):
            j = j0 * unroll + u
            src = pl.multiple_of(idx_ref[0, 0, j] * pitch, pitch)
            rows_ref[pl.ds(pl.multiple_of(j * pitch, pitch), pitch), :] = slab_ref[pl.ds(src, pitch), :]
        return c

    lax.fori_loop(0, cap // unroll, body, 0)
    for s in range(half // LANES):
        word = rows_ref[pl.ds(s, cap, stride=pitch), :]
        lo = lax.bitcast_convert_type(word << 16, F32)
        hi = lax.bitcast_convert_type(word & jnp.uint32(0xFFFF0000), F32)
        o_ref[0, :, s * LANES:(s + 1) * LANES] = lo.astype(BF16)
        o_ref[0, :, half + s * LANES:half + (s + 1) * LANES] = hi.astype(BF16)


def moe_gather(idx, slabs, *, seq, d):
    b, e, cap = idx.shape
    half = d // 2
    pitch = _slab_pitch(half)
    return pl.pallas_call(
        functools.partial(_gather_kernel, cap=cap, half=half, pitch=pitch, unroll=_tile(cap, 32)),
        grid=(b, e),
        in_specs=[
            pl.BlockSpec((1, 1, cap), lambda bi, ei: (bi * e + ei, 0, 0), memory_space=pltpu.SMEM),
            pl.BlockSpec((seq * pitch, LANES), lambda bi, ei: (bi, 0)),
        ],
        out_specs=pl.BlockSpec((1, cap, d), lambda bi, ei: (ei, bi, 0)),
        out_shape=jax.ShapeDtypeStruct((e, b * cap, d), BF16),
        scratch_shapes=[pltpu.VMEM((cap * pitch, LANES), U32)],
        compiler_params=_params(("parallel", "arbitrary")),
        name="moe_gather",
    )(idx.reshape(b * e, 1, cap), slabs)


def _ffn_kernel(xs_ref, w1_ref, w3_ref, w2_ref, gate_ref, o_ref, hid_ref, *, nt, tf):
    step = pl.program_id(1)

    @pl.when(step < nt)
    def _():
        xs = xs_ref[0]
        a = jnp.dot(xs, w1_ref[0, 0].astype(BF16), preferred_element_type=F32)
        g = jnp.dot(xs, w3_ref[0, 0].astype(BF16), preferred_element_type=F32)
        hid_ref[step] = (a * jax.nn.sigmoid(a) * g).astype(BF16)

    @pl.when(step >= nt)
    def _():
        hid = jnp.concatenate([hid_ref[k] for k in range(nt)], axis=1)
        y = jnp.dot(hid, w2_ref[0, 0].astype(BF16), preferred_element_type=F32)
        g_t = gate_ref[0].T
        o_ref[0] = jnp.concatenate([y[r * LANES:(r + 1) * LANES] * g_t[:, r:r + 1]
                                    for r in range(y.shape[0] // LANES)], axis=0)


def moe_ffn(xs, gate, w1, w3, w2, layer, *, tf=256, tn=512):
    e, m, d = xs.shape
    f = w1.shape[3]
    tf = _tile(f, tf)
    tn = _tile(d, tn)
    nt = f // tf
    nn = d // tn
    return pl.pallas_call(
        functools.partial(_ffn_kernel, nt=nt, tf=tf),
        grid=(e, nt + nn),
        in_specs=[
            pl.BlockSpec((1, m, d), lambda ei, t: (ei, 0, 0)),
            pl.BlockSpec((1, 1, d, tf), lambda ei, t: (layer, ei, 0, jnp.minimum(t, nt - 1))),
            pl.BlockSpec((1, 1, d, tf), lambda ei, t: (layer, ei, 0, jnp.minimum(t, nt - 1))),
            pl.BlockSpec((1, 1, f, tn), lambda ei, t: (layer, ei, 0, jnp.maximum(t - nt, 0))),
            pl.BlockSpec((1, m // LANES, LANES), lambda ei, t: (ei, 0, 0)),
        ],
        out_specs=pl.BlockSpec((1, m, tn), lambda ei, t: (ei, 0, jnp.maximum(t - nt, 0))),
        out_shape=jax.ShapeDtypeStruct((e, m, d), F32),
        scratch_shapes=[pltpu.VMEM((nt, m, tf), BF16)],
        compiler_params=_params(("parallel", "arbitrary")),
        name="moe_ffn",
    )(xs, w1, w3, w2, gate)


def _combine_kernel(idx_ref, ys_ref, o_ref, rows_ref, *, cap, half, pitch, unroll):
    ei = pl.program_id(2)

    @pl.when(ei == 0)
    def _():
        o_ref[...] = jnp.zeros(o_ref.shape, F32)

    for s in range(half // LANES):
        rows_ref[pl.ds(s, cap, stride=pitch), :] = ys_ref[0, :, s * LANES:(s + 1) * LANES]

    def body(j0, c):
        dsts, vals = [], []
        for u in range(unroll):
            j = j0 * unroll + u
            dst = pl.multiple_of(idx_ref[0, 0, j] * pitch, pitch)
            src = pl.multiple_of(j * pitch, pitch)
            vals.append(o_ref[0, 0, pl.ds(dst, pitch), :] + rows_ref[pl.ds(src, pitch), :])
            dsts.append(dst)
        for dst, val in zip(dsts, vals):
            o_ref[0, 0, pl.ds(dst, pitch), :] = val
        return c

    lax.fori_loop(0, cap // unroll, body, 0)


def moe_combine(idx, ys, *, seq, d):
    b, e, cap = idx.shape
    half = d // 2
    pitch = _slab_pitch(half)
    return pl.pallas_call(
        functools.partial(_combine_kernel, cap=cap, half=half, pitch=pitch, unroll=_tile(cap, 16)),
        grid=(b, 2, e),
        in_specs=[
            pl.BlockSpec((1, 1, cap), lambda bi, dh, ei: (bi * e + ei, 0, 0), memory_space=pltpu.SMEM),
            pl.BlockSpec((1, cap, half), lambda bi, dh, ei: (ei, bi, dh)),
        ],
        out_specs=pl.BlockSpec((1, 1, seq * pitch, LANES), lambda bi, dh, ei: (bi, dh, 0, 0)),
        out_shape=jax.ShapeDtypeStruct((b, 2, seq * pitch, LANES), F32),
        scratch_shapes=[pltpu.VMEM((cap * pitch, LANES), F32)],
        compiler_params=_params(("parallel", "parallel", "arbitrary")),
        name="moe_combine",
    )(idx.reshape(b * e, 1, cap), ys)


def _unslab(s0_ref, s1_ref, *, tm, half, pitch):
    parts = []
    for s_ref in (s0_ref, s1_ref):
        for s in range(half // LANES):
            parts.append(s_ref[0, 0, pl.ds(s, tm, stride=pitch), :])
    return jnp.concatenate(parts, axis=1)


def _unslab_kernel(*refs, tm, half, pitch, final_norm):
    x_ref, s0_ref, s1_ref = refs[:3]
    o_ref = refs[-1]
    y = x_ref[0] + _unslab(s0_ref, s1_ref, tm=tm, half=half, pitch=pitch)
    if final_norm:
        y = _rms(y, refs[3][...])
    o_ref[0] = y


def moe_residual(x, moe_slabs, final_g=None, *, tm=256):
    b, s, d = x.shape
    half = d // 2
    pitch = _slab_pitch(half)
    tm = _tile(s, tm)
    final_norm = final_g is not None
    in_specs = [
        pl.BlockSpec((1, tm, d), lambda bi, j: (bi, j, 0)),
        pl.BlockSpec((1, 1, tm * pitch, LANES), lambda bi, j: (bi, 0, j, 0)),
        pl.BlockSpec((1, 1, tm * pitch, LANES), lambda bi, j: (bi, 1, j, 0)),
    ]
    args = [x, moe_slabs, moe_slabs]
    if final_norm:
        in_specs.append(pl.BlockSpec((1, d), lambda bi, j: (0, 0)))
        args.append(final_g.reshape(1, d))
    return pl.pallas_call(
        functools.partial(_unslab_kernel, tm=tm, half=half, pitch=pitch, final_norm=final_norm),
        grid=(b, s // tm),
        in_specs=in_specs,
        out_specs=pl.BlockSpec((1, tm, d), lambda bi, j: (bi, j, 0)),
        out_shape=jax.ShapeDtypeStruct((b, s, d), F32),
        compiler_params=_params(("parallel", "parallel")),
        name="moe_residual",
    )(*args)


def expert_choice_ffn(aff, slabs, w1, w3, w2, layer, *, seq, d):
    cap = CAPACITY_FACTOR * seq // N_EXPERTS
    idx, gate = moe_topk(aff, cap)
    b, e, _ = idx.shape
    xs = moe_gather(idx, slabs, seq=seq, d=d)
    ys = moe_ffn(xs, gate.transpose(1, 0, 2).reshape(e, b * cap // LANES, LANES), w1, w3, w2, layer)
    return moe_combine(idx, ys, seq=seq, d=d)


def kernel(x, rel_bias, mix_norm, ffn_norm, final_norm, even_w_in, attn_sink, ssm_a_re, ssm_a_im, ssm_log_dt, ssm_b_re, ssm_b_im, ssm_c_re, ssm_c_im, ssm_d, glu_w, glu_b, even_w_out, odd_w_in, sgu_ln_g, sgu_ln_b, sgu_w, sgu_b, odd_w_out, router, moe_w1, moe_w3, moe_w2):
    b, s, d = x.shape
    t = b * s
    depth = mix_norm.shape[0]
    ssm_w = ssm_d.shape[-1]
    attn_w = even_w_out.shape[1] - ssm_w
    kv_w = (even_w_in.shape[-1] - attn_w - ssm_w) // 2
    bias = attention_bias(rel_bias)

    x2 = x.reshape(t, d)
    moe = None
    for layer in range(depth):
        i = layer // 2
        if layer % 2 == 0:
            if moe is not None:
                x2 = moe_residual(x2.reshape(b, s, d), moe).reshape(t, d)
            qkv, u = in_projection(x2, mix_norm[layer], even_w_in[i].astype(BF16), attn_w + 2 * kv_w)
            attn = windowed_attention(qkv.reshape(b, s, -1), attn_sink[i], bias,
                                      attn_w=attn_w, kv_w=kv_w)
            n_levels = max(1, (s // S5_CHUNK - 1).bit_length())
            operands = _s5_operands(ssm_a_re[i], ssm_a_im[i], ssm_log_dt[i], ssm_b_re[i], ssm_b_im[i],
                                    ssm_c_re[i], ssm_c_im[i], S5_CHUNK, n_levels)
            y = s5_scan(u, operands, seq=s, chunk=S5_CHUNK, n_levels=n_levels)
            ssm = s5_glu(y, u, ssm_d[i], glu_w[i].astype(BF16), glu_b[i])
            mixed, w_out = [attn.reshape(t, attn_w), ssm], even_w_out[i]
        else:
            gated, x2 = gmlp_gate(x2, moe, mix_norm[layer], odd_w_in[i].astype(BF16), sgu_ln_g[i],
                                  sgu_ln_b[i], sgu_w[i].astype(BF16), sgu_b[i], seq=s)
            mixed, w_out = [gated], odd_w_out[i]
        x2, aff, slabs = matmul_residual_route(mixed, w_out.astype(BF16), x2, ffn_norm[layer],
                                               router[layer], seq=s)
        moe = expert_choice_ffn(aff, slabs, moe_w1, moe_w3, moe_w2, layer, seq=s, d=d)
    return moe_residual(x2.reshape(b, s, d), moe, final_norm)
```

```python
import functools
import math

import jax
import jax.numpy as jnp
from jax import lax
from jax.experimental import pallas as pl
from jax.experimental.pallas import tpu as pltpu

F32 = jnp.float32
BF16 = jnp.bfloat16
I32 = jnp.int32
U32 = jnp.uint32

EPS = 1e-6
NEG_INF = -1e30

LANES = 128
SUBLANES = 8
VMEM_LIMIT = 56 << 20

ATTN_BLOCK = 128
ATTN_ROW_CHUNK = 32
HEAD_DIM = 128
KV_GROUP = 4
REL_BUCKETS = 32
REL_MAX_DIST = 128
S5_CHUNK = 32
S5_PITCH = 136
GMLP_CHUNK = 128
N_EXPERTS = 16
CAPACITY_FACTOR = 2


def _params(sem, vmem=VMEM_LIMIT):
    return pltpu.CompilerParams(dimension_semantics=sem, vmem_limit_bytes=vmem)


def _tile(n, want):
    t = min(n, want)
    while n % t:
        t //= 2
    return t


def _rms(x, g):
    ms = jnp.mean(x * x, axis=-1, keepdims=True)
    return x * lax.rsqrt(ms + EPS) * g


def _gelu(x):
    c = math.sqrt(2.0 / math.pi)
    return x * (0.5 * (1.0 + jnp.tanh(c * (x + 0.044715 * (x * x * x)))))


def _in_proj_kernel(x_ref, g_ref, w_ref, qkv_ref, u_ref):
    hn = _rms(x_ref[...], g_ref[...]).astype(BF16)
    y = jnp.dot(hn, w_ref[...], preferred_element_type=F32)
    split = qkv_ref.shape[1]
    qkv_ref[...] = y[:, :split]
    u_ref[...] = y[:, split:]


def in_projection(x, g, w_bf16, split, *, tm=512):
    m, k = x.shape
    n = w_bf16.shape[1]
    tm = _tile(m, tm)
    return pl.pallas_call(
        _in_proj_kernel,
        grid=(m // tm,),
        in_specs=[
            pl.BlockSpec((tm, k), lambda i: (i, 0)),
            pl.BlockSpec((1, k), lambda i: (0, 0)),
            pl.BlockSpec((k, n), lambda i: (0, 0), pipeline_mode=pl.Buffered(1)),
        ],
        out_specs=[pl.BlockSpec((tm, split), lambda i: (i, 0)),
                   pl.BlockSpec((tm, n - split), lambda i: (i, 0))],
        out_shape=[jax.ShapeDtypeStruct((m, split), F32),
                   jax.ShapeDtypeStruct((m, n - split), F32)],
        compiler_params=_params(("parallel",)),
        name="in_projection",
    )(x, g.reshape(1, k), w_bf16)


def _mm_res_route_kernel(*refs, n_lhs, tm, half, pitch):
    lhs = refs[:n_lhs]
    ws = refs[n_lhs:2 * n_lhs]
    res_ref, g_ref, rt_ref, o_ref, aff_ref, slab_ref = refs[2 * n_lhs:]
    acc = res_ref[...]
    for l_ref, w_ref in zip(lhs, ws):
        acc = acc + jnp.dot(l_ref[...], w_ref[...], preferred_element_type=F32)
    o_ref[...] = acc
    _route_rows(acc, g_ref, rt_ref, aff_ref, slab_ref, tm=tm, half=half, pitch=pitch)


def matmul_residual_route(lhs_list, w_bf16, res, norm_g, router, *, seq, tm=512):
    m, d = res.shape
    kk = lhs_list[0].shape[1]
    n_lhs = len(lhs_list)
    e = router.shape[1]
    assert all(l.shape == (m, kk) for l in lhs_list) and w_bf16.shape == (n_lhs * kk, d)
    tm = _tile(seq, tm)
    nt = seq // tm
    half = d // 2
    pitch = _slab_pitch(half)
    in_specs = [pl.BlockSpec((tm, kk), lambda i: (i, 0)) for _ in lhs_list]
    in_specs += [pl.BlockSpec((kk, d), functools.partial(lambda i, r: (r, 0), r=r),
                              pipeline_mode=pl.Buffered(1)) for r in range(n_lhs)]
    in_specs += [pl.BlockSpec((tm, d), lambda i: (i, 0)),
                 pl.BlockSpec((1, d), lambda i: (0, 0)),
                 pl.BlockSpec((e, d), lambda i: (0, 0))]
    return pl.pallas_call(
        functools.partial(_mm_res_route_kernel, n_lhs=n_lhs, tm=tm, half=half, pitch=pitch),
        grid=(m // tm,),
        in_specs=in_specs,
        out_specs=[pl.BlockSpec((tm, d), lambda i: (i, 0)),
                   pl.BlockSpec((1, e, tm), lambda i: (i // nt, 0, i % nt)),
                   pl.BlockSpec((tm * pitch, LANES), lambda i: (i, 0))],
        out_shape=[jax.ShapeDtypeStruct((m, d), F32),
                   jax.ShapeDtypeStruct((m // seq, e, seq), F32),
                   jax.ShapeDtypeStruct((m * pitch, LANES), U32)],
        compiler_params=_params(("parallel",)),
        name="matmul_residual_route",
    )(*lhs_list, *([w_bf16] * n_lhs), res, norm_g.reshape(1, d), router.T.astype(F32))


def _t5_bucket(rel):
    nb = REL_BUCKETS // 2
    max_exact = nb // 2
    base = jnp.where(rel > 0, nb, 0)
    n = jnp.abs(rel)
    nf = jnp.maximum(n, 1).astype(F32)
    large = max_exact + (jnp.log(nf / max_exact) / math.log(REL_MAX_DIST / max_exact)
                         * (nb - max_exact)).astype(I32)
    large = jnp.minimum(large, nb - 1)
    return base + jnp.where(n < max_exact, n, large)


def _bias_kernel(rb_ref, bucket_ref, o_ref, *, n_heads):
    bkt = bucket_ref[...]
    row = lax.broadcasted_iota(I32, bkt.shape, 0)
    col = lax.broadcasted_iota(I32, bkt.shape, 1)
    in_window = jnp.abs(col - ATTN_BLOCK - row) <= ATTN_BLOCK
    for h in range(n_heads):
        acc = jnp.zeros(bkt.shape, F32)
        for k in range(REL_BUCKETS):
            acc = jnp.where(bkt == k, rb_ref[k, h], acc)
        o_ref[h * ATTN_BLOCK:(h + 1) * ATTN_BLOCK, :] = jnp.where(in_window, acc, NEG_INF)


def attention_bias(rel_bias):
    n_heads = rel_bias.shape[1]
    q_off = jnp.arange(ATTN_BLOCK, dtype=I32)
    c_off = jnp.arange(3 * ATTN_BLOCK, dtype=I32)
    bucket = _t5_bucket(c_off[None, :] - ATTN_BLOCK - q_off[:, None])
    return pl.pallas_call(
        functools.partial(_bias_kernel, n_heads=n_heads),
        in_specs=[pl.BlockSpec(memory_space=pltpu.SMEM),
                  pl.BlockSpec(memory_space=pltpu.VMEM)],
        out_specs=pl.BlockSpec(memory_space=pltpu.VMEM),
        out_shape=jax.ShapeDtypeStruct((n_heads * ATTN_BLOCK, 3 * ATTN_BLOCK), F32),
        name="attention_bias",
    )(rel_bias.astype(F32), bucket)


def _attn_kernel(sink_ref, q_ref, kp_ref, kc_ref, kn_ref, vp_ref, vc_ref, vn_ref, bias_ref,
                 o_ref, s_ref, p_ref, *, seq, n_kv, qb):
    blk = ATTN_BLOCK
    col = lax.broadcasted_iota(I32, (1, 3 * blk), 1)
    scale = HEAD_DIM ** -0.5
    rc = ATTN_ROW_CHUNK

    def band(p_ref, c_ref, n_ref, ksl, j):
        parts = []
        for i in (j - 1, j, j + 1):
            if i < 0:
                parts.append(p_ref[0, :, ksl])
            elif i >= qb:
                parts.append(n_ref[0, :, ksl])
            else:
                parts.append(c_ref[0, i * blk:(i + 1) * blk, ksl])
        return jnp.concatenate(parts, axis=0).astype(BF16)

    for j in range(qb):
        kpos = (pl.program_id(1) * qb + j - 1) * blk + col
        in_seq = (kpos >= 0) & (kpos < seq)
        rows = slice(j * blk, (j + 1) * blk)
        for kh in range(n_kv):
            ksl = slice(kh * HEAD_DIM, (kh + 1) * HEAD_DIM)
            kband = band(kp_ref, kc_ref, kn_ref, ksl, j)
            vband = band(vp_ref, vc_ref, vn_ref, ksl, j)
            heads = [kh * KV_GROUP + g for g in range(KV_GROUP)]
            qs = jnp.concatenate([q_ref[0, rows, h * HEAD_DIM:(h + 1) * HEAD_DIM] for h in heads],
                                 axis=0).astype(BF16)
            s_ref[...] = lax.dot_general(qs, kband, (((1,), (1,)), ((), ())),
                                         preferred_element_type=F32)

            def softmax_rows(c, carry, kh=kh, in_seq=in_seq):
                r0 = pl.multiple_of(c * rc, rc)
                bias = bias_ref[pl.ds(kh * KV_GROUP * blk + r0, rc), :]
                s = jnp.where(in_seq, s_ref[pl.ds(r0, rc), :] * scale + bias, NEG_INF)
                sink = sink_ref[kh * KV_GROUP + c // (blk // rc)]
                m = jnp.maximum(jnp.max(s, axis=-1, keepdims=True), sink)
                p = jnp.exp(s - m)
                denom = jnp.sum(p, axis=-1, keepdims=True) + jnp.exp(sink - m)
                p_ref[pl.ds(r0, rc), :] = (p / denom).astype(BF16)
                return carry

            lax.fori_loop(0, KV_GROUP * blk // rc, softmax_rows, 0, unroll=True)
            o = jnp.dot(p_ref[...], vband, preferred_element_type=F32)
            for g, h in enumerate(heads):
                o_ref[0, rows, h * HEAD_DIM:(h + 1) * HEAD_DIM] = (
                    o[g * blk:(g + 1) * blk].astype(o_ref.dtype))


def windowed_attention(qkv, sink, bias, *, attn_w, kv_w, qb=4):
    b, s, _ = qkv.shape
    blk = ATTN_BLOCK
    nblk = s // blk
    qb = _tile(nblk, qb)
    n_kv = kv_w // HEAD_DIM
    kcol = attn_w // kv_w
    assert attn_w % kv_w == 0 and n_kv * KV_GROUP * HEAD_DIM == attn_w

    def edge_spec(col, first):
        return pl.BlockSpec(
            (1, blk, kv_w),
            lambda bi, n: (bi, jnp.clip(n * qb + (-1 if first else qb), 0, nblk - 1), col))

    def mid_spec(col):
        return pl.BlockSpec((1, qb * blk, kv_w), lambda bi, n: (bi, n, col))

    return pl.pallas_call(
        functools.partial(_attn_kernel, seq=s, n_kv=n_kv, qb=qb),
        grid=(b, nblk // qb),
        in_specs=[
            pl.BlockSpec(memory_space=pltpu.SMEM),
            pl.BlockSpec((1, qb * blk, attn_w), lambda bi, n: (bi, n, 0)),
            edge_spec(kcol, True), mid_spec(kcol), edge_spec(kcol, False),
            edge_spec(kcol + 1, True), mid_spec(kcol + 1), edge_spec(kcol + 1, False),
            pl.BlockSpec(bias.shape, lambda bi, n: (0, 0)),
        ],
        out_specs=pl.BlockSpec((1, qb * blk, attn_w), lambda bi, n: (bi, n, 0)),
        out_shape=jax.ShapeDtypeStruct((b, s, attn_w), BF16),
        scratch_shapes=[pltpu.VMEM((KV_GROUP * blk, 3 * blk), F32),
                        pltpu.VMEM((KV_GROUP * blk, 3 * blk), BF16)],
        compiler_params=_params(("parallel", "parallel")),
        name="windowed_attention",
    )(sink.astype(F32), qkv, qkv, qkv, qkv, qkv, qkv, qkv, bias)


def _s5_operands(a_re, a_im, log_dt, b_re, b_im, c_re, c_im, chunk, n_levels):
    ell = chunk
    p = a_re.shape[-1]
    dt = jnp.exp(log_dt)[..., None]
    mag = jnp.exp(a_re * dt)
    lb_re = mag * jnp.cos(a_im * dt)
    lb_im = mag * jnp.sin(a_im * dt)
    den = a_re * a_re + a_im * a_im
    nr = lb_re - 1.0
    coef_re = (nr * a_re + lb_im * a_im) / den
    coef_im = (lb_im * a_re - nr * a_im) / den
    bb_re = coef_re[..., None] * b_re - coef_im[..., None] * b_im
    bb_im = coef_re[..., None] * b_im + coef_im[..., None] * b_re
    bt_re, bt_im = bb_re.transpose(0, 1, 3, 2), bb_im.transpose(0, 1, 3, 2)

    def powers(tau):
        t = tau.astype(F32)[None, None, :, None]
        pm = jnp.exp((a_re * dt)[:, :, None, :] * t)
        ang = (a_im * dt)[:, :, None, :] * t
        return pm * jnp.cos(ang), pm * jnp.sin(ang)

    g = a_re.shape[1]
    assert 2 * p == LANES and g % 2 == 0
    odd = (jnp.arange(g) % 2 == 1)[:, None, None]
    cat = lambda *ts: jnp.concatenate(ts, axis=-1)
    lp = lambda t: jnp.pad(t, [(0, 0)] * (t.ndim - 1) + [(0, LANES - p)])
    sp = lambda t: jnp.where(odd, cat(jnp.zeros_like(t), t), cat(t, jnp.zeros_like(t)))
    ii = jnp.arange(ell)
    dn_re, dn_im = powers(ell - 1 - ii)
    up_re, up_im = powers(ii)
    e1_re, e1_im = powers(ii + 1)
    e2_re, e2_im = powers(ell - ii)
    lv_re, lv_im = powers(ell * (2 ** jnp.arange(n_levels)))
    lev = SUBLANES * pl.cdiv(n_levels, SUBLANES)
    lvp = lambda t: jnp.pad(sp(t), ((0, 0), (0, lev - n_levels), (0, 0)))
    wide = jnp.concatenate([
        cat(sp(dn_re[0]), sp(dn_re[0]), sp(up_re[1]), sp(up_re[1])),
        cat(sp(dn_im[0]), sp(dn_im[0]), sp(up_im[1]), sp(up_im[1])),
        cat(sp(e1_re[0]), sp(e1_re[0]), sp(e2_re[1]), sp(e2_re[1])),
        cat(sp(e1_im[0]), sp(e1_im[0]), sp(e2_im[1]), sp(e2_im[1])),
        cat(sp(bt_re[0]), sp(bt_im[0]), sp(bt_re[1]), sp(bt_im[1])),
        cat(-sp(bt_im[0]), sp(bt_re[0]), -sp(bt_im[1]), sp(bt_re[1])),
        cat(sp(c_re[0]), -sp(c_im[0]), sp(c_re[1]), -sp(c_im[1])),
        cat(-sp(c_im[0]), -sp(c_re[0]), -sp(c_im[1]), -sp(c_re[1])),
        cat(lvp(lv_re[0]), lvp(lv_re[0]), lvp(lv_re[1]), lvp(lv_re[1])),
        cat(-lvp(lv_im[0]), lvp(lv_im[0]), -lvp(lv_im[1]), lvp(lv_im[1])),
    ], axis=1)

    def lag_rows(fwd, bwd):
        f = jnp.pad(lp(fwd), ((0, 0), (ell - 1, 1), (0, 0)))
        b = jnp.pad(lp(bwd), ((0, 0), (0, ell), (0, 0)))
        return cat(f, b)

    narrow = jnp.concatenate([
        lag_rows(up_re[0], dn_re[1]), lag_rows(up_im[0], dn_im[1]),
        cat(lp(c_re[0]), lp(c_re[1])), cat(lp(c_im[0]), lp(c_im[1])),
        cat(lp(bt_re[0]), lp(bt_re[1])), cat(lp(bt_im[0]), lp(bt_im[1])),
    ], axis=1)
    return wide, narrow


def _s5_pack_kernel(u_ref, o_ref, t_ref, *, n_lt, h, chunk):
    def transpose_block(lt, c):
        src = pl.multiple_of(lt * LANES, LANES)
        dst = pl.multiple_of(lt * S5_PITCH, SUBLANES)
        t_ref[pl.ds(dst, LANES), :] = u_ref[pl.ds(src, LANES), :].T
        return c

    lax.fori_loop(0, n_lt, transpose_block, 0, unroll=8)
    n_sub = LANES // chunk

    def pack_group(g8, c):
        for kt in range(h // n_sub):
            ms = [t_ref[pl.ds(g8 * h + kt * n_sub + kk, n_lt, stride=S5_PITCH), :] for kk in range(n_sub)]
            for c_lo in range(n_sub):
                tile = jnp.concatenate([m[:, c_lo * chunk:(c_lo + 1) * chunk] for m in ms], axis=1)
                o_ref[g8, c_lo * n_lt:(c_lo + 1) * n_lt, kt * LANES:(kt + 1) * LANES] = tile.astype(BF16)
        return c

    lax.fori_loop(0, LANES // h, pack_group, 0)


def _s5_unpack_kernel(y_ref, o_ref, t_ref, *, n_lt, h, chunk):
    n_sub = LANES // chunk

    def unpack_group(g8, c):
        for hh in range(h):
            lane0 = (hh // n_sub) * LANES + (hh % n_sub) * chunk
            m = jnp.concatenate([y_ref[g8, c_lo * n_lt:(c_lo + 1) * n_lt, lane0:lane0 + chunk]
                                 for c_lo in range(n_sub)], axis=1)
            t_ref[pl.ds(g8 * h + hh, n_lt, stride=S5_PITCH), :] = m
        return c

    lax.fori_loop(0, LANES // h, unpack_group, 0)

    def transpose_block(lt, c):
        src = pl.multiple_of(lt * S5_PITCH, SUBLANES)
        dst = pl.multiple_of(lt * LANES, LANES)
        o_ref[pl.ds(dst, LANES), :] = t_ref[pl.ds(src, LANES), :].T
        return c

    lax.fori_loop(0, n_lt, transpose_block, 0, unroll=8)


def _shift_rows(z, r, seg):
    if r == 0:
        return z
    rows = z.shape[0]
    rolled = pltpu.roll(z, r % rows, 0)
    assert seg & (seg - 1) == 0
    pos = lax.broadcasted_iota(I32, z.shape, 0) & (seg - 1)
    keep = (pos >= r) if r > 0 else (pos < seg + r)
    return jnp.where(keep, rolled, 0.0)


def _shift_chunks(blocks, d, seg):
    n_sub = len(blocks)
    out = []
    for c_lo in range(n_sub):
        e, s_lo = divmod(c_lo - d, n_sub)
        out.append(_shift_rows(blocks[s_lo], -e, seg))
    return out


def _s5_group_tables(w, nr, perm, *, n_levels, chunk, h):
    ell, lh = chunk, chunk * h
    lev = SUBLANES * pl.cdiv(n_levels, SUBLANES)
    nt = (((1,), (1,)), ((), ()))
    pwf_re, pwf_im, pwe_re, pwe_im = (w[k * ell:(k + 1) * ell] for k in range(4))
    o = 4 * ell
    bb1, bb2, cc1, cc2 = (w[o + k * h:o + (k + 1) * h] for k in range(4))
    o += 4 * h
    pa_all, pb_all = w[o:o + lev], w[o + lev:o + 2 * lev]
    f_tab = jnp.concatenate([pwf_re * bb1[k:k + 1, :] + pwf_im * bb2[k:k + 1, :] for k in range(h)],
                            axis=0).astype(BF16)
    et_tab = jnp.concatenate([pwe_re * cc1[k:k + 1, :] + pwe_im * cc2[k:k + 1, :] for k in range(h)],
                             axis=0).astype(BF16)
    pwm_re, pwm_im = nr[:2 * ell], nr[2 * ell:4 * ell]
    o = 4 * ell
    cr, ci, br, bi = (nr[o + k * h:o + (k + 1) * h] for k in range(4))
    rep = lambda x: jnp.concatenate([jnp.broadcast_to(x[j:j + 1, :], (h, x.shape[1]))
                                     for j in range(x.shape[0])], axis=0)
    tile = lambda t: jnp.concatenate([t] * (2 * ell), axis=0)
    pr, pi = rep(pwm_re), rep(pwm_im)
    crt, cit = tile(cr), tile(ci)
    a = jnp.concatenate([pr * crt - pi * cit, pr * cit + pi * crt], axis=1)
    b = jnp.concatenate([br, -bi], axis=1)
    a1 = a.astype(BF16)
    a2 = (a - a1.astype(F32)).astype(BF16)
    b1 = b.astype(BF16)
    b2 = (b - b1.astype(F32)).astype(BF16)
    lead = lax.dot_general(jnp.concatenate([b1, b2], axis=0), a1, nt, preferred_element_type=F32)
    strip = lead[:h] + lead[h:] + lax.dot_general(b1, a2, nt, preferred_element_type=F32)
    t_jk = jnp.concatenate([strip[:, (ell - 1 - j) * h:(ell - 1 - j) * h + lh] for j in range(ell)],
                           axis=0).astype(BF16)
    t_tab = lax.dot_general(jnp.dot(perm, t_jk, preferred_element_type=F32).astype(BF16), perm, nt,
                            preferred_element_type=F32).astype(BF16)
    return f_tab, et_tab, t_tab, pa_all, pb_all


def _s5_kernel(u_ref, wide_ref, narrow_ref, perm_ref, o_ref, *, seg, n_levels, chunk, h):
    n_sub = LANES // chunk
    nt = (((1,), (1,)), ((), ()))
    perm = perm_ref[...]
    tabs = [_s5_group_tables(wide_ref[gi], narrow_ref[gi], perm, n_levels=n_levels, chunk=chunk, h=h)
            for gi in range(2)]
    us = [u_ref[gi] for gi in range(2)]
    z = sum(jnp.dot(u, t[0], preferred_element_type=F32) for u, t in zip(us, tabs))
    pa_all = tabs[0][3] + tabs[1][3]
    pb_all = tabs[0][4] + tabs[1][4]
    n_lt = z.shape[0] // n_sub
    half = 2 * LANES

    def swap(x):
        return jnp.concatenate([x[:, LANES:], x[:, :LANES]], axis=1)

    zf = [z[c * n_lt:(c + 1) * n_lt, :half] for c in range(n_sub)]
    zb = [z[c * n_lt:(c + 1) * n_lt, half:] for c in range(n_sub)]
    for k in range(n_levels):
        pa = pa_all[k:k + 1, :]
        pb = pb_all[k:k + 1, :]
        sf = _shift_chunks(zf, 2 ** k, seg)
        sb = _shift_chunks(zb, -(2 ** k), seg)
        zf = [a + pa[:, :half] * s + pb[:, :half] * swap(s) for a, s in zip(zf, sf)]
        zb = [a + pa[:, half:] * s + pb[:, half:] * swap(s) for a, s in zip(zb, sb)]
    hin = jnp.concatenate([jnp.concatenate(_shift_chunks(zf, 1, seg), axis=0),
                           jnp.concatenate(_shift_chunks(zb, -1, seg), axis=0)], axis=1).astype(BF16)
    for gi in range(2):
        y = jnp.dot(us[gi], tabs[gi][2], preferred_element_type=F32)
        o_ref[gi] = y + lax.dot_general(hin, tabs[gi][1], nt, preferred_element_type=F32)


def s5_scan(u, operands, *, seq, chunk, n_levels):
    wide, narrow = operands
    t, w = u.shape
    g = wide.shape[0]
    h = w // g
    lh = chunk * h
    n_lt = t // LANES
    rows = t // chunk
    gpt = LANES // h
    assert t % LANES == 0 and seq % LANES == 0 and LANES % chunk == 0 and h % (LANES // chunk) == 0
    scratch = pltpu.VMEM((n_lt * S5_PITCH, LANES), F32)
    ug = pl.pallas_call(
        functools.partial(_s5_pack_kernel, n_lt=n_lt, h=h, chunk=chunk),
        grid=(w // LANES,),
        in_specs=[pl.BlockSpec((t, LANES), lambda q: (0, q))],
        out_specs=pl.BlockSpec((gpt, rows, lh), lambda q: (q, 0, 0)),
        out_shape=jax.ShapeDtypeStruct((g, rows, lh), BF16),
        scratch_shapes=[scratch],
        compiler_params=_params(("parallel",)),
        name="s5_pack",
    )(u)
    kj = jnp.arange(lh)
    perm = (kj[:, None] // chunk + (kj[:, None] % chunk) * h == kj[None, :]).astype(BF16)
    yg = pl.pallas_call(
        functools.partial(_s5_kernel, seg=seq // LANES, n_levels=n_levels, chunk=chunk, h=h),
        grid=(g // 2,),
        in_specs=[
            pl.BlockSpec((2, rows, lh), lambda i: (i, 0, 0)),
            pl.BlockSpec((2,) + wide.shape[1:], lambda i: (i, 0, 0)),
            pl.BlockSpec((2,) + narrow.shape[1:], lambda i: (i, 0, 0)),
            pl.BlockSpec((lh, lh), lambda i: (0, 0)),
        ],
        out_specs=pl.BlockSpec((2, rows, lh), lambda i: (i, 0, 0)),
        out_shape=jax.ShapeDtypeStruct((g, rows, lh), F32),
        compiler_params=_params(("parallel",)),
        name="s5_scan",
    )(ug, wide, narrow, perm)
    return pl.pallas_call(
        functools.partial(_s5_unpack_kernel, n_lt=n_lt, h=h, chunk=chunk),
        grid=(w // LANES,),
        in_specs=[pl.BlockSpec((gpt, rows, lh), lambda q: (q, 0, 0))],
        out_specs=pl.BlockSpec((t, LANES), lambda q: (0, q)),
        out_shape=jax.ShapeDtypeStruct((t, w), F32),
        scratch_shapes=[scratch],
        compiler_params=_params(("parallel",)),
        name="s5_unpack",
    )(yg)


def _glu_kernel(y_ref, u_ref, d_ref, w_ref, b_ref, o_ref):
    y = y_ref[...] + d_ref[...] * u_ref[...]
    g = _gelu(y)
    z = jnp.dot(g.astype(BF16), w_ref[...], preferred_element_type=F32) + b_ref[...]
    o_ref[...] = (g * jax.nn.sigmoid(z)).astype(o_ref.dtype)


def s5_glu(y, u, d_skip, glu_w_bf16, glu_b, *, tm=1024):
    m, w = y.shape
    tm = _tile(m, tm)
    return pl.pallas_call(
        _glu_kernel,
        grid=(m // tm,),
        in_specs=[
            pl.BlockSpec((tm, w), lambda i: (i, 0)),
            pl.BlockSpec((tm, w), lambda i: (i, 0)),
            pl.BlockSpec((1, w), lambda i: (0, 0)),
            pl.BlockSpec((w, w), lambda i: (0, 0)),
            pl.BlockSpec((1, w), lambda i: (0, 0)),
        ],
        out_specs=pl.BlockSpec((tm, w), lambda i: (i, 0)),
        out_shape=jax.ShapeDtypeStruct((m, w), BF16),
        compiler_params=_params(("parallel",)),
        name="s5_glu",
    )(y, u, d_skip.reshape(1, w), glu_w_bf16, glu_b.reshape(1, w))


def _gmlp_kernel(x_ref, s0_ref, s1_ref, ng_ref, w_ref, g_ref, b_ref, ws_ref, bs_ref, o_ref, xn_ref, *,
                 width, n_heads, tm, half, pitch):
    hd = width // n_heads
    ck = GMLP_CHUNK
    x = x_ref[...] + _unslab(s0_ref, s1_ref, tm=tm, half=half, pitch=pitch)
    xn_ref[...] = x
    hn = _rms(x, ng_ref[...]).astype(BF16)
    z = _gelu(jnp.dot(hn, w_ref[...], preferred_element_type=F32))
    u, v = z[:, :width], z[:, width:]
    mu = jnp.mean(v, axis=-1, keepdims=True)
    var = jnp.mean(jnp.square(v - mu), axis=-1, keepdims=True)
    vn = ((v - mu) * lax.rsqrt(var + EPS) * g_ref[...] + b_ref[...]).astype(BF16)
    for c in range(tm // ck):
        rows = slice(c * ck, (c + 1) * ck)
        for h in range(n_heads):
            sl = slice(h * hd, (h + 1) * hd)
            mixed = jnp.dot(ws_ref[h], vn[rows, sl], preferred_element_type=F32) + bs_ref[:, h:h + 1]
            o_ref[rows, sl] = (u[rows, sl] * mixed).astype(o_ref.dtype)


def gmlp_gate(x, moe_slabs, norm_g, w_in_bf16, ln_g, ln_b, w_s_bf16, b_s, *, seq, tm=512):
    m, d = x.shape
    w2 = w_in_bf16.shape[1]
    width = w2 // 2
    n_heads = w_s_bf16.shape[0]
    ck = GMLP_CHUNK
    tm = max(ck, _tile(seq, tm))
    nt = seq // tm
    half = d // 2
    pitch = _slab_pitch(half)
    return pl.pallas_call(
        functools.partial(_gmlp_kernel, width=width, n_heads=n_heads, tm=tm, half=half, pitch=pitch),
        grid=(m // tm,),
        in_specs=[
            pl.BlockSpec((tm, d), lambda i: (i, 0)),
            pl.BlockSpec((1, 1, tm * pitch, LANES), lambda i: (i // nt, 0, i % nt, 0)),
            pl.BlockSpec((1, 1, tm * pitch, LANES), lambda i: (i // nt, 1, i % nt, 0)),
            pl.BlockSpec((1, d), lambda i: (0, 0)),
            pl.BlockSpec((d, w2), lambda i: (0, 0), pipeline_mode=pl.Buffered(1)),
            pl.BlockSpec((1, width), lambda i: (0, 0)),
            pl.BlockSpec((1, width), lambda i: (0, 0)),
            pl.BlockSpec((n_heads, ck, ck), lambda i: (0, 0, 0)),
            pl.BlockSpec((ck, n_heads), lambda i: (0, 0)),
        ],
        out_specs=[pl.BlockSpec((tm, width), lambda i: (i, 0)),
                   pl.BlockSpec((tm, d), lambda i: (i, 0))],
        out_shape=[jax.ShapeDtypeStruct((m, width), BF16),
                   jax.ShapeDtypeStruct((m, d), F32)],
        compiler_params=_params(("parallel",)),
        name="gmlp_gate",
    )(x, moe_slabs, moe_slabs, norm_g.reshape(1, d), w_in_bf16, ln_g.reshape(1, width),
      ln_b.reshape(1, width), w_s_bf16, b_s.T.astype(F32))


def _slab_pitch(half):
    return SUBLANES * pl.cdiv(half // LANES, SUBLANES)


def _route_rows(x, g_ref, rt_ref, aff_ref, slab_ref, *, tm, half, pitch):
    h = _rms(x, g_ref[...])
    nt = (((1,), (1,)), ((), ()))
    e = rt_ref.shape[0]
    h1 = h.astype(BF16)
    h1f = h1.astype(F32)
    h2 = (h - h1f).astype(BF16)
    r1 = rt_ref[...].astype(BF16)
    r2 = (rt_ref[...] - r1.astype(F32)).astype(BF16)
    lead = lax.dot_general(jnp.concatenate([r1, r2], axis=0), h1, nt, preferred_element_type=F32)
    logits = lead[:e] + lead[e:] + lax.dot_general(r1, h2, nt, preferred_element_type=F32)
    mx = jnp.max(logits, axis=0, keepdims=True)
    ex = jnp.exp(logits - mx)
    aff_ref[0] = ex / jnp.sum(ex, axis=0, keepdims=True)
    lo = lax.bitcast_convert_type(h1f[:, :half], U32) >> 16
    hi = lax.bitcast_convert_type(h1f[:, half:], U32) & jnp.uint32(0xFFFF0000)
    word = lo | hi
    for s in range(half // LANES):
        slab_ref[pl.ds(s, tm, stride=pitch), :] = word[:, s * LANES:(s + 1) * LANES]
    if pitch > half // LANES:
        for s in range(half // LANES, pitch):
            slab_ref[pl.ds(s, tm, stride=pitch), :] = jnp.zeros((tm, LANES), U32)


def _topk_kernel(aff_ref, idx_ref, gate_ref, blk_ref, *, cap):
    aff = aff_ref[0]
    e, s = aff.shape
    nb = s // LANES
    assert e & (e - 1) == 0
    bits = lax.bitcast_convert_type(aff, I32)

    def count_ge(t):
        return jnp.sum((bits >= t).astype(F32), axis=1, keepdims=True)

    def body(_, carry):
        lo, hi = carry
        mid = lo + ((hi - lo) >> 1)
        ok = count_ge(mid) >= cap
        return jnp.where(ok, mid, lo), jnp.where(ok, hi, mid)

    lo0 = jnp.zeros((e, 1), I32)
    hi0 = jnp.full((e, 1), 0x7F800000, I32)
    thr, _ = lax.fori_loop(0, 32, body, (lo0, hi0))
    gt = bits > thr
    eq = bits == thr
    need = cap - jnp.sum(gt.astype(F32), axis=1, keepdims=True)
    rows = nb * e
    by_block = lambda x: jnp.concatenate([x[:, k * LANES:(k + 1) * LANES] for k in range(nb)], axis=0)
    tri = (lax.broadcasted_iota(I32, (LANES, LANES), 0)
           <= lax.broadcasted_iota(I32, (LANES, LANES), 1)).astype(BF16)
    ri = lax.broadcasted_iota(I32, (rows, rows), 0)
    ci = lax.broadcasted_iota(I32, (rows, rows), 1)
    earlier = (((ci & (e - 1)) == (ri & (e - 1))) & (ci < ri)).astype(BF16)

    def counts(mask):
        within = jnp.dot(jnp.where(mask, 1.0, 0.0).astype(BF16), tri, preferred_element_type=F32)
        total = jnp.broadcast_to(within[:, LANES - 1:LANES], (rows, LANES)).astype(BF16)
        before = jnp.dot(earlier, total, preferred_element_type=F32)[:, :1]
        return within, before

    gt_r, eq_r = by_block(gt), by_block(eq)
    eq_in, eq_before = counts(eq_r)
    need_r = jnp.concatenate([need] * nb, axis=0)
    sel_r = gt_r | (eq_r & (eq_in + eq_before <= need_r))
    sel_in, sel_before = counts(sel_r)
    lane = lax.broadcasted_iota(I32, (rows, LANES), 1)
    blk_ref[0] = jnp.where(sel_r, sel_in, 0.0)
    blk_ref[1] = by_block(aff)
    blk_ref[2] = jnp.where(lane == 0, sel_before,
                           jnp.where(lane == 1, sel_before + sel_in[:, LANES - 1:LANES], 0.0))

    jrow = lax.broadcasted_iota(I32, (1, cap), 1).astype(F32)
    b_id = lax.broadcasted_iota(I32, (nb, cap), 0).astype(F32)
    l_id = lax.broadcasted_iota(I32, (LANES, cap), 0).astype(F32)
    for ei in range(e):
        rank_in = blk_ref[0, pl.ds(ei, nb, stride=e), :]
        aff_b = blk_ref[1, pl.ds(ei, nb, stride=e), :]
        cnt = blk_ref[2, pl.ds(ei, nb, stride=e), :]
        before, upto = cnt[:, 0:1], cnt[:, 1:2]
        blk = jnp.sum(jnp.where(upto <= jrow, 1.0, 0.0), axis=0, keepdims=True)
        onehot = b_id == blk
        rank = jrow - jnp.sum(jnp.where(onehot, before, 0.0), axis=0, keepdims=True) + 1.0
        oh = jnp.where(onehot, 1.0, 0.0).astype(BF16)
        ranks_t = jnp.dot(rank_in.T.astype(BF16), oh, preferred_element_type=F32)
        hit = ranks_t == rank
        a_t = aff_b.T
        g1 = a_t.astype(BF16)
        g2 = (a_t - g1.astype(F32)).astype(BF16)
        g3 = ((a_t - g1.astype(F32)) - g2.astype(F32)).astype(BF16)
        aff_t = (jnp.dot(g1, oh, preferred_element_type=F32) + jnp.dot(g2, oh, preferred_element_type=F32)
                 + jnp.dot(g3, oh, preferred_element_type=F32))
        local = jnp.sum(jnp.where(hit, l_id, 0.0), axis=0, keepdims=True)
        idx_ref[0, ei:ei + 1, :] = (blk * LANES + local).astype(I32)
        gate_ref[0, ei:ei + 1, :] = jnp.sum(jnp.where(hit, aff_t, 0.0), axis=0, keepdims=True)


def moe_topk(aff, cap):
    b, e, s = aff.shape
    return pl.pallas_call(
        functools.partial(_topk_kernel, cap=cap),
        grid=(b,),
        in_specs=[pl.BlockSpec((1, e, s), lambda i: (i, 0, 0))],
        out_specs=[pl.BlockSpec((1, e, cap), lambda i: (i, 0, 0)),
                   pl.BlockSpec((1, e, cap), lambda i: (i, 0, 0))],
        out_shape=[jax.ShapeDtypeStruct((b, e, cap), I32),
                   jax.ShapeDtypeStruct((b, e, cap), F32)],
        scratch_shapes=[pltpu.VMEM((3, s // LANES * e, LANES), F32)],
        compiler_params=_params(("parallel",)),
        name="moe_topk",
    )(aff)


def _gather_kernel(idx_ref, slab_ref, o_ref, rows_ref, *, cap, half, pitch, unroll):
    def body(j0, c):
        for u in range(unroll):
            j = j0 * unroll + u
            src = pl.multiple_of(idx_ref[0, 0, j] * pitch, pitch)
            rows_ref[pl.ds(pl.multiple_of(j * pitch, pitch), pitch), :] = slab_ref[pl.ds(src, pitch), :]
        return c

    lax.fori_loop(0, cap // unroll, body, 0)
    for s in range(half // LANES):
        word = rows_ref[pl.ds(s, cap, stride=pitch), :]
        lo = lax.bitcast_convert_type(word << 16, F32)
        hi = lax.bitcast_convert_type(word & jnp.uint32(0xFFFF0000), F32)
        o_ref[0, :, s * LANES:(s + 1) * LANES] = lo.astype(BF16)
        o_ref[0, :, half + s * LANES:half + (s + 1) * LANES] = hi.astype(BF16)


def moe_gather(idx, slabs, *, seq, d):
    b, e, cap = idx.shape
    half = d // 2
    pitch = _slab_pitch(half)
    return pl.pallas_call(
        functools.partial(_gather_kernel, cap=cap, half=half, pitch=pitch, unroll=_tile(cap, 32)),
        grid=(b, e),
        in_specs=[
            pl.BlockSpec((1, 1, cap), lambda bi, ei: (bi * e + ei, 0, 0), memory_space=pltpu.SMEM),
            pl.BlockSpec((seq * pitch, LANES), lambda bi, ei: (bi, 0)),
        ],
        out_specs=pl.BlockSpec((1, cap, d), lambda bi, ei: (ei, bi, 0)),
        out_shape=jax.ShapeDtypeStruct((e, b * cap, d), BF16),
        scratch_shapes=[pltpu.VMEM((cap * pitch, LANES), U32)],
        compiler_params=_params(("parallel", "arbitrary")),
        name="moe_gather",
    )(idx.reshape(b * e, 1, cap), slabs)


def _ffn_kernel(xs_ref, w1_ref, w3_ref, w2_ref, gate_ref, o_ref, hid_ref, *, nt, tf):
    step = pl.program_id(1)

    @pl.when(step < nt)
    def _():
        xs = xs_ref[0]
        a = jnp.dot(xs, w1_ref[0, 0].astype(BF16), preferred_element_type=F32)
        g = jnp.dot(xs, w3_ref[0, 0].astype(BF16), preferred_element_type=F32)
        hid_ref[step] = (a * jax.nn.sigmoid(a) * g).astype(BF16)

    @pl.when(step >= nt)
    def _():
        hid = jnp.concatenate([hid_ref[k] for k in range(nt)], axis=1)
        y = jnp.dot(hid, w2_ref[0, 0].astype(BF16), preferred_element_type=F32)
        g_t = gate_ref[0].T
        o_ref[0] = jnp.concatenate([y[r * LANES:(r + 1) * LANES] * g_t[:, r:r + 1]
                                    for r in range(y.shape[0] // LANES)], axis=0)


def moe_ffn(xs, gate, w1, w3, w2, layer, *, tf=256, tn=512):
    e, m, d = xs.shape
    f = w1.shape[3]
    tf = _tile(f, tf)
    tn = _tile(d, tn)
    nt = f // tf
    nn = d // tn
    return pl.pallas_call(
        functools.partial(_ffn_kernel, nt=nt, tf=tf),
        grid=(e, nt + nn),
        in_specs=[
            pl.BlockSpec((1, m, d), lambda ei, t: (ei, 0, 0)),
            pl.BlockSpec((1, 1, d, tf), lambda ei, t: (layer, ei, 0, jnp.minimum(t, nt - 1))),
            pl.BlockSpec((1, 1, d, tf), lambda ei, t: (layer, ei, 0, jnp.minimum(t, nt - 1))),
            pl.BlockSpec((1, 1, f, tn), lambda ei, t: (layer, ei, 0, jnp.maximum(t - nt, 0))),
            pl.BlockSpec((1, m // LANES, LANES), lambda ei, t: (ei, 0, 0)),
        ],
        out_specs=pl.BlockSpec((1, m, tn), lambda ei, t: (ei, 0, jnp.maximum(t - nt, 0))),
        out_shape=jax.ShapeDtypeStruct((e, m, d), F32),
        scratch_shapes=[pltpu.VMEM((nt, m, tf), BF16)],
        compiler_params=_params(("parallel", "arbitrary")),
        name="moe_ffn",
    )(xs, w1, w3, w2, gate)


def _combine_kernel(idx_ref, ys_ref, o_ref, rows_ref, *, cap, half, pitch, unroll):
    ei = pl.program_id(2)

    @pl.when(ei == 0)
    def _():
        o_ref[...] = jnp.zeros(o_ref.shape, F32)

    for s in range(half // LANES):
        rows_ref[pl.ds(s, cap, stride=pitch), :] = ys_ref[0, :, s * LANES:(s + 1) * LANES]

    def body(j0, c):
        dsts, vals = [], []
        for u in range(unroll):
            j = j0 * unroll + u
            dst = pl.multiple_of(idx_ref[0, 0, j] * pitch, pitch)
            src = pl.multiple_of(j * pitch, pitch)
            vals.append(o_ref[0, 0, pl.ds(dst, pitch), :] + rows_ref[pl.ds(src, pitch), :])
            dsts.append(dst)
        for dst, val in zip(dsts, vals):
            o_ref[0, 0, pl.ds(dst, pitch), :] = val
        return c

    lax.fori_loop(0, cap // unroll, body, 0)


def moe_combine(idx, ys, *, seq, d):
    b, e, cap = idx.shape
    half = d // 2
    pitch = _slab_pitch(half)
    return pl.pallas_call(
        functools.partial(_combine_kernel, cap=cap, half=half, pitch=pitch, unroll=_tile(cap, 16)),
        grid=(b, 2, e),
        in_specs=[
            pl.BlockSpec((1, 1, cap), lambda bi, dh, ei: (bi * e + ei, 0, 0), memory_space=pltpu.SMEM),
            pl.BlockSpec((1, cap, half), lambda bi, dh, ei: (ei, bi, dh)),
        ],
        out_specs=pl.BlockSpec((1, 1, seq * pitch, LANES), lambda bi, dh, ei: (bi, dh, 0, 0)),
        out_shape=jax.ShapeDtypeStruct((b, 2, seq * pitch, LANES), F32),
        scratch_shapes=[pltpu.VMEM((cap * pitch, LANES), F32)],
        compiler_params=_params(("parallel", "parallel", "arbitrary")),
        name="moe_combine",
    )(idx.reshape(b * e, 1, cap), ys)


def _unslab(s0_ref, s1_ref, *, tm, half, pitch):
    parts = []
    for s_ref in (s0_ref, s1_ref):
        for s in range(half // LANES):
            parts.append(s_ref[0, 0, pl.ds(s, tm, stride=pitch), :])
    return jnp.concatenate(parts, axis=1)


def _unslab_kernel(*refs, tm, half, pitch, final_norm):
    x_ref, s0_ref, s1_ref = refs[:3]
    o_ref = refs[-1]
    y = x_ref[0] + _unslab(s0_ref, s1_ref, tm=tm, half=half, pitch=pitch)
    if final_norm:
        y = _rms(y, refs[3][...])
    o_ref[0] = y


def moe_residual(x, moe_slabs, final_g=None, *, tm=512):
    b, s, d = x.shape
    half = d // 2
    pitch = _slab_pitch(half)
    tm = _tile(s, tm)
    final_norm = final_g is not None
    in_specs = [
        pl.BlockSpec((1, tm, d), lambda bi, j: (bi, j, 0)),
        pl.BlockSpec((1, 1, tm * pitch, LANES), lambda bi, j: (bi, 0, j, 0)),
        pl.BlockSpec((1, 1, tm * pitch, LANES), lambda bi, j: (bi, 1, j, 0)),
    ]
    args = [x, moe_slabs, moe_slabs]
    if final_norm:
        in_specs.append(pl.BlockSpec((1, d), lambda bi, j: (0, 0)))
        args.append(final_g.reshape(1, d))
    return pl.pallas_call(
        functools.partial(_unslab_kernel, tm=tm, half=half, pitch=pitch, final_norm=final_norm),
        grid=(b, s // tm),
        in_specs=in_specs,
        out_specs=pl.BlockSpec((1, tm, d), lambda bi, j: (bi, j, 0)),
        out_shape=jax.ShapeDtypeStruct((b, s, d), F32),
        compiler_params=_params(("parallel", "parallel")),
        name="moe_residual",
    )(*args)


def expert_choice_ffn(aff, slabs, w1, w3, w2, layer, *, seq, d):
    cap = CAPACITY_FACTOR * seq // N_EXPERTS
    idx, gate = moe_topk(aff, cap)
    b, e, _ = idx.shape
    xs = moe_gather(idx, slabs, seq=seq, d=d)
    ys = moe_ffn(xs, gate.transpose(1, 0, 2).reshape(e, b * cap // LANES, LANES), w1, w3, w2, layer)
    return moe_combine(idx, ys, seq=seq, d=d)


def kernel(x, rel_bias, mix_norm, ffn_norm, final_norm, even_w_in, attn_sink, ssm_a_re, ssm_a_im, ssm_log_dt, ssm_b_re, ssm_b_im, ssm_c_re, ssm_c_im, ssm_d, glu_w, glu_b, even_w_out, odd_w_in, sgu_ln_g, sgu_ln_b, sgu_w, sgu_b, odd_w_out, router, moe_w1, moe_w3, moe_w2):
    b, s, d = x.shape
    t = b * s
    depth = mix_norm.shape[0]
    ssm_w = ssm_d.shape[-1]
    attn_w = even_w_out.shape[1] - ssm_w
    kv_w = (even_w_in.shape[-1] - attn_w - ssm_w) // 2
    bias = attention_bias(rel_bias)

    x2 = x.reshape(t, d)
    moe = None
    for layer in range(depth):
        i = layer // 2
        if layer % 2 == 0:
            if moe is not None:
                x2 = moe_residual(x2.reshape(b, s, d), moe).reshape(t, d)
            qkv, u = in_projection(x2, mix_norm[layer], even_w_in[i].astype(BF16), attn_w + 2 * kv_w)
            attn = windowed_attention(qkv.reshape(b, s, -1), attn_sink[i], bias,
                                      attn_w=attn_w, kv_w=kv_w)
            n_levels = max(1, (s // S5_CHUNK - 1).bit_length())
            operands = _s5_operands(ssm_a_re[i], ssm_a_im[i], ssm_log_dt[i], ssm_b_re[i], ssm_b_im[i],
                                    ssm_c_re[i], ssm_c_im[i], S5_CHUNK, n_levels)
            y = s5_scan(u, operands, seq=s, chunk=S5_CHUNK, n_levels=n_levels)
            ssm = s5_glu(y, u, ssm_d[i], glu_w[i].astype(BF16), glu_b[i])
            mixed, w_out = [attn.reshape(t, attn_w), ssm], even_w_out[i]
        else:
            gated, x2 = gmlp_gate(x2, moe, mix_norm[layer], odd_w_in[i].astype(BF16), sgu_ln_g[i],
                                  sgu_ln_b[i], sgu_w[i].astype(BF16), sgu_b[i], seq=s)
            mixed, w_out = [gated], odd_w_out[i]
        x2, aff, slabs = matmul_residual_route(mixed, w_out.astype(BF16), x2, ffn_norm[layer],
                                               router[layer], seq=s)
        moe = expert_choice_ffn(aff, slabs, moe_w1, moe_w3, moe_w2, layer, seq=s, d=d)
    return moe_residual(x2.reshape(b, s, d), moe, final_norm)
```

```python
import functools
import math

import jax
import jax.numpy as jnp
from jax import lax
from jax.experimental import pallas as pl
from jax.experimental.pallas import tpu as pltpu

F32 = jnp.float32
BF16 = jnp.bfloat16
I32 = jnp.int32
U32 = jnp.uint32

EPS = 1e-6
NEG_INF = -1e30

LANES = 128
SUBLANES = 8
VMEM_LIMIT = 56 << 20

ATTN_BLOCK = 128
ATTN_ROW_CHUNK = 32
HEAD_DIM = 128
KV_GROUP = 4
REL_BUCKETS = 32
REL_MAX_DIST = 128
S5_CHUNK = 32
S5_PITCH = 136
GMLP_CHUNK = 128
N_EXPERTS = 16
CAPACITY_FACTOR = 2


def _params(sem, vmem=VMEM_LIMIT):
    return pltpu.CompilerParams(dimension_semantics=sem, vmem_limit_bytes=vmem)


def _tile(n, want):
    t = min(n, want)
    while n % t:
        t //= 2
    return t


def _rms(x, g):
    ms = jnp.mean(x * x, axis=-1, keepdims=True)
    return x * lax.rsqrt(ms + EPS) * g


def _gelu(x):
    c = math.sqrt(2.0 / math.pi)
    return x * (0.5 * (1.0 + jnp.tanh(c * (x + 0.044715 * (x * x * x)))))


def _in_proj_kernel(x_ref, g_ref, w_ref, qkv_ref, u_ref):
    hn = _rms(x_ref[...], g_ref[...]).astype(BF16)
    y = jnp.dot(hn, w_ref[...], preferred_element_type=F32)
    split = qkv_ref.shape[1]
    qkv_ref[...] = y[:, :split]
    u_ref[...] = y[:, split:]


def in_projection(x, g, w_bf16, split, *, tm=512):
    m, k = x.shape
    n = w_bf16.shape[1]
    tm = _tile(m, tm)
    return pl.pallas_call(
        _in_proj_kernel,
        grid=(m // tm,),
        in_specs=[
            pl.BlockSpec((tm, k), lambda i: (i, 0)),
            pl.BlockSpec((1, k), lambda i: (0, 0)),
            pl.BlockSpec((k, n), lambda i: (0, 0), pipeline_mode=pl.Buffered(1)),
        ],
        out_specs=[pl.BlockSpec((tm, split), lambda i: (i, 0)),
                   pl.BlockSpec((tm, n - split), lambda i: (i, 0))],
        out_shape=[jax.ShapeDtypeStruct((m, split), F32),
                   jax.ShapeDtypeStruct((m, n - split), F32)],
        compiler_params=_params(("parallel",)),
        name="in_projection",
    )(x, g.reshape(1, k), w_bf16)


def _mm_res_route_kernel(*refs, n_lhs, tm, half, pitch):
    lhs = refs[:n_lhs]
    ws = refs[n_lhs:2 * n_lhs]
    res_ref, g_ref, rt_ref, o_ref, aff_ref, slab_ref = refs[2 * n_lhs:]
    acc = res_ref[...]
    for l_ref, w_ref in zip(lhs, ws):
        acc = acc + jnp.dot(l_ref[...], w_ref[...], preferred_element_type=F32)
    o_ref[...] = acc
    _route_rows(acc, g_ref, rt_ref, aff_ref, slab_ref, tm=tm, half=half, pitch=pitch)


def matmul_residual_route(lhs_list, w_bf16, res, norm_g, router, *, seq, tm=512):
    m, d = res.shape
    kk = lhs_list[0].shape[1]
    n_lhs = len(lhs_list)
    e = router.shape[1]
    assert all(l.shape == (m, kk) for l in lhs_list) and w_bf16.shape == (n_lhs * kk, d)
    tm = _tile(seq, tm)
    nt = seq // tm
    half = d // 2
    pitch = _slab_pitch(half)
    in_specs = [pl.BlockSpec((tm, kk), lambda i: (i, 0)) for _ in lhs_list]
    in_specs += [pl.BlockSpec((kk, d), functools.partial(lambda i, r: (r, 0), r=r),
                              pipeline_mode=pl.Buffered(1)) for r in range(n_lhs)]
    in_specs += [pl.BlockSpec((tm, d), lambda i: (i, 0)),
                 pl.BlockSpec((1, d), lambda i: (0, 0)),
                 pl.BlockSpec((e, d), lambda i: (0, 0))]
    return pl.pallas_call(
        functools.partial(_mm_res_route_kernel, n_lhs=n_lhs, tm=tm, half=half, pitch=pitch),
        grid=(m // tm,),
        in_specs=in_specs,
        out_specs=[pl.BlockSpec((tm, d), lambda i: (i, 0)),
                   pl.BlockSpec((1, e, tm), lambda i: (i // nt, 0, i % nt)),
                   pl.BlockSpec((tm * pitch, LANES), lambda i: (i, 0))],
        out_shape=[jax.ShapeDtypeStruct((m, d), F32),
                   jax.ShapeDtypeStruct((m // seq, e, seq), F32),
                   jax.ShapeDtypeStruct((m * pitch, LANES), U32)],
        compiler_params=_params(("parallel",)),
        name="matmul_residual_route",
    )(*lhs_list, *([w_bf16] * n_lhs), res, norm_g.reshape(1, d), router.T.astype(F32))


def _t5_bucket(rel):
    nb = REL_BUCKETS // 2
    max_exact = nb // 2
    base = jnp.where(rel > 0, nb, 0)
    n = jnp.abs(rel)
    nf = jnp.maximum(n, 1).astype(F32)
    large = max_exact + (jnp.log(nf / max_exact) / math.log(REL_MAX_DIST / max_exact)
                         * (nb - max_exact)).astype(I32)
    large = jnp.minimum(large, nb - 1)
    return base + jnp.where(n < max_exact, n, large)


def _bias_kernel(rb_ref, bucket_ref, o_ref, *, n_heads):
    bkt = bucket_ref[...]
    row = lax.broadcasted_iota(I32, bkt.shape, 0)
    col = lax.broadcasted_iota(I32, bkt.shape, 1)
    in_window = jnp.abs(col - ATTN_BLOCK - row) <= ATTN_BLOCK
    for h in range(n_heads):
        acc = jnp.zeros(bkt.shape, F32)
        for k in range(REL_BUCKETS):
            acc = jnp.where(bkt == k, rb_ref[k, h], acc)
        o_ref[h * ATTN_BLOCK:(h + 1) * ATTN_BLOCK, :] = jnp.where(in_window, acc, NEG_INF)


def attention_bias(rel_bias):
    n_heads = rel_bias.shape[1]
    q_off = jnp.arange(ATTN_BLOCK, dtype=I32)
    c_off = jnp.arange(3 * ATTN_BLOCK, dtype=I32)
    bucket = _t5_bucket(c_off[None, :] - ATTN_BLOCK - q_off[:, None])
    return pl.pallas_call(
        functools.partial(_bias_kernel, n_heads=n_heads),
        in_specs=[pl.BlockSpec(memory_space=pltpu.SMEM),
                  pl.BlockSpec(memory_space=pltpu.VMEM)],
        out_specs=pl.BlockSpec(memory_space=pltpu.VMEM),
        out_shape=jax.ShapeDtypeStruct((n_heads * ATTN_BLOCK, 3 * ATTN_BLOCK), F32),
        name="attention_bias",
    )(rel_bias.astype(F32), bucket)


def _attn_kernel(sink_ref, q_ref, kp_ref, kc_ref, kn_ref, vp_ref, vc_ref, vn_ref, bias_ref,
                 o_ref, s_ref, p_ref, *, seq, n_kv, qb):
    blk = ATTN_BLOCK
    col = lax.broadcasted_iota(I32, (1, 3 * blk), 1)
    scale = HEAD_DIM ** -0.5
    rc = ATTN_ROW_CHUNK

    def band(p_ref, c_ref, n_ref, ksl, j):
        parts = []
        for i in (j - 1, j, j + 1):
            if i < 0:
                parts.append(p_ref[0, :, ksl])
            elif i >= qb:
                parts.append(n_ref[0, :, ksl])
            else:
                parts.append(c_ref[0, i * blk:(i + 1) * blk, ksl])
        return jnp.concatenate(parts, axis=0).astype(BF16)

    for j in range(qb):
        kpos = (pl.program_id(1) * qb + j - 1) * blk + col
        in_seq = (kpos >= 0) & (kpos < seq)
        rows = slice(j * blk, (j + 1) * blk)
        for kh in range(n_kv):
            ksl = slice(kh * HEAD_DIM, (kh + 1) * HEAD_DIM)
            kband = band(kp_ref, kc_ref, kn_ref, ksl, j)
            vband = band(vp_ref, vc_ref, vn_ref, ksl, j)
            heads = [kh * KV_GROUP + g for g in range(KV_GROUP)]
            qs = jnp.concatenate([q_ref[0, rows, h * HEAD_DIM:(h + 1) * HEAD_DIM] for h in heads],
                                 axis=0).astype(BF16)
            s_ref[...] = lax.dot_general(qs, kband, (((1,), (1,)), ((), ())),
                                         preferred_element_type=F32)

            def softmax_rows(c, carry, kh=kh, in_seq=in_seq):
                r0 = pl.multiple_of(c * rc, rc)
                bias = bias_ref[pl.ds(kh * KV_GROUP * blk + r0, rc), :]
                s = jnp.where(in_seq, s_ref[pl.ds(r0, rc), :] * scale + bias, NEG_INF)
                sink = sink_ref[kh * KV_GROUP + c // (blk // rc)]
                m = jnp.maximum(jnp.max(s, axis=-1, keepdims=True), sink)
                p = jnp.exp(s - m)
                denom = jnp.sum(p, axis=-1, keepdims=True) + jnp.exp(sink - m)
                p_ref[pl.ds(r0, rc), :] = (p / denom).astype(BF16)
                return carry

            lax.fori_loop(0, KV_GROUP * blk // rc, softmax_rows, 0, unroll=True)
            o = jnp.dot(p_ref[...], vband, preferred_element_type=F32)
            for g, h in enumerate(heads):
                o_ref[0, rows, h * HEAD_DIM:(h + 1) * HEAD_DIM] = (
                    o[g * blk:(g + 1) * blk].astype(o_ref.dtype))


def windowed_attention(qkv, sink, bias, *, attn_w, kv_w, qb=8):
    b, s, _ = qkv.shape
    blk = ATTN_BLOCK
    nblk = s // blk
    qb = _tile(nblk, qb)
    n_kv = kv_w // HEAD_DIM
    kcol = attn_w // kv_w
    assert attn_w % kv_w == 0 and n_kv * KV_GROUP * HEAD_DIM == attn_w

    def edge_spec(col, first):
        return pl.BlockSpec(
            (1, blk, kv_w),
            lambda bi, n: (bi, jnp.clip(n * qb + (-1 if first else qb), 0, nblk - 1), col))

    def mid_spec(col):
        return pl.BlockSpec((1, qb * blk, kv_w), lambda bi, n: (bi, n, col))

    return pl.pallas_call(
        functools.partial(_attn_kernel, seq=s, n_kv=n_kv, qb=qb),
        grid=(b, nblk // qb),
        in_specs=[
            pl.BlockSpec(memory_space=pltpu.SMEM),
            pl.BlockSpec((1, qb * blk, attn_w), lambda bi, n: (bi, n, 0)),
            edge_spec(kcol, True), mid_spec(kcol), edge_spec(kcol, False),
            edge_spec(kcol + 1, True), mid_spec(kcol + 1), edge_spec(kcol + 1, False),
            pl.BlockSpec(bias.shape, lambda bi, n: (0, 0)),
        ],
        out_specs=pl.BlockSpec((1, qb * blk, attn_w), lambda bi, n: (bi, n, 0)),
        out_shape=jax.ShapeDtypeStruct((b, s, attn_w), BF16),
        scratch_shapes=[pltpu.VMEM((KV_GROUP * blk, 3 * blk), F32),
                        pltpu.VMEM((KV_GROUP * blk, 3 * blk), BF16)],
        compiler_params=_params(("parallel", "parallel")),
        name="windowed_attention",
    )(sink.astype(F32), qkv, qkv, qkv, qkv, qkv, qkv, qkv, bias)


def _s5_operands(a_re, a_im, log_dt, b_re, b_im, c_re, c_im, chunk, n_levels):
    ell = chunk
    p = a_re.shape[-1]
    dt = jnp.exp(log_dt)[..., None]
    mag = jnp.exp(a_re * dt)
    lb_re = mag * jnp.cos(a_im * dt)
    lb_im = mag * jnp.sin(a_im * dt)
    den = a_re * a_re + a_im * a_im
    nr = lb_re - 1.0
    coef_re = (nr * a_re + lb_im * a_im) / den
    coef_im = (lb_im * a_re - nr * a_im) / den
    bb_re = coef_re[..., None] * b_re - coef_im[..., None] * b_im
    bb_im = coef_re[..., None] * b_im + coef_im[..., None] * b_re
    bt_re, bt_im = bb_re.transpose(0, 1, 3, 2), bb_im.transpose(0, 1, 3, 2)

    def powers(tau):
        t = tau.astype(F32)[None, None, :, None]
        pm = jnp.exp((a_re * dt)[:, :, None, :] * t)
        ang = (a_im * dt)[:, :, None, :] * t
        return pm * jnp.cos(ang), pm * jnp.sin(ang)

    g = a_re.shape[1]
    assert 2 * p == LANES and g % 2 == 0
    odd = (jnp.arange(g) % 2 == 1)[:, None, None]
    cat = lambda *ts: jnp.concatenate(ts, axis=-1)
    lp = lambda t: jnp.pad(t, [(0, 0)] * (t.ndim - 1) + [(0, LANES - p)])
    sp = lambda t: jnp.where(odd, cat(jnp.zeros_like(t), t), cat(t, jnp.zeros_like(t)))
    ii = jnp.arange(ell)
    dn_re, dn_im = powers(ell - 1 - ii)
    up_re, up_im = powers(ii)
    e1_re, e1_im = powers(ii + 1)
    e2_re, e2_im = powers(ell - ii)
    lv_re, lv_im = powers(ell * (2 ** jnp.arange(n_levels)))
    lev = SUBLANES * pl.cdiv(n_levels, SUBLANES)
    lvp = lambda t: jnp.pad(sp(t), ((0, 0), (0, lev - n_levels), (0, 0)))
    wide = jnp.concatenate([
        cat(sp(dn_re[0]), sp(dn_re[0]), sp(up_re[1]), sp(up_re[1])),
        cat(sp(dn_im[0]), sp(dn_im[0]), sp(up_im[1]), sp(up_im[1])),
        cat(sp(e1_re[0]), sp(e1_re[0]), sp(e2_re[1]), sp(e2_re[1])),
        cat(sp(e1_im[0]), sp(e1_im[0]), sp(e2_im[1]), sp(e2_im[1])),
        cat(sp(bt_re[0]), sp(bt_im[0]), sp(bt_re[1]), sp(bt_im[1])),
        cat(-sp(bt_im[0]), sp(bt_re[0]), -sp(bt_im[1]), sp(bt_re[1])),
        cat(sp(c_re[0]), -sp(c_im[0]), sp(c_re[1]), -sp(c_im[1])),
        cat(-sp(c_im[0]), -sp(c_re[0]), -sp(c_im[1]), -sp(c_re[1])),
        cat(lvp(lv_re[0]), lvp(lv_re[0]), lvp(lv_re[1]), lvp(lv_re[1])),
        cat(-lvp(lv_im[0]), lvp(lv_im[0]), -lvp(lv_im[1]), lvp(lv_im[1])),
    ], axis=1)

    def lag_rows(fwd, bwd):
        f = jnp.pad(lp(fwd), ((0, 0), (ell - 1, 1), (0, 0)))
        b = jnp.pad(lp(bwd), ((0, 0), (0, ell), (0, 0)))
        return cat(f, b)

    narrow = jnp.concatenate([
        lag_rows(up_re[0], dn_re[1]), lag_rows(up_im[0], dn_im[1]),
        cat(lp(c_re[0]), lp(c_re[1])), cat(lp(c_im[0]), lp(c_im[1])),
        cat(lp(bt_re[0]), lp(bt_re[1])), cat(lp(bt_im[0]), lp(bt_im[1])),
    ], axis=1)
    return wide, narrow


def _s5_pack_kernel(u_ref, o_ref, t_ref, *, n_lt, h, chunk):
    def transpose_block(lt, c):
        src = pl.multiple_of(lt * LANES, LANES)
        dst = pl.multiple_of(lt * S5_PITCH, SUBLANES)
        t_ref[pl.ds(dst, LANES), :] = u_ref[pl.ds(src, LANES), :].T
        return c

    lax.fori_loop(0, n_lt, transpose_block, 0, unroll=_tile(n_lt, 32))
    n_sub = LANES // chunk

    def pack_group(g8, c):
        for kt in range(h // n_sub):
            ms = [t_ref[pl.ds(g8 * h + kt * n_sub + kk, n_lt, stride=S5_PITCH), :] for kk in range(n_sub)]
            for c_lo in range(n_sub):
                tile = jnp.concatenate([m[:, c_lo * chunk:(c_lo + 1) * chunk] for m in ms], axis=1)
                o_ref[g8, c_lo * n_lt:(c_lo + 1) * n_lt, kt * LANES:(kt + 1) * LANES] = tile.astype(BF16)
        return c

    lax.fori_loop(0, LANES // h, pack_group, 0)


def _s5_unpack_kernel(y_ref, o_ref, t_ref, *, n_lt, h, chunk):
    n_sub = LANES // chunk

    def unpack_group(g8, c):
        for hh in range(h):
            lane0 = (hh // n_sub) * LANES + (hh % n_sub) * chunk
            m = jnp.concatenate([y_ref[g8, c_lo * n_lt:(c_lo + 1) * n_lt, lane0:lane0 + chunk]
                                 for c_lo in range(n_sub)], axis=1)
            t_ref[pl.ds(g8 * h + hh, n_lt, stride=S5_PITCH), :] = m
        return c

    lax.fori_loop(0, LANES // h, unpack_group, 0)

    def transpose_block(lt, c):
        src = pl.multiple_of(lt * S5_PITCH, SUBLANES)
        dst = pl.multiple_of(lt * LANES, LANES)
        o_ref[pl.ds(dst, LANES), :] = t_ref[pl.ds(src, LANES), :].T
        return c

    lax.fori_loop(0, n_lt, transpose_block, 0, unroll=_tile(n_lt, 32))


def _shift_rows(z, r, seg):
    if r == 0:
        return z
    rows = z.shape[0]
    rolled = pltpu.roll(z, r % rows, 0)
    assert seg & (seg - 1) == 0
    pos = lax.broadcasted_iota(I32, z.shape, 0) & (seg - 1)
    keep = (pos >= r) if r > 0 else (pos < seg + r)
    return jnp.where(keep, rolled, 0.0)


def _shift_chunks(blocks, d, seg):
    n_sub = len(blocks)
    out = []
    for c_lo in range(n_sub):
        e, s_lo = divmod(c_lo - d, n_sub)
        out.append(_shift_rows(blocks[s_lo], -e, seg))
    return out


def _s5_group_tables(w, nr, perm, *, n_levels, chunk, h):
    ell, lh = chunk, chunk * h
    lev = SUBLANES * pl.cdiv(n_levels, SUBLANES)
    nt = (((1,), (1,)), ((), ()))
    pwf_re, pwf_im, pwe_re, pwe_im = (w[k * ell:(k + 1) * ell] for k in range(4))
    o = 4 * ell
    bb1, bb2, cc1, cc2 = (w[o + k * h:o + (k + 1) * h] for k in range(4))
    o += 4 * h
    pa_all, pb_all = w[o:o + lev], w[o + lev:o + 2 * lev]
    f_tab = jnp.concatenate([pwf_re * bb1[k:k + 1, :] + pwf_im * bb2[k:k + 1, :] for k in range(h)],
                            axis=0).astype(BF16)
    et_tab = jnp.concatenate([pwe_re * cc1[k:k + 1, :] + pwe_im * cc2[k:k + 1, :] for k in range(h)],
                             axis=0).astype(BF16)
    pwm_re, pwm_im = nr[:2 * ell], nr[2 * ell:4 * ell]
    o = 4 * ell
    cr, ci, br, bi = (nr[o + k * h:o + (k + 1) * h] for k in range(4))
    rep = lambda x: jnp.concatenate([jnp.broadcast_to(x[j:j + 1, :], (h, x.shape[1]))
                                     for j in range(x.shape[0])], axis=0)
    tile = lambda t: jnp.concatenate([t] * (2 * ell), axis=0)
    pr, pi = rep(pwm_re), rep(pwm_im)
    crt, cit = tile(cr), tile(ci)
    a = jnp.concatenate([pr * crt - pi * cit, pr * cit + pi * crt], axis=1)
    b = jnp.concatenate([br, -bi], axis=1)
    a1 = a.astype(BF16)
    a2 = (a - a1.astype(F32)).astype(BF16)
    b1 = b.astype(BF16)
    b2 = (b - b1.astype(F32)).astype(BF16)
    lead = lax.dot_general(jnp.concatenate([b1, b2], axis=0), a1, nt, preferred_element_type=F32)
    strip = lead[:h] + lead[h:] + lax.dot_general(b1, a2, nt, preferred_element_type=F32)
    t_jk = jnp.concatenate([strip[:, (ell - 1 - j) * h:(ell - 1 - j) * h + lh] for j in range(ell)],
                           axis=0).astype(BF16)
    t_tab = lax.dot_general(jnp.dot(perm, t_jk, preferred_element_type=F32).astype(BF16), perm, nt,
                            preferred_element_type=F32).astype(BF16)
    return f_tab, et_tab, t_tab, pa_all, pb_all


def _s5_kernel(u_ref, wide_ref, narrow_ref, perm_ref, o_ref, *, seg, n_levels, chunk, h):
    n_sub = LANES // chunk
    nt = (((1,), (1,)), ((), ()))
    perm = perm_ref[...]
    tabs = [_s5_group_tables(wide_ref[gi], narrow_ref[gi], perm, n_levels=n_levels, chunk=chunk, h=h)
            for gi in range(2)]
    us = [u_ref[gi] for gi in range(2)]
    z = sum(jnp.dot(u, t[0], preferred_element_type=F32) for u, t in zip(us, tabs))
    pa_all = tabs[0][3] + tabs[1][3]
    pb_all = tabs[0][4] + tabs[1][4]
    n_lt = z.shape[0] // n_sub
    half = 2 * LANES

    def swap(x):
        return jnp.concatenate([x[:, LANES:], x[:, :LANES]], axis=1)

    zf = [z[c * n_lt:(c + 1) * n_lt, :half] for c in range(n_sub)]
    zb = [z[c * n_lt:(c + 1) * n_lt, half:] for c in range(n_sub)]
    for k in range(n_levels):
        pa = pa_all[k:k + 1, :]
        pb = pb_all[k:k + 1, :]
        sf = _shift_chunks(zf, 2 ** k, seg)
        sb = _shift_chunks(zb, -(2 ** k), seg)
        zf = [a + pa[:, :half] * s + pb[:, :half] * swap(s) for a, s in zip(zf, sf)]
        zb = [a + pa[:, half:] * s + pb[:, half:] * swap(s) for a, s in zip(zb, sb)]
    hin = jnp.concatenate([jnp.concatenate(_shift_chunks(zf, 1, seg), axis=0),
                           jnp.concatenate(_shift_chunks(zb, -1, seg), axis=0)], axis=1).astype(BF16)
    for gi in range(2):
        y = jnp.dot(us[gi], tabs[gi][2], preferred_element_type=F32)
        o_ref[gi] = y + lax.dot_general(hin, tabs[gi][1], nt, preferred_element_type=F32)


def s5_scan(u, operands, *, seq, chunk, n_levels):
    wide, narrow = operands
    t, w = u.shape
    g = wide.shape[0]
    h = w // g
    lh = chunk * h
    n_lt = t // LANES
    rows = t // chunk
    gpt = LANES // h
    assert t % LANES == 0 and seq % LANES == 0 and LANES % chunk == 0 and h % (LANES // chunk) == 0
    scratch = pltpu.VMEM((n_lt * S5_PITCH, LANES), F32)
    ug = pl.pallas_call(
        functools.partial(_s5_pack_kernel, n_lt=n_lt, h=h, chunk=chunk),
        grid=(w // LANES,),
        in_specs=[pl.BlockSpec((t, LANES), lambda q: (0, q))],
        out_specs=pl.BlockSpec((gpt, rows, lh), lambda q: (q, 0, 0)),
        out_shape=jax.ShapeDtypeStruct((g, rows, lh), BF16),
        scratch_shapes=[scratch],
        compiler_params=_params(("parallel",)),
        name="s5_pack",
    )(u)
    kj = jnp.arange(lh)
    perm = (kj[:, None] // chunk + (kj[:, None] % chunk) * h == kj[None, :]).astype(BF16)
    yg = pl.pallas_call(
        functools.partial(_s5_kernel, seg=seq // LANES, n_levels=n_levels, chunk=chunk, h=h),
        grid=(g // 2,),
        in_specs=[
            pl.BlockSpec((2, rows, lh), lambda i: (i, 0, 0)),
            pl.BlockSpec((2,) + wide.shape[1:], lambda i: (i, 0, 0)),
            pl.BlockSpec((2,) + narrow.shape[1:], lambda i: (i, 0, 0)),
            pl.BlockSpec((lh, lh), lambda i: (0, 0)),
        ],
        out_specs=pl.BlockSpec((2, rows, lh), lambda i: (i, 0, 0)),
        out_shape=jax.ShapeDtypeStruct((g, rows, lh), F32),
        compiler_params=_params(("parallel",)),
        name="s5_scan",
    )(ug, wide, narrow, perm)
    return pl.pallas_call(
        functools.partial(_s5_unpack_kernel, n_lt=n_lt, h=h, chunk=chunk),
        grid=(w // LANES,),
        in_specs=[pl.BlockSpec((gpt, rows, lh), lambda q: (q, 0, 0))],
        out_specs=pl.BlockSpec((t, LANES), lambda q: (0, q)),
        out_shape=jax.ShapeDtypeStruct((t, w), F32),
        scratch_shapes=[scratch],
        compiler_params=_params(("parallel",)),
        name="s5_unpack",
    )(yg)


def _glu_kernel(y_ref, u_ref, d_ref, w_ref, b_ref, o_ref):
    y = y_ref[...] + d_ref[...] * u_ref[...]
    g = _gelu(y)
    z = jnp.dot(g.astype(BF16), w_ref[...], preferred_element_type=F32) + b_ref[...]
    o_ref[...] = (g * jax.nn.sigmoid(z)).astype(o_ref.dtype)


def s5_glu(y, u, d_skip, glu_w_bf16, glu_b, *, tm=1024):
    m, w = y.shape
    tm = _tile(m, tm)
    return pl.pallas_call(
        _glu_kernel,
        grid=(m // tm,),
        in_specs=[
            pl.BlockSpec((tm, w), lambda i: (i, 0)),
            pl.BlockSpec((tm, w), lambda i: (i, 0)),
            pl.BlockSpec((1, w), lambda i: (0, 0)),
            pl.BlockSpec((w, w), lambda i: (0, 0)),
            pl.BlockSpec((1, w), lambda i: (0, 0)),
        ],
        out_specs=pl.BlockSpec((tm, w), lambda i: (i, 0)),
        out_shape=jax.ShapeDtypeStruct((m, w), BF16),
        compiler_params=_params(("parallel",)),
        name="s5_glu",
    )(y, u, d_skip.reshape(1, w), glu_w_bf16, glu_b.reshape(1, w))


def _gmlp_kernel(x_ref, s0_ref, s1_ref, ng_ref, w_ref, g_ref, b_ref, ws_ref, bs_ref, o_ref, xn_ref, *,
                 width, n_heads, tm, half, pitch):
    hd = width // n_heads
    ck = GMLP_CHUNK
    x = x_ref[...] + _unslab(s0_ref, s1_ref, tm=tm, half=half, pitch=pitch)
    xn_ref[...] = x
    hn = _rms(x, ng_ref[...]).astype(BF16)
    z = _gelu(jnp.dot(hn, w_ref[...], preferred_element_type=F32))
    u, v = z[:, :width], z[:, width:]
    mu = jnp.mean(v, axis=-1, keepdims=True)
    var = jnp.mean(jnp.square(v - mu), axis=-1, keepdims=True)
    vn = ((v - mu) * lax.rsqrt(var + EPS) * g_ref[...] + b_ref[...]).astype(BF16)
    for c in range(tm // ck):
        rows = slice(c * ck, (c + 1) * ck)
        for h in range(n_heads):
            sl = slice(h * hd, (h + 1) * hd)
            mixed = jnp.dot(ws_ref[h], vn[rows, sl], preferred_element_type=F32) + bs_ref[:, h:h + 1]
            o_ref[rows, sl] = (u[rows, sl] * mixed).astype(o_ref.dtype)


def gmlp_gate(x, moe_slabs, norm_g, w_in_bf16, ln_g, ln_b, w_s_bf16, b_s, *, seq, tm=512):
    m, d = x.shape
    w2 = w_in_bf16.shape[1]
    width = w2 // 2
    n_heads = w_s_bf16.shape[0]
    ck = GMLP_CHUNK
    tm = max(ck, _tile(seq, tm))
    nt = seq // tm
    half = d // 2
    pitch = _slab_pitch(half)
    return pl.pallas_call(
        functools.partial(_gmlp_kernel, width=width, n_heads=n_heads, tm=tm, half=half, pitch=pitch),
        grid=(m // tm,),
        in_specs=[
            pl.BlockSpec((tm, d), lambda i: (i, 0)),
            pl.BlockSpec((1, 1, tm * pitch, LANES), lambda i: (i // nt, 0, i % nt, 0)),
            pl.BlockSpec((1, 1, tm * pitch, LANES), lambda i: (i // nt, 1, i % nt, 0)),
            pl.BlockSpec((1, d), lambda i: (0, 0)),
            pl.BlockSpec((d, w2), lambda i: (0, 0), pipeline_mode=pl.Buffered(1)),
            pl.BlockSpec((1, width), lambda i: (0, 0)),
            pl.BlockSpec((1, width), lambda i: (0, 0)),
            pl.BlockSpec((n_heads, ck, ck), lambda i: (0, 0, 0)),
            pl.BlockSpec((ck, n_heads), lambda i: (0, 0)),
        ],
        out_specs=[pl.BlockSpec((tm, width), lambda i: (i, 0)),
                   pl.BlockSpec((tm, d), lambda i: (i, 0))],
        out_shape=[jax.ShapeDtypeStruct((m, width), BF16),
                   jax.ShapeDtypeStruct((m, d), F32)],
        compiler_params=_params(("parallel",)),
        name="gmlp_gate",
    )(x, moe_slabs, moe_slabs, norm_g.reshape(1, d), w_in_bf16, ln_g.reshape(1, width),
      ln_b.reshape(1, width), w_s_bf16, b_s.T.astype(F32))


def _slab_pitch(half):
    return SUBLANES * pl.cdiv(half // LANES, SUBLANES)


def _route_rows(x, g_ref, rt_ref, aff_ref, slab_ref, *, tm, half, pitch):
    h = _rms(x, g_ref[...])
    nt = (((1,), (1,)), ((), ()))
    e = rt_ref.shape[0]
    h1 = h.astype(BF16)
    h1f = h1.astype(F32)
    h2 = (h - h1f).astype(BF16)
    r1 = rt_ref[...].astype(BF16)
    r2 = (rt_ref[...] - r1.astype(F32)).astype(BF16)
    lead = lax.dot_general(jnp.concatenate([r1, r2], axis=0), h1, nt, preferred_element_type=F32)
    logits = lead[:e] + lead[e:] + lax.dot_general(r1, h2, nt, preferred_element_type=F32)
    mx = jnp.max(logits, axis=0, keepdims=True)
    ex = jnp.exp(logits - mx)
    aff_ref[0] = ex / jnp.sum(ex, axis=0, keepdims=True)
    lo = lax.bitcast_convert_type(h1f[:, :half], U32) >> 16
    hi = lax.bitcast_convert_type(h1f[:, half:], U32) & jnp.uint32(0xFFFF0000)
    word = lo | hi
    for s in range(half // LANES):
        slab_ref[pl.ds(s, tm, stride=pitch), :] = word[:, s * LANES:(s + 1) * LANES]
    if pitch > half // LANES:
        for s in range(half // LANES, pitch):
            slab_ref[pl.ds(s, tm, stride=pitch), :] = jnp.zeros((tm, LANES), U32)


def _topk_kernel(aff_ref, idx_ref, gate_ref, blk_ref, *, cap):
    aff = aff_ref[0]
    e, s = aff.shape
    nb = s // LANES
    assert e & (e - 1) == 0
    bits = lax.bitcast_convert_type(aff, I32)

    def count_ge(t):
        return jnp.sum((bits >= t).astype(F32), axis=1, keepdims=True)

    def body(_, carry):
        lo, hi = carry
        mid = lo + ((hi - lo) >> 1)
        ok = count_ge(mid) >= cap
        return jnp.where(ok, mid, lo), jnp.where(ok, hi, mid)

    lo0 = jnp.zeros((e, 1), I32)
    hi0 = jnp.full((e, 1), 0x7F800000, I32)
    thr, _ = lax.fori_loop(0, 32, body, (lo0, hi0))
    gt = bits > thr
    eq = bits == thr
    need = cap - jnp.sum(gt.astype(F32), axis=1, keepdims=True)
    rows = nb * e
    by_block = lambda x: jnp.concatenate([x[:, k * LANES:(k + 1) * LANES] for k in range(nb)], axis=0)
    tri = (lax.broadcasted_iota(I32, (LANES, LANES), 0)
           <= lax.broadcasted_iota(I32, (LANES, LANES), 1)).astype(BF16)
    ri = lax.broadcasted_iota(I32, (rows, rows), 0)
    ci = lax.broadcasted_iota(I32, (rows, rows), 1)
    earlier = (((ci & (e - 1)) == (ri & (e - 1))) & (ci < ri)).astype(BF16)

    def counts(mask):
        within = jnp.dot(jnp.where(mask, 1.0, 0.0).astype(BF16), tri, preferred_element_type=F32)
        total = jnp.broadcast_to(within[:, LANES - 1:LANES], (rows, LANES)).astype(BF16)
        before = jnp.dot(earlier, total, preferred_element_type=F32)[:, :1]
        return within, before

    gt_r, eq_r = by_block(gt), by_block(eq)
    eq_in, eq_before = counts(eq_r)
    need_r = jnp.concatenate([need] * nb, axis=0)
    sel_r = gt_r | (eq_r & (eq_in + eq_before <= need_r))
    sel_in, sel_before = counts(sel_r)
    lane = lax.broadcasted_iota(I32, (rows, LANES), 1)
    blk_ref[0] = jnp.where(sel_r, sel_in, 0.0)
    blk_ref[1] = by_block(aff)
    blk_ref[2] = jnp.where(lane == 0, sel_before,
                           jnp.where(lane == 1, sel_before + sel_in[:, LANES - 1:LANES], 0.0))

    jrow = lax.broadcasted_iota(I32, (1, cap), 1).astype(F32)
    b_id = lax.broadcasted_iota(I32, (nb, cap), 0).astype(F32)
    l_id = lax.broadcasted_iota(I32, (LANES, cap), 0).astype(F32)
    for ei in range(e):
        rank_in = blk_ref[0, pl.ds(ei, nb, stride=e), :]
        aff_b = blk_ref[1, pl.ds(ei, nb, stride=e), :]
        cnt = blk_ref[2, pl.ds(ei, nb, stride=e), :]
        before, upto = cnt[:, 0:1], cnt[:, 1:2]
        blk = jnp.sum(jnp.where(upto <= jrow, 1.0, 0.0), axis=0, keepdims=True)
        onehot = b_id == blk
        rank = jrow - jnp.sum(jnp.where(onehot, before, 0.0), axis=0, keepdims=True) + 1.0
        oh = jnp.where(onehot, 1.0, 0.0).astype(BF16)
        ranks_t = jnp.dot(rank_in.T.astype(BF16), oh, preferred_element_type=F32)
        hit = ranks_t == rank
        a_t = aff_b.T
        g1 = a_t.astype(BF16)
        g2 = (a_t - g1.astype(F32)).astype(BF16)
        g3 = ((a_t - g1.astype(F32)) - g2.astype(F32)).astype(BF16)
        aff_t = (jnp.dot(g1, oh, preferred_element_type=F32) + jnp.dot(g2, oh, preferred_element_type=F32)
                 + jnp.dot(g3, oh, preferred_element_type=F32))
        local = jnp.sum(jnp.where(hit, l_id, 0.0), axis=0, keepdims=True)
        idx_ref[0, ei:ei + 1, :] = (blk * LANES + local).astype(I32)
        gate_ref[0, ei:ei + 1, :] = jnp.sum(jnp.where(hit, aff_t, 0.0), axis=0, keepdims=True)


def moe_topk(aff, cap):
    b, e, s = aff.shape
    return pl.pallas_call(
        functools.partial(_topk_kernel, cap=cap),
        grid=(b,),
        in_specs=[pl.BlockSpec((1, e, s), lambda i: (i, 0, 0))],
        out_specs=[pl.BlockSpec((1, e, cap), lambda i: (i, 0, 0)),
                   pl.BlockSpec((1, e, cap), lambda i: (i, 0, 0))],
        out_shape=[jax.ShapeDtypeStruct((b, e, cap), I32),
                   jax.ShapeDtypeStruct((b, e, cap), F32)],
        scratch_shapes=[pltpu.VMEM((3, s // LANES * e, LANES), F32)],
        compiler_params=_params(("parallel",)),
        name="moe_topk",
    )(aff)


def _gather_kernel(idx_ref, slab_ref, o_ref, rows_ref, *, cap, half, pitch, unroll):
    def body(j0, c):
        for u in range(unroll):
            j = j0 * unroll + u
            src = pl.multiple_of(idx_ref[0, 0, j] * pitch, pitch)
            rows_ref[pl.ds(pl.multiple_of(j * pitch, pitch), pitch), :] = slab_ref[pl.ds(src, pitch), :]
        return c

    lax.fori_loop(0, cap // unroll, body, 0)
    for s in range(half // LANES):
        word = rows_ref[pl.ds(s, cap, stride=pitch), :]
        lo = lax.bitcast_convert_type(word << 16, F32)
        hi = lax.bitcast_convert_type(word & jnp.uint32(0xFFFF0000), F32)
        o_ref[0, :, s * LANES:(s + 1) * LANES] = lo.astype(BF16)
        o_ref[0, :, half + s * LANES:half + (s + 1) * LANES] = hi.astype(BF16)


def moe_gather(idx, slabs, *, seq, d):
    b, e, cap = idx.shape
    half = d // 2
    pitch = _slab_pitch(half)
    return pl.pallas_call(
        functools.partial(_gather_kernel, cap=cap, half=half, pitch=pitch, unroll=_tile(cap, 32)),
        grid=(b, e),
        in_specs=[
            pl.BlockSpec((1, 1, cap), lambda bi, ei: (bi * e + ei, 0, 0), memory_space=pltpu.SMEM),
            pl.BlockSpec((seq * pitch, LANES), lambda bi, ei: (bi, 0)),
        ],
        out_specs=pl.BlockSpec((1, cap, d), lambda bi, ei: (ei, bi, 0)),
        out_shape=jax.ShapeDtypeStruct((e, b * cap, d), BF16),
        scratch_shapes=[pltpu.VMEM((cap * pitch, LANES), U32)],
        compiler_params=_params(("parallel", "arbitrary")),
        name="moe_gather",
    )(idx.reshape(b * e, 1, cap), slabs)


def _ffn_kernel(xs_ref, w1_ref, w3_ref, w2_ref, gate_ref, o_ref, hid_ref, *, nt, tf):
    step = pl.program_id(1)

    @pl.when(step < nt)
    def _():
        xs = xs_ref[0]
        a = jnp.dot(xs, w1_ref[0, 0].astype(BF16), preferred_element_type=F32)
        g = jnp.dot(xs, w3_ref[0, 0].astype(BF16), preferred_element_type=F32)
        hid_ref[step] = (a * jax.nn.sigmoid(a) * g).astype(BF16)

    @pl.when(step >= nt)
    def _():
        hid = jnp.concatenate([hid_ref[k] for k in range(nt)], axis=1)
        y = jnp.dot(hid, w2_ref[0, 0].astype(BF16), preferred_element_type=F32)
        g_t = gate_ref[0].T
        o_ref[0] = jnp.concatenate([y[r * LANES:(r + 1) * LANES] * g_t[:, r:r + 1]
                                    for r in range(y.shape[0] // LANES)], axis=0)


def moe_ffn(xs, gate, w1, w3, w2, layer, *, tf=256, tn=512):
    e, m, d = xs.shape
    f = w1.shape[3]
    tf = _tile(f, tf)
    tn = _tile(d, tn)
    nt = f // tf
    nn = d // tn
    return pl.pallas_call(
        functools.partial(_ffn_kernel, nt=nt, tf=tf),
        grid=(e, nt + nn),
        in_specs=[
            pl.BlockSpec((1, m, d), lambda ei, t: (ei, 0, 0)),
            pl.BlockSpec((1, 1, d, tf), lambda ei, t: (layer, ei, 0, jnp.minimum(t, nt - 1))),
            pl.BlockSpec((1, 1, d, tf), lambda ei, t: (layer, ei, 0, jnp.minimum(t, nt - 1))),
            pl.BlockSpec((1, 1, f, tn), lambda ei, t: (layer, ei, 0, jnp.maximum(t - nt, 0))),
            pl.BlockSpec((1, m // LANES, LANES), lambda ei, t: (ei, 0, 0)),
        ],
        out_specs=pl.BlockSpec((1, m, tn), lambda ei, t: (ei, 0, jnp.maximum(t - nt, 0))),
        out_shape=jax.ShapeDtypeStruct((e, m, d), F32),
        scratch_shapes=[pltpu.VMEM((nt, m, tf), BF16)],
        compiler_params=_params(("parallel", "arbitrary")),
        name="moe_ffn",
    )(xs, w1, w3, w2, gate)


def _combine_kernel(idx_ref, ys_ref, o_ref, rows_ref, *, cap, half, pitch, unroll):
    ei = pl.program_id(2)

    @pl.when(ei == 0)
    def _():
        o_ref[...] = jnp.zeros(o_ref.shape, F32)

    for s in range(half // LANES):
        rows_ref[pl.ds(s, cap, stride=pitch), :] = ys_ref[0, :, s * LANES:(s + 1) * LANES]

    def body(j0, c):
        dsts, vals = [], []
        for u in range(unroll):
            j = j0 * unroll + u
            dst = pl.multiple_of(idx_ref[0, 0, j] * pitch, pitch)
            src = pl.multiple_of(j * pitch, pitch)
            vals.append(o_ref[0, 0, pl.ds(dst, pitch), :] + rows_ref[pl.ds(src, pitch), :])
            dsts.append(dst)
        for dst, val in zip(dsts, vals):
            o_ref[0, 0, pl.ds(dst, pitch), :] = val
        return c

    lax.fori_loop(0, cap // unroll, body, 0)


def moe_combine(idx, ys, *, seq, d):
    b, e, cap = idx.shape
    half = d // 2
    pitch = _slab_pitch(half)
    return pl.pallas_call(
        functools.partial(_combine_kernel, cap=cap, half=half, pitch=pitch, unroll=_tile(cap, 16)),
        grid=(b, 2, e),
        in_specs=[
            pl.BlockSpec((1, 1, cap), lambda bi, dh, ei: (bi * e + ei, 0, 0), memory_space=pltpu.SMEM),
            pl.BlockSpec((1, cap, half), lambda bi, dh, ei: (ei, bi, dh)),
        ],
        out_specs=pl.BlockSpec((1, 1, seq * pitch, LANES), lambda bi, dh, ei: (bi, dh, 0, 0)),
        out_shape=jax.ShapeDtypeStruct((b, 2, seq * pitch, LANES), F32),
        scratch_shapes=[pltpu.VMEM((cap * pitch, LANES), F32)],
        compiler_params=_params(("parallel", "parallel", "arbitrary")),
        name="moe_combine",
    )(idx.reshape(b * e, 1, cap), ys)


def _unslab(s0_ref, s1_ref, *, tm, half, pitch):
    parts = []
    for s_ref in (s0_ref, s1_ref):
        for s in range(half // LANES):
            parts.append(s_ref[0, 0, pl.ds(s, tm, stride=pitch), :])
    return jnp.concatenate(parts, axis=1)


def _unslab_kernel(*refs, tm, half, pitch, final_norm):
    x_ref, s0_ref, s1_ref = refs[:3]
    o_ref = refs[-1]
    y = x_ref[0] + _unslab(s0_ref, s1_ref, tm=tm, half=half, pitch=pitch)
    if final_norm:
        y = _rms(y, refs[3][...])
    o_ref[0] = y


def moe_residual(x, moe_slabs, final_g=None, *, tm=512):
    b, s, d = x.shape
    half = d // 2
    pitch = _slab_pitch(half)
    tm = _tile(s, tm)
    final_norm = final_g is not None
    in_specs = [
        pl.BlockSpec((1, tm, d), lambda bi, j: (bi, j, 0)),
        pl.BlockSpec((1, 1, tm * pitch, LANES), lambda bi, j: (bi, 0, j, 0)),
        pl.BlockSpec((1, 1, tm * pitch, LANES), lambda bi, j: (bi, 1, j, 0)),
    ]
    args = [x, moe_slabs, moe_slabs]
    if final_norm:
        in_specs.append(pl.BlockSpec((1, d), lambda bi, j: (0, 0)))
        args.append(final_g.reshape(1, d))
    return pl.pallas_call(
        functools.partial(_unslab_kernel, tm=tm, half=half, pitch=pitch, final_norm=final_norm),
        grid=(b, s // tm),
        in_specs=in_specs,
        out_specs=pl.BlockSpec((1, tm, d), lambda bi, j: (bi, j, 0)),
        out_shape=jax.ShapeDtypeStruct((b, s, d), F32),
        compiler_params=_params(("parallel", "parallel")),
        name="moe_residual",
    )(*args)


def expert_choice_ffn(aff, slabs, w1, w3, w2, layer, *, seq, d):
    cap = CAPACITY_FACTOR * seq // N_EXPERTS
    idx, gate = moe_topk(aff, cap)
    b, e, _ = idx.shape
    xs = moe_gather(idx, slabs, seq=seq, d=d)
    ys = moe_ffn(xs, gate.transpose(1, 0, 2).reshape(e, b * cap // LANES, LANES), w1, w3, w2, layer)
    return moe_combine(idx, ys, seq=seq, d=d)


def kernel(x, rel_bias, mix_norm, ffn_norm, final_norm, even_w_in, attn_sink, ssm_a_re, ssm_a_im, ssm_log_dt, ssm_b_re, ssm_b_im, ssm_c_re, ssm_c_im, ssm_d, glu_w, glu_b, even_w_out, odd_w_in, sgu_ln_g, sgu_ln_b, sgu_w, sgu_b, odd_w_out, router, moe_w1, moe_w3, moe_w2):
    b, s, d = x.shape
    t = b * s
    depth = mix_norm.shape[0]
    ssm_w = ssm_d.shape[-1]
    attn_w = even_w_out.shape[1] - ssm_w
    kv_w = (even_w_in.shape[-1] - attn_w - ssm_w) // 2
    bias = attention_bias(rel_bias)

    x2 = x.reshape(t, d)
    moe = None
    for layer in range(depth):
        i = layer // 2
        if layer % 2 == 0:
            if moe is not None:
                x2 = moe_residual(x2.reshape(b, s, d), moe).reshape(t, d)
            qkv, u = in_projection(x2, mix_norm[layer], even_w_in[i].astype(BF16), attn_w + 2 * kv_w)
            attn = windowed_attention(qkv.reshape(b, s, -1), attn_sink[i], bias,
                                      attn_w=attn_w, kv_w=kv_w)
            n_levels = max(1, (s // S5_CHUNK - 1).bit_length())
            operands = _s5_operands(ssm_a_re[i], ssm_a_im[i], ssm_log_dt[i], ssm_b_re[i], ssm_b_im[i],
                                    ssm_c_re[i], ssm_c_im[i], S5_CHUNK, n_levels)
            y = s5_scan(u, operands, seq=s, chunk=S5_CHUNK, n_levels=n_levels)
            ssm = s5_glu(y, u, ssm_d[i], glu_w[i].astype(BF16), glu_b[i])
            mixed, w_out = [attn.reshape(t, attn_w), ssm], even_w_out[i]
        else:
            gated, x2 = gmlp_gate(x2, moe, mix_norm[layer], odd_w_in[i].astype(BF16), sgu_ln_g[i],
                                  sgu_ln_b[i], sgu_w[i].astype(BF16), sgu_b[i], seq=s)
            mixed, w_out = [gated], odd_w_out[i]
        x2, aff, slabs = matmul_residual_route(mixed, w_out.astype(BF16), x2, ffn_norm[layer],
                                               router[layer], seq=s)
        moe = expert_choice_ffn(aff, slabs, moe_w1, moe_w3, moe_w2, layer, seq=s, d=d)
    return moe_residual(x2.reshape(b, s, d), moe, final_norm)
```
